```python
import jax, jax.numpy as jnp
from jax import lax
import numpy as np

D_MODEL = 1024
BATCH = 8
SEQ = 4096
DEPTH = 1

D_RNN = D_MODEL // 2
RNN_BLOCKS = 8
RNN_BLOCK = D_RNN // RNN_BLOCKS
CONV_WIDTH = 4
LRU_C = 8.0
N_HEADS = 8
HEAD_DIM = 64
N_KV = 2
REP = N_HEADS // N_KV
D_ATT = N_HEADS * HEAD_DIM
CMP_LEN = 32
CMP_STRIDE = 16
CMP_HIDDEN = 256
SLC_BLOCK = 64
SLC_TOPK = 16
WINDOW = 512
Q_BLOCK = 64
D_FF = 4 * D_MODEL
EPS = 1e-6
NEG = -1e30
FORCED = 1e4

kernel_name = 'hybrid_rglru_nsa_block'


def rms_norm(x, g):
    xf = x.astype(jnp.float32)
    y = xf * lax.rsqrt(jnp.mean(xf * xf, axis=-1, keepdims=True) + EPS)
    return (y * g.astype(jnp.float32)).astype(x.dtype)


def modulate(h, shift, scale):
    return h * (1.0 + scale[:, None, :]) + shift[:, None, :]


def alibi_slopes():
    return jnp.asarray(2.0 ** (-8.0 * np.arange(1, N_HEADS + 1) / N_HEADS), jnp.float32)


def causal_depthwise_conv(x, w, b):
    y = lax.conv_general_dilated(x, w[:, None, :], window_strides=(1,), padding=[(CONV_WIDTH - 1, 0)],
                                 dimension_numbers=('NWC', 'WIO', 'NWC'), feature_group_count=x.shape[-1])
    return y + b


def block_diag(x, w, b):
    xb = x.reshape(x.shape[:-1] + (RNN_BLOCKS, RNN_BLOCK))
    return jnp.einsum('bsni,nij->bsnj', xb, w.astype(jnp.float32)).reshape(x.shape) + b.astype(jnp.float32)


def rg_lru(x, w_a, b_a, w_x, b_x, lam):
    xf = x.astype(jnp.float32)
    r = jax.nn.sigmoid(block_diag(xf, w_a, b_a))
    i = jax.nn.sigmoid(block_diag(xf, w_x, b_x))
    log_a = -LRU_C * r * jax.nn.softplus(-lam.astype(jnp.float32))
    a = jnp.exp(log_a)
    u = jnp.sqrt(-jnp.expm1(2.0 * log_a)) * (i * xf)

    def combine(e1, e2):
        a1, b1 = e1
        a2, b2 = e2
        return a1 * a2, a2 * b1 + b2

    _, h = lax.associative_scan(combine, (a, u), axis=1)
    return h.astype(x.dtype)


def to_heads(t):
    B, S, _ = t.shape
    return t.reshape(B, S, N_KV, HEAD_DIM).transpose(0, 2, 1, 3)


def compress_blocks(k, pos_emb, w1, w2):
    B, G, S, D = k.shape
    n_cmp = (S - CMP_LEN) // CMP_STRIDE + 1
    idx = CMP_STRIDE * jnp.arange(n_cmp)[:, None] + jnp.arange(CMP_LEN)[None, :]
    blocks = (k[:, :, idx] + pos_emb).reshape(B, G, n_cmp, CMP_LEN * D)
    return jax.nn.gelu(blocks @ w1) @ w2


def nsa_attention(q, k_c, v_c, k_s, v_s, k_w, v_w, gate_logits,
                  cmp_pos_k, cmp_w1_k, cmp_w2_k, cmp_pos_v, cmp_w1_v, cmp_w2_v):
    B, S, _ = q.shape
    f32 = jnp.float32
    n_cmp = (S - CMP_LEN) // CMP_STRIDE + 1
    n_slc = S // SLC_BLOCK
    n_sel = min(SLC_TOPK, n_slc)
    n_qb = S // Q_BLOCK
    qh = q.reshape(B, S, N_KV, REP, HEAD_DIM).transpose(0, 2, 3, 1, 4) * (HEAD_DIM ** -0.5)
    kc = compress_blocks(to_heads(k_c), cmp_pos_k, cmp_w1_k, cmp_w2_k)
    vc = compress_blocks(to_heads(v_c), cmp_pos_v, cmp_w1_v, cmp_w2_v).astype(f32)
    ks = to_heads(k_s).reshape(B, N_KV, n_slc, SLC_BLOCK, HEAD_DIM)
    vs = to_heads(v_s).reshape(B, N_KV, n_slc, SLC_BLOCK, HEAD_DIM)
    pad = ((0, 0), (0, 0), (WINDOW, 0), (0, 0))
    kw = jnp.pad(to_heads(k_w), pad)
    vw = jnp.pad(to_heads(v_w), pad)
    gates = jax.nn.sigmoid(gate_logits.astype(f32)).reshape(B, S, 3, N_KV, REP).transpose(2, 0, 3, 4, 1)
    slopes = alibi_slopes().reshape(1, N_KV, REP, 1, 1)
    c_start = CMP_STRIDE * jnp.arange(n_cmp)
    cmp_end = c_start + (CMP_LEN - 1)
    s_start = SLC_BLOCK * jnp.arange(n_slc)
    overlap = ((c_start[:, None] < s_start[None, :] + SLC_BLOCK) &
               (c_start[:, None] + CMP_LEN > s_start[None, :])).astype(f32)
    blk_ids = jnp.arange(n_slc)
    gather = jax.vmap(jax.vmap(lambda kb, ix: kb[ix]))

    def one_block(i):
        q0 = i * Q_BLOCK
        qb = lax.dynamic_slice_in_dim(qh, q0, Q_BLOCK, axis=3)
        t = q0 + jnp.arange(Q_BLOCK)
        d_c = (t[:, None] - cmp_end[None, :]).astype(f32)
        sc = jnp.einsum('bgrqd,bgcd->bgrqc', qb, kc).astype(f32) - slopes * d_c
        sc = jnp.where(d_c >= 0, sc, NEG)
        p_c = jax.nn.softmax(sc, axis=-1) * (t >= CMP_LEN - 1).astype(f32)[:, None]
        o_c = jnp.einsum('bgrqc,bgcd->bgrqd', p_c, vc)
        imp = jnp.einsum('bgrqc,cj->bgqj', p_c, overlap)
        cur = t // SLC_BLOCK
        forced = (blk_ids[None, :] == 0) | (blk_ids[None, :] == cur[:, None]) | (blk_ids[None, :] == cur[:, None] - 1)
        visible = s_start[None, :] <= t[:, None]
        imp = jnp.where(forced, FORCED, jnp.where(visible, imp, NEG))
        _, sel = lax.top_k(imp, n_sel)
        k_sel = gather(ks, sel)
        v_sel = gather(vs, sel).astype(f32)
        pos = sel[..., None] * SLC_BLOCK + jnp.arange(SLC_BLOCK)
        d_s = (t[:, None, None] - pos).astype(f32)[:, :, None]
        ss = jnp.einsum('bgrqd,bgqnkd->bgrqnk', qb, k_sel).astype(f32) - slopes[..., None] * d_s
        ss = jnp.where(d_s >= 0, ss, NEG)
        p_s = jax.nn.softmax(ss.reshape(ss.shape[:4] + (-1,)), axis=-1).reshape(ss.shape)
        o_s = jnp.einsum('bgrqnk,bgqnkd->bgrqd', p_s, v_sel)
        kwb = lax.dynamic_slice_in_dim(kw, q0, WINDOW + Q_BLOCK, axis=2)
        vwb = lax.dynamic_slice_in_dim(vw, q0, WINDOW + Q_BLOCK, axis=2).astype(f32)
        s_pos = q0 - WINDOW + jnp.arange(WINDOW + Q_BLOCK)
        d_w = (t[:, None] - s_pos[None, :]).astype(f32)
        ok = (d_w >= 0) & (d_w < WINDOW) & (s_pos[None, :] >= 0)
        sw = jnp.einsum('bgrqd,bgkd->bgrqk', qb, kwb).astype(f32) - slopes * d_w
        sw = jnp.where(ok, sw, NEG)
        o_w = jnp.einsum('bgrqk,bgkd->bgrqd', jax.nn.softmax(sw, axis=-1), vwb)
        gb = lax.dynamic_slice_in_dim(gates, q0, Q_BLOCK, axis=4)[..., None]
        return gb[0] * o_c + gb[1] * o_s + gb[2] * o_w

    out = lax.map(one_block, jnp.arange(n_qb))
    return out.transpose(1, 0, 4, 2, 3, 5).reshape(B, S, D_ATT).astype(q.dtype)


def hybrid_mixer(h, w_in, conv_w, conv_b, lru_wa, lru_ba, lru_wx, lru_bx, lru_lambda,
                 cmp_pos_k, cmp_w1_k, cmp_w2_k, cmp_pos_v, cmp_w1_v, cmp_w2_v,
                 norm_rnn_out, norm_att_out, w_out):
    sizes = [D_RNN, D_RNN, D_ATT] + [N_KV * HEAD_DIM] * 6 + [3 * N_HEADS]
    splits = np.cumsum(sizes)[:-1].tolist()
    z = h @ w_in
    g_rnn, x_rnn, q, k_c, v_c, k_s, v_s, k_w, v_w, gate_logits = jnp.split(z, splits, axis=-1)
    rnn = jax.nn.gelu(g_rnn) * rg_lru(causal_depthwise_conv(x_rnn, conv_w, conv_b),
                                      lru_wa, lru_ba, lru_wx, lru_bx, lru_lambda)
    att = nsa_attention(q, k_c, v_c, k_s, v_s, k_w, v_w, gate_logits,
                        cmp_pos_k, cmp_w1_k, cmp_w2_k, cmp_pos_v, cmp_w1_v, cmp_w2_v)
    y = jnp.concatenate([rms_norm(rnn, norm_rnn_out), rms_norm(att, norm_att_out)], axis=-1)
    return y @ w_out


def setup_inputs(seed: int = 0) -> dict:
    key = jax.random.key(seed)
    ks = jax.random.split(key, 32)
    L = DEPTH
    f32 = jnp.float32

    def nrm(k, shape, scale):
        return jax.random.normal(k, shape, f32) * scale

    def gain(k, n):
        return 1.0 + 0.05 * jax.random.normal(k, (L, n), f32)

    d_in = 2 * D_RNN + D_ATT + 6 * N_KV * HEAD_DIM + 3 * N_HEADS
    u = jax.random.uniform(ks[31], (L, D_RNN), f32, minval=0.9, maxval=0.999)
    a0 = u ** (1.0 / LRU_C)
    lam = jnp.log(a0) - jnp.log1p(-a0)
    return {
        'x': nrm(ks[0], (BATCH, SEQ, D_MODEL), 1.0),
        'c': nrm(ks[1], (BATCH, D_MODEL), 1.0),
        'ada_w': nrm(ks[2], (L, D_MODEL, 6 * D_MODEL), 0.1 * D_MODEL ** -0.5),
        'ada_b': nrm(ks[3], (L, 6 * D_MODEL), 0.01),
        'pre_norm_mix': gain(ks[4], D_MODEL),
        'w_in': nrm(ks[5], (L, D_MODEL, d_in), D_MODEL ** -0.5),
        'conv_w': nrm(ks[6], (L, CONV_WIDTH, D_RNN), CONV_WIDTH ** -0.5),
        'conv_b': nrm(ks[7], (L, D_RNN), 0.01),
        'lru_wa': nrm(ks[8], (L, RNN_BLOCKS, RNN_BLOCK, RNN_BLOCK), RNN_BLOCK ** -0.5),
        'lru_ba': nrm(ks[9], (L, D_RNN), 0.01),
        'lru_wx': nrm(ks[10], (L, RNN_BLOCKS, RNN_BLOCK, RNN_BLOCK), RNN_BLOCK ** -0.5),
        'lru_bx': nrm(ks[11], (L, D_RNN), 0.01),
        'lru_lambda': lam,
        'cmp_pos_k': nrm(ks[12], (L, CMP_LEN, HEAD_DIM), 0.1),
        'cmp_w1_k': nrm(ks[13], (L, CMP_LEN * HEAD_DIM, CMP_HIDDEN), (CMP_LEN * HEAD_DIM) ** -0.5),
        'cmp_w2_k': nrm(ks[14], (L, CMP_HIDDEN, HEAD_DIM), CMP_HIDDEN ** -0.5),
        'cmp_pos_v': nrm(ks[15], (L, CMP_LEN, HEAD_DIM), 0.1),
        'cmp_w1_v': nrm(ks[16], (L, CMP_LEN * HEAD_DIM, CMP_HIDDEN), (CMP_LEN * HEAD_DIM) ** -0.5),
        'cmp_w2_v': nrm(ks[17], (L, CMP_HIDDEN, HEAD_DIM), CMP_HIDDEN ** -0.5),
        'norm_rnn_out': gain(ks[18], D_RNN),
        'norm_att_out': gain(ks[19], D_ATT),
        'w_out': nrm(ks[20], (L, D_RNN + D_ATT, D_MODEL), (D_RNN + D_ATT) ** -0.5),
        'post_norm_mix': gain(ks[21], D_MODEL),
        'pre_norm_mlp': gain(ks[22], D_MODEL),
        'w_ff1': nrm(ks[23], (L, D_MODEL, D_FF), D_MODEL ** -0.5),
        'w_ff2': nrm(ks[24], (L, D_FF, D_MODEL), D_FF ** -0.5),
        'post_norm_mlp': gain(ks[25], D_MODEL),
    }


def reference(x, c, ada_w, ada_b, pre_norm_mix, w_in, conv_w, conv_b, lru_wa, lru_ba, lru_wx, lru_bx,
              lru_lambda, cmp_pos_k, cmp_w1_k, cmp_w2_k, cmp_pos_v, cmp_w1_v, cmp_w2_v,
              norm_rnn_out, norm_att_out, w_out, post_norm_mix, pre_norm_mlp, w_ff1, w_ff2, post_norm_mlp):
    for l in range(DEPTH):
        mod = jax.nn.silu(c) @ ada_w[l] + ada_b[l]
        sh1, sc1, g1, sh2, sc2, g2 = jnp.split(mod, 6, axis=-1)
        h = modulate(rms_norm(x, pre_norm_mix[l]), sh1, sc1)
        y = hybrid_mixer(h, w_in[l], conv_w[l], conv_b[l], lru_wa[l], lru_ba[l], lru_wx[l], lru_bx[l],
                         lru_lambda[l], cmp_pos_k[l], cmp_w1_k[l], cmp_w2_k[l], cmp_pos_v[l], cmp_w1_v[l],
                         cmp_w2_v[l], norm_rnn_out[l], norm_att_out[l], w_out[l])
        x = x + (1.0 + g1[:, None, :]) * rms_norm(y, post_norm_mix[l])
        h = modulate(rms_norm(x, pre_norm_mlp[l]), sh2, sc2)
        y = jnp.square(jax.nn.relu(h @ w_ff1[l])) @ w_ff2[l]
        x = x + (1.0 + g2[:, None, :]) * rms_norm(y, post_norm_mlp[l])
    return x
```

```python
import functools

import jax
import jax.numpy as jnp
from jax import lax
from jax.experimental import pallas as pl
from jax.experimental.pallas import tpu as pltpu

F32 = jnp.float32
BF16 = jnp.bfloat16
I32 = jnp.int32

D_MODEL = 1024
D_RNN = 512
RNN_BLOCKS = 8
CONV_WIDTH = 4
LRU_C = 8.0
N_HEADS = 8
HEAD_DIM = 64
N_KV = 2
REP = N_HEADS // N_KV
D_ATT = N_HEADS * HEAD_DIM
CMP_LEN = 32
CMP_STRIDE = 16
CMP_HIDDEN = 256
SLC_BLOCK = 64
SLC_TOPK = 16
WINDOW = 512
D_FF = 4 * D_MODEL
EPS = 1e-6
NEG = -1e30
FORCED = 1e4

KV_COLS = 6 * N_KV * HEAD_DIM
N_GATE = 3 * N_HEADS
GATE_PAD = 128
OFF_Q = 2 * D_RNN
OFF_KV = OFF_Q + D_ATT
OFF_GATE = OFF_KV + KV_COLS
D_IN_PAD = OFF_GATE + GATE_PAD

TM_IN = 512
TM_OUT = 512
TQ = 256
TK = 256
VROWS = 80
VMEM_LIMIT = 56 * 1024 * 1024


def _gelu_tanh(x):
    return 0.5 * x * (1.0 + jnp.tanh(0.7978845608028654 * (x + 0.044715 * (x * x * x))))


def _rms(x):
    return x * lax.rsqrt(jnp.mean(x * x, axis=-1, keepdims=True) + EPS)


def _ada_kernel(c_ref, w_ref, b_ref, o_ref):
    c = c_ref[...]
    a = c * jax.nn.sigmoid(c)
    o_ref[...] = jnp.dot(a, w_ref[...], preferred_element_type=F32,
                         precision=lax.Precision.HIGHEST) + b_ref[...]


def _ada(c, w, b):
    B, D = c.shape
    N = w.shape[1]
    tn = 1024
    return pl.pallas_call(
        _ada_kernel,
        grid=(N // tn,),
        in_specs=[pl.BlockSpec((B, D), lambda j: (0, 0)),
                  pl.BlockSpec((D, tn), lambda j: (0, j)),
                  pl.BlockSpec((1, tn), lambda j: (0, j))],
        out_specs=pl.BlockSpec((B, tn), lambda j: (0, j)),
        out_shape=jax.ShapeDtypeStruct((B, N), F32),
        name="ada",
    )(c, w, b.reshape(1, N))


def _inproj_kernel(x_ref, gain_ref, sc_ref, sh_ref, w_ref, cw_ref, cb_ref, wa_ref, ba_ref, wx_ref, bx_ref,
                   lam_ref, grnn_ref,
                   rnn_ref, q_ref, kv_ref, gate_ref,
                   xbuf, hcar, a_s, u_s, h_s):
    tm = x_ref.shape[1]

    @pl.when(pl.program_id(1) == 0)
    def _():
        xbuf[0:8, :] = jnp.zeros((8, D_RNN), F32)
        hcar[...] = jnp.zeros((1, D_RNN), F32)

    x = x_ref[0]
    h = _rms(x) * (gain_ref[...] * (1.0 + sc_ref[0])) + sh_ref[0]
    hb = h.astype(BF16)

    q_ref[0] = (jnp.dot(hb, w_ref[:, OFF_Q:OFF_KV], preferred_element_type=F32)
                * (HEAD_DIM ** -0.5)).astype(BF16)
    kv_ref[0] = jnp.dot(hb, w_ref[:, OFF_KV:OFF_GATE], preferred_element_type=F32).astype(BF16)
    gate_ref[0] = jax.nn.sigmoid(jnp.dot(hb, w_ref[:, OFF_GATE:D_IN_PAD], preferred_element_type=F32))

    xr = jnp.dot(hb, w_ref[:, D_RNN:OFF_Q], preferred_element_type=F32)
    xbuf[8:8 + tm, :] = xr
    y = (cw_ref[3:4, :] * xr + cw_ref[2:3, :] * xbuf[7:7 + tm, :]
         + cw_ref[1:2, :] * xbuf[6:6 + tm, :] + cw_ref[0:1, :] * xbuf[5:5 + tm, :]) + cb_ref[...]
    xbuf[0:8, :] = xbuf[tm:tm + 8, :]

    yb = y.astype(BF16)
    r = jax.nn.sigmoid(jnp.dot(yb, wa_ref[...], preferred_element_type=F32) + ba_ref[...])
    i = jax.nn.sigmoid(jnp.dot(yb, wx_ref[...], preferred_element_type=F32) + bx_ref[...])
    nl = -lam_ref[...]
    softplus = jnp.maximum(nl, 0.0) + jnp.log(1.0 + jnp.exp(-jnp.abs(nl)))
    a = jnp.exp((-LRU_C) * r * softplus)
    a_s[...] = a
    u_s[...] = jnp.sqrt(1.0 - a * a) * (i * y)

    rows = lax.broadcasted_iota(I32, (8, D_RNN), 0)

    def group(gi, hprev):
        r0 = pl.multiple_of(gi * 8, 8)
        ag = a_s[pl.ds(r0, 8), :]
        ug = u_s[pl.ds(r0, 8), :]
        for k in (1, 2, 4):
            a_sh = jnp.where(rows >= k, pltpu.roll(ag, k, 0), 1.0)
            u_sh = jnp.where(rows >= k, pltpu.roll(ug, k, 0), 0.0)
            ug = ag * u_sh + ug
            ag = ag * a_sh
        hg = ag * hprev + ug
        h_s[pl.ds(r0, 8), :] = hg
        return hg[7:8, :]

    hcar[...] = lax.fori_loop(0, tm // 8, group, hcar[...], unroll=4)

    g = jnp.dot(hb, w_ref[:, 0:D_RNN], preferred_element_type=F32)
    rnn = _gelu_tanh(g) * h_s[...]
    rnn_ref[0] = (_rms(rnn) * grnn_ref[...]).astype(BF16)


def _inproj(x, gain, sc, sh, w_in, conv_w, conv_b, wa, ba, wx, bx, lam, grnn):
    B, S, D = x.shape
    tm = min(TM_IN, S)
    row = lambda n: pl.BlockSpec((1, n), lambda b, s: (0, 0))
    per_b = lambda n: pl.BlockSpec((1, 1, n), lambda b, s: (b, 0, 0))
    full = lambda a: pl.BlockSpec(a.shape, lambda b, s: (0,) * a.ndim)
    tok = lambda n: pl.BlockSpec((1, tm, n), lambda b, s: (b, s, 0))
    return pl.pallas_call(
        _inproj_kernel,
        grid=(B, S // tm),
        in_specs=[tok(D), row(D), per_b(D), per_b(D), full(w_in), full(conv_w), row(D_RNN),
                  full(wa), row(D_RNN), full(wx), row(D_RNN), row(D_RNN), row(D_RNN)],
        out_specs=[tok(D_RNN), tok(D_ATT), tok(KV_COLS), tok(GATE_PAD)],
        out_shape=[jax.ShapeDtypeStruct((B, S, D_RNN), BF16),
                   jax.ShapeDtypeStruct((B, S, D_ATT), BF16),
                   jax.ShapeDtypeStruct((B, S, KV_COLS), BF16),
                   jax.ShapeDtypeStruct((B, S, GATE_PAD), F32)],
        scratch_shapes=[pltpu.VMEM((tm + 8, D_RNN), F32), pltpu.VMEM((1, D_RNN), F32),
                        pltpu.VMEM((tm, D_RNN), F32), pltpu.VMEM((tm, D_RNN), F32),
                        pltpu.VMEM((tm, D_RNN), F32)],
        compiler_params=pltpu.CompilerParams(dimension_semantics=("arbitrary", "arbitrary"),
                                             vmem_limit_bytes=VMEM_LIMIT),
        name="inproj",
    )(x, gain, sc, sh, w_in, conv_w, conv_b, wa, ba, wx, bx, lam, grnn)


def _compress_kernel(rk_ref, rv_ref, w1k_ref, w2k_ref, pk_ref, w1v_ref, w2v_ref, pv_ref, kc_ref, vc_ref):
    def one(r_ref, w1_ref, w2_ref, pos_ref, o_ref):
        n = r_ref.shape[2]
        p = jnp.dot(r_ref[0, 0], w1_ref[...], preferred_element_type=F32)
        posb = jnp.dot(pos_ref[...], w1_ref[...], preferred_element_type=F32)
        bias = posb[0:1, 0:CMP_HIDDEN] + posb[1:2, CMP_HIDDEN:]
        pre = p[:, 0:CMP_HIDDEN] + pltpu.roll(p[:, CMP_HIDDEN:], n - 1, 0) + bias
        hid = _gelu_tanh(pre).astype(BF16)
        o_ref[0, 0] = jnp.dot(hid, w2_ref[...], preferred_element_type=F32).astype(BF16)

    one(rk_ref, w1k_ref, w2k_ref, pk_ref, kc_ref)
    one(rv_ref, w1v_ref, w2v_ref, pv_ref, vc_ref)


def _compress(rk, rv, w1k, w2k, posk, w1v, w2v, posv):
    B, G, n, width = rk.shape
    blk = pl.BlockSpec((1, 1, n, width), lambda b, g: (b, g, 0, 0))
    full = lambda a: pl.BlockSpec(a.shape, lambda b, g: (0,) * a.ndim)
    out = pl.BlockSpec((1, 1, n, HEAD_DIM), lambda b, g: (b, g, 0, 0))
    return pl.pallas_call(
        _compress_kernel,
        grid=(B, G),
        in_specs=[blk, blk, full(w1k), full(w2k), full(posk), full(w1v), full(w2v), full(posv)],
        out_specs=[out, out],
        out_shape=[jax.ShapeDtypeStruct((B, G, n, HEAD_DIM), BF16)] * 2,
        compiler_params=pltpu.CompilerParams(vmem_limit_bytes=VMEM_LIMIT),
        name="compress",
    )(rk, rv, w1k, w2k, posk, w1v, w2v, posv)


def _attn_kernel(qT_ref, ks_ref, vs_ref, kw_ref, vw_ref, kc_ref, vcT_ref, g_ref,
                 o_ref,
                 ksa, kwa, vsa, vwa, qa, qw, m_s, acc_s, oacc, val_s, cbias, wbias, *, n_sel):
    S = ks_ref.shape[2]
    nkt = S // TK
    ncmp = kc_ref.shape[2]
    b = pl.program_id(0)
    g = pl.program_id(1)
    qi = pl.program_id(2)
    q0 = qi * TQ
    W4 = REP * TQ

    @pl.when((b == 0) & (g == 0) & (qi == 0))
    def _():
        for kt in range(nkt):
            key = kt * TK + lax.broadcasted_iota(I32, (TK, 256), 0)
            col = lax.broadcasted_iota(I32, (TK, 256), 1)
            blk = (key // SLC_BLOCK).astype(F32)
            off = (key % SLC_BLOCK).astype(F32)
            onehot = (col - HEAD_DIM == key // SLC_BLOCK).astype(F32)
            aug = jnp.where(col == 2 * HEAD_DIM, blk, jnp.where(col == 2 * HEAD_DIM + 1, off, onehot))
            ksa[kt * TK:(kt + 1) * TK, :] = aug.astype(BF16)
            augw = jnp.where(col == HEAD_DIM, blk, jnp.where(col == HEAD_DIM + 1, off, 0.0))
            kwa[kt * TK:(kt + 1) * TK, :] = augw[:, 0:128].astype(BF16)
            ones_row = (lax.broadcasted_iota(I32, (VROWS - HEAD_DIM, TK), 0) == 0).astype(BF16)
            vsa[kt, HEAD_DIM:VROWS, :] = ones_row
            vwa[kt, HEAD_DIM:VROWS, :] = ones_row
        ko = lax.broadcasted_iota(I32, (TK, TQ), 0)
        to = lax.broadcasted_iota(I32, (TK, TQ), 1)
        cbias[...] = jnp.where(ko <= to, 0.0, NEG)
        wbias[...] = jnp.where(ko > to, 0.0, NEG)
        qa[...] = jnp.zeros(qa.shape, BF16)
        qw[...] = jnp.zeros(qw.shape, BF16)

    lane = lax.broadcasted_iota(I32, (1, W4), 1)
    head = g * REP + lane // TQ
    slope = lax.bitcast_convert_type((126 - head) << 23, F32)

    @pl.when(qi == 0)
    def _():
        for kt in range(nkt):
            ksa[kt * TK:(kt + 1) * TK, 0:HEAD_DIM] = ks_ref[0, 0, kt * TK:(kt + 1) * TK, :]
            kwa[kt * TK:(kt + 1) * TK, 0:HEAD_DIM] = kw_ref[0, 0, kt * TK:(kt + 1) * TK, :]
            vsa[kt, 0:HEAD_DIM, :] = vs_ref[0, 0, kt]
            vwa[kt, 0:HEAD_DIM, :] = vw_ref[0, 0, kt]
        r16 = lax.broadcasted_iota(I32, (16, W4), 0)
        alibi = jnp.where(r16 == 0, slope * SLC_BLOCK, jnp.where(r16 == 1, slope, 0.0)).astype(BF16)
        qa[2 * HEAD_DIM:2 * HEAD_DIM + 16, :] = alibi
        qw[HEAD_DIM:HEAD_DIM + 16, :] = alibi

    q = qT_ref[0]
    for r in range(REP):
        qr = q[r * HEAD_DIM:(r + 1) * HEAD_DIM, :]
        qa[0:HEAD_DIM, r * TQ:(r + 1) * TQ] = qr
        qw[0:HEAD_DIM, r * TQ:(r + 1) * TQ] = qr

    gates = g_ref[0, 0]

    sc = jnp.dot(kc_ref[0, 0], qa[0:HEAD_DIM, :], preferred_element_type=F32)
    c_i = lax.broadcasted_iota(I32, (ncmp, W4), 0)
    t_i = q0 + (lax.broadcasted_iota(I32, (ncmp, W4), 1) % TQ)
    d_c = t_i - (CMP_STRIDE * c_i + (CMP_LEN - 1))
    sc = jnp.where(d_c >= 0, sc - slope * d_c.astype(F32), NEG)
    e = jnp.exp(sc - jnp.max(sc, axis=0, keepdims=True))
    tq = q0 + lane % TQ
    p = e * ((1.0 / jnp.sum(e, axis=0, keepdims=True)) * (tq >= CMP_LEN - 1).astype(F32))
    ocT = jnp.dot(vcT_ref[0, 0], p.astype(BF16), preferred_element_type=F32)
    for r in range(REP):
        oacc[:, r * TQ:(r + 1) * TQ] = gates[0, r:r + 1, :] * ocT[:, r * TQ:(r + 1) * TQ]

    psum = p[:, 0:TQ]
    for r in range(1, REP):
        psum = psum + p[:, r * TQ:(r + 1) * TQ]
    nblk = S // SLC_BLOCK
    jj = lax.broadcasted_iota(I32, (nblk, ncmp), 0)
    cc = lax.broadcasted_iota(I32, (nblk, ncmp), 1)
    ovT = ((CMP_STRIDE * cc < SLC_BLOCK * jj + SLC_BLOCK)
           & (CMP_STRIDE * cc + CMP_LEN > SLC_BLOCK * jj)).astype(F32)
    impT = jnp.dot(ovT, psum, preferred_element_type=F32, precision=lax.Precision.HIGHEST)

    j_i = lax.broadcasted_iota(I32, (nblk, TQ), 0)
    t1 = q0 + lax.broadcasted_iota(I32, (nblk, TQ), 1)
    cur = t1 // SLC_BLOCK
    forced = (j_i == 0) | (j_i == cur) | (j_i == cur - 1)
    visible = SLC_BLOCK * j_i <= t1
    val = jnp.where(forced, FORCED, jnp.where(visible, impT, NEG))
    val_s[...] = val
    rank = jnp.zeros((nblk, TQ), F32)
    for i in range(nblk):
        row = val_s[i:i + 1, :]
        rank = rank + jnp.where(j_i > i, jnp.where(row >= val, 1.0, 0.0), jnp.where(row > val, 1.0, 0.0))
    selbias = jnp.where(rank < n_sel, 0.0, NEG).astype(BF16)
    for r in range(REP):
        qa[HEAD_DIM:HEAD_DIM + nblk, r * TQ:(r + 1) * TQ] = selbias

    def step(k_tile, v_tile, q_op, bias):
        s = jnp.dot(k_tile, q_op, preferred_element_type=F32)
        if bias is not None:
            s = s + jnp.concatenate([bias] * REP, axis=1)
        m_old = m_s[...]
        m_new = jnp.maximum(m_old, jnp.max(s, axis=0, keepdims=True))
        pt = jnp.exp(s - m_new).astype(BF16)
        acc_s[...] = acc_s[...] * jnp.exp(m_old - m_new) + jnp.dot(v_tile, pt, preferred_element_type=F32)
        m_s[...] = m_new

    def reset():
        m_s[...] = jnp.full(m_s.shape, NEG, F32)
        acc_s[...] = jnp.zeros(acc_s.shape, F32)

    def finish(gate_rows):
        acc = acc_s[...]
        o = acc[0:HEAD_DIM, :] / acc[HEAD_DIM:HEAD_DIM + 1, :]
        return [gate_rows[r:r + 1, :] * o[:, r * TQ:(r + 1) * TQ] for r in range(REP)]

    reset()

    def sel_body(kt, carry):
        k0 = pl.multiple_of(kt * TK, TK)
        step(ksa[pl.ds(k0, TK), :], vsa[kt], qa[...], None)
        return carry

    lax.fori_loop(0, qi, sel_body, 0)
    k0 = pl.multiple_of(qi * TK, TK)
    step(ksa[pl.ds(k0, TK), :], vsa[qi], qa[...], cbias[...])
    for r, o_r in enumerate(finish(gates[1])):
        oacc[:, r * TQ:(r + 1) * TQ] += o_r

    reset()

    @pl.when(qi >= 2)
    def _():
        kk = pl.multiple_of((qi - 2) * TK, TK)
        step(kwa[pl.ds(kk, TK), :], vwa[qi - 2], qw[...], wbias[...])

    @pl.when(qi >= 1)
    def _():
        kk = pl.multiple_of((qi - 1) * TK, TK)
        step(kwa[pl.ds(kk, TK), :], vwa[qi - 1], qw[...], None)

    step(kwa[pl.ds(k0, TK), :], vwa[qi], qw[...], cbias[...])
    for r, o_r in enumerate(finish(gates[2])):
        o_ref[0, r * HEAD_DIM:(r + 1) * HEAD_DIM, :] = oacc[:, r * TQ:(r + 1) * TQ] + o_r


def _attention(qT, ks, vs, kw, vw, kc, vcT, gT):
    B, _, S = qT.shape
    G = N_KV
    nkt = S // TK
    ncmp = kc.shape[2]
    n_sel = min(SLC_TOPK, S // SLC_BLOCK)
    W4 = REP * TQ
    res4 = lambda a: pl.BlockSpec((1, 1) + a.shape[2:], lambda b, g, i: (b, g) + (0,) * (a.ndim - 2))
    return pl.pallas_call(
        functools.partial(_attn_kernel, n_sel=n_sel),
        grid=(B, G, S // TQ),
        in_specs=[pl.BlockSpec((1, REP * HEAD_DIM, TQ), lambda b, g, i: (b, g, i)),
                  res4(ks), res4(vs), res4(kw), res4(vw), res4(kc), res4(vcT),
                  pl.BlockSpec((1, 1, 3, REP, TQ), lambda b, g, i: (b, g, 0, 0, i))],
        out_specs=pl.BlockSpec((1, REP * HEAD_DIM, TQ), lambda b, g, i: (b, g, i)),
        out_shape=jax.ShapeDtypeStruct((B, D_ATT, S), F32),
        scratch_shapes=[pltpu.VMEM((S, 256), BF16), pltpu.VMEM((S, 128), BF16),
                        pltpu.VMEM((nkt, VROWS, TK), BF16), pltpu.VMEM((nkt, VROWS, TK), BF16),
                        pltpu.VMEM((256, W4), BF16), pltpu.VMEM((128, W4), BF16),
                        pltpu.VMEM((1, W4), F32), pltpu.VMEM((VROWS, W4), F32),
                        pltpu.VMEM((HEAD_DIM, W4), F32), pltpu.VMEM((S // SLC_BLOCK, TQ), F32),
                        pltpu.VMEM((TK, TQ), F32), pltpu.VMEM((TK, TQ), F32)],
        compiler_params=pltpu.CompilerParams(dimension_semantics=("arbitrary", "arbitrary", "arbitrary"),
                                             vmem_limit_bytes=VMEM_LIMIT),
        name="attn",
    )(qT, ks, vs, kw, vw, kc, vcT, gT)


def _outmlp_kernel(x_ref, rnn_ref, att_ref, gatt_ref, wo_ref, gpost_ref, g1_ref, gpre_ref, sc2_ref, sh2_ref,
                   w1_ref, w2_ref, gpost2_ref, g2_ref, o_ref):
    att_n = (_rms(att_ref[0]) * gatt_ref[...]).astype(BF16)
    y = (jnp.dot(rnn_ref[0], wo_ref[0:D_RNN, :], preferred_element_type=F32)
         + jnp.dot(att_n, wo_ref[D_RNN:, :], preferred_element_type=F32))
    x1 = x_ref[0] + (1.0 + g1_ref[0]) * (_rms(y) * gpost_ref[...])
    h2 = (_rms(x1) * (gpre_ref[...] * (1.0 + sc2_ref[0])) + sh2_ref[0]).astype(BF16)
    fc = 1024
    ff = jnp.zeros(x1.shape, F32)
    for c in range(D_FF // fc):
        hid = jnp.maximum(jnp.dot(h2, w1_ref[:, c * fc:(c + 1) * fc], preferred_element_type=F32), 0.0)
        ff = ff + jnp.dot((hid * hid).astype(BF16), w2_ref[c * fc:(c + 1) * fc, :], preferred_element_type=F32)
    o_ref[0] = x1 + (1.0 + g2_ref[0]) * (_rms(ff) * gpost2_ref[...])


def _outmlp(x, rnn_n, att, gatt, wo, gpost, g1, gpre, sc2, sh2, w1, w2, gpost2, g2):
    B, S, D = x.shape
    tm = min(TM_OUT, S)
    row = lambda n: pl.BlockSpec((1, n), lambda b, s: (0, 0))
    per_b = lambda n: pl.BlockSpec((1, 1, n), lambda b, s: (b, 0, 0))
    const = lambda a: pl.BlockSpec(a.shape, lambda b, s: (0,) * a.ndim, pipeline_mode=pl.Buffered(1))
    tok = lambda n: pl.BlockSpec((1, tm, n), lambda b, s: (b, s, 0))
    return pl.pallas_call(
        _outmlp_kernel,
        grid=(B, S // tm),
        in_specs=[tok(D), tok(D_RNN), tok(D_ATT), row(D_ATT), const(wo), row(D), per_b(D), row(D),
                  per_b(D), per_b(D), const(w1), const(w2), row(D), per_b(D)],
        out_specs=tok(D),
        out_shape=jax.ShapeDtypeStruct((B, S, D), F32),
        compiler_params=pltpu.CompilerParams(dimension_semantics=("arbitrary", "arbitrary"),
                                             vmem_limit_bytes=VMEM_LIMIT),
        name="outmlp",
    )(x, rnn_n, att, gatt, wo, gpost, g1, gpre, sc2, sh2, w1, w2, gpost2, g2)


def _block_diag(w):
    n, k, _ = w.shape
    return jnp.einsum('nij,nm->nimj', w, jnp.eye(n, dtype=w.dtype)).reshape(n * k, n * k)


def _layer(x, c, ada_w, ada_b, pre_norm_mix, w_in, conv_w, conv_b, lru_wa, lru_ba, lru_wx, lru_bx, lru_lambda,
           cmp_pos_k, cmp_w1_k, cmp_w2_k, cmp_pos_v, cmp_w1_v, cmp_w2_v, norm_rnn_out, norm_att_out, w_out,
           post_norm_mix, pre_norm_mlp, w_ff1, w_ff2, post_norm_mlp):
    B, S, D = x.shape
    G = N_KV
    row = lambda v: v.reshape(1, -1)

    mod = _ada(c, ada_w, ada_b)
    sh1, sc1, g1, sh2, sc2, g2 = [m.reshape(B, 1, D) for m in jnp.split(mod, 6, axis=-1)]

    w_in_p = jnp.pad(w_in, ((0, 0), (0, D_IN_PAD - w_in.shape[1]))).astype(BF16)
    wa = _block_diag(lru_wa).astype(BF16)
    wx = _block_diag(lru_wx).astype(BF16)
    half = CMP_LEN // 2 * HEAD_DIM

    def w1_cat(w1):
        return jnp.concatenate([w1[:half], w1[half:]], axis=1).astype(BF16)

    def pos_rows(pos):
        return jnp.pad(pos.reshape(2, half), ((0, 14), (0, 0))).astype(BF16)

    rnn_n, q, kv, gates = _inproj(x, row(pre_norm_mix), sc1, sh1, w_in_p, conv_w, row(conv_b), wa, row(lru_ba),
                                  wx, row(lru_bx), row(lru_lambda), row(norm_rnn_out))

    kv6 = kv.reshape(B, S, 6, G, HEAD_DIM)
    by_group = lambda t: t.transpose(0, 2, 1, 3)
    rows16 = lambda t: by_group(t).reshape(B, G, S // CMP_STRIDE, CMP_STRIDE * HEAD_DIM)
    tilesT = lambda t: t.reshape(B, S // TK, TK, G, HEAD_DIM).transpose(0, 3, 1, 4, 2)
    kc, vc = _compress(rows16(kv6[:, :, 0]), rows16(kv6[:, :, 1]),
                       w1_cat(cmp_w1_k), cmp_w2_k.astype(BF16), pos_rows(cmp_pos_k),
                       w1_cat(cmp_w1_v), cmp_w2_v.astype(BF16), pos_rows(cmp_pos_v))
    qT = q.transpose(0, 2, 1)
    gT = gates[:, :, :N_GATE].reshape(B, S, 3, G, REP).transpose(0, 3, 2, 4, 1)
    attT = _attention(qT, by_group(kv6[:, :, 2]), tilesT(kv6[:, :, 3]), by_group(kv6[:, :, 4]),
                      tilesT(kv6[:, :, 5]), kc, vc.transpose(0, 1, 3, 2), gT)
    att = attT.transpose(0, 2, 1)

    return _outmlp(x, rnn_n, att, row(norm_att_out), w_out.astype(BF16), row(post_norm_mix), g1,
                   row(pre_norm_mlp), sc2, sh2, w_ff1.astype(BF16), w_ff2.astype(BF16), row(post_norm_mlp), g2)


def kernel(x, c, ada_w, ada_b, pre_norm_mix, w_in, conv_w, conv_b, lru_wa, lru_ba, lru_wx, lru_bx, lru_lambda,
           cmp_pos_k, cmp_w1_k, cmp_w2_k, cmp_pos_v, cmp_w1_v, cmp_w2_v, norm_rnn_out, norm_att_out, w_out,
           post_norm_mix, pre_norm_mlp, w_ff1, w_ff2, post_norm_mlp):
    for l in range(ada_w.shape[0]):
        x = _layer(x, c, ada_w[l], ada_b[l], pre_norm_mix[l], w_in[l], conv_w[l], conv_b[l], lru_wa[l], lru_ba[l],
                   lru_wx[l], lru_bx[l], lru_lambda[l], cmp_pos_k[l], cmp_w1_k[l], cmp_w2_k[l], cmp_pos_v[l],
                   cmp_w1_v[l], cmp_w2_v[l], norm_rnn_out[l], norm_att_out[l], w_out[l], post_norm_mix[l],
                   pre_norm_mlp[l], w_ff1[l], w_ff2[l], post_norm_mlp[l])
    return x
```

```python
import functools

import jax
import jax.numpy as jnp
from jax import lax
from jax.experimental import pallas as pl
from jax.experimental.pallas import tpu as pltpu

F32 = jnp.float32
BF16 = jnp.bfloat16
I32 = jnp.int32

D_MODEL = 1024
D_RNN = 512
RNN_BLOCKS = 8
CONV_WIDTH = 4
LRU_C = 8.0
N_HEADS = 8
HEAD_DIM = 64
N_KV = 2
REP = N_HEADS // N_KV
D_ATT = N_HEADS * HEAD_DIM
CMP_LEN = 32
CMP_STRIDE = 16
CMP_HIDDEN = 256
SLC_BLOCK = 64
SLC_TOPK = 16
WINDOW = 512
D_FF = 4 * D_MODEL
EPS = 1e-6
NEG = -1e30
FORCED = 1e4

KV_COLS = 6 * N_KV * HEAD_DIM
N_GATE = 3 * N_HEADS
GATE_PAD = 128
OFF_Q = 2 * D_RNN
OFF_KV = OFF_Q + D_ATT
OFF_GATE = OFF_KV + KV_COLS
D_IN_PAD = OFF_GATE + GATE_PAD

TM_IN = 512
TM_OUT = 512
TQ = 256
TK = 256
VROWS = 80
VMEM_LIMIT = 56 * 1024 * 1024


def _gelu_tanh(x):
    return 0.5 * x * (1.0 + jnp.tanh(0.7978845608028654 * (x + 0.044715 * (x * x * x))))


def _rms(x):
    return x * lax.rsqrt(jnp.mean(x * x, axis=-1, keepdims=True) + EPS)


def _ada_kernel(c_ref, w_ref, b_ref, o_ref):
    c = c_ref[...]
    a = c * jax.nn.sigmoid(c)
    o_ref[...] = jnp.dot(a, w_ref[...], preferred_element_type=F32,
                         precision=lax.Precision.HIGHEST) + b_ref[...]


def _ada(c, w, b):
    B, D = c.shape
    N = w.shape[1]
    tn = 1024
    return pl.pallas_call(
        _ada_kernel,
        grid=(N // tn,),
        in_specs=[pl.BlockSpec((B, D), lambda j: (0, 0)),
                  pl.BlockSpec((D, tn), lambda j: (0, j)),
                  pl.BlockSpec((1, tn), lambda j: (0, j))],
        out_specs=pl.BlockSpec((B, tn), lambda j: (0, j)),
        out_shape=jax.ShapeDtypeStruct((B, N), F32),
        name="ada",
    )(c, w, b.reshape(1, N))


def _inproj_kernel(x_ref, gain_ref, sc_ref, sh_ref, w_ref, cw_ref, cb_ref, wa_ref, ba_ref, wx_ref, bx_ref,
                   lam_ref, grnn_ref,
                   rnn_ref, q_ref, kv_ref, gate_ref,
                   xbuf, hcar, a_s, u_s, h_s):
    tm = x_ref.shape[1]

    @pl.when(pl.program_id(1) == 0)
    def _():
        xbuf[0:8, :] = jnp.zeros((8, D_RNN), F32)
        hcar[...] = jnp.zeros((1, D_RNN), F32)

    x = x_ref[0]
    h = _rms(x) * (gain_ref[...] * (1.0 + sc_ref[0])) + sh_ref[0]
    hb = h.astype(BF16)

    q_ref[0] = (jnp.dot(hb, w_ref[:, OFF_Q:OFF_KV], preferred_element_type=F32)
                * (HEAD_DIM ** -0.5)).astype(BF16)
    kv_ref[0] = jnp.dot(hb, w_ref[:, OFF_KV:OFF_GATE], preferred_element_type=F32).astype(BF16)
    gate_ref[0] = jax.nn.sigmoid(jnp.dot(hb, w_ref[:, OFF_GATE:D_IN_PAD], preferred_element_type=F32))

    xr = jnp.dot(hb, w_ref[:, D_RNN:OFF_Q], preferred_element_type=F32)
    xbuf[8:8 + tm, :] = xr
    y = (cw_ref[3:4, :] * xr + cw_ref[2:3, :] * xbuf[7:7 + tm, :]
         + cw_ref[1:2, :] * xbuf[6:6 + tm, :] + cw_ref[0:1, :] * xbuf[5:5 + tm, :]) + cb_ref[...]
    xbuf[0:8, :] = xbuf[tm:tm + 8, :]

    yb = y.astype(BF16)
    r = jax.nn.sigmoid(jnp.dot(yb, wa_ref[...], preferred_element_type=F32) + ba_ref[...])
    i = jax.nn.sigmoid(jnp.dot(yb, wx_ref[...], preferred_element_type=F32) + bx_ref[...])
    nl = -lam_ref[...]
    softplus = jnp.maximum(nl, 0.0) + jnp.log(1.0 + jnp.exp(-jnp.abs(nl)))
    a = jnp.exp((-LRU_C) * r * softplus)
    a_s[...] = a
    u_s[...] = jnp.sqrt(1.0 - a * a) * (i * y)

    rows = lax.broadcasted_iota(I32, (8, D_RNN), 0)

    def group(gi, hprev):
        r0 = pl.multiple_of(gi * 8, 8)
        ag = a_s[pl.ds(r0, 8), :]
        ug = u_s[pl.ds(r0, 8), :]
        for k in (1, 2, 4):
            a_sh = jnp.where(rows >= k, pltpu.roll(ag, k, 0), 1.0)
            u_sh = jnp.where(rows >= k, pltpu.roll(ug, k, 0), 0.0)
            ug = ag * u_sh + ug
            ag = ag * a_sh
        hg = ag * hprev + ug
        h_s[pl.ds(r0, 8), :] = hg
        return hg[7:8, :]

    hcar[...] = lax.fori_loop(0, tm // 8, group, hcar[...], unroll=4)

    g = jnp.dot(hb, w_ref[:, 0:D_RNN], preferred_element_type=F32)
    rnn = _gelu_tanh(g) * h_s[...]
    rnn_ref[0] = (_rms(rnn) * grnn_ref[...]).astype(BF16)


def _inproj(x, gain, sc, sh, w_in, conv_w, conv_b, wa, ba, wx, bx, lam, grnn):
    B, S, D = x.shape
    tm = min(TM_IN, S)
    row = lambda n: pl.BlockSpec((1, n), lambda b, s: (0, 0))
    per_b = lambda n: pl.BlockSpec((1, 1, n), lambda b, s: (b, 0, 0))
    full = lambda a: pl.BlockSpec(a.shape, lambda b, s: (0,) * a.ndim)
    tok = lambda n: pl.BlockSpec((1, tm, n), lambda b, s: (b, s, 0))
    return pl.pallas_call(
        _inproj_kernel,
        grid=(B, S // tm),
        in_specs=[tok(D), row(D), per_b(D), per_b(D), full(w_in), full(conv_w), row(D_RNN),
                  full(wa), row(D_RNN), full(wx), row(D_RNN), row(D_RNN), row(D_RNN)],
        out_specs=[tok(D_RNN), tok(D_ATT), tok(KV_COLS), tok(GATE_PAD)],
        out_shape=[jax.ShapeDtypeStruct((B, S, D_RNN), BF16),
                   jax.ShapeDtypeStruct((B, S, D_ATT), BF16),
                   jax.ShapeDtypeStruct((B, S, KV_COLS), BF16),
                   jax.ShapeDtypeStruct((B, S, GATE_PAD), F32)],
        scratch_shapes=[pltpu.VMEM((tm + 8, D_RNN), F32), pltpu.VMEM((1, D_RNN), F32),
                        pltpu.VMEM((tm, D_RNN), F32), pltpu.VMEM((tm, D_RNN), F32),
                        pltpu.VMEM((tm, D_RNN), F32)],
        compiler_params=pltpu.CompilerParams(dimension_semantics=("arbitrary", "arbitrary"),
                                             vmem_limit_bytes=VMEM_LIMIT),
        name="inproj",
    )(x, gain, sc, sh, w_in, conv_w, conv_b, wa, ba, wx, bx, lam, grnn)


def _compress_kernel(rk_ref, rv_ref, w1k_ref, w2k_ref, pk_ref, w1v_ref, w2v_ref, pv_ref, kc_ref, vc_ref):
    def one(r_ref, w1_ref, w2_ref, pos_ref, o_ref):
        n = r_ref.shape[2]
        p = jnp.dot(r_ref[0, 0], w1_ref[...], preferred_element_type=F32)
        posb = jnp.dot(pos_ref[...], w1_ref[...], preferred_element_type=F32)
        bias = posb[0:1, 0:CMP_HIDDEN] + posb[1:2, CMP_HIDDEN:]
        pre = p[:, 0:CMP_HIDDEN] + pltpu.roll(p[:, CMP_HIDDEN:], n - 1, 0) + bias
        hid = _gelu_tanh(pre).astype(BF16)
        o_ref[0, 0] = jnp.dot(hid, w2_ref[...], preferred_element_type=F32).astype(BF16)

    one(rk_ref, w1k_ref, w2k_ref, pk_ref, kc_ref)
    one(rv_ref, w1v_ref, w2v_ref, pv_ref, vc_ref)


def _compress(rk, rv, w1k, w2k, posk, w1v, w2v, posv):
    B, G, n, width = rk.shape
    blk = pl.BlockSpec((1, 1, n, width), lambda b, g: (b, g, 0, 0))
    full = lambda a: pl.BlockSpec(a.shape, lambda b, g: (0,) * a.ndim)
    out = pl.BlockSpec((1, 1, n, HEAD_DIM), lambda b, g: (b, g, 0, 0))
    return pl.pallas_call(
        _compress_kernel,
        grid=(B, G),
        in_specs=[blk, blk, full(w1k), full(w2k), full(posk), full(w1v), full(w2v), full(posv)],
        out_specs=[out, out],
        out_shape=[jax.ShapeDtypeStruct((B, G, n, HEAD_DIM), BF16)] * 2,
        compiler_params=pltpu.CompilerParams(vmem_limit_bytes=VMEM_LIMIT),
        name="compress",
    )(rk, rv, w1k, w2k, posk, w1v, w2v, posv)


def _attn_kernel(qT_ref, ks_ref, vs_ref, kw_ref, vw_ref, kc_ref, vcT_ref, g_ref,
                 o_ref,
                 ksa, kwa, vsa, vwa, qa, qw, m_s, al_s, acc_s, m_w, acc_w, sb0, sw0, sw1, ps0, pw0, pw1, oacc, val_s,
                 selbias_s, cbias, wbias, act_s, *, n_sel):
    S = ks_ref.shape[2]
    nkt = S // TK
    ncmp = kc_ref.shape[2]
    b = pl.program_id(0)
    g = pl.program_id(1)
    qi = pl.program_id(2)
    q0 = qi * TQ
    W4 = REP * TQ

    @pl.when((b == 0) & (g == 0) & (qi == 0))
    def _():
        for kt in range(nkt):
            key = kt * TK + lax.broadcasted_iota(I32, (TK, 256), 0)
            col = lax.broadcasted_iota(I32, (TK, 256), 1)
            blk = (key // SLC_BLOCK).astype(F32)
            off = (key % SLC_BLOCK).astype(F32)
            onehot = (col - HEAD_DIM == key // SLC_BLOCK).astype(F32)
            aug = jnp.where(col == 2 * HEAD_DIM, blk, jnp.where(col == 2 * HEAD_DIM + 1, off, onehot))
            ksa[kt * TK:(kt + 1) * TK, :] = aug.astype(BF16)
            augw = jnp.where(col == HEAD_DIM, blk, jnp.where(col == HEAD_DIM + 1, off, 0.0))
            kwa[kt * TK:(kt + 1) * TK, :] = augw[:, 0:128].astype(BF16)
            ones_row = (lax.broadcasted_iota(I32, (VROWS - HEAD_DIM, TK), 0) == 0).astype(BF16)
            vsa[kt, HEAD_DIM:VROWS, :] = ones_row
            vwa[kt, HEAD_DIM:VROWS, :] = ones_row
        ko = lax.broadcasted_iota(I32, (TK, TQ), 0)
        to = lax.broadcasted_iota(I32, (TK, TQ), 1)
        cbias[...] = jnp.where(ko <= to, 0.0, NEG)
        wbias[...] = jnp.where(ko > to, 0.0, NEG)
        qa[...] = jnp.zeros(qa.shape, BF16)
        qw[...] = jnp.zeros(qw.shape, BF16)

    lane = lax.broadcasted_iota(I32, (1, W4), 1)
    head = g * REP + lane // TQ
    slope = lax.bitcast_convert_type((126 - head) << 23, F32)

    @pl.when(qi == 0)
    def _():
        for kt in range(nkt):
            ksa[kt * TK:(kt + 1) * TK, 0:HEAD_DIM] = ks_ref[0, 0, kt * TK:(kt + 1) * TK, :]
            kwa[kt * TK:(kt + 1) * TK, 0:HEAD_DIM] = kw_ref[0, 0, kt * TK:(kt + 1) * TK, :]
            vsa[kt, 0:HEAD_DIM, :] = vs_ref[0, 0, kt]
            vwa[kt, 0:HEAD_DIM, :] = vw_ref[0, 0, kt]
        r16 = lax.broadcasted_iota(I32, (16, W4), 0)
        alibi = jnp.where(r16 == 0, slope * SLC_BLOCK, jnp.where(r16 == 1, slope, 0.0)).astype(BF16)
        qa[2 * HEAD_DIM:2 * HEAD_DIM + 16, :] = alibi
        qw[HEAD_DIM:HEAD_DIM + 16, :] = alibi

    q = qT_ref[0]
    for r in range(REP):
        qr = q[r * HEAD_DIM:(r + 1) * HEAD_DIM, :]
        qa[0:HEAD_DIM, r * TQ:(r + 1) * TQ] = qr
        qw[0:HEAD_DIM, r * TQ:(r + 1) * TQ] = qr

    gates = g_ref[0, 0]

    def stage_scores(k_tile, q_op, bias, s_ref, m_ref):
        s = jnp.dot(k_tile, q_op, preferred_element_type=F32)
        if bias is not None:
            s = s + bias
        s_ref[...] = s
        m_old = m_ref[...]
        m_new = jnp.maximum(m_old, jnp.max(s, axis=0, keepdims=True))
        m_ref[...] = m_new
        return m_new, jnp.exp(m_old - m_new)

    def stage_probs(s_ref, m_row, p_ref):
        p_ref[...] = jnp.exp(s_ref[...] - m_row).astype(BF16)

    def stage_values(v_tile, p_ref, al_row, acc_ref):
        acc_ref[...] = acc_ref[...] * al_row + jnp.dot(v_tile, p_ref[...], preferred_element_type=F32)

    def reset(m_ref, acc_ref):
        m_ref[...] = jnp.full(m_ref.shape, NEG, F32)
        acc_ref[...] = jnp.zeros(acc_ref.shape, F32)

    def finish(acc_ref, gate_rows):
        acc = acc_ref[...]
        o = acc[0:HEAD_DIM, :] / acc[HEAD_DIM:HEAD_DIM + 1, :]
        return [gate_rows[r:r + 1, :] * o[:, r * TQ:(r + 1) * TQ] for r in range(REP)]

    tile4 = lambda bias: jnp.concatenate([bias] * REP, axis=1)
    kd = pl.multiple_of(qi * TK, TK)
    causal4 = tile4(cbias[...])

    t1_ = jnp.maximum(qi - 1, 0)
    t2_ = jnp.maximum(qi - 2, 0)
    k1 = pl.multiple_of(t1_ * TK, TK)
    k2 = pl.multiple_of(t2_ * TK, TK)
    reset(m_w, acc_w)
    m0, a0 = stage_scores(kwa[pl.ds(kd, TK), :], qw[...], causal4, sw0, m_w)
    m1, a1 = stage_scores(kwa[pl.ds(k1, TK), :], qw[...], jnp.where(qi >= 1, 0.0, NEG), sw1, m_w)
    sc = jnp.dot(kc_ref[0, 0], qa[0:HEAD_DIM, :], preferred_element_type=F32)
    stage_probs(sw0, m0, pw0)
    stage_values(vwa[qi], pw0, a0, acc_w)
    m2, a2 = stage_scores(kwa[pl.ds(k2, TK), :], qw[...], tile4(jnp.where(qi >= 2, wbias[...], NEG)), sw0, m_w)
    stage_probs(sw1, m1, pw1)
    stage_values(vwa[t1_], pw1, a1, acc_w)
    stage_probs(sw0, m2, pw0)
    stage_values(vwa[t2_], pw0, a2, acc_w)
    o_win = finish(acc_w, gates[2])

    c_i = lax.broadcasted_iota(I32, (ncmp, W4), 0)
    t_i = q0 + (lax.broadcasted_iota(I32, (ncmp, W4), 1) % TQ)
    d_c = t_i - (CMP_STRIDE * c_i + (CMP_LEN - 1))
    sc = jnp.where(d_c >= 0, sc - slope * d_c.astype(F32), NEG)
    e = jnp.exp(sc - jnp.max(sc, axis=0, keepdims=True))
    tq = q0 + lane % TQ
    p = e * ((1.0 / jnp.sum(e, axis=0, keepdims=True)) * (tq >= CMP_LEN - 1).astype(F32))
    ocT = jnp.dot(vcT_ref[0, 0], p.astype(BF16), preferred_element_type=F32)
    for r in range(REP):
        oacc[:, r * TQ:(r + 1) * TQ] = gates[0, r:r + 1, :] * ocT[:, r * TQ:(r + 1) * TQ] + o_win[r]

    psum = p[:, 0:TQ]
    for r in range(1, REP):
        psum = psum + p[:, r * TQ:(r + 1) * TQ]
    nblk = S // SLC_BLOCK
    jj = lax.broadcasted_iota(I32, (nblk, ncmp), 0)
    cc = lax.broadcasted_iota(I32, (nblk, ncmp), 1)
    ovT = ((CMP_STRIDE * cc < SLC_BLOCK * jj + SLC_BLOCK)
           & (CMP_STRIDE * cc + CMP_LEN > SLC_BLOCK * jj)).astype(F32)
    impT = jnp.dot(ovT, psum, preferred_element_type=F32, precision=lax.Precision.HIGHEST)

    j_i = lax.broadcasted_iota(I32, (nblk, TQ), 0)
    t1 = q0 + lax.broadcasted_iota(I32, (nblk, TQ), 1)
    cur = t1 // SLC_BLOCK
    forced = (j_i == 0) | (j_i == cur) | (j_i == cur - 1)
    visible = SLC_BLOCK * j_i <= t1
    val_s[...] = jnp.where(forced, FORCED, jnp.where(visible, impT, NEG))
    ngrp = nblk // 8
    vals = [val_s[8 * jb:8 * jb + 8, :] for jb in range(ngrp)]
    ranks = [jnp.zeros((8, TQ), F32) for _ in range(ngrp)]
    j8 = lax.broadcasted_iota(I32, (8, TQ), 0)
    for i in range(nblk):
        row = jnp.broadcast_to(val_s[i:i + 1, :], (8, TQ))
        for jb in range(ngrp):
            if 8 * jb > i:
                hit = row >= vals[jb]
            elif 8 * jb + 7 < i:
                hit = row > vals[jb]
            else:
                hit = jnp.where(j8 > i - 8 * jb, jnp.where(row >= vals[jb], 1.0, 0.0),
                                jnp.where(row > vals[jb], 1.0, 0.0)) > 0.5
            ranks[jb] = ranks[jb] + jnp.where(hit, 1.0, 0.0)
    blocks_per_tile = TK // SLC_BLOCK
    n_act = jnp.int32(0)
    for jb in range(ngrp):
        chosen = ranks[jb] < n_sel
        selb = jnp.where(chosen, 0.0, NEG)
        for r in range(REP):
            selbias_s[8 * jb:8 * jb + 8, r * TQ:(r + 1) * TQ] = selb
        any_q = jnp.max(jnp.where(chosen, 1.0, 0.0), axis=1, keepdims=True)
        for h in range(8 // blocks_per_tile):
            kt = (8 * jb) // blocks_per_tile + h
            hit = jnp.max(any_q[h * blocks_per_tile:(h + 1) * blocks_per_tile, :]) > 0.5
            act_s[n_act] = kt
            n_act = n_act + jnp.where(hit & (kt < qi), 1, 0)
    qa[HEAD_DIM:HEAD_DIM + nblk, :] = selbias_s[...].astype(BF16)

    reset(m_s, acc_s)
    m_d, a_d = stage_scores(ksa[pl.ds(kd, TK), :], qa[...], causal4, sb0, m_s)
    al_s[...] = a_d

    def sel_body(k, carry):
        m_row = m_s[...]
        al_row = al_s[...]
        stage_probs(sb0, m_row, ps0)
        kt = act_s[k]
        k0 = pl.multiple_of(kt * TK, TK)
        _, al_next = stage_scores(ksa[pl.ds(k0, TK), :], qa[...], None, sb0, m_s)
        al_s[...] = al_next
        stage_values(vsa[jnp.where(k == 0, qi, act_s[jnp.maximum(k - 1, 0)])], ps0, al_row, acc_s)
        return carry

    lax.fori_loop(0, n_act, sel_body, 0)
    stage_probs(sb0, m_s[...], ps0)
    stage_values(vsa[jnp.where(n_act == 0, qi, act_s[jnp.maximum(n_act - 1, 0)])], ps0, al_s[...], acc_s)
    for r, o_r in enumerate(finish(acc_s, gates[1])):
        o_ref[0, r * HEAD_DIM:(r + 1) * HEAD_DIM, :] = oacc[:, r * TQ:(r + 1) * TQ] + o_r


def _attention(qT, ks, vs, kw, vw, kc, vcT, gT):
    B, _, S = qT.shape
    G = N_KV
    nkt = S // TK
    ncmp = kc.shape[2]
    n_sel = min(SLC_TOPK, S // SLC_BLOCK)
    W4 = REP * TQ
    res4 = lambda a: pl.BlockSpec((1, 1) + a.shape[2:], lambda b, g, i: (b, g) + (0,) * (a.ndim - 2))
    return pl.pallas_call(
        functools.partial(_attn_kernel, n_sel=n_sel),
        grid=(B, G, S // TQ),
        in_specs=[pl.BlockSpec((1, REP * HEAD_DIM, TQ), lambda b, g, i: (b, g, i)),
                  res4(ks), res4(vs), res4(kw), res4(vw), res4(kc), res4(vcT),
                  pl.BlockSpec((1, 1, 3, REP, TQ), lambda b, g, i: (b, g, 0, 0, i))],
        out_specs=pl.BlockSpec((1, REP * HEAD_DIM, TQ), lambda b, g, i: (b, g, i)),
        out_shape=jax.ShapeDtypeStruct((B, D_ATT, S), F32),
        scratch_shapes=[pltpu.VMEM((S, 256), BF16), pltpu.VMEM((S, 128), BF16),
                        pltpu.VMEM((nkt, VROWS, TK), BF16), pltpu.VMEM((nkt, VROWS, TK), BF16),
                        pltpu.VMEM((256, W4), BF16), pltpu.VMEM((128, W4), BF16),
                        pltpu.VMEM((1, W4), F32), pltpu.VMEM((1, W4), F32), pltpu.VMEM((VROWS, W4), F32),
                        pltpu.VMEM((1, W4), F32), pltpu.VMEM((VROWS, W4), F32),
                        pltpu.VMEM((TK, W4), F32), pltpu.VMEM((TK, W4), F32), pltpu.VMEM((TK, W4), F32),
                        pltpu.VMEM((TK, W4), BF16), pltpu.VMEM((TK, W4), BF16), pltpu.VMEM((TK, W4), BF16),
                        pltpu.VMEM((HEAD_DIM, W4), F32), pltpu.VMEM((S // SLC_BLOCK, TQ), F32),
                        pltpu.VMEM((S // SLC_BLOCK, W4), F32),
                        pltpu.VMEM((TK, TQ), F32), pltpu.VMEM((TK, TQ), F32),
                        pltpu.SMEM((nkt,), I32)],
        compiler_params=pltpu.CompilerParams(dimension_semantics=("arbitrary", "arbitrary", "arbitrary"),
                                             vmem_limit_bytes=VMEM_LIMIT),
        name="attn",
    )(qT, ks, vs, kw, vw, kc, vcT, gT)


def _outmlp_kernel(x_ref, rnn_ref, att_ref, gatt_ref, wo_ref, gpost_ref, g1_ref, gpre_ref, sc2_ref, sh2_ref,
                   w1_ref, w2_ref, gpost2_ref, g2_ref, o_ref):
    att_n = (_rms(att_ref[0]) * gatt_ref[...]).astype(BF16)
    y = (jnp.dot(rnn_ref[0], wo_ref[0:D_RNN, :], preferred_element_type=F32)
         + jnp.dot(att_n, wo_ref[D_RNN:, :], preferred_element_type=F32))
    x1 = x_ref[0] + (1.0 + g1_ref[0]) * (_rms(y) * gpost_ref[...])
    h2 = (_rms(x1) * (gpre_ref[...] * (1.0 + sc2_ref[0])) + sh2_ref[0]).astype(BF16)
    fc = 1024
    ff = jnp.zeros(x1.shape, F32)
    for c in range(D_FF // fc):
        hid = jnp.maximum(jnp.dot(h2, w1_ref[:, c * fc:(c + 1) * fc], preferred_element_type=F32), 0.0)
        ff = ff + jnp.dot((hid * hid).astype(BF16), w2_ref[c * fc:(c + 1) * fc, :], preferred_element_type=F32)
    o_ref[0] = x1 + (1.0 + g2_ref[0]) * (_rms(ff) * gpost2_ref[...])


def _outmlp(x, rnn_n, att, gatt, wo, gpost, g1, gpre, sc2, sh2, w1, w2, gpost2, g2):
    B, S, D = x.shape
    tm = min(TM_OUT, S)
    row = lambda n: pl.BlockSpec((1, n), lambda b, s: (0, 0))
    per_b = lambda n: pl.BlockSpec((1, 1, n), lambda b, s: (b, 0, 0))
    const = lambda a: pl.BlockSpec(a.shape, lambda b, s: (0,) * a.ndim, pipeline_mode=pl.Buffered(1))
    tok = lambda n: pl.BlockSpec((1, tm, n), lambda b, s: (b, s, 0))
    return pl.pallas_call(
        _outmlp_kernel,
        grid=(B, S // tm),
        in_specs=[tok(D), tok(D_RNN), tok(D_ATT), row(D_ATT), const(wo), row(D), per_b(D), row(D),
                  per_b(D), per_b(D), const(w1), const(w2), row(D), per_b(D)],
        out_specs=tok(D),
        out_shape=jax.ShapeDtypeStruct((B, S, D), F32),
        compiler_params=pltpu.CompilerParams(dimension_semantics=("arbitrary", "arbitrary"),
                                             vmem_limit_bytes=VMEM_LIMIT),
        name="outmlp",
    )(x, rnn_n, att, gatt, wo, gpost, g1, gpre, sc2, sh2, w1, w2, gpost2, g2)


def _block_diag(w):
    n, k, _ = w.shape
    return jnp.einsum('nij,nm->nimj', w, jnp.eye(n, dtype=w.dtype)).reshape(n * k, n * k)


def _layer(x, c, ada_w, ada_b, pre_norm_mix, w_in, conv_w, conv_b, lru_wa, lru_ba, lru_wx, lru_bx, lru_lambda,
           cmp_pos_k, cmp_w1_k, cmp_w2_k, cmp_pos_v, cmp_w1_v, cmp_w2_v, norm_rnn_out, norm_att_out, w_out,
           post_norm_mix, pre_norm_mlp, w_ff1, w_ff2, post_norm_mlp):
    B, S, D = x.shape
    G = N_KV
    row = lambda v: v.reshape(1, -1)

    mod = _ada(c, ada_w, ada_b)
    sh1, sc1, g1, sh2, sc2, g2 = [m.reshape(B, 1, D) for m in jnp.split(mod, 6, axis=-1)]

    w_in_p = jnp.pad(w_in, ((0, 0), (0, D_IN_PAD - w_in.shape[1]))).astype(BF16)
    wa = _block_diag(lru_wa).astype(BF16)
    wx = _block_diag(lru_wx).astype(BF16)
    half = CMP_LEN // 2 * HEAD_DIM

    def w1_cat(w1):
        return jnp.concatenate([w1[:half], w1[half:]], axis=1).astype(BF16)

    def pos_rows(pos):
        return jnp.pad(pos.reshape(2, half), ((0, 14), (0, 0))).astype(BF16)

    rnn_n, q, kv, gates = _inproj(x, row(pre_norm_mix), sc1, sh1, w_in_p, conv_w, row(conv_b), wa, row(lru_ba),
                                  wx, row(lru_bx), row(lru_lambda), row(norm_rnn_out))

    kv6 = kv.reshape(B, S, 6, G, HEAD_DIM)
    by_group = lambda t: t.transpose(0, 2, 1, 3)
    rows16 = lambda t: by_group(t).reshape(B, G, S // CMP_STRIDE, CMP_STRIDE * HEAD_DIM)
    tilesT = lambda t: t.reshape(B, S // TK, TK, G, HEAD_DIM).transpose(0, 3, 1, 4, 2)
    kc, vc = _compress(rows16(kv6[:, :, 0]), rows16(kv6[:, :, 1]),
                       w1_cat(cmp_w1_k), cmp_w2_k.astype(BF16), pos_rows(cmp_pos_k),
                       w1_cat(cmp_w1_v), cmp_w2_v.astype(BF16), pos_rows(cmp_pos_v))
    qT = q.transpose(0, 2, 1)
    gT = gates[:, :, :N_GATE].reshape(B, S, 3, G, REP).transpose(0, 3, 2, 4, 1)
    attT = _attention(qT, by_group(kv6[:, :, 2]), tilesT(kv6[:, :, 3]), by_group(kv6[:, :, 4]),
                      tilesT(kv6[:, :, 5]), kc, vc.transpose(0, 1, 3, 2), gT)
    att = attT.transpose(0, 2, 1)

    return _outmlp(x, rnn_n, att, row(norm_att_out), w_out.astype(BF16), row(post_norm_mix), g1,
                   row(pre_norm_mlp), sc2, sh2, w_ff1.astype(BF16), w_ff2.astype(BF16), row(post_norm_mlp), g2)


def kernel(x, c, ada_w, ada_b, pre_norm_mix, w_in, conv_w, conv_b, lru_wa, lru_ba, lru_wx, lru_bx, lru_lambda,
           cmp_pos_k, cmp_w1_k, cmp_w2_k, cmp_pos_v, cmp_w1_v, cmp_w2_v, norm_rnn_out, norm_att_out, w_out,
           post_norm_mix, pre_norm_mlp, w_ff1, w_ff2, post_norm_mlp):
    for l in range(ada_w.shape[0]):
        x = _layer(x, c, ada_w[l], ada_b[l], pre_norm_mix[l], w_in[l], conv_w[l], conv_b[l], lru_wa[l], lru_ba[l],
                   lru_wx[l], lru_bx[l], lru_lambda[l], cmp_pos_k[l], cmp_w1_k[l], cmp_w2_k[l], cmp_pos_v[l],
                   cmp_w1_v[l], cmp_w2_v[l], norm_rnn_out[l], norm_att_out[l], w_out[l], post_norm_mix[l],
                   pre_norm_mlp[l], w_ff1[l], w_ff2[l], post_norm_mlp[l])
    return x
```

```python
import functools

import jax
import jax.numpy as jnp
from jax import lax
from jax.experimental import pallas as pl
from jax.experimental.pallas import tpu as pltpu

F32 = jnp.float32
BF16 = jnp.bfloat16
I32 = jnp.int32

D_MODEL = 1024
D_RNN = 512
RNN_BLOCKS = 8
CONV_WIDTH = 4
LRU_C = 8.0
N_HEADS = 8
HEAD_DIM = 64
N_KV = 2
REP = N_HEADS // N_KV
D_ATT = N_HEADS * HEAD_DIM
CMP_LEN = 32
CMP_STRIDE = 16
CMP_HIDDEN = 256
SLC_BLOCK = 64
SLC_TOPK = 16
WINDOW = 512
D_FF = 4 * D_MODEL
EPS = 1e-6
NEG = -1e30
FORCED = 1e4

KV_COLS = 6 * N_KV * HEAD_DIM
N_GATE = 3 * N_HEADS
GATE_PAD = 128
GATE_ROWS = 16
OFF_Q = 2 * D_RNN
OFF_KV = OFF_Q + D_ATT
OFF_GATE = OFF_KV + KV_COLS
D_IN_PAD = OFF_GATE + GATE_PAD

TM_IN = 512
TM_OUT = 512
TQ = 256
TK = 256
VROWS = 80
VMEM_LIMIT = 56 * 1024 * 1024


def _gelu_tanh(x):
    return 0.5 * x * (1.0 + jnp.tanh(0.7978845608028654 * (x + 0.044715 * (x * x * x))))


def _rms(x):
    return x * lax.rsqrt(jnp.mean(x * x, axis=-1, keepdims=True) + EPS)


def _ada_kernel(c_ref, w_ref, b_ref, o_ref):
    c = c_ref[...]
    a = c * jax.nn.sigmoid(c)
    o_ref[...] = jnp.dot(a, w_ref[...], preferred_element_type=F32,
                         precision=lax.Precision.HIGHEST) + b_ref[...]


def _ada(c, w, b):
    B, D = c.shape
    N = w.shape[1]
    tn = 1024
    return pl.pallas_call(
        _ada_kernel,
        grid=(N // tn,),
        in_specs=[pl.BlockSpec((B, D), lambda j: (0, 0)),
                  pl.BlockSpec((D, tn), lambda j: (0, j)),
                  pl.BlockSpec((1, tn), lambda j: (0, j))],
        out_specs=pl.BlockSpec((B, tn), lambda j: (0, j)),
        out_shape=jax.ShapeDtypeStruct((B, N), F32),
        name="ada",
    )(c, w, b.reshape(1, N))


def _inproj_kernel(x_ref, gain_ref, sc_ref, sh_ref, w_ref, cw_ref, cb_ref, wa_ref, ba_ref, wx_ref, bx_ref,
                   lam_ref, grnn_ref,
                   rnn_ref, qT_ref, ks_ref, vs_ref, kw_ref, vw_ref, kc_ref, vc_ref, g_ref,
                   xbuf, hcar, a_s, u_s, h_s):
    tm = x_ref.shape[1]
    hd = HEAD_DIM

    @pl.when(pl.program_id(1) == 0)
    def _():
        xbuf[0:8, :] = jnp.zeros((8, D_RNN), F32)
        hcar[...] = jnp.zeros((1, D_RNN), F32)

    x = x_ref[0]
    h = _rms(x) * (gain_ref[...] * (1.0 + sc_ref[0])) + sh_ref[0]
    hb = h.astype(BF16)

    xr = jnp.dot(hb, w_ref[:, D_RNN:OFF_Q], preferred_element_type=F32)
    xbuf[8:8 + tm, :] = xr
    y = (cw_ref[3:4, :] * xr + cw_ref[2:3, :] * xbuf[7:7 + tm, :]
         + cw_ref[1:2, :] * xbuf[6:6 + tm, :] + cw_ref[0:1, :] * xbuf[5:5 + tm, :]) + cb_ref[...]
    xbuf[0:8, :] = xbuf[tm:tm + 8, :]

    yb = y.astype(BF16)
    r = jax.nn.sigmoid(jnp.dot(yb, wa_ref[...], preferred_element_type=F32) + ba_ref[...])
    i = jax.nn.sigmoid(jnp.dot(yb, wx_ref[...], preferred_element_type=F32) + bx_ref[...])
    nl = -lam_ref[...]
    softplus = jnp.maximum(nl, 0.0) + jnp.log(1.0 + jnp.exp(-jnp.abs(nl)))
    a = jnp.exp((-LRU_C) * r * softplus)
    a_s[...] = a
    u_s[...] = jnp.sqrt(1.0 - a * a) * (i * y)

    qT_ref[0] = (jnp.dot(hb, w_ref[:, OFF_Q:OFF_KV], preferred_element_type=F32)
                 * (HEAD_DIM ** -0.5)).T.astype(BF16)
    kv = jnp.dot(hb, w_ref[:, OFF_KV:OFF_GATE], preferred_element_type=F32)
    width = N_KV * hd
    vsT = kv[:, 3 * width:4 * width].T
    vwT = kv[:, 5 * width:6 * width].T
    for gi in range(N_KV):
        kc_ref[0, gi] = kv[:, gi * hd:(gi + 1) * hd]
        vc_ref[0, gi] = kv[:, width + gi * hd:width + (gi + 1) * hd]
        ks_ref[0, gi] = kv[:, 2 * width + gi * hd:2 * width + (gi + 1) * hd].astype(BF16)
        kw_ref[0, gi] = kv[:, 4 * width + gi * hd:4 * width + (gi + 1) * hd].astype(BF16)
        for j in range(tm // TK):
            vs_ref[0, gi, j] = vsT[gi * hd:(gi + 1) * hd, j * TK:(j + 1) * TK].astype(BF16)
            vw_ref[0, gi, j] = vwT[gi * hd:(gi + 1) * hd, j * TK:(j + 1) * TK].astype(BF16)
    gates = jax.nn.sigmoid(jnp.dot(hb, w_ref[:, OFF_GATE:D_IN_PAD], preferred_element_type=F32)).T
    for gi in range(N_KV):
        g_ref[0, gi] = gates[gi * GATE_ROWS:(gi + 1) * GATE_ROWS, :]
    g = jnp.dot(hb, w_ref[:, 0:D_RNN], preferred_element_type=F32)

    rows = lax.broadcasted_iota(I32, (8, D_RNN), 0)
    hprev = hcar[...]
    for gi in range(tm // 8):
        ag = a_s[gi * 8:gi * 8 + 8, :]
        ug = u_s[gi * 8:gi * 8 + 8, :]
        for k in (1, 2, 4):
            a_sh = jnp.where(rows >= k, pltpu.roll(ag, k, 0), 1.0)
            u_sh = jnp.where(rows >= k, pltpu.roll(ug, k, 0), 0.0)
            ug = ag * u_sh + ug
            ag = ag * a_sh
        hg = ag * hprev + ug
        h_s[gi * 8:gi * 8 + 8, :] = hg
        hprev = hg[7:8, :]
    hcar[...] = hprev

    rnn = _gelu_tanh(g) * h_s[...]
    rnn_ref[0] = (_rms(rnn) * grnn_ref[...]).astype(BF16)


def _inproj(x, gain, sc, sh, w_in, conv_w, conv_b, wa, ba, wx, bx, lam, grnn):
    B, S, D = x.shape
    tm = min(TM_IN, S)
    row = lambda n: pl.BlockSpec((1, n), lambda b, s: (0, 0))
    per_b = lambda n: pl.BlockSpec((1, 1, n), lambda b, s: (b, 0, 0))
    full = lambda a: pl.BlockSpec(a.shape, lambda b, s: (0,) * a.ndim)
    tok = lambda n: pl.BlockSpec((1, tm, n), lambda b, s: (b, s, 0))
    G = N_KV
    grp = pl.BlockSpec((1, G, tm, HEAD_DIM), lambda b, s: (b, 0, s, 0))
    grpT = pl.BlockSpec((1, G, tm // TK, HEAD_DIM, TK), lambda b, s: (b, 0, s, 0, 0))
    return pl.pallas_call(
        _inproj_kernel,
        grid=(B, S // tm),
        in_specs=[tok(D), row(D), per_b(D), per_b(D), full(w_in), full(conv_w), row(D_RNN),
                  full(wa), row(D_RNN), full(wx), row(D_RNN), row(D_RNN), row(D_RNN)],
        out_specs=[tok(D_RNN),
                   pl.BlockSpec((1, D_ATT, tm), lambda b, s: (b, 0, s)),
                   grp, grpT, grp, grpT, grp, grp,
                   pl.BlockSpec((1, G, GATE_ROWS, tm), lambda b, s: (b, 0, 0, s))],
        out_shape=[jax.ShapeDtypeStruct((B, S, D_RNN), BF16),
                   jax.ShapeDtypeStruct((B, D_ATT, S), BF16),
                   jax.ShapeDtypeStruct((B, G, S, HEAD_DIM), BF16),
                   jax.ShapeDtypeStruct((B, G, S // TK, HEAD_DIM, TK), BF16),
                   jax.ShapeDtypeStruct((B, G, S, HEAD_DIM), BF16),
                   jax.ShapeDtypeStruct((B, G, S // TK, HEAD_DIM, TK), BF16),
                   jax.ShapeDtypeStruct((B, G, S, HEAD_DIM), F32),
                   jax.ShapeDtypeStruct((B, G, S, HEAD_DIM), F32),
                   jax.ShapeDtypeStruct((B, G, GATE_ROWS, S), F32)],
        scratch_shapes=[pltpu.VMEM((tm + 8, D_RNN), F32), pltpu.VMEM((1, D_RNN), F32),
                        pltpu.VMEM((tm, D_RNN), F32), pltpu.VMEM((tm, D_RNN), F32),
                        pltpu.VMEM((tm, D_RNN), F32)],
        compiler_params=pltpu.CompilerParams(dimension_semantics=("arbitrary", "arbitrary"),
                                             vmem_limit_bytes=VMEM_LIMIT),
        name="inproj",
    )(x, gain, sc, sh, w_in, conv_w, conv_b, wa, ba, wx, bx, lam, grnn)


def _compress_kernel(rk_ref, rv_ref, w1k_ref, w2k_ref, pk_ref, w1v_ref, w2v_ref, pv_ref, kc_ref, vc_ref):
    def one(r_ref, w1_ref, w2_ref, pos_ref, o_ref):
        n = r_ref.shape[2] // CMP_STRIDE
        p = jnp.zeros((n, 2 * CMP_HIDDEN), F32)
        for l in range(CMP_STRIDE):
            rows = r_ref[0, 0, pl.ds(l, n, stride=CMP_STRIDE), :].astype(BF16)
            p = p + jnp.dot(rows, w1_ref[l * HEAD_DIM:(l + 1) * HEAD_DIM, :], preferred_element_type=F32)
        posb = jnp.dot(pos_ref[...], w1_ref[...], preferred_element_type=F32)
        bias = posb[0:1, 0:CMP_HIDDEN] + posb[1:2, CMP_HIDDEN:]
        pre = p[:, 0:CMP_HIDDEN] + pltpu.roll(p[:, CMP_HIDDEN:], n - 1, 0) + bias
        hid = _gelu_tanh(pre).astype(BF16)
        o_ref[0, 0] = jnp.dot(hid, w2_ref[...], preferred_element_type=F32).astype(BF16)

    one(rk_ref, w1k_ref, w2k_ref, pk_ref, kc_ref)
    one(rv_ref, w1v_ref, w2v_ref, pv_ref, vc_ref)


def _compress(rk, rv, w1k, w2k, posk, w1v, w2v, posv):
    B, G, S, width = rk.shape
    n = S // CMP_STRIDE
    blk = pl.BlockSpec((1, 1, S, width), lambda b, g: (b, g, 0, 0))
    full = lambda a: pl.BlockSpec(a.shape, lambda b, g: (0,) * a.ndim)
    out = pl.BlockSpec((1, 1, n, HEAD_DIM), lambda b, g: (b, g, 0, 0))
    return pl.pallas_call(
        _compress_kernel,
        grid=(B, G),
        in_specs=[blk, blk, full(w1k), full(w2k), full(posk), full(w1v), full(w2v), full(posv)],
        out_specs=[out, out],
        out_shape=[jax.ShapeDtypeStruct((B, G, n, HEAD_DIM), BF16)] * 2,
        compiler_params=pltpu.CompilerParams(vmem_limit_bytes=VMEM_LIMIT),
        name="compress",
    )(rk, rv, w1k, w2k, posk, w1v, w2v, posv)


def _attn_kernel(qT_ref, ks_ref, vs_ref, kw_ref, vw_ref, kc_ref, vcT_ref, g_ref,
                 o_ref,
                 ksa, kwa, vsa, vwa, qa, qw, m_s, al_s, acc_s, m_w, acc_w, sb0, sw0, sw1, ps0, pw0, pw1, oacc, val_s,
                 selbias_s, cbias, wbias, oT_s, act_s, *, n_sel):
    S = ks_ref.shape[2]
    nkt = S // TK
    ncmp = kc_ref.shape[2]
    b = pl.program_id(0)
    g = pl.program_id(1)
    qi = pl.program_id(2)
    q0 = qi * TQ
    W4 = REP * TQ

    @pl.when((b == 0) & (g == 0) & (qi == 0))
    def _():
        for kt in range(nkt):
            key = kt * TK + lax.broadcasted_iota(I32, (TK, 256), 0)
            col = lax.broadcasted_iota(I32, (TK, 256), 1)
            blk = (key // SLC_BLOCK).astype(F32)
            off = (key % SLC_BLOCK).astype(F32)
            onehot = (col - HEAD_DIM == key // SLC_BLOCK).astype(F32)
            aug = jnp.where(col == 2 * HEAD_DIM, blk, jnp.where(col == 2 * HEAD_DIM + 1, off, onehot))
            ksa[kt * TK:(kt + 1) * TK, :] = aug.astype(BF16)
            augw = jnp.where(col == HEAD_DIM, blk, jnp.where(col == HEAD_DIM + 1, off, 0.0))
            kwa[kt * TK:(kt + 1) * TK, :] = augw[:, 0:128].astype(BF16)
            ones_row = (lax.broadcasted_iota(I32, (VROWS - HEAD_DIM, TK), 0) == 0).astype(BF16)
            vsa[kt, HEAD_DIM:VROWS, :] = ones_row
            vwa[kt, HEAD_DIM:VROWS, :] = ones_row
        ko = lax.broadcasted_iota(I32, (TK, TQ), 0)
        to = lax.broadcasted_iota(I32, (TK, TQ), 1)
        cbias[...] = jnp.where(ko <= to, 0.0, NEG)
        wbias[...] = jnp.where(ko > to, 0.0, NEG)
        qa[...] = jnp.zeros(qa.shape, BF16)
        qw[...] = jnp.zeros(qw.shape, BF16)

    lane = lax.broadcasted_iota(I32, (1, W4), 1)
    head = g * REP + lane // TQ
    slope = lax.bitcast_convert_type((126 - head) << 23, F32)

    @pl.when(qi == 0)
    def _():
        for kt in range(nkt):
            ksa[kt * TK:(kt + 1) * TK, 0:HEAD_DIM] = ks_ref[0, 0, kt * TK:(kt + 1) * TK, :]
            kwa[kt * TK:(kt + 1) * TK, 0:HEAD_DIM] = kw_ref[0, 0, kt * TK:(kt + 1) * TK, :]
            vsa[kt, 0:HEAD_DIM, :] = vs_ref[0, 0, kt]
            vwa[kt, 0:HEAD_DIM, :] = vw_ref[0, 0, kt]
        r16 = lax.broadcasted_iota(I32, (16, W4), 0)
        alibi = jnp.where(r16 == 0, slope * SLC_BLOCK, jnp.where(r16 == 1, slope, 0.0)).astype(BF16)
        qa[2 * HEAD_DIM:2 * HEAD_DIM + 16, :] = alibi
        qw[HEAD_DIM:HEAD_DIM + 16, :] = alibi

    q = qT_ref[0]
    for r in range(REP):
        qr = q[r * HEAD_DIM:(r + 1) * HEAD_DIM, :]
        qa[0:HEAD_DIM, r * TQ:(r + 1) * TQ] = qr
        qw[0:HEAD_DIM, r * TQ:(r + 1) * TQ] = qr

    gall = g_ref[0, 0]
    gates = [gall[br * REP:(br + 1) * REP, :] for br in range(3)]

    def stage_scores(k_tile, q_op, bias, s_ref, m_ref):
        s = jnp.dot(k_tile, q_op, preferred_element_type=F32)
        if bias is not None:
            s = s + bias
        s_ref[...] = s
        m_old = m_ref[...]
        m_new = jnp.maximum(m_old, jnp.max(s, axis=0, keepdims=True))
        m_ref[...] = m_new
        return m_new, jnp.exp(m_old - m_new)

    def stage_probs(s_ref, m_row, p_ref):
        p_ref[...] = jnp.exp(s_ref[...] - m_row).astype(BF16)

    def stage_values(v_tile, p_ref, al_row, acc_ref):
        acc_ref[...] = acc_ref[...] * al_row + jnp.dot(v_tile, p_ref[...], preferred_element_type=F32)

    def reset(m_ref, acc_ref):
        m_ref[...] = jnp.full(m_ref.shape, NEG, F32)
        acc_ref[...] = jnp.zeros(acc_ref.shape, F32)

    def finish(acc_ref, gate_rows):
        acc = acc_ref[...]
        o = acc[0:HEAD_DIM, :] / acc[HEAD_DIM:HEAD_DIM + 1, :]
        return [gate_rows[r:r + 1, :] * o[:, r * TQ:(r + 1) * TQ] for r in range(REP)]

    tile4 = lambda bias: jnp.concatenate([bias] * REP, axis=1)
    kd = pl.multiple_of(qi * TK, TK)
    causal4 = tile4(cbias[...])

    t1_ = jnp.maximum(qi - 1, 0)
    t2_ = jnp.maximum(qi - 2, 0)
    k1 = pl.multiple_of(t1_ * TK, TK)
    k2 = pl.multiple_of(t2_ * TK, TK)
    reset(m_w, acc_w)
    m0, a0 = stage_scores(kwa[pl.ds(kd, TK), :], qw[...], causal4, sw0, m_w)
    m1, a1 = stage_scores(kwa[pl.ds(k1, TK), :], qw[...], jnp.where(qi >= 1, 0.0, NEG), sw1, m_w)
    sc = jnp.dot(kc_ref[0, 0], qa[0:HEAD_DIM, :], preferred_element_type=F32)
    stage_probs(sw0, m0, pw0)
    stage_values(vwa[qi], pw0, a0, acc_w)
    m2, a2 = stage_scores(kwa[pl.ds(k2, TK), :], qw[...], tile4(jnp.where(qi >= 2, wbias[...], NEG)), sw0, m_w)
    stage_probs(sw1, m1, pw1)
    stage_values(vwa[t1_], pw1, a1, acc_w)
    stage_probs(sw0, m2, pw0)
    stage_values(vwa[t2_], pw0, a2, acc_w)
    o_win = finish(acc_w, gates[2])

    c_i = lax.broadcasted_iota(I32, (ncmp, W4), 0)
    t_i = q0 + (lax.broadcasted_iota(I32, (ncmp, W4), 1) % TQ)
    d_c = t_i - (CMP_STRIDE * c_i + (CMP_LEN - 1))
    sc = jnp.where(d_c >= 0, sc - slope * d_c.astype(F32), NEG)
    e = jnp.exp(sc - jnp.max(sc, axis=0, keepdims=True))
    tq = q0 + lane % TQ
    p = e * ((1.0 / jnp.sum(e, axis=0, keepdims=True)) * (tq >= CMP_LEN - 1).astype(F32))
    ocT = jnp.dot(vcT_ref[0, 0], p.astype(BF16), preferred_element_type=F32)
    for r in range(REP):
        oacc[:, r * TQ:(r + 1) * TQ] = gates[0][r:r + 1, :] * ocT[:, r * TQ:(r + 1) * TQ] + o_win[r]

    psum = p[:, 0:TQ]
    for r in range(1, REP):
        psum = psum + p[:, r * TQ:(r + 1) * TQ]
    nblk = S // SLC_BLOCK
    jj = lax.broadcasted_iota(I32, (nblk, ncmp), 0)
    cc = lax.broadcasted_iota(I32, (nblk, ncmp), 1)
    ovT = ((CMP_STRIDE * cc < SLC_BLOCK * jj + SLC_BLOCK)
           & (CMP_STRIDE * cc + CMP_LEN > SLC_BLOCK * jj)).astype(F32)
    impT = jnp.dot(ovT, psum, preferred_element_type=F32, precision=lax.Precision.HIGHEST)

    j_i = lax.broadcasted_iota(I32, (nblk, TQ), 0)
    t1 = q0 + lax.broadcasted_iota(I32, (nblk, TQ), 1)
    cur = t1 // SLC_BLOCK
    forced = (j_i == 0) | (j_i == cur) | (j_i == cur - 1)
    visible = SLC_BLOCK * j_i <= t1
    val_s[...] = jnp.where(forced, FORCED, jnp.where(visible, impT, NEG))
    ngrp = nblk // 8
    vals = [val_s[8 * jb:8 * jb + 8, :] for jb in range(ngrp)]
    ranks = [jnp.zeros((8, TQ), F32) for _ in range(ngrp)]
    j8 = lax.broadcasted_iota(I32, (8, TQ), 0)
    for i in range(nblk):
        row = jnp.broadcast_to(val_s[i:i + 1, :], (8, TQ))
        for jb in range(ngrp):
            if 8 * jb > i:
                hit = row >= vals[jb]
            elif 8 * jb + 7 < i:
                hit = row > vals[jb]
            else:
                hit = jnp.where(j8 > i - 8 * jb, jnp.where(row >= vals[jb], 1.0, 0.0),
                                jnp.where(row > vals[jb], 1.0, 0.0)) > 0.5
            ranks[jb] = ranks[jb] + jnp.where(hit, 1.0, 0.0)
    blocks_per_tile = TK // SLC_BLOCK
    n_act = jnp.int32(0)
    for jb in range(ngrp):
        chosen = ranks[jb] < n_sel
        selb = jnp.where(chosen, 0.0, NEG)
        for r in range(REP):
            selbias_s[8 * jb:8 * jb + 8, r * TQ:(r + 1) * TQ] = selb
        any_q = jnp.max(jnp.where(chosen, 1.0, 0.0), axis=1, keepdims=True)
        for h in range(8 // blocks_per_tile):
            kt = (8 * jb) // blocks_per_tile + h
            hit = jnp.max(any_q[h * blocks_per_tile:(h + 1) * blocks_per_tile, :]) > 0.5
            act_s[n_act] = kt
            n_act = n_act + jnp.where(hit & (kt < qi), 1, 0)
    qa[HEAD_DIM:HEAD_DIM + nblk, :] = selbias_s[...].astype(BF16)

    reset(m_s, acc_s)
    m_d, a_d = stage_scores(ksa[pl.ds(kd, TK), :], qa[...], causal4, sb0, m_s)
    al_s[...] = a_d

    def sel_body(k, carry):
        m_row = m_s[...]
        al_row = al_s[...]
        stage_probs(sb0, m_row, ps0)
        kt = act_s[k]
        k0 = pl.multiple_of(kt * TK, TK)
        _, al_next = stage_scores(ksa[pl.ds(k0, TK), :], qa[...], None, sb0, m_s)
        al_s[...] = al_next
        stage_values(vsa[jnp.where(k == 0, qi, act_s[jnp.maximum(k - 1, 0)])], ps0, al_row, acc_s)
        return carry

    lax.fori_loop(0, n_act, sel_body, 0)
    stage_probs(sb0, m_s[...], ps0)
    stage_values(vsa[jnp.where(n_act == 0, qi, act_s[jnp.maximum(n_act - 1, 0)])], ps0, al_s[...], acc_s)
    for r, o_r in enumerate(finish(acc_s, gates[1])):
        oT_s[r * HEAD_DIM:(r + 1) * HEAD_DIM, :] = oacc[:, r * TQ:(r + 1) * TQ] + o_r
    o_ref[0] = oT_s[...].T.astype(BF16)


def _attention(qT, ks, vs, kw, vw, kc, vcT, gT):
    B, _, S = qT.shape
    G = N_KV
    nkt = S // TK
    ncmp = kc.shape[2]
    n_sel = min(SLC_TOPK, S // SLC_BLOCK)
    W4 = REP * TQ
    res4 = lambda a: pl.BlockSpec((1, 1) + a.shape[2:], lambda b, g, i: (b, g) + (0,) * (a.ndim - 2))
    return pl.pallas_call(
        functools.partial(_attn_kernel, n_sel=n_sel),
        grid=(B, G, S // TQ),
        in_specs=[pl.BlockSpec((1, REP * HEAD_DIM, TQ), lambda b, g, i: (b, g, i)),
                  res4(ks), res4(vs), res4(kw), res4(vw), res4(kc), res4(vcT),
                  pl.BlockSpec((1, 1, GATE_ROWS, TQ), lambda b, g, i: (b, g, 0, i))],
        out_specs=pl.BlockSpec((1, TQ, REP * HEAD_DIM), lambda b, g, i: (b, i, g)),
        out_shape=jax.ShapeDtypeStruct((B, S, D_ATT), BF16),
        scratch_shapes=[pltpu.VMEM((S, 256), BF16), pltpu.VMEM((S, 128), BF16),
                        pltpu.VMEM((nkt, VROWS, TK), BF16), pltpu.VMEM((nkt, VROWS, TK), BF16),
                        pltpu.VMEM((256, W4), BF16), pltpu.VMEM((128, W4), BF16),
                        pltpu.VMEM((1, W4), F32), pltpu.VMEM((1, W4), F32), pltpu.VMEM((VROWS, W4), F32),
                        pltpu.VMEM((1, W4), F32), pltpu.VMEM((VROWS, W4), F32),
                        pltpu.VMEM((TK, W4), F32), pltpu.VMEM((TK, W4), F32), pltpu.VMEM((TK, W4), F32),
                        pltpu.VMEM((TK, W4), BF16), pltpu.VMEM((TK, W4), BF16), pltpu.VMEM((TK, W4), BF16),
                        pltpu.VMEM((HEAD_DIM, W4), F32), pltpu.VMEM((S // SLC_BLOCK, TQ), F32),
                        pltpu.VMEM((S // SLC_BLOCK, W4), F32),
                        pltpu.VMEM((TK, TQ), F32), pltpu.VMEM((TK, TQ), F32),
                        pltpu.VMEM((REP * HEAD_DIM, TQ), F32),
                        pltpu.SMEM((nkt,), I32)],
        compiler_params=pltpu.CompilerParams(dimension_semantics=("arbitrary", "arbitrary", "arbitrary"),
                                             vmem_limit_bytes=VMEM_LIMIT),
        name="attn",
    )(qT, ks, vs, kw, vw, kc, vcT, gT)


def _outmlp_kernel(x_ref, rnn_ref, att_ref, gatt_ref, wo_ref, gpost_ref, g1_ref, gpre_ref, sc2_ref, sh2_ref,
                   w1_ref, w2_ref, gpost2_ref, g2_ref, o_ref):
    att_n = (_rms(att_ref[0].astype(F32)) * gatt_ref[...]).astype(BF16)
    y = (jnp.dot(rnn_ref[0], wo_ref[0:D_RNN, :], preferred_element_type=F32)
         + jnp.dot(att_n, wo_ref[D_RNN:, :], preferred_element_type=F32))
    x1 = x_ref[0] + (1.0 + g1_ref[0]) * (_rms(y) * gpost_ref[...])
    h2 = (_rms(x1) * (gpre_ref[...] * (1.0 + sc2_ref[0])) + sh2_ref[0]).astype(BF16)
    fc = 1024
    ff = jnp.zeros(x1.shape, F32)
    for c in range(D_FF // fc):
        hid = jnp.maximum(jnp.dot(h2, w1_ref[:, c * fc:(c + 1) * fc], preferred_element_type=F32), 0.0)
        ff = ff + jnp.dot((hid * hid).astype(BF16), w2_ref[c * fc:(c + 1) * fc, :], preferred_element_type=F32)
    o_ref[0] = x1 + (1.0 + g2_ref[0]) * (_rms(ff) * gpost2_ref[...])


def _outmlp(x, rnn_n, att, gatt, wo, gpost, g1, gpre, sc2, sh2, w1, w2, gpost2, g2):
    B, S, D = x.shape
    tm = min(TM_OUT, S)
    row = lambda n: pl.BlockSpec((1, n), lambda b, s: (0, 0))
    per_b = lambda n: pl.BlockSpec((1, 1, n), lambda b, s: (b, 0, 0))
    const = lambda a: pl.BlockSpec(a.shape, lambda b, s: (0,) * a.ndim, pipeline_mode=pl.Buffered(1))
    tok = lambda n: pl.BlockSpec((1, tm, n), lambda b, s: (b, s, 0))
    return pl.pallas_call(
        _outmlp_kernel,
        grid=(B, S // tm),
        in_specs=[tok(D), tok(D_RNN), tok(D_ATT), row(D_ATT), const(wo), row(D), per_b(D), row(D),
                  per_b(D), per_b(D), const(w1), const(w2), row(D), per_b(D)],
        out_specs=tok(D),
        out_shape=jax.ShapeDtypeStruct((B, S, D), F32),
        compiler_params=pltpu.CompilerParams(dimension_semantics=("arbitrary", "arbitrary"),
                                             vmem_limit_bytes=VMEM_LIMIT),
        name="outmlp",
    )(x, rnn_n, att, gatt, wo, gpost, g1, gpre, sc2, sh2, w1, w2, gpost2, g2)


def _block_diag(w):
    n, k, _ = w.shape
    return jnp.einsum('nij,nm->nimj', w, jnp.eye(n, dtype=w.dtype)).reshape(n * k, n * k)


def _layer(x, c, ada_w, ada_b, pre_norm_mix, w_in, conv_w, conv_b, lru_wa, lru_ba, lru_wx, lru_bx, lru_lambda,
           cmp_pos_k, cmp_w1_k, cmp_w2_k, cmp_pos_v, cmp_w1_v, cmp_w2_v, norm_rnn_out, norm_att_out, w_out,
           post_norm_mix, pre_norm_mlp, w_ff1, w_ff2, post_norm_mlp):
    B, S, D = x.shape
    G = N_KV
    row = lambda v: v.reshape(1, -1)

    mod = _ada(c, ada_w, ada_b)
    sh1, sc1, g1, sh2, sc2, g2 = [m.reshape(B, 1, D) for m in jnp.split(mod, 6, axis=-1)]

    gate_cols = [OFF_GATE + br * N_HEADS + g * REP + r for g in range(G) for br in range(3) for r in range(REP)]
    w_gate = w_in[:, jnp.asarray(gate_cols)].reshape(D, G, 3 * REP)
    w_gate = jnp.pad(w_gate, ((0, 0), (0, 0), (0, GATE_ROWS - 3 * REP))).reshape(D, G * GATE_ROWS)
    w_in_p = jnp.concatenate([w_in[:, :OFF_GATE], jnp.pad(w_gate, ((0, 0), (0, GATE_PAD - G * GATE_ROWS)))],
                             axis=1).astype(BF16)
    wa = _block_diag(lru_wa).astype(BF16)
    wx = _block_diag(lru_wx).astype(BF16)
    half = CMP_LEN // 2 * HEAD_DIM

    def w1_cat(w1):
        return jnp.concatenate([w1[:half], w1[half:]], axis=1).astype(BF16)

    def pos_rows(pos):
        return jnp.pad(pos.reshape(2, half), ((0, 14), (0, 0))).astype(BF16)

    rnn_n, qT, ks, vsT, kw, vwT, kc_in, vc_in, gT = _inproj(
        x, row(pre_norm_mix), sc1, sh1, w_in_p, conv_w, row(conv_b), wa, row(lru_ba), wx, row(lru_bx),
        row(lru_lambda), row(norm_rnn_out))
    kc, vc = _compress(kc_in, vc_in, w1_cat(cmp_w1_k), cmp_w2_k.astype(BF16), pos_rows(cmp_pos_k),
                       w1_cat(cmp_w1_v), cmp_w2_v.astype(BF16), pos_rows(cmp_pos_v))
    att = _attention(qT, ks, vsT, kw, vwT, kc, vc.transpose(0, 1, 3, 2), gT)

    return _outmlp(x, rnn_n, att, row(norm_att_out), w_out.astype(BF16), row(post_norm_mix), g1,
                   row(pre_norm_mlp), sc2, sh2, w_ff1.astype(BF16), w_ff2.astype(BF16), row(post_norm_mlp), g2)


def kernel(x, c, ada_w, ada_b, pre_norm_mix, w_in, conv_w, conv_b, lru_wa, lru_ba, lru_wx, lru_bx, lru_lambda,
           cmp_pos_k, cmp_w1_k, cmp_w2_k, cmp_pos_v, cmp_w1_v, cmp_w2_v, norm_rnn_out, norm_att_out, w_out,
           post_norm_mix, pre_norm_mlp, w_ff1, w_ff2, post_norm_mlp):
    for l in range(ada_w.shape[0]):
        x = _layer(x, c, ada_w[l], ada_b[l], pre_norm_mix[l], w_in[l], conv_w[l], conv_b[l], lru_wa[l], lru_ba[l],
                   lru_wx[l], lru_bx[l], lru_lambda[l], cmp_pos_k[l], cmp_w1_k[l], cmp_w2_k[l], cmp_pos_v[l],
                   cmp_w1_v[l], cmp_w2_v[l], norm_rnn_out[l], norm_att_out[l], w_out[l], post_norm_mix[l],
                   pre_norm_mlp[l], w_ff1[l], w_ff2[l], post_norm_mlp[l])
    return x
```

```python
import functools

import jax
import jax.numpy as jnp
from jax import lax
from jax.experimental import pallas as pl
from jax.experimental.pallas import tpu as pltpu

F32 = jnp.float32
BF16 = jnp.bfloat16
I32 = jnp.int32

D_MODEL = 1024
D_RNN = 512
RNN_BLOCKS = 8
CONV_WIDTH = 4
LRU_C = 8.0
N_HEADS = 8
HEAD_DIM = 64
N_KV = 2
REP = N_HEADS // N_KV
D_ATT = N_HEADS * HEAD_DIM
CMP_LEN = 32
CMP_STRIDE = 16
CMP_HIDDEN = 256
SLC_BLOCK = 64
SLC_TOPK = 16
WINDOW = 512
D_FF = 4 * D_MODEL
EPS = 1e-6
NEG = -1e30
FORCED = 1e4
LOG2E = 1.4426950408889634

KV_COLS = 6 * N_KV * HEAD_DIM
N_GATE = 3 * N_HEADS
GATE_PAD = 128
GATE_ROWS = 16
OFF_Q = 2 * D_RNN
OFF_KV = OFF_Q + D_ATT
OFF_GATE = OFF_KV + KV_COLS
D_IN_PAD = OFF_GATE + GATE_PAD

TM_IN = 512
TM_OUT = 512
TQ = 256
TK = 256
KAUG = 256
VROWS = 80
VMEM_LIMIT = 56 * 1024 * 1024

SEL, WIN = 0, 1
MASK_NONE, MASK_CAUSAL, MASK_WINDOW_LOW = 0, 1, 2
POS_COL = 2 * HEAD_DIM


def _gelu_tanh(x):
    return 0.5 * x * (1.0 + jnp.tanh(0.7978845608028654 * (x + 0.044715 * (x * x * x))))


def _rms(x):
    return x * lax.rsqrt(jnp.mean(x * x, axis=-1, keepdims=True) + EPS)


def _ada_kernel(c_ref, w_ref, b_ref, o_ref):
    c = c_ref[...]
    a = c * jax.nn.sigmoid(c)
    o_ref[...] = jnp.dot(a, w_ref[...], preferred_element_type=F32,
                         precision=lax.Precision.HIGHEST) + b_ref[...]


def _ada(c, w, b):
    B, D = c.shape
    N = w.shape[1]
    tn = 1024
    return pl.pallas_call(
        _ada_kernel,
        grid=(N // tn,),
        in_specs=[pl.BlockSpec((B, D), lambda j: (0, 0)),
                  pl.BlockSpec((D, tn), lambda j: (0, j)),
                  pl.BlockSpec((1, tn), lambda j: (0, j))],
        out_specs=pl.BlockSpec((B, tn), lambda j: (0, j)),
        out_shape=jax.ShapeDtypeStruct((B, N), F32),
        name="ada",
    )(c, w, b.reshape(1, N))


def _inproj_kernel(x_ref, gain_ref, sc_ref, sh_ref, w_ref, cw_ref, cb_ref, wa_ref, ba_ref, wx_ref, bx_ref,
                   lam_ref, grnn_ref,
                   rnn_ref, qT_ref, ks_ref, vs_ref, kw_ref, vw_ref, kc_ref, vc_ref, g_ref,
                   xbuf, hcar, a_s, u_s, h_s):
    tm = x_ref.shape[1]
    hd = HEAD_DIM

    @pl.when(pl.program_id(1) == 0)
    def _():
        xbuf[0:8, :] = jnp.zeros((8, D_RNN), F32)
        hcar[...] = jnp.zeros((1, D_RNN), F32)

    x = x_ref[0]
    h = _rms(x) * (gain_ref[...] * (1.0 + sc_ref[0])) + sh_ref[0]
    hb = h.astype(BF16)

    xr = jnp.dot(hb, w_ref[:, D_RNN:OFF_Q], preferred_element_type=F32)
    xbuf[8:8 + tm, :] = xr
    y = (cw_ref[3:4, :] * xr + cw_ref[2:3, :] * xbuf[7:7 + tm, :]
         + cw_ref[1:2, :] * xbuf[6:6 + tm, :] + cw_ref[0:1, :] * xbuf[5:5 + tm, :]) + cb_ref[...]
    xbuf[0:8, :] = xbuf[tm:tm + 8, :]

    yb = y.astype(BF16)
    r = jax.nn.sigmoid(jnp.dot(yb, wa_ref[...], preferred_element_type=F32) + ba_ref[...])
    i = jax.nn.sigmoid(jnp.dot(yb, wx_ref[...], preferred_element_type=F32) + bx_ref[...])
    nl = -lam_ref[...]
    softplus = jnp.maximum(nl, 0.0) + jnp.log(1.0 + jnp.exp(-jnp.abs(nl)))
    a = jnp.exp((-LRU_C) * r * softplus)
    a_s[...] = a
    u_s[...] = jnp.sqrt(1.0 - a * a) * (i * y)

    qT_ref[0] = (jnp.dot(hb, w_ref[:, OFF_Q:OFF_KV], preferred_element_type=F32)
                 * (HEAD_DIM ** -0.5 * LOG2E)).T.astype(BF16)
    kv = jnp.dot(hb, w_ref[:, OFF_KV:OFF_GATE], preferred_element_type=F32)
    width = N_KV * hd
    vsT = kv[:, 3 * width:4 * width].T
    vwT = kv[:, 5 * width:6 * width].T
    for gi in range(N_KV):
        kc_ref[0, gi] = kv[:, gi * hd:(gi + 1) * hd]
        vc_ref[0, gi] = kv[:, width + gi * hd:width + (gi + 1) * hd]
        ks_ref[0, gi] = kv[:, 2 * width + gi * hd:2 * width + (gi + 1) * hd].astype(BF16)
        kw_ref[0, gi] = kv[:, 4 * width + gi * hd:4 * width + (gi + 1) * hd].astype(BF16)
        for j in range(tm // TK):
            vs_ref[0, gi, j] = vsT[gi * hd:(gi + 1) * hd, j * TK:(j + 1) * TK].astype(BF16)
            vw_ref[0, gi, j] = vwT[gi * hd:(gi + 1) * hd, j * TK:(j + 1) * TK].astype(BF16)
    gates = jax.nn.sigmoid(jnp.dot(hb, w_ref[:, OFF_GATE:D_IN_PAD], preferred_element_type=F32)).T
    for gi in range(N_KV):
        g_ref[0, gi] = gates[gi * GATE_ROWS:(gi + 1) * GATE_ROWS, :]
    g = jnp.dot(hb, w_ref[:, 0:D_RNN], preferred_element_type=F32)

    rows = lax.broadcasted_iota(I32, (8, D_RNN), 0)
    hprev = hcar[...]
    for gi in range(tm // 8):
        ag = a_s[gi * 8:gi * 8 + 8, :]
        ug = u_s[gi * 8:gi * 8 + 8, :]
        for k in (1, 2, 4):
            a_sh = jnp.where(rows >= k, pltpu.roll(ag, k, 0), 1.0)
            u_sh = jnp.where(rows >= k, pltpu.roll(ug, k, 0), 0.0)
            ug = ag * u_sh + ug
            ag = ag * a_sh
        hg = ag * hprev + ug
        h_s[gi * 8:gi * 8 + 8, :] = hg
        hprev = hg[7:8, :]
    hcar[...] = hprev

    rnn = _gelu_tanh(g) * h_s[...]
    rnn_ref[0] = (_rms(rnn) * grnn_ref[...]).astype(BF16)


def _inproj(x, gain, sc, sh, w_in, conv_w, conv_b, wa, ba, wx, bx, lam, grnn):
    B, S, D = x.shape
    tm = min(TM_IN, S)
    row = lambda n: pl.BlockSpec((1, n), lambda b, s: (0, 0))
    per_b = lambda n: pl.BlockSpec((1, 1, n), lambda b, s: (b, 0, 0))
    full = lambda a: pl.BlockSpec(a.shape, lambda b, s: (0,) * a.ndim)
    tok = lambda n: pl.BlockSpec((1, tm, n), lambda b, s: (b, s, 0))
    G = N_KV
    grp = pl.BlockSpec((1, G, tm, HEAD_DIM), lambda b, s: (b, 0, s, 0))
    grpT = pl.BlockSpec((1, G, tm // TK, HEAD_DIM, TK), lambda b, s: (b, 0, s, 0, 0))
    return pl.pallas_call(
        _inproj_kernel,
        grid=(B, S // tm),
        in_specs=[tok(D), row(D), per_b(D), per_b(D), full(w_in), full(conv_w), row(D_RNN),
                  full(wa), row(D_RNN), full(wx), row(D_RNN), row(D_RNN), row(D_RNN)],
        out_specs=[tok(D_RNN),
                   pl.BlockSpec((1, D_ATT, tm), lambda b, s: (b, 0, s)),
                   grp, grpT, grp, grpT, grp, grp,
                   pl.BlockSpec((1, G, GATE_ROWS, tm), lambda b, s: (b, 0, 0, s))],
        out_shape=[jax.ShapeDtypeStruct((B, S, D_RNN), BF16),
                   jax.ShapeDtypeStruct((B, D_ATT, S), BF16),
                   jax.ShapeDtypeStruct((B, G, S, HEAD_DIM), BF16),
                   jax.ShapeDtypeStruct((B, G, S // TK, HEAD_DIM, TK), BF16),
                   jax.ShapeDtypeStruct((B, G, S, HEAD_DIM), BF16),
                   jax.ShapeDtypeStruct((B, G, S // TK, HEAD_DIM, TK), BF16),
                   jax.ShapeDtypeStruct((B, G, S, HEAD_DIM), F32),
                   jax.ShapeDtypeStruct((B, G, S, HEAD_DIM), F32),
                   jax.ShapeDtypeStruct((B, G, GATE_ROWS, S), F32)],
        scratch_shapes=[pltpu.VMEM((tm + 8, D_RNN), F32), pltpu.VMEM((1, D_RNN), F32),
                        pltpu.VMEM((tm, D_RNN), F32), pltpu.VMEM((tm, D_RNN), F32),
                        pltpu.VMEM((tm, D_RNN), F32)],
        compiler_params=pltpu.CompilerParams(dimension_semantics=("arbitrary", "arbitrary"),
                                             vmem_limit_bytes=VMEM_LIMIT),
        name="inproj",
    )(x, gain, sc, sh, w_in, conv_w, conv_b, wa, ba, wx, bx, lam, grnn)


def _compress_kernel(rk_ref, rv_ref, w1k_ref, w2k_ref, pk_ref, w1v_ref, w2v_ref, pv_ref, kc_ref, vc_ref):
    def one(r_ref, w1_ref, w2_ref, pos_ref, o_ref):
        n = r_ref.shape[2] // CMP_STRIDE
        p = jnp.zeros((n, 2 * CMP_HIDDEN), F32)
        for l in range(CMP_STRIDE):
            rows = r_ref[0, 0, pl.ds(l, n, stride=CMP_STRIDE), :].astype(BF16)
            p = p + jnp.dot(rows, w1_ref[l * HEAD_DIM:(l + 1) * HEAD_DIM, :], preferred_element_type=F32)
        posb = jnp.dot(pos_ref[...], w1_ref[...], preferred_element_type=F32)
        bias = posb[0:1, 0:CMP_HIDDEN] + posb[1:2, CMP_HIDDEN:]
        pre = p[:, 0:CMP_HIDDEN] + pltpu.roll(p[:, CMP_HIDDEN:], n - 1, 0) + bias
        hid = _gelu_tanh(pre).astype(BF16)
        o_ref[0, 0] = jnp.dot(hid, w2_ref[...], preferred_element_type=F32).astype(BF16)

    one(rk_ref, w1k_ref, w2k_ref, pk_ref, kc_ref)
    one(rv_ref, w1v_ref, w2v_ref, pv_ref, vc_ref)


def _compress(rk, rv, w1k, w2k, posk, w1v, w2v, posv):
    B, G, S, width = rk.shape
    n = S // CMP_STRIDE
    blk = pl.BlockSpec((1, 1, S, width), lambda b, g: (b, g, 0, 0))
    full = lambda a: pl.BlockSpec(a.shape, lambda b, g: (0,) * a.ndim)
    out = pl.BlockSpec((1, 1, n, HEAD_DIM), lambda b, g: (b, g, 0, 0))
    return pl.pallas_call(
        _compress_kernel,
        grid=(B, G),
        in_specs=[blk, blk, full(w1k), full(w2k), full(posk), full(w1v), full(w2v), full(posv)],
        out_specs=[out, out],
        out_shape=[jax.ShapeDtypeStruct((B, G, n, HEAD_DIM), BF16)] * 2,
        compiler_params=pltpu.CompilerParams(vmem_limit_bytes=VMEM_LIMIT),
        name="compress",
    )(rk, rv, w1k, w2k, posk, w1v, w2v, posv)


def _attn_kernel(qT_ref, ks_ref, vs_ref, kw_ref, vw_ref, kc_ref, vcT_ref, g_ref,
                 o_ref,
                 kall, vall, qall, kca, m_all, acc_all, al_s, sbuf, pbuf, oacc, val_s, selbias_s, bias_tbl, cmask,
                 oT_s, it_br, it_tile, it_mask, *, n_sel):
    S = ks_ref.shape[2]
    nkt = S // TK
    ncmp = kc_ref.shape[2]
    nblk = S // SLC_BLOCK
    cmp_per_tile = TQ // CMP_STRIDE
    b = pl.program_id(0)
    g = pl.program_id(1)
    qi = pl.program_id(2)
    q0 = qi * TQ
    W4 = REP * TQ

    @pl.when((b == 0) & (g == 0) & (qi == 0))
    def _():
        col = lax.broadcasted_iota(I32, (TK, KAUG), 1)
        is_blk = (col == POS_COL) | (col == POS_COL + 2)
        is_off = (col == POS_COL + 1) | (col == POS_COL + 3)
        ones_row = (lax.broadcasted_iota(I32, (VROWS - HEAD_DIM, TK), 0) == 0).astype(BF16)
        for kt in range(nkt):
            key = kt * TK + lax.broadcasted_iota(I32, (TK, KAUG), 0)
            pos = jnp.where(is_blk, (key // SLC_BLOCK).astype(F32),
                            jnp.where(is_off, (key % SLC_BLOCK).astype(F32), 0.0))
            onehot = (col - HEAD_DIM == key // SLC_BLOCK).astype(F32)
            kall[SEL, kt * TK:(kt + 1) * TK, :] = (pos + onehot).astype(BF16)
            kall[WIN, kt * TK:(kt + 1) * TK, :] = pos.astype(BF16)
            vall[SEL, kt, HEAD_DIM:VROWS, :] = ones_row
            vall[WIN, kt, HEAD_DIM:VROWS, :] = ones_row
        ko = lax.broadcasted_iota(I32, (TK, W4), 0)
        to = lax.broadcasted_iota(I32, (TK, W4), 1) % TQ
        bias_tbl[MASK_NONE] = jnp.zeros((TK, W4), F32)
        bias_tbl[MASK_CAUSAL] = jnp.where(ko <= to, 0.0, NEG)
        bias_tbl[MASK_WINDOW_LOW] = jnp.where(ko > to, 0.0, NEG)
        u = lax.broadcasted_iota(I32, (2 * ncmp, W4), 0)
        tc = lax.broadcasted_iota(I32, (2 * ncmp, W4), 1) % TQ
        cmask[...] = jnp.where(u <= ncmp + ((tc + 1) // CMP_STRIDE) - 2, 0.0, NEG)
        cc = lax.broadcasted_iota(I32, (ncmp, KAUG), 0)
        colc = lax.broadcasted_iota(I32, (ncmp, KAUG), 1)
        kca[...] = jnp.where((colc == POS_COL) | (colc == POS_COL + 2), (cc // 4).astype(F32),
                             jnp.where((colc == POS_COL + 1) | (colc == POS_COL + 3),
                                       (CMP_STRIDE * (cc % 4)).astype(F32), 0.0)).astype(BF16)
        qall[...] = jnp.zeros(qall.shape, BF16)

    lane = lax.broadcasted_iota(I32, (1, W4), 1)
    head = g * REP + lane // TQ
    slope = lax.bitcast_convert_type((126 - head) << 23, F32)

    @pl.when(qi == 0)
    def _():
        for kt in range(nkt):
            kall[SEL, kt * TK:(kt + 1) * TK, 0:HEAD_DIM] = ks_ref[0, 0, kt * TK:(kt + 1) * TK, :]
            kall[WIN, kt * TK:(kt + 1) * TK, 0:HEAD_DIM] = kw_ref[0, 0, kt * TK:(kt + 1) * TK, :]
            vall[SEL, kt, 0:HEAD_DIM, :] = vs_ref[0, 0, kt]
            vall[WIN, kt, 0:HEAD_DIM, :] = vw_ref[0, 0, kt]
        kca[:, 0:HEAD_DIM] = kc_ref[0, 0]
        c_hi = (slope * LOG2E).astype(BF16).astype(F32)
        c_lo = slope * LOG2E - c_hi
        r16 = lax.broadcasted_iota(I32, (16, W4), 0)
        alibi = jnp.where(r16 == 0, c_hi * SLC_BLOCK, jnp.where(r16 == 1, c_hi, jnp.where(
            r16 == 2, c_lo * SLC_BLOCK, jnp.where(r16 == 3, c_lo, 0.0)))).astype(BF16)
        qall[SEL, POS_COL:POS_COL + 16, :] = alibi
        qall[WIN, POS_COL:POS_COL + 16, :] = alibi

    q = qT_ref[0]
    for r in range(REP):
        qr = q[r * HEAD_DIM:(r + 1) * HEAD_DIM, :]
        qall[SEL, 0:HEAD_DIM, r * TQ:(r + 1) * TQ] = qr
        qall[WIN, 0:HEAD_DIM, r * TQ:(r + 1) * TQ] = qr

    gall = g_ref[0, 0]
    gates = [gall[br * REP:(br + 1) * REP, :] for br in range(3)]

    def stage_scores(br, tile, mask):
        r0 = pl.multiple_of(tile * TK, TK)
        s = jnp.dot(kall[br, pl.ds(r0, TK), :], qall[br], preferred_element_type=F32) + bias_tbl[mask]
        sbuf[...] = s
        m_old = m_all[br]
        m_new = jnp.maximum(m_old, jnp.max(s, axis=0, keepdims=True))
        m_all[br] = m_new
        al_s[...] = jnp.exp2(m_old - m_new)

    def stage_probs(br):
        pbuf[...] = jnp.exp2(sbuf[...] - m_all[br]).astype(BF16)

    def stage_values(br, tile, al_row):
        acc_all[br] = acc_all[br] * al_row + jnp.dot(vall[br, tile], pbuf[...], preferred_element_type=F32)

    def push(n, br, tile, mask, cond):
        it_br[n] = br
        it_tile[n] = tile
        it_mask[n] = mask
        return n + jnp.where(cond, 1, 0)

    m_all[...] = jnp.full(m_all.shape, NEG, F32)
    acc_all[...] = jnp.zeros(acc_all.shape, F32)
    n_it = push(jnp.int32(0), WIN, qi, MASK_CAUSAL, True)
    n_it = push(n_it, WIN, qi - 2, MASK_WINDOW_LOW, qi >= 2)
    n_it = push(n_it, WIN, qi - 1, MASK_NONE, qi >= 1)
    n_it = push(n_it, SEL, qi, MASK_CAUSAL, True)
    stage_scores(WIN, qi, MASK_CAUSAL)

    c0 = pl.multiple_of(ncmp - cmp_per_tile * qi, cmp_per_tile)
    sc = jnp.dot(kca[...], qall[WIN], preferred_element_type=F32) + cmask[pl.ds(c0, ncmp), :]
    e = jnp.exp2(sc - jnp.max(sc, axis=0, keepdims=True))
    tq = q0 + lane % TQ
    p = e * ((1.0 / jnp.sum(e, axis=0, keepdims=True)) * (tq >= CMP_LEN - 1).astype(F32))
    ocT = jnp.dot(vcT_ref[0, 0], p.astype(BF16), preferred_element_type=F32)
    for r in range(REP):
        oacc[:, r * TQ:(r + 1) * TQ] = gates[0][r:r + 1, :] * ocT[:, r * TQ:(r + 1) * TQ]

    psum = p[:, 0:TQ]
    for r in range(1, REP):
        psum = psum + p[:, r * TQ:(r + 1) * TQ]
    jj = lax.broadcasted_iota(I32, (nblk, ncmp), 0)
    cc = lax.broadcasted_iota(I32, (nblk, ncmp), 1)
    ovT = ((CMP_STRIDE * cc < SLC_BLOCK * jj + SLC_BLOCK)
           & (CMP_STRIDE * cc + CMP_LEN > SLC_BLOCK * jj)).astype(F32)
    impT = jnp.dot(ovT, psum, preferred_element_type=F32, precision=lax.Precision.HIGHEST)

    j_i = lax.broadcasted_iota(I32, (nblk, TQ), 0)
    t1 = q0 + lax.broadcasted_iota(I32, (nblk, TQ), 1)
    cur = t1 // SLC_BLOCK
    forced = (j_i == 0) | (j_i == cur) | (j_i == cur - 1)
    visible = SLC_BLOCK * j_i <= t1
    val_s[...] = jnp.where(forced, FORCED, jnp.where(visible, impT, NEG))
    ngrp = nblk // 8
    vals = [val_s[8 * jb:8 * jb + 8, :] for jb in range(ngrp)]
    ranks = [jnp.zeros((8, TQ), F32) for _ in range(ngrp)]
    j8 = lax.broadcasted_iota(I32, (8, TQ), 0)
    for i in range(nblk):
        row = jnp.broadcast_to(val_s[i:i + 1, :], (8, TQ))
        for jb in range(ngrp):
            if 8 * jb > i:
                hit = row >= vals[jb]
            elif 8 * jb + 7 < i:
                hit = row > vals[jb]
            else:
                hit = jnp.where(j8 > i - 8 * jb, jnp.where(row >= vals[jb], 1.0, 0.0),
                                jnp.where(row > vals[jb], 1.0, 0.0)) > 0.5
            ranks[jb] = ranks[jb] + jnp.where(hit, 1.0, 0.0)
    blocks_per_tile = TK // SLC_BLOCK
    for jb in range(ngrp):
        chosen = ranks[jb] < n_sel
        selb = jnp.where(chosen, 0.0, NEG)
        for r in range(REP):
            selbias_s[8 * jb:8 * jb + 8, r * TQ:(r + 1) * TQ] = selb
        any_q = jnp.max(jnp.where(chosen, 1.0, 0.0), axis=1, keepdims=True)
        for hh in range(8 // blocks_per_tile):
            kt = (8 * jb) // blocks_per_tile + hh
            hit = jnp.max(any_q[hh * blocks_per_tile:(hh + 1) * blocks_per_tile, :]) > 0.5
            n_it = push(n_it, SEL, kt, MASK_NONE, hit & (kt < qi))
    qall[SEL, HEAD_DIM:HEAD_DIM + nblk, :] = selbias_s[...].astype(BF16)

    def body(k, carry):
        pb = it_br[k - 1]
        pt = it_tile[k - 1]
        al_row = al_s[...]
        stage_probs(pb)
        stage_scores(it_br[k], it_tile[k], it_mask[k])
        stage_values(pb, pt, al_row)
        return carry

    lax.fori_loop(1, n_it, body, 0)
    pb = it_br[n_it - 1]
    stage_probs(pb)
    stage_values(pb, it_tile[n_it - 1], al_s[...])

    def normalised(br):
        acc = acc_all[br]
        return acc[0:HEAD_DIM, :] / acc[HEAD_DIM:HEAD_DIM + 1, :]

    o_sel = normalised(SEL)
    o_win = normalised(WIN)
    for r in range(REP):
        lanes = slice(r * TQ, (r + 1) * TQ)
        oT_s[r * HEAD_DIM:(r + 1) * HEAD_DIM, :] = (oacc[:, lanes] + gates[1][r:r + 1, :] * o_sel[:, lanes]
                                                    + gates[2][r:r + 1, :] * o_win[:, lanes])
    o_ref[0] = oT_s[...].T.astype(BF16)


def _attention(qT, ks, vs, kw, vw, kc, vcT, gT):
    B, _, S = qT.shape
    G = N_KV
    nkt = S // TK
    ncmp = kc.shape[2]
    nblk = S // SLC_BLOCK
    n_sel = min(SLC_TOPK, nblk)
    n_items = nkt + 8
    W4 = REP * TQ
    res4 = lambda a: pl.BlockSpec((1, 1) + a.shape[2:], lambda b, g, i: (b, g) + (0,) * (a.ndim - 2))
    return pl.pallas_call(
        functools.partial(_attn_kernel, n_sel=n_sel),
        grid=(B, G, S // TQ),
        in_specs=[pl.BlockSpec((1, REP * HEAD_DIM, TQ), lambda b, g, i: (b, g, i)),
                  res4(ks), res4(vs), res4(kw), res4(vw), res4(kc), res4(vcT),
                  pl.BlockSpec((1, 1, GATE_ROWS, TQ), lambda b, g, i: (b, g, 0, i))],
        out_specs=pl.BlockSpec((1, TQ, REP * HEAD_DIM), lambda b, g, i: (b, i, g)),
        out_shape=jax.ShapeDtypeStruct((B, S, D_ATT), BF16),
        scratch_shapes=[pltpu.VMEM((2, S, KAUG), BF16),
                        pltpu.VMEM((2, nkt, VROWS, TK), BF16),
                        pltpu.VMEM((2, KAUG, W4), BF16),
                        pltpu.VMEM((ncmp, KAUG), BF16),
                        pltpu.VMEM((2, 1, W4), F32),
                        pltpu.VMEM((2, VROWS, W4), F32),
                        pltpu.VMEM((1, W4), F32),
                        pltpu.VMEM((TK, W4), F32),
                        pltpu.VMEM((TK, W4), BF16),
                        pltpu.VMEM((HEAD_DIM, W4), F32),
                        pltpu.VMEM((nblk, TQ), F32),
                        pltpu.VMEM((nblk, W4), F32),
                        pltpu.VMEM((3, TK, W4), F32),
                        pltpu.VMEM((2 * ncmp, W4), F32),
                        pltpu.VMEM((REP * HEAD_DIM, TQ), F32),
                        pltpu.SMEM((n_items,), I32), pltpu.SMEM((n_items,), I32), pltpu.SMEM((n_items,), I32)],
        compiler_params=pltpu.CompilerParams(dimension_semantics=("arbitrary", "arbitrary", "arbitrary"),
                                             vmem_limit_bytes=VMEM_LIMIT),
        name="attn",
    )(qT, ks, vs, kw, vw, kc, vcT, gT)


def _outmlp_kernel(x_ref, rnn_ref, att_ref, gatt_ref, wo_ref, gpost_ref, g1_ref, gpre_ref, sc2_ref, sh2_ref,
                   w1_ref, w2_ref, gpost2_ref, g2_ref, o_ref):
    att_n = (_rms(att_ref[0].astype(F32)) * gatt_ref[...]).astype(BF16)
    y = (jnp.dot(rnn_ref[0], wo_ref[0:D_RNN, :], preferred_element_type=F32)
         + jnp.dot(att_n, wo_ref[D_RNN:, :], preferred_element_type=F32))
    x1 = x_ref[0] + (1.0 + g1_ref[0]) * (_rms(y) * gpost_ref[...])
    h2 = (_rms(x1) * (gpre_ref[...] * (1.0 + sc2_ref[0])) + sh2_ref[0]).astype(BF16)
    fc = 1024
    ff = jnp.zeros(x1.shape, F32)
    for c in range(D_FF // fc):
        hid = jnp.maximum(jnp.dot(h2, w1_ref[:, c * fc:(c + 1) * fc], preferred_element_type=F32), 0.0)
        ff = ff + jnp.dot((hid * hid).astype(BF16), w2_ref[c * fc:(c + 1) * fc, :], preferred_element_type=F32)
    o_ref[0] = x1 + (1.0 + g2_ref[0]) * (_rms(ff) * gpost2_ref[...])


def _outmlp(x, rnn_n, att, gatt, wo, gpost, g1, gpre, sc2, sh2, w1, w2, gpost2, g2):
    B, S, D = x.shape
    tm = min(TM_OUT, S)
    row = lambda n: pl.BlockSpec((1, n), lambda b, s: (0, 0))
    per_b = lambda n: pl.BlockSpec((1, 1, n), lambda b, s: (b, 0, 0))
    const = lambda a: pl.BlockSpec(a.shape, lambda b, s: (0,) * a.ndim, pipeline_mode=pl.Buffered(1))
    tok = lambda n: pl.BlockSpec((1, tm, n), lambda b, s: (b, s, 0))
    return pl.pallas_call(
        _outmlp_kernel,
        grid=(B, S // tm),
        in_specs=[tok(D), tok(D_RNN), tok(D_ATT), row(D_ATT), const(wo), row(D), per_b(D), row(D),
                  per_b(D), per_b(D), const(w1), const(w2), row(D), per_b(D)],
        out_specs=tok(D),
        out_shape=jax.ShapeDtypeStruct((B, S, D), F32),
        compiler_params=pltpu.CompilerParams(dimension_semantics=("arbitrary", "arbitrary"),
                                             vmem_limit_bytes=VMEM_LIMIT),
        name="outmlp",
    )(x, rnn_n, att, gatt, wo, gpost, g1, gpre, sc2, sh2, w1, w2, gpost2, g2)


def _block_diag(w):
    n, k, _ = w.shape
    return jnp.einsum('nij,nm->nimj', w, jnp.eye(n, dtype=w.dtype)).reshape(n * k, n * k)


def _layer(x, c, ada_w, ada_b, pre_norm_mix, w_in, conv_w, conv_b, lru_wa, lru_ba, lru_wx, lru_bx, lru_lambda,
           cmp_pos_k, cmp_w1_k, cmp_w2_k, cmp_pos_v, cmp_w1_v, cmp_w2_v, norm_rnn_out, norm_att_out, w_out,
           post_norm_mix, pre_norm_mlp, w_ff1, w_ff2, post_norm_mlp):
    B, S, D = x.shape
    G = N_KV
    row = lambda v: v.reshape(1, -1)

    mod = _ada(c, ada_w, ada_b)
    sh1, sc1, g1, sh2, sc2, g2 = [m.reshape(B, 1, D) for m in jnp.split(mod, 6, axis=-1)]

    gate_cols = [OFF_GATE + br * N_HEADS + g * REP + r for g in range(G) for br in range(3) for r in range(REP)]
    w_gate = w_in[:, jnp.asarray(gate_cols)].reshape(D, G, 3 * REP)
    w_gate = jnp.pad(w_gate, ((0, 0), (0, 0), (0, GATE_ROWS - 3 * REP))).reshape(D, G * GATE_ROWS)
    w_in_p = jnp.concatenate([w_in[:, :OFF_GATE], jnp.pad(w_gate, ((0, 0), (0, GATE_PAD - G * GATE_ROWS)))],
                             axis=1).astype(BF16)
    wa = _block_diag(lru_wa).astype(BF16)
    wx = _block_diag(lru_wx).astype(BF16)
    half = CMP_LEN // 2 * HEAD_DIM

    def w1_cat(w1):
        return jnp.concatenate([w1[:half], w1[half:]], axis=1).astype(BF16)

    def pos_rows(pos):
        return jnp.pad(pos.reshape(2, half), ((0, 14), (0, 0))).astype(BF16)

    rnn_n, qT, ks, vsT, kw, vwT, kc_in, vc_in, gT = _inproj(
        x, row(pre_norm_mix), sc1, sh1, w_in_p, conv_w, row(conv_b), wa, row(lru_ba), wx, row(lru_bx),
        row(lru_lambda), row(norm_rnn_out))
    kc, vc = _compress(kc_in, vc_in, w1_cat(cmp_w1_k), cmp_w2_k.astype(BF16), pos_rows(cmp_pos_k),
                       w1_cat(cmp_w1_v), cmp_w2_v.astype(BF16), pos_rows(cmp_pos_v))
    att = _attention(qT, ks, vsT, kw, vwT, kc, vc.transpose(0, 1, 3, 2), gT)

    return _outmlp(x, rnn_n, att, row(norm_att_out), w_out.astype(BF16), row(post_norm_mix), g1,
                   row(pre_norm_mlp), sc2, sh2, w_ff1.astype(BF16), w_ff2.astype(BF16), row(post_norm_mlp), g2)


def kernel(x, c, ada_w, ada_b, pre_norm_mix, w_in, conv_w, conv_b, lru_wa, lru_ba, lru_wx, lru_bx, lru_lambda,
           cmp_pos_k, cmp_w1_k, cmp_w2_k, cmp_pos_v, cmp_w1_v, cmp_w2_v, norm_rnn_out, norm_att_out, w_out,
           post_norm_mix, pre_norm_mlp, w_ff1, w_ff2, post_norm_mlp):
    for l in range(ada_w.shape[0]):
        x = _layer(x, c, ada_w[l], ada_b[l], pre_norm_mix[l], w_in[l], conv_w[l], conv_b[l], lru_wa[l], lru_ba[l],
                   lru_wx[l], lru_bx[l], lru_lambda[l], cmp_pos_k[l], cmp_w1_k[l], cmp_w2_k[l], cmp_pos_v[l],
                   cmp_w1_v[l], cmp_w2_v[l], norm_rnn_out[l], norm_att_out[l], w_out[l], post_norm_mix[l],
                   pre_norm_mlp[l], w_ff1[l], w_ff2[l], post_norm_mlp[l])
    return x
```

```python
import functools

import jax
import jax.numpy as jnp
from jax import lax
from jax.experimental import pallas as pl
from jax.experimental.pallas import tpu as pltpu

F32 = jnp.float32
BF16 = jnp.bfloat16
I32 = jnp.int32

D_MODEL = 1024
D_RNN = 512
RNN_BLOCKS = 8
CONV_WIDTH = 4
LRU_C = 8.0
N_HEADS = 8
HEAD_DIM = 64
N_KV = 2
REP = N_HEADS // N_KV
D_ATT = N_HEADS * HEAD_DIM
CMP_LEN = 32
CMP_STRIDE = 16
CMP_HIDDEN = 256
SLC_BLOCK = 64
SLC_TOPK = 16
WINDOW = 512
D_FF = 4 * D_MODEL
EPS = 1e-6
NEG = -1e30
FORCED = 1e4
LOG2E = 1.4426950408889634

KV_COLS = 6 * N_KV * HEAD_DIM
N_GATE = 3 * N_HEADS
GATE_PAD = 128
GATE_ROWS = 16
OFF_Q = 2 * D_RNN
OFF_KV = OFF_Q + D_ATT
OFF_GATE = OFF_KV + KV_COLS
D_IN_PAD = OFF_GATE + GATE_PAD

TM_IN = 512
TM_OUT = 512
TQ = 256
TK = 256
KAUG = 256
VROWS = 80
VMEM_LIMIT = 56 * 1024 * 1024

SEL, WIN = 0, 1
MASK_NONE, MASK_CAUSAL, MASK_WINDOW_LOW, MASK_ALL = 0, 1, 2, 3
POS_COL = 2 * HEAD_DIM


def _gelu_tanh(x):
    return 0.5 * x * (1.0 + jnp.tanh(0.7978845608028654 * (x + 0.044715 * (x * x * x))))


def _rms(x):
    return x * lax.rsqrt(jnp.mean(x * x, axis=-1, keepdims=True) + EPS)


def _ada_kernel(c_ref, w_ref, b_ref, o_ref):
    c = c_ref[...]
    a = c * jax.nn.sigmoid(c)
    o_ref[...] = jnp.dot(a, w_ref[...], preferred_element_type=F32,
                         precision=lax.Precision.HIGHEST) + b_ref[...]


def _ada(c, w, b):
    B, D = c.shape
    N = w.shape[1]
    tn = 1024
    return pl.pallas_call(
        _ada_kernel,
        grid=(N // tn,),
        in_specs=[pl.BlockSpec((B, D), lambda j: (0, 0)),
                  pl.BlockSpec((D, tn), lambda j: (0, j)),
                  pl.BlockSpec((1, tn), lambda j: (0, j))],
        out_specs=pl.BlockSpec((B, tn), lambda j: (0, j)),
        out_shape=jax.ShapeDtypeStruct((B, N), F32),
        name="ada",
    )(c, w, b.reshape(1, N))


def _inproj_kernel(x_ref, gain_ref, sc_ref, sh_ref, w_ref, cw_ref, cb_ref, wa_ref, ba_ref, wx_ref, bx_ref,
                   lam_ref, grnn_ref,
                   rnn_ref, qT_ref, ks_ref, vs_ref, kw_ref, vw_ref, kc_ref, vc_ref, g_ref,
                   xbuf, hcar, a_s, u_s, h_s):
    tm = x_ref.shape[1]
    hd = HEAD_DIM

    @pl.when(pl.program_id(1) == 0)
    def _():
        xbuf[0:8, :] = jnp.zeros((8, D_RNN), F32)
        hcar[...] = jnp.zeros((1, D_RNN), F32)

    x = x_ref[0]
    h = _rms(x) * (gain_ref[...] * (1.0 + sc_ref[0])) + sh_ref[0]
    hb = h.astype(BF16)

    xr = jnp.dot(hb, w_ref[:, D_RNN:OFF_Q], preferred_element_type=F32)
    xbuf[8:8 + tm, :] = xr
    y = (cw_ref[3:4, :] * xr + cw_ref[2:3, :] * xbuf[7:7 + tm, :]
         + cw_ref[1:2, :] * xbuf[6:6 + tm, :] + cw_ref[0:1, :] * xbuf[5:5 + tm, :]) + cb_ref[...]
    xbuf[0:8, :] = xbuf[tm:tm + 8, :]

    yb = y.astype(BF16)
    r = jax.nn.sigmoid(jnp.dot(yb, wa_ref[...], preferred_element_type=F32) + ba_ref[...])
    i = jax.nn.sigmoid(jnp.dot(yb, wx_ref[...], preferred_element_type=F32) + bx_ref[...])
    nl = -lam_ref[...]
    softplus = jnp.maximum(nl, 0.0) + jnp.log(1.0 + jnp.exp(-jnp.abs(nl)))
    a = jnp.exp((-LRU_C) * r * softplus)
    a_s[...] = a
    u_s[...] = jnp.sqrt(1.0 - a * a) * (i * y)

    qT_ref[0] = (jnp.dot(hb, w_ref[:, OFF_Q:OFF_KV], preferred_element_type=F32)
                 * (HEAD_DIM ** -0.5 * LOG2E)).T.astype(BF16)
    kv = jnp.dot(hb, w_ref[:, OFF_KV:OFF_GATE], preferred_element_type=F32)
    width = N_KV * hd
    vsT = kv[:, 3 * width:4 * width].T
    vwT = kv[:, 5 * width:6 * width].T
    for gi in range(N_KV):
        kc_ref[0, gi] = kv[:, gi * hd:(gi + 1) * hd]
        vc_ref[0, gi] = kv[:, width + gi * hd:width + (gi + 1) * hd]
        ks_ref[0, gi] = kv[:, 2 * width + gi * hd:2 * width + (gi + 1) * hd].astype(BF16)
        kw_ref[0, gi] = kv[:, 4 * width + gi * hd:4 * width + (gi + 1) * hd].astype(BF16)
        for j in range(tm // TK):
            vs_ref[0, gi, j] = vsT[gi * hd:(gi + 1) * hd, j * TK:(j + 1) * TK].astype(BF16)
            vw_ref[0, gi, j] = vwT[gi * hd:(gi + 1) * hd, j * TK:(j + 1) * TK].astype(BF16)
    gates = jax.nn.sigmoid(jnp.dot(hb, w_ref[:, OFF_GATE:D_IN_PAD], preferred_element_type=F32)).T
    for gi in range(N_KV):
        g_ref[0, gi] = gates[gi * GATE_ROWS:(gi + 1) * GATE_ROWS, :]
    g = jnp.dot(hb, w_ref[:, 0:D_RNN], preferred_element_type=F32)

    rows = lax.broadcasted_iota(I32, (8, D_RNN), 0)
    hprev = hcar[...]
    for gi in range(tm // 8):
        ag = a_s[gi * 8:gi * 8 + 8, :]
        ug = u_s[gi * 8:gi * 8 + 8, :]
        for k in (1, 2, 4):
            a_sh = jnp.where(rows >= k, pltpu.roll(ag, k, 0), 1.0)
            u_sh = jnp.where(rows >= k, pltpu.roll(ug, k, 0), 0.0)
            ug = ag * u_sh + ug
            ag = ag * a_sh
        hg = ag * hprev + ug
        h_s[gi * 8:gi * 8 + 8, :] = hg
        hprev = hg[7:8, :]
    hcar[...] = hprev

    rnn = _gelu_tanh(g) * h_s[...]
    rnn_ref[0] = (_rms(rnn) * grnn_ref[...]).astype(BF16)


def _inproj(x, gain, sc, sh, w_in, conv_w, conv_b, wa, ba, wx, bx, lam, grnn):
    B, S, D = x.shape
    tm = min(TM_IN, S)
    row = lambda n: pl.BlockSpec((1, n), lambda b, s: (0, 0))
    per_b = lambda n: pl.BlockSpec((1, 1, n), lambda b, s: (b, 0, 0))
    full = lambda a: pl.BlockSpec(a.shape, lambda b, s: (0,) * a.ndim)
    tok = lambda n: pl.BlockSpec((1, tm, n), lambda b, s: (b, s, 0))
    G = N_KV
    grp = pl.BlockSpec((1, G, tm, HEAD_DIM), lambda b, s: (b, 0, s, 0))
    grpT = pl.BlockSpec((1, G, tm // TK, HEAD_DIM, TK), lambda b, s: (b, 0, s, 0, 0))
    return pl.pallas_call(
        _inproj_kernel,
        grid=(B, S // tm),
        in_specs=[tok(D), row(D), per_b(D), per_b(D), full(w_in), full(conv_w), row(D_RNN),
                  full(wa), row(D_RNN), full(wx), row(D_RNN), row(D_RNN), row(D_RNN)],
        out_specs=[tok(D_RNN),
                   pl.BlockSpec((1, D_ATT, tm), lambda b, s: (b, 0, s)),
                   grp, grpT, grp, grpT, grp, grp,
                   pl.BlockSpec((1, G, GATE_ROWS, tm), lambda b, s: (b, 0, 0, s))],
        out_shape=[jax.ShapeDtypeStruct((B, S, D_RNN), BF16),
                   jax.ShapeDtypeStruct((B, D_ATT, S), BF16),
                   jax.ShapeDtypeStruct((B, G, S, HEAD_DIM), BF16),
                   jax.ShapeDtypeStruct((B, G, S // TK, HEAD_DIM, TK), BF16),
                   jax.ShapeDtypeStruct((B, G, S, HEAD_DIM), BF16),
                   jax.ShapeDtypeStruct((B, G, S // TK, HEAD_DIM, TK), BF16),
                   jax.ShapeDtypeStruct((B, G, S, HEAD_DIM), F32),
                   jax.ShapeDtypeStruct((B, G, S, HEAD_DIM), F32),
                   jax.ShapeDtypeStruct((B, G, GATE_ROWS, S), F32)],
        scratch_shapes=[pltpu.VMEM((tm + 8, D_RNN), F32), pltpu.VMEM((1, D_RNN), F32),
                        pltpu.VMEM((tm, D_RNN), F32), pltpu.VMEM((tm, D_RNN), F32),
                        pltpu.VMEM((tm, D_RNN), F32)],
        compiler_params=pltpu.CompilerParams(dimension_semantics=("arbitrary", "arbitrary"),
                                             vmem_limit_bytes=VMEM_LIMIT),
        name="inproj",
    )(x, gain, sc, sh, w_in, conv_w, conv_b, wa, ba, wx, bx, lam, grnn)


def _compress_kernel(rk_ref, rv_ref, w1k_ref, w2k_ref, pk_ref, w1v_ref, w2v_ref, pv_ref, kc_ref, vc_ref):
    def one(r_ref, w1_ref, w2_ref, pos_ref, o_ref):
        n = r_ref.shape[2] // CMP_STRIDE
        p = jnp.zeros((n, 2 * CMP_HIDDEN), F32)
        for l in range(CMP_STRIDE):
            rows = r_ref[0, 0, pl.ds(l, n, stride=CMP_STRIDE), :].astype(BF16)
            p = p + jnp.dot(rows, w1_ref[l * HEAD_DIM:(l + 1) * HEAD_DIM, :], preferred_element_type=F32)
        posb = jnp.dot(pos_ref[...], w1_ref[...], preferred_element_type=F32)
        bias = posb[0:1, 0:CMP_HIDDEN] + posb[1:2, CMP_HIDDEN:]
        pre = p[:, 0:CMP_HIDDEN] + pltpu.roll(p[:, CMP_HIDDEN:], n - 1, 0) + bias
        hid = _gelu_tanh(pre).astype(BF16)
        o_ref[0, 0] = jnp.dot(hid, w2_ref[...], preferred_element_type=F32).astype(BF16)

    one(rk_ref, w1k_ref, w2k_ref, pk_ref, kc_ref)
    one(rv_ref, w1v_ref, w2v_ref, pv_ref, vc_ref)


def _compress(rk, rv, w1k, w2k, posk, w1v, w2v, posv):
    B, G, S, width = rk.shape
    n = S // CMP_STRIDE
    blk = pl.BlockSpec((1, 1, S, width), lambda b, g: (b, g, 0, 0))
    full = lambda a: pl.BlockSpec(a.shape, lambda b, g: (0,) * a.ndim)
    out = pl.BlockSpec((1, 1, n, HEAD_DIM), lambda b, g: (b, g, 0, 0))
    return pl.pallas_call(
        _compress_kernel,
        grid=(B, G),
        in_specs=[blk, blk, full(w1k), full(w2k), full(posk), full(w1v), full(w2v), full(posv)],
        out_specs=[out, out],
        out_shape=[jax.ShapeDtypeStruct((B, G, n, HEAD_DIM), BF16)] * 2,
        compiler_params=pltpu.CompilerParams(vmem_limit_bytes=VMEM_LIMIT),
        name="compress",
    )(rk, rv, w1k, w2k, posk, w1v, w2v, posv)


def _attn_kernel(qT_ref, ks_ref, vs_ref, kw_ref, vw_ref, kc_ref, vcT_ref, g_ref,
                 o_ref,
                 kall, vall, qall, kca, m_all, acc_all, al_s, sbuf, pbuf, oacc, val_s, rank_s, selbias_s,
                 bias_tbl, cmask,
                 oT_s, wt_s, st_s, *, n_sel):
    S = ks_ref.shape[2]
    nkt = S // TK
    ncmp = kc_ref.shape[2]
    nblk = S // SLC_BLOCK
    cmp_per_tile = TQ // CMP_STRIDE
    b = pl.program_id(0)
    g = pl.program_id(1)
    qi = pl.program_id(2)
    q0 = qi * TQ
    W4 = REP * TQ

    @pl.when((b == 0) & (g == 0) & (qi == 0))
    def _():
        col = lax.broadcasted_iota(I32, (TK, KAUG), 1)
        is_blk = (col == POS_COL) | (col == POS_COL + 2)
        is_off = (col == POS_COL + 1) | (col == POS_COL + 3)
        ones_row = (lax.broadcasted_iota(I32, (VROWS - HEAD_DIM, TK), 0) == 0).astype(BF16)
        for kt in range(nkt):
            key = kt * TK + lax.broadcasted_iota(I32, (TK, KAUG), 0)
            pos = jnp.where(is_blk, (key // SLC_BLOCK).astype(F32),
                            jnp.where(is_off, (key % SLC_BLOCK).astype(F32), 0.0))
            onehot = (col - HEAD_DIM == key // SLC_BLOCK).astype(F32)
            kall[SEL, kt * TK:(kt + 1) * TK, :] = (pos + onehot).astype(BF16)
            kall[WIN, kt * TK:(kt + 1) * TK, :] = pos.astype(BF16)
            vall[SEL, kt, HEAD_DIM:VROWS, :] = ones_row
            vall[WIN, kt, HEAD_DIM:VROWS, :] = ones_row
        ko = lax.broadcasted_iota(I32, (TK, W4), 0)
        to = lax.broadcasted_iota(I32, (TK, W4), 1) % TQ
        bias_tbl[MASK_NONE] = jnp.zeros((TK, W4), F32)
        bias_tbl[MASK_CAUSAL] = jnp.where(ko <= to, 0.0, NEG)
        bias_tbl[MASK_WINDOW_LOW] = jnp.where(ko > to, 0.0, NEG)
        bias_tbl[MASK_ALL] = jnp.full((TK, W4), NEG, F32)
        u = lax.broadcasted_iota(I32, (2 * ncmp, W4), 0)
        tc = lax.broadcasted_iota(I32, (2 * ncmp, W4), 1) % TQ
        cmask[...] = jnp.where(u <= ncmp + ((tc + 1) // CMP_STRIDE) - 2, 0.0, NEG)
        cc = lax.broadcasted_iota(I32, (ncmp, KAUG), 0)
        colc = lax.broadcasted_iota(I32, (ncmp, KAUG), 1)
        kca[...] = jnp.where((colc == POS_COL) | (colc == POS_COL + 2), (cc // 4).astype(F32),
                             jnp.where((colc == POS_COL + 1) | (colc == POS_COL + 3),
                                       (CMP_STRIDE * (cc % 4)).astype(F32), 0.0)).astype(BF16)
        qall[...] = jnp.zeros(qall.shape, BF16)

    lane = lax.broadcasted_iota(I32, (1, W4), 1)
    head = g * REP + lane // TQ
    slope = lax.bitcast_convert_type((126 - head) << 23, F32)

    @pl.when(qi == 0)
    def _():
        for kt in range(nkt):
            kall[SEL, kt * TK:(kt + 1) * TK, 0:HEAD_DIM] = ks_ref[0, 0, kt * TK:(kt + 1) * TK, :]
            kall[WIN, kt * TK:(kt + 1) * TK, 0:HEAD_DIM] = kw_ref[0, 0, kt * TK:(kt + 1) * TK, :]
            vall[SEL, kt, 0:HEAD_DIM, :] = vs_ref[0, 0, kt]
            vall[WIN, kt, 0:HEAD_DIM, :] = vw_ref[0, 0, kt]
        kca[:, 0:HEAD_DIM] = kc_ref[0, 0]
        c_hi = (slope * LOG2E).astype(BF16).astype(F32)
        c_lo = slope * LOG2E - c_hi
        r16 = lax.broadcasted_iota(I32, (16, W4), 0)
        alibi = jnp.where(r16 == 0, c_hi * SLC_BLOCK, jnp.where(r16 == 1, c_hi, jnp.where(
            r16 == 2, c_lo * SLC_BLOCK, jnp.where(r16 == 3, c_lo, 0.0)))).astype(BF16)
        qall[SEL, POS_COL:POS_COL + 16, :] = alibi
        qall[WIN, POS_COL:POS_COL + 16, :] = alibi

    q = qT_ref[0]
    for r in range(REP):
        qr = q[r * HEAD_DIM:(r + 1) * HEAD_DIM, :]
        qall[SEL, 0:HEAD_DIM, r * TQ:(r + 1) * TQ] = qr
        qall[WIN, 0:HEAD_DIM, r * TQ:(r + 1) * TQ] = qr

    gall = g_ref[0, 0]
    gates = [gall[br * REP:(br + 1) * REP, :] for br in range(3)]

    def stage_scores(br, tiles_masks):
        col_max = None
        for h, (tile, mask) in enumerate(tiles_masks):
            r0 = pl.multiple_of(tile * TK, TK)
            s = jnp.dot(kall[br, pl.ds(r0, TK), :], qall[br], preferred_element_type=F32)
            if mask is not None:
                s = s + bias_tbl[mask]
            sbuf[h * TK:(h + 1) * TK, :] = s
            c = jnp.max(s, axis=0, keepdims=True)
            col_max = c if col_max is None else jnp.maximum(col_max, c)
        m_old = m_all[br]
        m_new = jnp.maximum(m_old, col_max)
        m_all[br] = m_new
        return jnp.exp2(m_old - m_new)

    def stage_probs(br, ntile):
        pbuf[0:ntile * TK, :] = jnp.exp2(sbuf[0:ntile * TK, :] - m_all[br]).astype(BF16)

    def stage_values(br, tiles, al_row):
        acc = acc_all[br] * al_row
        for h, tile in enumerate(tiles):
            acc = acc + jnp.dot(vall[br, tile], pbuf[h * TK:(h + 1) * TK, :], preferred_element_type=F32)
        acc_all[br] = acc

    def start_branch(br, tl, mask1):
        al_s[...] = stage_scores(br, [(tl[0], MASK_CAUSAL), (tl[1], mask1)])

    def finish_branch(br, n, tl):
        rest = jnp.maximum(n - 2, 0)
        npair = rest // 2

        def body(j, carry):
            al_prev = al_s[...]
            stage_probs(br, 2)
            al_s[...] = stage_scores(br, [(tl[2 * j], None), (tl[2 * j + 1], None)])
            stage_values(br, [tl[2 * j - 2], tl[2 * j - 1]], al_prev)
            return carry

        lax.fori_loop(1, 1 + npair, body, 0)
        last = 2 * npair

        @pl.when(rest % 2 == 1)
        def _():
            al_prev = al_s[...]
            stage_probs(br, 2)
            al_k = stage_scores(br, [(tl[n - 1], None)])
            stage_values(br, [tl[last], tl[last + 1]], al_prev)
            stage_probs(br, 1)
            stage_values(br, [tl[n - 1]], al_k)

        @pl.when(rest % 2 == 0)
        def _():
            stage_probs(br, 2)
            stage_values(br, [tl[last], tl[last + 1]], al_s[...])

    m_all[...] = jnp.full(m_all.shape, NEG, F32)
    acc_all[...] = jnp.zeros(acc_all.shape, F32)
    n_win = 1 + jnp.where(qi >= 2, 1, 0) + jnp.where(qi >= 1, 1, 0)
    wt_s[0] = qi
    wt_s[1] = jnp.maximum(qi - 2, 0)
    wt_s[2] = jnp.maximum(qi - 1, 0)
    start_branch(WIN, wt_s, jnp.where(qi >= 2, MASK_WINDOW_LOW, jnp.where(qi >= 1, MASK_NONE, MASK_ALL)))

    c0 = pl.multiple_of(ncmp - cmp_per_tile * qi, cmp_per_tile)
    sc = jnp.dot(kca[...], qall[WIN], preferred_element_type=F32) + cmask[pl.ds(c0, ncmp), :]
    e = jnp.exp2(sc - jnp.max(sc, axis=0, keepdims=True))
    tq = q0 + lane % TQ
    p = e * ((1.0 / jnp.sum(e, axis=0, keepdims=True)) * (tq >= CMP_LEN - 1).astype(F32))
    ocT = jnp.dot(vcT_ref[0, 0], p.astype(BF16), preferred_element_type=F32)
    for r in range(REP):
        oacc[:, r * TQ:(r + 1) * TQ] = gates[0][r:r + 1, :] * ocT[:, r * TQ:(r + 1) * TQ]

    psum = p[:, 0:TQ]
    for r in range(1, REP):
        psum = psum + p[:, r * TQ:(r + 1) * TQ]
    jj = lax.broadcasted_iota(I32, (nblk, ncmp), 0)
    cc = lax.broadcasted_iota(I32, (nblk, ncmp), 1)
    ovT = ((CMP_STRIDE * cc < SLC_BLOCK * jj + SLC_BLOCK)
           & (CMP_STRIDE * cc + CMP_LEN > SLC_BLOCK * jj)).astype(F32)
    impT = jnp.dot(ovT, psum, preferred_element_type=F32, precision=lax.Precision.HIGHEST)

    j_i = lax.broadcasted_iota(I32, (nblk, TQ), 0)
    t1 = q0 + lax.broadcasted_iota(I32, (nblk, TQ), 1)
    cur = t1 // SLC_BLOCK
    forced = (j_i == 0) | (j_i == cur) | (j_i == cur - 1)
    visible = SLC_BLOCK * j_i <= t1
    val_s[...] = jnp.where(forced, FORCED, jnp.where(visible, impT, NEG))
    ngrp = nblk // 8
    rank_s[...] = jnp.zeros(rank_s.shape, F32)
    j8 = lax.broadcasted_iota(I32, (8, TQ), 0)
    for ib in range(ngrp):
        @pl.when(8 * ib * SLC_BLOCK < q0 + TQ)
        def _(ib=ib):
            vals = [val_s[8 * jb:8 * jb + 8, :] for jb in range(ngrp)]
            ranks = [rank_s[8 * jb:8 * jb + 8, :] for jb in range(ngrp)]
            for i in range(8 * ib, 8 * ib + 8):
                row = jnp.broadcast_to(val_s[i:i + 1, :], (8, TQ))
                for jb in range(ngrp):
                    if jb > ib:
                        hit = row >= vals[jb]
                    elif jb < ib:
                        hit = row > vals[jb]
                    else:
                        hit = jnp.where(j8 > i - 8 * jb, jnp.where(row >= vals[jb], 1.0, 0.0),
                                        jnp.where(row > vals[jb], 1.0, 0.0)) > 0.5
                    ranks[jb] = ranks[jb] + jnp.where(hit, 1.0, 0.0)
            for jb in range(ngrp):
                rank_s[8 * jb:8 * jb + 8, :] = ranks[jb]
    ranks = [rank_s[8 * jb:8 * jb + 8, :] for jb in range(ngrp)]
    blocks_per_tile = TK // SLC_BLOCK
    st_s[0] = qi
    n_selt = jnp.int32(1)
    for jb in range(ngrp):
        chosen = ranks[jb] < n_sel
        selb = jnp.where(chosen, 0.0, NEG)
        for r in range(REP):
            selbias_s[8 * jb:8 * jb + 8, r * TQ:(r + 1) * TQ] = selb
        any_q = jnp.max(jnp.where(chosen, 1.0, 0.0), axis=1, keepdims=True)
        for hh in range(8 // blocks_per_tile):
            kt = (8 * jb) // blocks_per_tile + hh
            hit = jnp.max(any_q[hh * blocks_per_tile:(hh + 1) * blocks_per_tile, :]) > 0.5
            st_s[n_selt] = kt
            n_selt = n_selt + jnp.where(hit & (kt < qi), 1, 0)
    qall[SEL, HEAD_DIM:HEAD_DIM + nblk, :] = selbias_s[...].astype(BF16)

    finish_branch(WIN, n_win, wt_s)
    start_branch(SEL, st_s, jnp.where(n_selt >= 2, MASK_NONE, MASK_ALL))
    finish_branch(SEL, n_selt, st_s)

    def normalised(br):
        acc = acc_all[br]
        return acc[0:HEAD_DIM, :] / acc[HEAD_DIM:HEAD_DIM + 1, :]

    o_sel = normalised(SEL)
    o_win = normalised(WIN)
    for r in range(REP):
        lanes = slice(r * TQ, (r + 1) * TQ)
        oT_s[r * HEAD_DIM:(r + 1) * HEAD_DIM, :] = (oacc[:, lanes] + gates[1][r:r + 1, :] * o_sel[:, lanes]
                                                    + gates[2][r:r + 1, :] * o_win[:, lanes])
    o_ref[0] = oT_s[...].T.astype(BF16)


def _attention(qT, ks, vs, kw, vw, kc, vcT, gT):
    B, _, S = qT.shape
    G = N_KV
    nkt = S // TK
    ncmp = kc.shape[2]
    nblk = S // SLC_BLOCK
    n_sel = min(SLC_TOPK, nblk)
    W4 = REP * TQ
    res4 = lambda a: pl.BlockSpec((1, 1) + a.shape[2:], lambda b, g, i: (b, g) + (0,) * (a.ndim - 2))
    return pl.pallas_call(
        functools.partial(_attn_kernel, n_sel=n_sel),
        grid=(B, G, S // TQ),
        in_specs=[pl.BlockSpec((1, REP * HEAD_DIM, TQ), lambda b, g, i: (b, g, i)),
                  res4(ks), res4(vs), res4(kw), res4(vw), res4(kc), res4(vcT),
                  pl.BlockSpec((1, 1, GATE_ROWS, TQ), lambda b, g, i: (b, g, 0, i))],
        out_specs=pl.BlockSpec((1, TQ, REP * HEAD_DIM), lambda b, g, i: (b, i, g)),
        out_shape=jax.ShapeDtypeStruct((B, S, D_ATT), BF16),
        scratch_shapes=[pltpu.VMEM((2, S, KAUG), BF16),
                        pltpu.VMEM((2, nkt, VROWS, TK), BF16),
                        pltpu.VMEM((2, KAUG, W4), BF16),
                        pltpu.VMEM((ncmp, KAUG), BF16),
                        pltpu.VMEM((2, 1, W4), F32),
                        pltpu.VMEM((2, VROWS, W4), F32),
                        pltpu.VMEM((1, W4), F32),
                        pltpu.VMEM((2 * TK, W4), F32),
                        pltpu.VMEM((2 * TK, W4), BF16),
                        pltpu.VMEM((HEAD_DIM, W4), F32),
                        pltpu.VMEM((nblk, TQ), F32),
                        pltpu.VMEM((nblk, TQ), F32),
                        pltpu.VMEM((nblk, W4), F32),
                        pltpu.VMEM((4, TK, W4), F32),
                        pltpu.VMEM((2 * ncmp, W4), F32),
                        pltpu.VMEM((REP * HEAD_DIM, TQ), F32),
                        pltpu.SMEM((4,), I32),
                        pltpu.SMEM((nkt + 2,), I32)],
        compiler_params=pltpu.CompilerParams(dimension_semantics=("arbitrary", "arbitrary", "arbitrary"),
                                             vmem_limit_bytes=VMEM_LIMIT),
        name="attn",
    )(qT, ks, vs, kw, vw, kc, vcT, gT)


def _outmlp_kernel(x_ref, rnn_ref, att_ref, gatt_ref, wo_ref, gpost_ref, g1_ref, gpre_ref, sc2_ref, sh2_ref,
                   w1_ref, w2_ref, gpost2_ref, g2_ref, o_ref):
    att_n = (_rms(att_ref[0].astype(F32)) * gatt_ref[...]).astype(BF16)
    y = (jnp.dot(rnn_ref[0], wo_ref[0:D_RNN, :], preferred_element_type=F32)
         + jnp.dot(att_n, wo_ref[D_RNN:, :], preferred_element_type=F32))
    x1 = x_ref[0] + (1.0 + g1_ref[0]) * (_rms(y) * gpost_ref[...])
    h2 = (_rms(x1) * (gpre_ref[...] * (1.0 + sc2_ref[0])) + sh2_ref[0]).astype(BF16)
    fc = 1024
    ff = jnp.zeros(x1.shape, F32)
    for c in range(D_FF // fc):
        hid = jnp.maximum(jnp.dot(h2, w1_ref[:, c * fc:(c + 1) * fc], preferred_element_type=F32), 0.0)
        ff = ff + jnp.dot((hid * hid).astype(BF16), w2_ref[c * fc:(c + 1) * fc, :], preferred_element_type=F32)
    o_ref[0] = x1 + (1.0 + g2_ref[0]) * (_rms(ff) * gpost2_ref[...])


def _outmlp(x, rnn_n, att, gatt, wo, gpost, g1, gpre, sc2, sh2, w1, w2, gpost2, g2):
    B, S, D = x.shape
    tm = min(TM_OUT, S)
    row = lambda n: pl.BlockSpec((1, n), lambda b, s: (0, 0))
    per_b = lambda n: pl.BlockSpec((1, 1, n), lambda b, s: (b, 0, 0))
    const = lambda a: pl.BlockSpec(a.shape, lambda b, s: (0,) * a.ndim, pipeline_mode=pl.Buffered(1))
    tok = lambda n: pl.BlockSpec((1, tm, n), lambda b, s: (b, s, 0))
    return pl.pallas_call(
        _outmlp_kernel,
        grid=(B, S // tm),
        in_specs=[tok(D), tok(D_RNN), tok(D_ATT), row(D_ATT), const(wo), row(D), per_b(D), row(D),
                  per_b(D), per_b(D), const(w1), const(w2), row(D), per_b(D)],
        out_specs=tok(D),
        out_shape=jax.ShapeDtypeStruct((B, S, D), F32),
        compiler_params=pltpu.CompilerParams(dimension_semantics=("arbitrary", "arbitrary"),
                                             vmem_limit_bytes=VMEM_LIMIT),
        name="outmlp",
    )(x, rnn_n, att, gatt, wo, gpost, g1, gpre, sc2, sh2, w1, w2, gpost2, g2)


def _block_diag(w):
    n, k, _ = w.shape
    return jnp.einsum('nij,nm->nimj', w, jnp.eye(n, dtype=w.dtype)).reshape(n * k, n * k)


def _layer(x, c, ada_w, ada_b, pre_norm_mix, w_in, conv_w, conv_b, lru_wa, lru_ba, lru_wx, lru_bx, lru_lambda,
           cmp_pos_k, cmp_w1_k, cmp_w2_k, cmp_pos_v, cmp_w1_v, cmp_w2_v, norm_rnn_out, norm_att_out, w_out,
           post_norm_mix, pre_norm_mlp, w_ff1, w_ff2, post_norm_mlp):
    B, S, D = x.shape
    G = N_KV
    row = lambda v: v.reshape(1, -1)

    mod = _ada(c, ada_w, ada_b)
    sh1, sc1, g1, sh2, sc2, g2 = [m.reshape(B, 1, D) for m in jnp.split(mod, 6, axis=-1)]

    gate_cols = [OFF_GATE + br * N_HEADS + g * REP + r for g in range(G) for br in range(3) for r in range(REP)]
    w_gate = w_in[:, jnp.asarray(gate_cols)].reshape(D, G, 3 * REP)
    w_gate = jnp.pad(w_gate, ((0, 0), (0, 0), (0, GATE_ROWS - 3 * REP))).reshape(D, G * GATE_ROWS)
    w_in_p = jnp.concatenate([w_in[:, :OFF_GATE], jnp.pad(w_gate, ((0, 0), (0, GATE_PAD - G * GATE_ROWS)))],
                             axis=1).astype(BF16)
    wa = _block_diag(lru_wa).astype(BF16)
    wx = _block_diag(lru_wx).astype(BF16)
    half = CMP_LEN // 2 * HEAD_DIM

    def w1_cat(w1):
        return jnp.concatenate([w1[:half], w1[half:]], axis=1).astype(BF16)

    def pos_rows(pos):
        return jnp.pad(pos.reshape(2, half), ((0, 14), (0, 0))).astype(BF16)

    rnn_n, qT, ks, vsT, kw, vwT, kc_in, vc_in, gT = _inproj(
        x, row(pre_norm_mix), sc1, sh1, w_in_p, conv_w, row(conv_b), wa, row(lru_ba), wx, row(lru_bx),
        row(lru_lambda), row(norm_rnn_out))
    kc, vc = _compress(kc_in, vc_in, w1_cat(cmp_w1_k), cmp_w2_k.astype(BF16), pos_rows(cmp_pos_k),
                       w1_cat(cmp_w1_v), cmp_w2_v.astype(BF16), pos_rows(cmp_pos_v))
    att = _attention(qT, ks, vsT, kw, vwT, kc, vc.transpose(0, 1, 3, 2), gT)

    return _outmlp(x, rnn_n, att, row(norm_att_out), w_out.astype(BF16), row(post_norm_mix), g1,
                   row(pre_norm_mlp), sc2, sh2, w_ff1.astype(BF16), w_ff2.astype(BF16), row(post_norm_mlp), g2)


def kernel(x, c, ada_w, ada_b, pre_norm_mix, w_in, conv_w, conv_b, lru_wa, lru_ba, lru_wx, lru_bx, lru_lambda,
           cmp_pos_k, cmp_w1_k, cmp_w2_k, cmp_pos_v, cmp_w1_v, cmp_w2_v, norm_rnn_out, norm_att_out, w_out,
           post_norm_mix, pre_norm_mlp, w_ff1, w_ff2, post_norm_mlp):
    for l in range(ada_w.shape[0]):
        x = _layer(x, c, ada_w[l], ada_b[l], pre_norm_mix[l], w_in[l], conv_w[l], conv_b[l], lru_wa[l], lru_ba[l],
                   lru_wx[l], lru_bx[l], lru_lambda[l], cmp_pos_k[l], cmp_w1_k[l], cmp_w2_k[l], cmp_pos_v[l],
                   cmp_w1_v[l], cmp_w2_v[l], norm_rnn_out[l], norm_att_out[l], w_out[l], post_norm_mix[l],
                   pre_norm_mlp[l], w_ff1[l], w_ff2[l], post_norm_mlp[l])
    return x
```

```python
import functools

import jax
import jax.numpy as jnp
from jax import lax
from jax.experimental import pallas as pl
from jax.experimental.pallas import tpu as pltpu

F32 = jnp.float32
BF16 = jnp.bfloat16
I32 = jnp.int32

D_MODEL = 1024
D_RNN = 512
RNN_BLOCKS = 8
CONV_WIDTH = 4
LRU_C = 8.0
N_HEADS = 8
HEAD_DIM = 64
N_KV = 2
REP = N_HEADS // N_KV
D_ATT = N_HEADS * HEAD_DIM
CMP_LEN = 32
CMP_STRIDE = 16
CMP_HIDDEN = 256
SLC_BLOCK = 64
SLC_TOPK = 16
WINDOW = 512
D_FF = 4 * D_MODEL
EPS = 1e-6
NEG = -1e30
FORCED = 1e4
LOG2E = 1.4426950408889634

KV_COLS = 6 * N_KV * HEAD_DIM
N_GATE = 3 * N_HEADS
GATE_PAD = 128
GATE_ROWS = 16
OFF_Q = 2 * D_RNN
OFF_KV = OFF_Q + D_ATT
OFF_GATE = OFF_KV + KV_COLS
D_IN_PAD = OFF_GATE + GATE_PAD

TM_IN = 512
TM_OUT = 512
TQ = 256
TK = 256
KAUG = 256
VROWS = 80
VMEM_LIMIT = 56 * 1024 * 1024

SEL, WIN = 0, 1
MASK_NONE, MASK_CAUSAL, MASK_WINDOW_LOW, MASK_ALL = 0, 1, 2, 3
POS_COL = 2 * HEAD_DIM


def _gelu_tanh(x):
    return 0.5 * x * (1.0 + jnp.tanh(0.7978845608028654 * (x + 0.044715 * (x * x * x))))


def _rms(x):
    return x * lax.rsqrt(jnp.mean(x * x, axis=-1, keepdims=True) + EPS)


def _ada_kernel(c_ref, w_ref, b_ref, o_ref):
    c = c_ref[...]
    a = c * jax.nn.sigmoid(c)
    o_ref[...] = jnp.dot(a, w_ref[...], preferred_element_type=F32,
                         precision=lax.Precision.HIGHEST) + b_ref[...]


def _ada(c, w, b):
    B, D = c.shape
    N = w.shape[1]
    tn = 1024
    return pl.pallas_call(
        _ada_kernel,
        grid=(N // tn,),
        in_specs=[pl.BlockSpec((B, D), lambda j: (0, 0)),
                  pl.BlockSpec((D, tn), lambda j: (0, j)),
                  pl.BlockSpec((1, tn), lambda j: (0, j))],
        out_specs=pl.BlockSpec((B, tn), lambda j: (0, j)),
        out_shape=jax.ShapeDtypeStruct((B, N), F32),
        name="ada",
    )(c, w, b.reshape(1, N))


def _inproj_kernel(x_ref, gain_ref, sc_ref, sh_ref, w_ref, cw_ref, cb_ref, wa_ref, ba_ref, wx_ref, bx_ref,
                   lam_ref, grnn_ref,
                   rnn_ref, qT_ref, ks_ref, vs_ref, kw_ref, vw_ref, kc_ref, vc_ref, g_ref,
                   xbuf, hcar, a_s, u_s, h_s):
    tm = x_ref.shape[1]
    hd = HEAD_DIM

    @pl.when(pl.program_id(1) == 0)
    def _():
        xbuf[0:8, :] = jnp.zeros((8, D_RNN), F32)
        hcar[...] = jnp.zeros((1, D_RNN), F32)

    x = x_ref[0]
    h = _rms(x) * (gain_ref[...] * (1.0 + sc_ref[0])) + sh_ref[0]
    hb = h.astype(BF16)

    xr = jnp.dot(hb, w_ref[:, D_RNN:OFF_Q], preferred_element_type=F32)
    xbuf[8:8 + tm, :] = xr
    y = (cw_ref[3:4, :] * xr + cw_ref[2:3, :] * xbuf[7:7 + tm, :]
         + cw_ref[1:2, :] * xbuf[6:6 + tm, :] + cw_ref[0:1, :] * xbuf[5:5 + tm, :]) + cb_ref[...]
    xbuf[0:8, :] = xbuf[tm:tm + 8, :]

    yb = y.astype(BF16)
    r = jax.nn.sigmoid(jnp.dot(yb, wa_ref[...], preferred_element_type=F32) + ba_ref[...])
    i = jax.nn.sigmoid(jnp.dot(yb, wx_ref[...], preferred_element_type=F32) + bx_ref[...])
    nl = -lam_ref[...]
    softplus = jnp.maximum(nl, 0.0) + jnp.log(1.0 + jnp.exp(-jnp.abs(nl)))
    a = jnp.exp((-LRU_C) * r * softplus)
    a_s[...] = a
    u_s[...] = jnp.sqrt(1.0 - a * a) * (i * y)

    qT_ref[0] = (jnp.dot(hb, w_ref[:, OFF_Q:OFF_KV], preferred_element_type=F32)
                 * (HEAD_DIM ** -0.5 * LOG2E)).T.astype(BF16)
    kv = jnp.dot(hb, w_ref[:, OFF_KV:OFF_GATE], preferred_element_type=F32)
    width = N_KV * hd
    vsT = kv[:, 3 * width:4 * width].T
    vwT = kv[:, 5 * width:6 * width].T
    for gi in range(N_KV):
        kc_ref[0, gi] = kv[:, gi * hd:(gi + 1) * hd]
        vc_ref[0, gi] = kv[:, width + gi * hd:width + (gi + 1) * hd]
        ks_ref[0, gi] = kv[:, 2 * width + gi * hd:2 * width + (gi + 1) * hd].astype(BF16)
        kw_ref[0, gi] = kv[:, 4 * width + gi * hd:4 * width + (gi + 1) * hd].astype(BF16)
        for j in range(tm // TK):
            vs_ref[0, gi, j] = vsT[gi * hd:(gi + 1) * hd, j * TK:(j + 1) * TK].astype(BF16)
            vw_ref[0, gi, j] = vwT[gi * hd:(gi + 1) * hd, j * TK:(j + 1) * TK].astype(BF16)
    gates = jax.nn.sigmoid(jnp.dot(hb, w_ref[:, OFF_GATE:D_IN_PAD], preferred_element_type=F32)).T
    for gi in range(N_KV):
        g_ref[0, gi] = gates[gi * GATE_ROWS:(gi + 1) * GATE_ROWS, :]
    g = jnp.dot(hb, w_ref[:, 0:D_RNN], preferred_element_type=F32)

    rows = lax.broadcasted_iota(I32, (8, D_RNN), 0)
    hprev = hcar[...]
    for gi in range(tm // 8):
        ag = a_s[gi * 8:gi * 8 + 8, :]
        ug = u_s[gi * 8:gi * 8 + 8, :]
        for k in (1, 2, 4):
            a_sh = jnp.where(rows >= k, pltpu.roll(ag, k, 0), 1.0)
            u_sh = jnp.where(rows >= k, pltpu.roll(ug, k, 0), 0.0)
            ug = ag * u_sh + ug
            ag = ag * a_sh
        hg = ag * hprev + ug
        h_s[gi * 8:gi * 8 + 8, :] = hg
        hprev = hg[7:8, :]
    hcar[...] = hprev

    rnn = _gelu_tanh(g) * h_s[...]
    rnn_ref[0] = (_rms(rnn) * grnn_ref[...]).astype(BF16)


def _inproj(x, gain, sc, sh, w_in, conv_w, conv_b, wa, ba, wx, bx, lam, grnn):
    B, S, D = x.shape
    tm = min(TM_IN, S)
    row = lambda n: pl.BlockSpec((1, n), lambda b, s: (0, 0))
    per_b = lambda n: pl.BlockSpec((1, 1, n), lambda b, s: (b, 0, 0))
    full = lambda a: pl.BlockSpec(a.shape, lambda b, s: (0,) * a.ndim)
    tok = lambda n: pl.BlockSpec((1, tm, n), lambda b, s: (b, s, 0))
    G = N_KV
    grp = pl.BlockSpec((1, G, tm, HEAD_DIM), lambda b, s: (b, 0, s, 0))
    grpT = pl.BlockSpec((1, G, tm // TK, HEAD_DIM, TK), lambda b, s: (b, 0, s, 0, 0))
    return pl.pallas_call(
        _inproj_kernel,
        grid=(B, S // tm),
        in_specs=[tok(D), row(D), per_b(D), per_b(D), full(w_in), full(conv_w), row(D_RNN),
                  full(wa), row(D_RNN), full(wx), row(D_RNN), row(D_RNN), row(D_RNN)],
        out_specs=[tok(D_RNN),
                   pl.BlockSpec((1, D_ATT, tm), lambda b, s: (b, 0, s)),
                   grp, grpT, grp, grpT, grp, grp,
                   pl.BlockSpec((1, G, GATE_ROWS, tm), lambda b, s: (b, 0, 0, s))],
        out_shape=[jax.ShapeDtypeStruct((B, S, D_RNN), BF16),
                   jax.ShapeDtypeStruct((B, D_ATT, S), BF16),
                   jax.ShapeDtypeStruct((B, G, S, HEAD_DIM), BF16),
                   jax.ShapeDtypeStruct((B, G, S // TK, HEAD_DIM, TK), BF16),
                   jax.ShapeDtypeStruct((B, G, S, HEAD_DIM), BF16),
                   jax.ShapeDtypeStruct((B, G, S // TK, HEAD_DIM, TK), BF16),
                   jax.ShapeDtypeStruct((B, G, S, HEAD_DIM), F32),
                   jax.ShapeDtypeStruct((B, G, S, HEAD_DIM), F32),
                   jax.ShapeDtypeStruct((B, G, GATE_ROWS, S), F32)],
        scratch_shapes=[pltpu.VMEM((tm + 8, D_RNN), F32), pltpu.VMEM((1, D_RNN), F32),
                        pltpu.VMEM((tm, D_RNN), F32), pltpu.VMEM((tm, D_RNN), F32),
                        pltpu.VMEM((tm, D_RNN), F32)],
        compiler_params=pltpu.CompilerParams(dimension_semantics=("arbitrary", "arbitrary"),
                                             vmem_limit_bytes=VMEM_LIMIT),
        name="inproj",
    )(x, gain, sc, sh, w_in, conv_w, conv_b, wa, ba, wx, bx, lam, grnn)


def _compress_kernel(rk_ref, rv_ref, w1k_ref, w2k_ref, pk_ref, w1v_ref, w2v_ref, pv_ref, kc_ref, vc_ref):
    def one(r_ref, w1_ref, w2_ref, pos_ref, o_ref):
        n = r_ref.shape[2] // CMP_STRIDE
        p = jnp.zeros((n, 2 * CMP_HIDDEN), F32)
        for l in range(CMP_STRIDE):
            rows = r_ref[0, 0, pl.ds(l, n, stride=CMP_STRIDE), :].astype(BF16)
            p = p + jnp.dot(rows, w1_ref[l * HEAD_DIM:(l + 1) * HEAD_DIM, :], preferred_element_type=F32)
        posb = jnp.dot(pos_ref[...], w1_ref[...], preferred_element_type=F32)
        bias = posb[0:1, 0:CMP_HIDDEN] + posb[1:2, CMP_HIDDEN:]
        pre = p[:, 0:CMP_HIDDEN] + pltpu.roll(p[:, CMP_HIDDEN:], n - 1, 0) + bias
        hid = _gelu_tanh(pre).astype(BF16)
        o_ref[0, 0] = jnp.dot(hid, w2_ref[...], preferred_element_type=F32).astype(BF16)

    one(rk_ref, w1k_ref, w2k_ref, pk_ref, kc_ref)
    one(rv_ref, w1v_ref, w2v_ref, pv_ref, vc_ref)


def _compress(rk, rv, w1k, w2k, posk, w1v, w2v, posv):
    B, G, S, width = rk.shape
    n = S // CMP_STRIDE
    blk = pl.BlockSpec((1, 1, S, width), lambda b, g: (b, g, 0, 0))
    full = lambda a: pl.BlockSpec(a.shape, lambda b, g: (0,) * a.ndim)
    out = pl.BlockSpec((1, 1, n, HEAD_DIM), lambda b, g: (b, g, 0, 0))
    return pl.pallas_call(
        _compress_kernel,
        grid=(B, G),
        in_specs=[blk, blk, full(w1k), full(w2k), full(posk), full(w1v), full(w2v), full(posv)],
        out_specs=[out, out],
        out_shape=[jax.ShapeDtypeStruct((B, G, n, HEAD_DIM), BF16)] * 2,
        compiler_params=pltpu.CompilerParams(vmem_limit_bytes=VMEM_LIMIT),
        name="compress",
    )(rk, rv, w1k, w2k, posk, w1v, w2v, posv)


def _attn_kernel(qT_ref, ks_ref, vs_ref, kw_ref, vw_ref, kc_ref, vcT_ref, g_ref,
                 o_ref,
                 kall, vall, qall, kca, m_all, acc_all, al_s, sbuf, pbuf, oacc, val_s, rank_s, selbias_s,
                 bias_tbl, cmask,
                 oT_s, st_s, *, n_sel):
    S = ks_ref.shape[2]
    nkt = S // TK
    ncmp = kc_ref.shape[2]
    nblk = S // SLC_BLOCK
    cmp_per_tile = TQ // CMP_STRIDE
    b = pl.program_id(0)
    g = pl.program_id(1)
    qi = pl.program_id(2)
    q0 = qi * TQ
    W4 = REP * TQ

    @pl.when((b == 0) & (g == 0) & (qi == 0))
    def _():
        col = lax.broadcasted_iota(I32, (TK, KAUG), 1)
        is_blk = (col == POS_COL) | (col == POS_COL + 2)
        is_off = (col == POS_COL + 1) | (col == POS_COL + 3)
        ones_row = (lax.broadcasted_iota(I32, (VROWS - HEAD_DIM, TK), 0) == 0).astype(BF16)
        for kt in range(nkt):
            key = kt * TK + lax.broadcasted_iota(I32, (TK, KAUG), 0)
            pos = jnp.where(is_blk, (key // SLC_BLOCK).astype(F32),
                            jnp.where(is_off, (key % SLC_BLOCK).astype(F32), 0.0))
            onehot = (col - HEAD_DIM == key // SLC_BLOCK).astype(F32)
            kall[SEL, kt * TK:(kt + 1) * TK, :] = (pos + onehot).astype(BF16)
            kall[WIN, kt * TK:(kt + 1) * TK, :] = pos.astype(BF16)
            vall[SEL, kt, HEAD_DIM:VROWS, :] = ones_row
            vall[WIN, kt, HEAD_DIM:VROWS, :] = ones_row
        ko = lax.broadcasted_iota(I32, (TK, W4), 0)
        to = lax.broadcasted_iota(I32, (TK, W4), 1) % TQ
        bias_tbl[MASK_NONE] = jnp.zeros((TK, W4), F32)
        bias_tbl[MASK_CAUSAL] = jnp.where(ko <= to, 0.0, NEG)
        bias_tbl[MASK_WINDOW_LOW] = jnp.where(ko > to, 0.0, NEG)
        bias_tbl[MASK_ALL] = jnp.full((TK, W4), NEG, F32)
        u = lax.broadcasted_iota(I32, (2 * ncmp, W4), 0)
        tc = lax.broadcasted_iota(I32, (2 * ncmp, W4), 1) % TQ
        cmask[...] = jnp.where(u <= ncmp + ((tc + 1) // CMP_STRIDE) - 2, 0.0, NEG)
        cc = lax.broadcasted_iota(I32, (ncmp, KAUG), 0)
        colc = lax.broadcasted_iota(I32, (ncmp, KAUG), 1)
        kca[...] = jnp.where((colc == POS_COL) | (colc == POS_COL + 2), (cc // 4).astype(F32),
                             jnp.where((colc == POS_COL + 1) | (colc == POS_COL + 3),
                                       (CMP_STRIDE * (cc % 4)).astype(F32), 0.0)).astype(BF16)
        qall[...] = jnp.zeros(qall.shape, BF16)

    lane = lax.broadcasted_iota(I32, (1, W4), 1)
    head = g * REP + lane // TQ
    slope = lax.bitcast_convert_type((126 - head) << 23, F32)

    @pl.when(qi == 0)
    def _():
        for kt in range(nkt):
            kall[SEL, kt * TK:(kt + 1) * TK, 0:HEAD_DIM] = ks_ref[0, 0, kt * TK:(kt + 1) * TK, :]
            kall[WIN, kt * TK:(kt + 1) * TK, 0:HEAD_DIM] = kw_ref[0, 0, kt * TK:(kt + 1) * TK, :]
            vall[SEL, kt, 0:HEAD_DIM, :] = vs_ref[0, 0, kt]
            vall[WIN, kt, 0:HEAD_DIM, :] = vw_ref[0, 0, kt]
        kca[:, 0:HEAD_DIM] = kc_ref[0, 0]
        c_hi = (slope * LOG2E).astype(BF16).astype(F32)
        c_lo = slope * LOG2E - c_hi
        r16 = lax.broadcasted_iota(I32, (16, W4), 0)
        alibi = jnp.where(r16 == 0, c_hi * SLC_BLOCK, jnp.where(r16 == 1, c_hi, jnp.where(
            r16 == 2, c_lo * SLC_BLOCK, jnp.where(r16 == 3, c_lo, 0.0)))).astype(BF16)
        qall[SEL, POS_COL:POS_COL + 16, :] = alibi
        qall[WIN, POS_COL:POS_COL + 16, :] = alibi

    q = qT_ref[0]
    for r in range(REP):
        qr = q[r * HEAD_DIM:(r + 1) * HEAD_DIM, :]
        qall[SEL, 0:HEAD_DIM, r * TQ:(r + 1) * TQ] = qr
        qall[WIN, 0:HEAD_DIM, r * TQ:(r + 1) * TQ] = qr

    gall = g_ref[0, 0]
    gates = [gall[br * REP:(br + 1) * REP, :] for br in range(3)]

    def stage_scores(br, tiles_masks):
        col_max = None
        for h, (tile, mask) in enumerate(tiles_masks):
            r0 = pl.multiple_of(tile * TK, TK)
            s = jnp.dot(kall[br, pl.ds(r0, TK), :], qall[br], preferred_element_type=F32)
            if mask is not None:
                s = s + bias_tbl[mask]
            sbuf[h * TK:(h + 1) * TK, :] = s
            c = jnp.max(s, axis=0, keepdims=True)
            col_max = c if col_max is None else jnp.maximum(col_max, c)
        m_old = m_all[br]
        m_new = jnp.maximum(m_old, col_max)
        m_all[br] = m_new
        return jnp.exp2(m_old - m_new)

    def stage_probs(br, ntile):
        pbuf[0:ntile * TK, :] = jnp.exp2(sbuf[0:ntile * TK, :] - m_all[br]).astype(BF16)

    def stage_values(br, tiles, al_row):
        acc = acc_all[br] * al_row
        for h, tile in enumerate(tiles):
            acc = acc + jnp.dot(vall[br, tile], pbuf[h * TK:(h + 1) * TK, :], preferred_element_type=F32)
        acc_all[br] = acc

    def start_branch(br, tl, mask1):
        al_s[...] = stage_scores(br, [(tl[0], MASK_CAUSAL), (tl[1], mask1)])

    def finish_branch(br, n, tl):
        rest = jnp.maximum(n - 2, 0)
        npair = rest // 2

        def body(j, carry):
            al_prev = al_s[...]
            stage_probs(br, 2)
            al_s[...] = stage_scores(br, [(tl[2 * j], None), (tl[2 * j + 1], None)])
            stage_values(br, [tl[2 * j - 2], tl[2 * j - 1]], al_prev)
            return carry

        lax.fori_loop(1, 1 + npair, body, 0)
        last = 2 * npair

        @pl.when(rest % 2 == 1)
        def _():
            al_prev = al_s[...]
            stage_probs(br, 2)
            al_k = stage_scores(br, [(tl[n - 1], None)])
            stage_values(br, [tl[last], tl[last + 1]], al_prev)
            stage_probs(br, 1)
            stage_values(br, [tl[n - 1]], al_k)

        @pl.when(rest % 2 == 0)
        def _():
            stage_probs(br, 2)
            stage_values(br, [tl[last], tl[last + 1]], al_s[...])

    m_all[...] = jnp.full(m_all.shape, NEG, F32)
    acc_all[...] = jnp.zeros(acc_all.shape, F32)
    w1 = jnp.maximum(qi - 2, 0)
    w2 = jnp.maximum(qi - 1, 0)
    al_w0 = stage_scores(WIN, [(qi, MASK_CAUSAL),
                               (w1, jnp.where(qi >= 2, MASK_WINDOW_LOW, jnp.where(qi >= 1, MASK_NONE, MASK_ALL)))])

    c0 = pl.multiple_of(ncmp - cmp_per_tile * qi, cmp_per_tile)
    sc = jnp.dot(kca[...], qall[WIN], preferred_element_type=F32) + cmask[pl.ds(c0, ncmp), :]
    e = jnp.exp2(sc - jnp.max(sc, axis=0, keepdims=True))
    tq = q0 + lane % TQ
    p = e * ((1.0 / jnp.sum(e, axis=0, keepdims=True)) * (tq >= CMP_LEN - 1).astype(F32))
    ocT = jnp.dot(vcT_ref[0, 0], p.astype(BF16), preferred_element_type=F32)
    for r in range(REP):
        oacc[:, r * TQ:(r + 1) * TQ] = gates[0][r:r + 1, :] * ocT[:, r * TQ:(r + 1) * TQ]

    stage_probs(WIN, 2)
    al_w1 = stage_scores(WIN, [(w2, jnp.where(qi >= 2, MASK_NONE, MASK_ALL))])
    stage_values(WIN, [qi, w1], al_w0)

    psum = p[:, 0:TQ]
    for r in range(1, REP):
        psum = psum + p[:, r * TQ:(r + 1) * TQ]
    jj = lax.broadcasted_iota(I32, (nblk, ncmp), 0)
    cc = lax.broadcasted_iota(I32, (nblk, ncmp), 1)
    ovT = ((CMP_STRIDE * cc < SLC_BLOCK * jj + SLC_BLOCK)
           & (CMP_STRIDE * cc + CMP_LEN > SLC_BLOCK * jj)).astype(F32)
    impT = jnp.dot(ovT, psum, preferred_element_type=F32, precision=lax.Precision.HIGHEST)

    j_i = lax.broadcasted_iota(I32, (nblk, TQ), 0)
    t1 = q0 + lax.broadcasted_iota(I32, (nblk, TQ), 1)
    cur = t1 // SLC_BLOCK
    forced = (j_i == 0) | (j_i == cur) | (j_i == cur - 1)
    visible = SLC_BLOCK * j_i <= t1
    val_s[...] = jnp.where(forced, FORCED, jnp.where(visible, impT, NEG))
    ngrp = nblk // 8
    rank_s[...] = jnp.zeros(rank_s.shape, F32)
    j8 = lax.broadcasted_iota(I32, (8, TQ), 0)
    for ib in range(ngrp):
        @pl.when(8 * ib * SLC_BLOCK < q0 + TQ)
        def _(ib=ib):
            vals = [val_s[8 * jb:8 * jb + 8, :] for jb in range(ngrp)]
            ranks = [rank_s[8 * jb:8 * jb + 8, :] for jb in range(ngrp)]
            for i in range(8 * ib, 8 * ib + 8):
                row = jnp.broadcast_to(val_s[i:i + 1, :], (8, TQ))
                for jb in range(ngrp):
                    if jb > ib:
                        hit = row >= vals[jb]
                    elif jb < ib:
                        hit = row > vals[jb]
                    else:
                        hit = jnp.where(j8 > i - 8 * jb, jnp.where(row >= vals[jb], 1.0, 0.0),
                                        jnp.where(row > vals[jb], 1.0, 0.0)) > 0.5
                    ranks[jb] = ranks[jb] + jnp.where(hit, 1.0, 0.0)
            for jb in range(ngrp):
                rank_s[8 * jb:8 * jb + 8, :] = ranks[jb]
    ranks = [rank_s[8 * jb:8 * jb + 8, :] for jb in range(ngrp)]
    blocks_per_tile = TK // SLC_BLOCK
    st_s[0] = qi
    n_selt = jnp.int32(1)
    for jb in range(ngrp):
        chosen = ranks[jb] < n_sel
        selb = jnp.where(chosen, 0.0, NEG)
        for r in range(REP):
            selbias_s[8 * jb:8 * jb + 8, r * TQ:(r + 1) * TQ] = selb
        any_q = jnp.max(jnp.where(chosen, 1.0, 0.0), axis=1, keepdims=True)
        for hh in range(8 // blocks_per_tile):
            kt = (8 * jb) // blocks_per_tile + hh
            hit = jnp.max(any_q[hh * blocks_per_tile:(hh + 1) * blocks_per_tile, :]) > 0.5
            st_s[n_selt] = kt
            n_selt = n_selt + jnp.where(hit & (kt < qi), 1, 0)
    qall[SEL, HEAD_DIM:HEAD_DIM + nblk, :] = selbias_s[...].astype(BF16)

    stage_probs(WIN, 1)
    stage_values(WIN, [w2], al_w1)
    start_branch(SEL, st_s, jnp.where(n_selt >= 2, MASK_NONE, MASK_ALL))
    finish_branch(SEL, n_selt, st_s)

    def normalised(br):
        acc = acc_all[br]
        return acc[0:HEAD_DIM, :] / acc[HEAD_DIM:HEAD_DIM + 1, :]

    o_sel = normalised(SEL)
    o_win = normalised(WIN)
    for r in range(REP):
        lanes = slice(r * TQ, (r + 1) * TQ)
        oT_s[r * HEAD_DIM:(r + 1) * HEAD_DIM, :] = (oacc[:, lanes] + gates[1][r:r + 1, :] * o_sel[:, lanes]
                                                    + gates[2][r:r + 1, :] * o_win[:, lanes])
    o_ref[0] = oT_s[...].T.astype(BF16)


def _attention(qT, ks, vs, kw, vw, kc, vcT, gT):
    B, _, S = qT.shape
    G = N_KV
    nkt = S // TK
    ncmp = kc.shape[2]
    nblk = S // SLC_BLOCK
    n_sel = min(SLC_TOPK, nblk)
    W4 = REP * TQ
    res4 = lambda a: pl.BlockSpec((1, 1) + a.shape[2:], lambda b, g, i: (b, g) + (0,) * (a.ndim - 2))
    return pl.pallas_call(
        functools.partial(_attn_kernel, n_sel=n_sel),
        grid=(B, G, S // TQ),
        in_specs=[pl.BlockSpec((1, REP * HEAD_DIM, TQ), lambda b, g, i: (b, g, i)),
                  res4(ks), res4(vs), res4(kw), res4(vw), res4(kc), res4(vcT),
                  pl.BlockSpec((1, 1, GATE_ROWS, TQ), lambda b, g, i: (b, g, 0, i))],
        out_specs=pl.BlockSpec((1, TQ, REP * HEAD_DIM), lambda b, g, i: (b, i, g)),
        out_shape=jax.ShapeDtypeStruct((B, S, D_ATT), BF16),
        scratch_shapes=[pltpu.VMEM((2, S, KAUG), BF16),
                        pltpu.VMEM((2, nkt, VROWS, TK), BF16),
                        pltpu.VMEM((2, KAUG, W4), BF16),
                        pltpu.VMEM((ncmp, KAUG), BF16),
                        pltpu.VMEM((2, 1, W4), F32),
                        pltpu.VMEM((2, VROWS, W4), F32),
                        pltpu.VMEM((1, W4), F32),
                        pltpu.VMEM((2 * TK, W4), F32),
                        pltpu.VMEM((2 * TK, W4), BF16),
                        pltpu.VMEM((HEAD_DIM, W4), F32),
                        pltpu.VMEM((nblk, TQ), F32),
                        pltpu.VMEM((nblk, TQ), F32),
                        pltpu.VMEM((nblk, W4), F32),
                        pltpu.VMEM((4, TK, W4), F32),
                        pltpu.VMEM((2 * ncmp, W4), F32),
                        pltpu.VMEM((REP * HEAD_DIM, TQ), F32),
                        pltpu.SMEM((nkt + 2,), I32)],
        compiler_params=pltpu.CompilerParams(dimension_semantics=("arbitrary", "arbitrary", "arbitrary"),
                                             vmem_limit_bytes=VMEM_LIMIT),
        name="attn",
    )(qT, ks, vs, kw, vw, kc, vcT, gT)


def _outmlp_kernel(x_ref, rnn_ref, att_ref, gatt_ref, wo_ref, gpost_ref, g1_ref, gpre_ref, sc2_ref, sh2_ref,
                   w1_ref, w2_ref, gpost2_ref, g2_ref, o_ref):
    att_n = (_rms(att_ref[0].astype(F32)) * gatt_ref[...]).astype(BF16)
    y = (jnp.dot(rnn_ref[0], wo_ref[0:D_RNN, :], preferred_element_type=F32)
         + jnp.dot(att_n, wo_ref[D_RNN:, :], preferred_element_type=F32))
    x1 = x_ref[0] + (1.0 + g1_ref[0]) * (_rms(y) * gpost_ref[...])
    h2 = (_rms(x1) * (gpre_ref[...] * (1.0 + sc2_ref[0])) + sh2_ref[0]).astype(BF16)
    fc = 1024
    ff = jnp.zeros(x1.shape, F32)
    for c in range(D_FF // fc):
        hid = jnp.maximum(jnp.dot(h2, w1_ref[:, c * fc:(c + 1) * fc], preferred_element_type=F32), 0.0)
        ff = ff + jnp.dot((hid * hid).astype(BF16), w2_ref[c * fc:(c + 1) * fc, :], preferred_element_type=F32)
    o_ref[0] = x1 + (1.0 + g2_ref[0]) * (_rms(ff) * gpost2_ref[...])


def _outmlp(x, rnn_n, att, gatt, wo, gpost, g1, gpre, sc2, sh2, w1, w2, gpost2, g2):
    B, S, D = x.shape
    tm = min(TM_OUT, S)
    row = lambda n: pl.BlockSpec((1, n), lambda b, s: (0, 0))
    per_b = lambda n: pl.BlockSpec((1, 1, n), lambda b, s: (b, 0, 0))
    const = lambda a: pl.BlockSpec(a.shape, lambda b, s: (0,) * a.ndim, pipeline_mode=pl.Buffered(1))
    tok = lambda n: pl.BlockSpec((1, tm, n), lambda b, s: (b, s, 0))
    return pl.pallas_call(
        _outmlp_kernel,
        grid=(B, S // tm),
        in_specs=[tok(D), tok(D_RNN), tok(D_ATT), row(D_ATT), const(wo), row(D), per_b(D), row(D),
                  per_b(D), per_b(D), const(w1), const(w2), row(D), per_b(D)],
        out_specs=tok(D),
        out_shape=jax.ShapeDtypeStruct((B, S, D), F32),
        compiler_params=pltpu.CompilerParams(dimension_semantics=("arbitrary", "arbitrary"),
                                             vmem_limit_bytes=VMEM_LIMIT),
        name="outmlp",
    )(x, rnn_n, att, gatt, wo, gpost, g1, gpre, sc2, sh2, w1, w2, gpost2, g2)


def _block_diag(w):
    n, k, _ = w.shape
    return jnp.einsum('nij,nm->nimj', w, jnp.eye(n, dtype=w.dtype)).reshape(n * k, n * k)


def _layer(x, c, ada_w, ada_b, pre_norm_mix, w_in, conv_w, conv_b, lru_wa, lru_ba, lru_wx, lru_bx, lru_lambda,
           cmp_pos_k, cmp_w1_k, cmp_w2_k, cmp_pos_v, cmp_w1_v, cmp_w2_v, norm_rnn_out, norm_att_out, w_out,
           post_norm_mix, pre_norm_mlp, w_ff1, w_ff2, post_norm_mlp):
    B, S, D = x.shape
    G = N_KV
    row = lambda v: v.reshape(1, -1)

    mod = _ada(c, ada_w, ada_b)
    sh1, sc1, g1, sh2, sc2, g2 = [m.reshape(B, 1, D) for m in jnp.split(mod, 6, axis=-1)]

    gate_cols = [OFF_GATE + br * N_HEADS + g * REP + r for g in range(G) for br in range(3) for r in range(REP)]
    w_gate = w_in[:, jnp.asarray(gate_cols)].reshape(D, G, 3 * REP)
    w_gate = jnp.pad(w_gate, ((0, 0), (0, 0), (0, GATE_ROWS - 3 * REP))).reshape(D, G * GATE_ROWS)
    w_in_p = jnp.concatenate([w_in[:, :OFF_GATE], jnp.pad(w_gate, ((0, 0), (0, GATE_PAD - G * GATE_ROWS)))],
                             axis=1).astype(BF16)
    wa = _block_diag(lru_wa).astype(BF16)
    wx = _block_diag(lru_wx).astype(BF16)
    half = CMP_LEN // 2 * HEAD_DIM

    def w1_cat(w1):
        return jnp.concatenate([w1[:half], w1[half:]], axis=1).astype(BF16)

    def pos_rows(pos):
        return jnp.pad(pos.reshape(2, half), ((0, 14), (0, 0))).astype(BF16)

    rnn_n, qT, ks, vsT, kw, vwT, kc_in, vc_in, gT = _inproj(
        x, row(pre_norm_mix), sc1, sh1, w_in_p, conv_w, row(conv_b), wa, row(lru_ba), wx, row(lru_bx),
        row(lru_lambda), row(norm_rnn_out))
    kc, vc = _compress(kc_in, vc_in, w1_cat(cmp_w1_k), cmp_w2_k.astype(BF16), pos_rows(cmp_pos_k),
                       w1_cat(cmp_w1_v), cmp_w2_v.astype(BF16), pos_rows(cmp_pos_v))
    att = _attention(qT, ks, vsT, kw, vwT, kc, vc.transpose(0, 1, 3, 2), gT)

    return _outmlp(x, rnn_n, att, row(norm_att_out), w_out.astype(BF16), row(post_norm_mix), g1,
                   row(pre_norm_mlp), sc2, sh2, w_ff1.astype(BF16), w_ff2.astype(BF16), row(post_norm_mlp), g2)


def kernel(x, c, ada_w, ada_b, pre_norm_mix, w_in, conv_w, conv_b, lru_wa, lru_ba, lru_wx, lru_bx, lru_lambda,
           cmp_pos_k, cmp_w1_k, cmp_w2_k, cmp_pos_v, cmp_w1_v, cmp_w2_v, norm_rnn_out, norm_att_out, w_out,
           post_norm_mix, pre_norm_mlp, w_ff1, w_ff2, post_norm_mlp):
    for l in range(ada_w.shape[0]):
        x = _layer(x, c, ada_w[l], ada_b[l], pre_norm_mix[l], w_in[l], conv_w[l], conv_b[l], lru_wa[l], lru_ba[l],
                   lru_wx[l], lru_bx[l], lru_lambda[l], cmp_pos_k[l], cmp_w1_k[l], cmp_w2_k[l], cmp_pos_v[l],
                   cmp_w1_v[l], cmp_w2_v[l], norm_rnn_out[l], norm_att_out[l], w_out[l], post_norm_mix[l],
                   pre_norm_mlp[l], w_ff1[l], w_ff2[l], post_norm_mlp[l])
    return x
```

```python
import functools

import jax
import jax.numpy as jnp
from jax import lax
from jax.experimental import pallas as pl
from jax.experimental.pallas import tpu as pltpu

F32 = jnp.float32
BF16 = jnp.bfloat16
I32 = jnp.int32

D_MODEL = 1024
D_RNN = 512
RNN_BLOCKS = 8
CONV_WIDTH = 4
LRU_C = 8.0
N_HEADS = 8
HEAD_DIM = 64
N_KV = 2
REP = N_HEADS // N_KV
D_ATT = N_HEADS * HEAD_DIM
CMP_LEN = 32
CMP_STRIDE = 16
CMP_HIDDEN = 256
SLC_BLOCK = 64
SLC_TOPK = 16
WINDOW = 512
D_FF = 4 * D_MODEL
EPS = 1e-6
NEG = -1e30
FORCED = 1e4
LOG2E = 1.4426950408889634

KV_COLS = 6 * N_KV * HEAD_DIM
N_GATE = 3 * N_HEADS
GATE_PAD = 128
GATE_ROWS = 16
OFF_Q = 2 * D_RNN
OFF_KV = OFF_Q + D_ATT
OFF_GATE = OFF_KV + KV_COLS
D_IN_PAD = OFF_GATE + GATE_PAD

TM_IN = 512
TM_OUT = 512
TQ = 256
TK = 256
KAUG = 256
VROWS = 80
VMEM_LIMIT = 56 * 1024 * 1024

SEL, WIN = 0, 1
MASK_NONE, MASK_CAUSAL, MASK_WINDOW_LOW, MASK_ALL = 0, 1, 2, 3
POS_COL = 2 * HEAD_DIM


def _gelu_tanh(x):
    return 0.5 * x * (1.0 + jnp.tanh(0.7978845608028654 * (x + 0.044715 * (x * x * x))))


def _rms(x):
    return x * lax.rsqrt(jnp.mean(x * x, axis=-1, keepdims=True) + EPS)


def _ada_kernel(c_ref, w_ref, b_ref, o_ref):
    c = c_ref[...]
    a = c * jax.nn.sigmoid(c)
    o_ref[...] = jnp.dot(a, w_ref[...], preferred_element_type=F32,
                         precision=lax.Precision.HIGHEST) + b_ref[...]


def _ada(c, w, b):
    B, D = c.shape
    N = w.shape[1]
    tn = 1024
    return pl.pallas_call(
        _ada_kernel,
        grid=(N // tn,),
        in_specs=[pl.BlockSpec((B, D), lambda j: (0, 0)),
                  pl.BlockSpec((D, tn), lambda j: (0, j)),
                  pl.BlockSpec((1, tn), lambda j: (0, j))],
        out_specs=pl.BlockSpec((B, tn), lambda j: (0, j)),
        out_shape=jax.ShapeDtypeStruct((B, N), F32),
        name="ada",
    )(c, w, b.reshape(1, N))


def _inproj_kernel(x_ref, gain_ref, sc_ref, sh_ref, w_ref, cw_ref, cb_ref, wa_ref, ba_ref, wx_ref, bx_ref,
                   lam_ref, grnn_ref,
                   rnn_ref, qT_ref, ks_ref, vs_ref, kw_ref, vw_ref, kc_ref, vc_ref, g_ref,
                   xbuf, hcar, a_s, u_s, h_s):
    tm = x_ref.shape[1]
    hd = HEAD_DIM

    @pl.when(pl.program_id(1) == 0)
    def _():
        xbuf[0:8, :] = jnp.zeros((8, D_RNN), F32)
        hcar[...] = jnp.zeros((1, D_RNN), F32)

    x = x_ref[0]
    h = _rms(x) * (gain_ref[...] * (1.0 + sc_ref[0])) + sh_ref[0]
    hb = h.astype(BF16)

    xr = jnp.dot(hb, w_ref[:, D_RNN:OFF_Q], preferred_element_type=F32)
    xbuf[8:8 + tm, :] = xr
    y = (cw_ref[3:4, :] * xr + cw_ref[2:3, :] * xbuf[7:7 + tm, :]
         + cw_ref[1:2, :] * xbuf[6:6 + tm, :] + cw_ref[0:1, :] * xbuf[5:5 + tm, :]) + cb_ref[...]
    xbuf[0:8, :] = xbuf[tm:tm + 8, :]

    yb = y.astype(BF16)
    r = jax.nn.sigmoid(jnp.dot(yb, wa_ref[...], preferred_element_type=F32) + ba_ref[...])
    i = jax.nn.sigmoid(jnp.dot(yb, wx_ref[...], preferred_element_type=F32) + bx_ref[...])
    nl = -lam_ref[...]
    softplus = jnp.maximum(nl, 0.0) + jnp.log(1.0 + jnp.exp(-jnp.abs(nl)))
    a = jnp.exp((-LRU_C) * r * softplus)
    a_s[...] = a
    u_s[...] = jnp.sqrt(1.0 - a * a) * (i * y)

    qT_ref[0] = (jnp.dot(hb, w_ref[:, OFF_Q:OFF_KV], preferred_element_type=F32)
                 * (HEAD_DIM ** -0.5 * LOG2E)).T.astype(BF16)
    kv = jnp.dot(hb, w_ref[:, OFF_KV:OFF_GATE], preferred_element_type=F32)
    width = N_KV * hd
    vsT = kv[:, 3 * width:4 * width].T
    vwT = kv[:, 5 * width:6 * width].T
    for gi in range(N_KV):
        kc_ref[0, gi] = kv[:, gi * hd:(gi + 1) * hd]
        vc_ref[0, gi] = kv[:, width + gi * hd:width + (gi + 1) * hd]
        ks_ref[0, gi] = kv[:, 2 * width + gi * hd:2 * width + (gi + 1) * hd].astype(BF16)
        kw_ref[0, gi] = kv[:, 4 * width + gi * hd:4 * width + (gi + 1) * hd].astype(BF16)
        for j in range(tm // TK):
            vs_ref[0, gi, j] = vsT[gi * hd:(gi + 1) * hd, j * TK:(j + 1) * TK].astype(BF16)
            vw_ref[0, gi, j] = vwT[gi * hd:(gi + 1) * hd, j * TK:(j + 1) * TK].astype(BF16)
    gates = jax.nn.sigmoid(jnp.dot(hb, w_ref[:, OFF_GATE:D_IN_PAD], preferred_element_type=F32)).T
    for gi in range(N_KV):
        g_ref[0, gi] = gates[gi * GATE_ROWS:(gi + 1) * GATE_ROWS, :]
    g = jnp.dot(hb, w_ref[:, 0:D_RNN], preferred_element_type=F32)

    rows = lax.broadcasted_iota(I32, (8, D_RNN), 0)
    hprev = hcar[...]
    for gi in range(tm // 8):
        ag = a_s[gi * 8:gi * 8 + 8, :]
        ug = u_s[gi * 8:gi * 8 + 8, :]
        for k in (1, 2, 4):
            a_sh = jnp.where(rows >= k, pltpu.roll(ag, k, 0), 1.0)
            u_sh = jnp.where(rows >= k, pltpu.roll(ug, k, 0), 0.0)
            ug = ag * u_sh + ug
            ag = ag * a_sh
        hg = ag * hprev + ug
        h_s[gi * 8:gi * 8 + 8, :] = hg
        hprev = hg[7:8, :]
    hcar[...] = hprev

    rnn = _gelu_tanh(g) * h_s[...]
    rnn_ref[0] = (_rms(rnn) * grnn_ref[...]).astype(BF16)


def _inproj(x, gain, sc, sh, w_in, conv_w, conv_b, wa, ba, wx, bx, lam, grnn):
    B, S, D = x.shape
    tm = min(TM_IN, S)
    row = lambda n: pl.BlockSpec((1, n), lambda b, s: (0, 0))
    per_b = lambda n: pl.BlockSpec((1, 1, n), lambda b, s: (b, 0, 0))
    full = lambda a: pl.BlockSpec(a.shape, lambda b, s: (0,) * a.ndim)
    tok = lambda n: pl.BlockSpec((1, tm, n), lambda b, s: (b, s, 0))
    G = N_KV
    grp = pl.BlockSpec((1, G, tm, HEAD_DIM), lambda b, s: (b, 0, s, 0))
    grpT = pl.BlockSpec((1, G, tm // TK, HEAD_DIM, TK), lambda b, s: (b, 0, s, 0, 0))
    return pl.pallas_call(
        _inproj_kernel,
        grid=(B, S // tm),
        in_specs=[tok(D), row(D), per_b(D), per_b(D), full(w_in), full(conv_w), row(D_RNN),
                  full(wa), row(D_RNN), full(wx), row(D_RNN), row(D_RNN), row(D_RNN)],
        out_specs=[tok(D_RNN),
                   pl.BlockSpec((1, D_ATT, tm), lambda b, s: (b, 0, s)),
                   grp, grpT, grp, grpT, grp, grp,
                   pl.BlockSpec((1, G, GATE_ROWS, tm), lambda b, s: (b, 0, 0, s))],
        out_shape=[jax.ShapeDtypeStruct((B, S, D_RNN), BF16),
                   jax.ShapeDtypeStruct((B, D_ATT, S), BF16),
                   jax.ShapeDtypeStruct((B, G, S, HEAD_DIM), BF16),
                   jax.ShapeDtypeStruct((B, G, S // TK, HEAD_DIM, TK), BF16),
                   jax.ShapeDtypeStruct((B, G, S, HEAD_DIM), BF16),
                   jax.ShapeDtypeStruct((B, G, S // TK, HEAD_DIM, TK), BF16),
                   jax.ShapeDtypeStruct((B, G, S, HEAD_DIM), F32),
                   jax.ShapeDtypeStruct((B, G, S, HEAD_DIM), F32),
                   jax.ShapeDtypeStruct((B, G, GATE_ROWS, S), F32)],
        scratch_shapes=[pltpu.VMEM((tm + 8, D_RNN), F32), pltpu.VMEM((1, D_RNN), F32),
                        pltpu.VMEM((tm, D_RNN), F32), pltpu.VMEM((tm, D_RNN), F32),
                        pltpu.VMEM((tm, D_RNN), F32)],
        compiler_params=pltpu.CompilerParams(dimension_semantics=("arbitrary", "arbitrary"),
                                             vmem_limit_bytes=VMEM_LIMIT),
        name="inproj",
    )(x, gain, sc, sh, w_in, conv_w, conv_b, wa, ba, wx, bx, lam, grnn)


def _compress_kernel(rk_ref, rv_ref, w1k_ref, w2k_ref, pk_ref, w1v_ref, w2v_ref, pv_ref, kc_ref, vc_ref):
    def one(r_ref, w1_ref, w2_ref, pos_ref, o_ref):
        n = r_ref.shape[2] // CMP_STRIDE
        p = jnp.zeros((n, 2 * CMP_HIDDEN), F32)
        for l in range(CMP_STRIDE):
            rows = r_ref[0, 0, pl.ds(l, n, stride=CMP_STRIDE), :].astype(BF16)
            p = p + jnp.dot(rows, w1_ref[l * HEAD_DIM:(l + 1) * HEAD_DIM, :], preferred_element_type=F32)
        posb = jnp.dot(pos_ref[...], w1_ref[...], preferred_element_type=F32)
        bias = posb[0:1, 0:CMP_HIDDEN] + posb[1:2, CMP_HIDDEN:]
        pre = p[:, 0:CMP_HIDDEN] + pltpu.roll(p[:, CMP_HIDDEN:], n - 1, 0) + bias
        hid = _gelu_tanh(pre).astype(BF16)
        o_ref[0, 0] = jnp.dot(hid, w2_ref[...], preferred_element_type=F32).astype(BF16)

    one(rk_ref, w1k_ref, w2k_ref, pk_ref, kc_ref)
    one(rv_ref, w1v_ref, w2v_ref, pv_ref, vc_ref)


def _compress(rk, rv, w1k, w2k, posk, w1v, w2v, posv):
    B, G, S, width = rk.shape
    n = S // CMP_STRIDE
    blk = pl.BlockSpec((1, 1, S, width), lambda b, g: (b, g, 0, 0))
    full = lambda a: pl.BlockSpec(a.shape, lambda b, g: (0,) * a.ndim)
    out = pl.BlockSpec((1, 1, n, HEAD_DIM), lambda b, g: (b, g, 0, 0))
    return pl.pallas_call(
        _compress_kernel,
        grid=(B, G),
        in_specs=[blk, blk, full(w1k), full(w2k), full(posk), full(w1v), full(w2v), full(posv)],
        out_specs=[out, out],
        out_shape=[jax.ShapeDtypeStruct((B, G, n, HEAD_DIM), BF16)] * 2,
        compiler_params=pltpu.CompilerParams(vmem_limit_bytes=VMEM_LIMIT),
        name="compress",
    )(rk, rv, w1k, w2k, posk, w1v, w2v, posv)


def _attn_kernel(qT_ref, ks_ref, vs_ref, kw_ref, vw_ref, kc_ref, vcT_ref, g_ref,
                 o_ref,
                 kall, vall, qall, kca, m_all, acc_all, al_s, sbuf, pbuf, oacc, val_s, selbias_s,
                 bias_tbl, cmask,
                 oT_s, st_s, *, n_sel):
    S = ks_ref.shape[2]
    nkt = S // TK
    ncmp = kc_ref.shape[2]
    nblk = S // SLC_BLOCK
    cmp_per_tile = TQ // CMP_STRIDE
    b = pl.program_id(0)
    g = pl.program_id(1)
    qi = pl.program_id(2)
    q0 = qi * TQ
    W4 = REP * TQ

    @pl.when((b == 0) & (g == 0) & (qi == 0))
    def _():
        col = lax.broadcasted_iota(I32, (TK, KAUG), 1)
        is_blk = (col == POS_COL) | (col == POS_COL + 2)
        is_off = (col == POS_COL + 1) | (col == POS_COL + 3)
        ones_row = (lax.broadcasted_iota(I32, (VROWS - HEAD_DIM, TK), 0) == 0).astype(BF16)
        for kt in range(nkt):
            key = kt * TK + lax.broadcasted_iota(I32, (TK, KAUG), 0)
            pos = jnp.where(is_blk, (key // SLC_BLOCK).astype(F32),
                            jnp.where(is_off, (key % SLC_BLOCK).astype(F32), 0.0))
            onehot = (col - HEAD_DIM == key // SLC_BLOCK).astype(F32)
            kall[SEL, kt * TK:(kt + 1) * TK, :] = (pos + onehot).astype(BF16)
            kall[WIN, kt * TK:(kt + 1) * TK, :] = pos.astype(BF16)
            vall[SEL, kt, HEAD_DIM:VROWS, :] = ones_row
            vall[WIN, kt, HEAD_DIM:VROWS, :] = ones_row
        ko = lax.broadcasted_iota(I32, (TK, W4), 0)
        to = lax.broadcasted_iota(I32, (TK, W4), 1) % TQ
        bias_tbl[MASK_NONE] = jnp.zeros((TK, W4), F32)
        bias_tbl[MASK_CAUSAL] = jnp.where(ko <= to, 0.0, NEG)
        bias_tbl[MASK_WINDOW_LOW] = jnp.where(ko > to, 0.0, NEG)
        bias_tbl[MASK_ALL] = jnp.full((TK, W4), NEG, F32)
        u = lax.broadcasted_iota(I32, (2 * ncmp, W4), 0)
        tc = lax.broadcasted_iota(I32, (2 * ncmp, W4), 1) % TQ
        cmask[...] = jnp.where(u <= ncmp + ((tc + 1) // CMP_STRIDE) - 2, 0.0, NEG)
        cc = lax.broadcasted_iota(I32, (ncmp, KAUG), 0)
        colc = lax.broadcasted_iota(I32, (ncmp, KAUG), 1)
        kca[...] = jnp.where((colc == POS_COL) | (colc == POS_COL + 2), (cc // 4).astype(F32),
                             jnp.where((colc == POS_COL + 1) | (colc == POS_COL + 3),
                                       (CMP_STRIDE * (cc % 4)).astype(F32), 0.0)).astype(BF16)
        qall[...] = jnp.zeros(qall.shape, BF16)

    lane = lax.broadcasted_iota(I32, (1, W4), 1)
    head = g * REP + lane // TQ
    slope = lax.bitcast_convert_type((126 - head) << 23, F32)

    @pl.when(qi == 0)
    def _():
        for kt in range(nkt):
            kall[SEL, kt * TK:(kt + 1) * TK, 0:HEAD_DIM] = ks_ref[0, 0, kt * TK:(kt + 1) * TK, :]
            kall[WIN, kt * TK:(kt + 1) * TK, 0:HEAD_DIM] = kw_ref[0, 0, kt * TK:(kt + 1) * TK, :]
            vall[SEL, kt, 0:HEAD_DIM, :] = vs_ref[0, 0, kt]
            vall[WIN, kt, 0:HEAD_DIM, :] = vw_ref[0, 0, kt]
        kca[:, 0:HEAD_DIM] = kc_ref[0, 0]
        c_hi = (slope * LOG2E).astype(BF16).astype(F32)
        c_lo = slope * LOG2E - c_hi
        r16 = lax.broadcasted_iota(I32, (16, W4), 0)
        alibi = jnp.where(r16 == 0, c_hi * SLC_BLOCK, jnp.where(r16 == 1, c_hi, jnp.where(
            r16 == 2, c_lo * SLC_BLOCK, jnp.where(r16 == 3, c_lo, 0.0)))).astype(BF16)
        qall[SEL, POS_COL:POS_COL + 16, :] = alibi
        qall[WIN, POS_COL:POS_COL + 16, :] = alibi

    q = qT_ref[0]
    for r in range(REP):
        qr = q[r * HEAD_DIM:(r + 1) * HEAD_DIM, :]
        qall[SEL, 0:HEAD_DIM, r * TQ:(r + 1) * TQ] = qr
        qall[WIN, 0:HEAD_DIM, r * TQ:(r + 1) * TQ] = qr

    gall = g_ref[0, 0]
    gates = [gall[br * REP:(br + 1) * REP, :] for br in range(3)]

    def stage_scores(br, tiles_masks):
        col_max = None
        for h, (tile, mask) in enumerate(tiles_masks):
            r0 = pl.multiple_of(tile * TK, TK)
            s = jnp.dot(kall[br, pl.ds(r0, TK), :], qall[br], preferred_element_type=F32)
            if mask is not None:
                s = s + bias_tbl[mask]
            sbuf[h * TK:(h + 1) * TK, :] = s
            c = jnp.max(s, axis=0, keepdims=True)
            col_max = c if col_max is None else jnp.maximum(col_max, c)
        m_old = m_all[br]
        m_new = jnp.maximum(m_old, col_max)
        m_all[br] = m_new
        return jnp.exp2(m_old - m_new)

    def stage_probs(br, ntile):
        pbuf[0:ntile * TK, :] = jnp.exp2(sbuf[0:ntile * TK, :] - m_all[br]).astype(BF16)

    def stage_values(br, tiles, al_row):
        acc = acc_all[br] * al_row
        for h, tile in enumerate(tiles):
            acc = acc + jnp.dot(vall[br, tile], pbuf[h * TK:(h + 1) * TK, :], preferred_element_type=F32)
        acc_all[br] = acc

    def start_branch(br, tl, mask1):
        al_s[...] = stage_scores(br, [(tl[0], MASK_CAUSAL), (tl[1], mask1)])

    def finish_branch(br, n, tl):
        rest = jnp.maximum(n - 2, 0)
        npair = rest // 2

        def body(j, carry):
            al_prev = al_s[...]
            stage_probs(br, 2)
            al_s[...] = stage_scores(br, [(tl[2 * j], None), (tl[2 * j + 1], None)])
            stage_values(br, [tl[2 * j - 2], tl[2 * j - 1]], al_prev)
            return carry

        lax.fori_loop(1, 1 + npair, body, 0)
        last = 2 * npair

        @pl.when(rest % 2 == 1)
        def _():
            al_prev = al_s[...]
            stage_probs(br, 2)
            al_k = stage_scores(br, [(tl[n - 1], None)])
            stage_values(br, [tl[last], tl[last + 1]], al_prev)
            stage_probs(br, 1)
            stage_values(br, [tl[n - 1]], al_k)

        @pl.when(rest % 2 == 0)
        def _():
            stage_probs(br, 2)
            stage_values(br, [tl[last], tl[last + 1]], al_s[...])

    m_all[...] = jnp.full(m_all.shape, NEG, F32)
    acc_all[...] = jnp.zeros(acc_all.shape, F32)
    w1 = jnp.maximum(qi - 2, 0)
    w2 = jnp.maximum(qi - 1, 0)
    al_w0 = stage_scores(WIN, [(qi, MASK_CAUSAL),
                               (w1, jnp.where(qi >= 2, MASK_WINDOW_LOW, jnp.where(qi >= 1, MASK_NONE, MASK_ALL)))])

    c0 = pl.multiple_of(ncmp - cmp_per_tile * qi, cmp_per_tile)
    sc = jnp.dot(kca[...], qall[WIN], preferred_element_type=F32) + cmask[pl.ds(c0, ncmp), :]
    e = jnp.exp2(sc - jnp.max(sc, axis=0, keepdims=True))
    tq = q0 + lane % TQ
    p = e * ((1.0 / jnp.sum(e, axis=0, keepdims=True)) * (tq >= CMP_LEN - 1).astype(F32))
    ocT = jnp.dot(vcT_ref[0, 0], p.astype(BF16), preferred_element_type=F32)
    for r in range(REP):
        oacc[:, r * TQ:(r + 1) * TQ] = gates[0][r:r + 1, :] * ocT[:, r * TQ:(r + 1) * TQ]

    stage_probs(WIN, 2)
    al_w1 = stage_scores(WIN, [(w2, jnp.where(qi >= 2, MASK_NONE, MASK_ALL))])
    stage_values(WIN, [qi, w1], al_w0)

    psum = p[:, 0:TQ]
    for r in range(1, REP):
        psum = psum + p[:, r * TQ:(r + 1) * TQ]
    jj = lax.broadcasted_iota(I32, (nblk, ncmp), 0)
    cc = lax.broadcasted_iota(I32, (nblk, ncmp), 1)
    ovT = ((CMP_STRIDE * cc < SLC_BLOCK * jj + SLC_BLOCK)
           & (CMP_STRIDE * cc + CMP_LEN > SLC_BLOCK * jj)).astype(F32)
    impT = jnp.dot(ovT, psum, preferred_element_type=F32, precision=lax.Precision.HIGHEST)

    j_i = lax.broadcasted_iota(I32, (nblk, TQ), 0)
    t1 = q0 + lax.broadcasted_iota(I32, (nblk, TQ), 1)
    cur = t1 // SLC_BLOCK
    forced = (j_i == 0) | (j_i == cur) | (j_i == cur - 1)
    visible = SLC_BLOCK * j_i <= t1
    val_s[...] = jnp.where(forced, FORCED, jnp.where(visible, impT, NEG))
    stage_probs(WIN, 1)
    stage_values(WIN, [w2], al_w1)
    ngrp = nblk // 8
    vals = [val_s[8 * jb:8 * jb + 8, :] for jb in range(ngrp)]
    ranks = [jnp.zeros((8, TQ), F32) for _ in range(ngrp)]
    j8 = lax.broadcasted_iota(I32, (8, TQ), 0)
    for i in range(nblk):
        ib = i // 8
        row = jnp.broadcast_to(val_s[i:i + 1, :], (8, TQ))
        for jb in range(ngrp):
            if jb > ib:
                hit = row >= vals[jb]
            elif jb < ib:
                hit = row > vals[jb]
            else:
                hit = jnp.where(j8 > i - 8 * jb, jnp.where(row >= vals[jb], 1.0, 0.0),
                                jnp.where(row > vals[jb], 1.0, 0.0)) > 0.5
            ranks[jb] = ranks[jb] + jnp.where(hit, 1.0, 0.0)
    blocks_per_tile = TK // SLC_BLOCK
    st_s[0] = qi
    n_selt = jnp.int32(1)
    for jb in range(ngrp):
        chosen = ranks[jb] < n_sel
        selb = jnp.where(chosen, 0.0, NEG)
        for r in range(REP):
            selbias_s[8 * jb:8 * jb + 8, r * TQ:(r + 1) * TQ] = selb
        any_q = jnp.max(jnp.where(chosen, 1.0, 0.0), axis=1, keepdims=True)
        for hh in range(8 // blocks_per_tile):
            kt = (8 * jb) // blocks_per_tile + hh
            hit = jnp.max(any_q[hh * blocks_per_tile:(hh + 1) * blocks_per_tile, :]) > 0.5
            st_s[n_selt] = kt
            n_selt = n_selt + jnp.where(hit & (kt < qi), 1, 0)
    qall[SEL, HEAD_DIM:HEAD_DIM + nblk, :] = selbias_s[...].astype(BF16)

    start_branch(SEL, st_s, jnp.where(n_selt >= 2, MASK_NONE, MASK_ALL))
    finish_branch(SEL, n_selt, st_s)

    def normalised(br):
        acc = acc_all[br]
        return acc[0:HEAD_DIM, :] / acc[HEAD_DIM:HEAD_DIM + 1, :]

    o_sel = normalised(SEL)
    o_win = normalised(WIN)
    for r in range(REP):
        lanes = slice(r * TQ, (r + 1) * TQ)
        oT_s[r * HEAD_DIM:(r + 1) * HEAD_DIM, :] = (oacc[:, lanes] + gates[1][r:r + 1, :] * o_sel[:, lanes]
                                                    + gates[2][r:r + 1, :] * o_win[:, lanes])
    o_ref[0] = oT_s[...].T.astype(BF16)


def _attention(qT, ks, vs, kw, vw, kc, vcT, gT):
    B, _, S = qT.shape
    G = N_KV
    nkt = S // TK
    ncmp = kc.shape[2]
    nblk = S // SLC_BLOCK
    n_sel = min(SLC_TOPK, nblk)
    W4 = REP * TQ
    res4 = lambda a: pl.BlockSpec((1, 1) + a.shape[2:], lambda b, g, i: (b, g) + (0,) * (a.ndim - 2))
    return pl.pallas_call(
        functools.partial(_attn_kernel, n_sel=n_sel),
        grid=(B, G, S // TQ),
        in_specs=[pl.BlockSpec((1, REP * HEAD_DIM, TQ), lambda b, g, i: (b, g, i)),
                  res4(ks), res4(vs), res4(kw), res4(vw), res4(kc), res4(vcT),
                  pl.BlockSpec((1, 1, GATE_ROWS, TQ), lambda b, g, i: (b, g, 0, i))],
        out_specs=pl.BlockSpec((1, TQ, REP * HEAD_DIM), lambda b, g, i: (b, i, g)),
        out_shape=jax.ShapeDtypeStruct((B, S, D_ATT), BF16),
        scratch_shapes=[pltpu.VMEM((2, S, KAUG), BF16),
                        pltpu.VMEM((2, nkt, VROWS, TK), BF16),
                        pltpu.VMEM((2, KAUG, W4), BF16),
                        pltpu.VMEM((ncmp, KAUG), BF16),
                        pltpu.VMEM((2, 1, W4), F32),
                        pltpu.VMEM((2, VROWS, W4), F32),
                        pltpu.VMEM((1, W4), F32),
                        pltpu.VMEM((2 * TK, W4), F32),
                        pltpu.VMEM((2 * TK, W4), BF16),
                        pltpu.VMEM((HEAD_DIM, W4), F32),
                        pltpu.VMEM((nblk, TQ), F32),
                        pltpu.VMEM((nblk, W4), F32),
                        pltpu.VMEM((4, TK, W4), F32),
                        pltpu.VMEM((2 * ncmp, W4), F32),
                        pltpu.VMEM((REP * HEAD_DIM, TQ), F32),
                        pltpu.SMEM((nkt + 2,), I32)],
        compiler_params=pltpu.CompilerParams(dimension_semantics=("arbitrary", "arbitrary", "arbitrary"),
                                             vmem_limit_bytes=VMEM_LIMIT),
        name="attn",
    )(qT, ks, vs, kw, vw, kc, vcT, gT)


def _outmlp_kernel(x_ref, rnn_ref, att_ref, gatt_ref, wo_ref, gpost_ref, g1_ref, gpre_ref, sc2_ref, sh2_ref,
                   w1_ref, w2_ref, gpost2_ref, g2_ref, o_ref):
    att_n = (_rms(att_ref[0].astype(F32)) * gatt_ref[...]).astype(BF16)
    y = (jnp.dot(rnn_ref[0], wo_ref[0:D_RNN, :], preferred_element_type=F32)
         + jnp.dot(att_n, wo_ref[D_RNN:, :], preferred_element_type=F32))
    x1 = x_ref[0] + (1.0 + g1_ref[0]) * (_rms(y) * gpost_ref[...])
    h2 = (_rms(x1) * (gpre_ref[...] * (1.0 + sc2_ref[0])) + sh2_ref[0]).astype(BF16)
    fc = 1024
    ff = jnp.zeros(x1.shape, F32)
    for c in range(D_FF // fc):
        hid = jnp.maximum(jnp.dot(h2, w1_ref[:, c * fc:(c + 1) * fc], preferred_element_type=F32), 0.0)
        ff = ff + jnp.dot((hid * hid).astype(BF16), w2_ref[c * fc:(c + 1) * fc, :], preferred_element_type=F32)
    o_ref[0] = x1 + (1.0 + g2_ref[0]) * (_rms(ff) * gpost2_ref[...])


def _outmlp(x, rnn_n, att, gatt, wo, gpost, g1, gpre, sc2, sh2, w1, w2, gpost2, g2):
    B, S, D = x.shape
    tm = min(TM_OUT, S)
    row = lambda n: pl.BlockSpec((1, n), lambda b, s: (0, 0))
    per_b = lambda n: pl.BlockSpec((1, 1, n), lambda b, s: (b, 0, 0))
    const = lambda a: pl.BlockSpec(a.shape, lambda b, s: (0,) * a.ndim, pipeline_mode=pl.Buffered(1))
    tok = lambda n: pl.BlockSpec((1, tm, n), lambda b, s: (b, s, 0))
    return pl.pallas_call(
        _outmlp_kernel,
        grid=(B, S // tm),
        in_specs=[tok(D), tok(D_RNN), tok(D_ATT), row(D_ATT), const(wo), row(D), per_b(D), row(D),
                  per_b(D), per_b(D), const(w1), const(w2), row(D), per_b(D)],
        out_specs=tok(D),
        out_shape=jax.ShapeDtypeStruct((B, S, D), F32),
        compiler_params=pltpu.CompilerParams(dimension_semantics=("arbitrary", "arbitrary"),
                                             vmem_limit_bytes=VMEM_LIMIT),
        name="outmlp",
    )(x, rnn_n, att, gatt, wo, gpost, g1, gpre, sc2, sh2, w1, w2, gpost2, g2)


def _block_diag(w):
    n, k, _ = w.shape
    return jnp.einsum('nij,nm->nimj', w, jnp.eye(n, dtype=w.dtype)).reshape(n * k, n * k)


def _layer(x, c, ada_w, ada_b, pre_norm_mix, w_in, conv_w, conv_b, lru_wa, lru_ba, lru_wx, lru_bx, lru_lambda,
           cmp_pos_k, cmp_w1_k, cmp_w2_k, cmp_pos_v, cmp_w1_v, cmp_w2_v, norm_rnn_out, norm_att_out, w_out,
           post_norm_mix, pre_norm_mlp, w_ff1, w_ff2, post_norm_mlp):
    B, S, D = x.shape
    G = N_KV
    row = lambda v: v.reshape(1, -1)

    mod = _ada(c, ada_w, ada_b)
    sh1, sc1, g1, sh2, sc2, g2 = [m.reshape(B, 1, D) for m in jnp.split(mod, 6, axis=-1)]

    gate_cols = [OFF_GATE + br * N_HEADS + g * REP + r for g in range(G) for br in range(3) for r in range(REP)]
    w_gate = w_in[:, jnp.asarray(gate_cols)].reshape(D, G, 3 * REP)
    w_gate = jnp.pad(w_gate, ((0, 0), (0, 0), (0, GATE_ROWS - 3 * REP))).reshape(D, G * GATE_ROWS)
    w_in_p = jnp.concatenate([w_in[:, :OFF_GATE], jnp.pad(w_gate, ((0, 0), (0, GATE_PAD - G * GATE_ROWS)))],
                             axis=1).astype(BF16)
    wa = _block_diag(lru_wa).astype(BF16)
    wx = _block_diag(lru_wx).astype(BF16)
    half = CMP_LEN // 2 * HEAD_DIM

    def w1_cat(w1):
        return jnp.concatenate([w1[:half], w1[half:]], axis=1).astype(BF16)

    def pos_rows(pos):
        return jnp.pad(pos.reshape(2, half), ((0, 14), (0, 0))).astype(BF16)

    rnn_n, qT, ks, vsT, kw, vwT, kc_in, vc_in, gT = _inproj(
        x, row(pre_norm_mix), sc1, sh1, w_in_p, conv_w, row(conv_b), wa, row(lru_ba), wx, row(lru_bx),
        row(lru_lambda), row(norm_rnn_out))
    kc, vc = _compress(kc_in, vc_in, w1_cat(cmp_w1_k), cmp_w2_k.astype(BF16), pos_rows(cmp_pos_k),
                       w1_cat(cmp_w1_v), cmp_w2_v.astype(BF16), pos_rows(cmp_pos_v))
    att = _attention(qT, ks, vsT, kw, vwT, kc, vc.transpose(0, 1, 3, 2), gT)

    return _outmlp(x, rnn_n, att, row(norm_att_out), w_out.astype(BF16), row(post_norm_mix), g1,
                   row(pre_norm_mlp), sc2, sh2, w_ff1.astype(BF16), w_ff2.astype(BF16), row(post_norm_mlp), g2)


def kernel(x, c, ada_w, ada_b, pre_norm_mix, w_in, conv_w, conv_b, lru_wa, lru_ba, lru_wx, lru_bx, lru_lambda,
           cmp_pos_k, cmp_w1_k, cmp_w2_k, cmp_pos_v, cmp_w1_v, cmp_w2_v, norm_rnn_out, norm_att_out, w_out,
           post_norm_mix, pre_norm_mlp, w_ff1, w_ff2, post_norm_mlp):
    for l in range(ada_w.shape[0]):
        x = _layer(x, c, ada_w[l], ada_b[l], pre_norm_mix[l], w_in[l], conv_w[l], conv_b[l], lru_wa[l], lru_ba[l],
                   lru_wx[l], lru_bx[l], lru_lambda[l], cmp_pos_k[l], cmp_w1_k[l], cmp_w2_k[l], cmp_pos_v[l],
                   cmp_w1_v[l], cmp_w2_v[l], norm_rnn_out[l], norm_att_out[l], w_out[l], post_norm_mix[l],
                   pre_norm_mlp[l], w_ff1[l], w_ff2[l], post_norm_mlp[l])
    return x
```

```python
import functools

import jax
import jax.numpy as jnp
from jax import lax
from jax.experimental import pallas as pl
from jax.experimental.pallas import tpu as pltpu

F32 = jnp.float32
BF16 = jnp.bfloat16
I32 = jnp.int32

D_MODEL = 1024
D_RNN = 512
RNN_BLOCKS = 8
CONV_WIDTH = 4
LRU_C = 8.0
N_HEADS = 8
HEAD_DIM = 64
N_KV = 2
REP = N_HEADS // N_KV
D_ATT = N_HEADS * HEAD_DIM
CMP_LEN = 32
CMP_STRIDE = 16
CMP_HIDDEN = 256
SLC_BLOCK = 64
SLC_TOPK = 16
WINDOW = 512
D_FF = 4 * D_MODEL
EPS = 1e-6
NEG = -1e30
FORCED = 1e4
LOG2E = 1.4426950408889634

KV_COLS = 6 * N_KV * HEAD_DIM
N_GATE = 3 * N_HEADS
GATE_PAD = 128
GATE_ROWS = 16
OFF_Q = 2 * D_RNN
OFF_KV = OFF_Q + D_ATT
OFF_GATE = OFF_KV + KV_COLS
D_IN_PAD = OFF_GATE + GATE_PAD

TM_IN = 512
TM_OUT = 1024
TQ = 256
TK = 256
KAUG = 256
VROWS = 80
VMEM_LIMIT = 56 * 1024 * 1024

SEL, WIN = 0, 1
MASK_NONE, MASK_CAUSAL, MASK_WINDOW_LOW, MASK_ALL = 0, 1, 2, 3
POS_COL = 2 * HEAD_DIM


def _gelu_tanh(x):
    return 0.5 * x * (1.0 + jnp.tanh(0.7978845608028654 * (x + 0.044715 * (x * x * x))))


def _rms(x):
    return x * lax.rsqrt(jnp.mean(x * x, axis=-1, keepdims=True) + EPS)


def _ada_kernel(c_ref, w_ref, b_ref, o_ref):
    c = c_ref[...]
    a = c * jax.nn.sigmoid(c)
    o_ref[...] = jnp.dot(a, w_ref[...], preferred_element_type=F32,
                         precision=lax.Precision.HIGHEST) + b_ref[...]


def _ada(c, w, b):
    B, D = c.shape
    N = w.shape[1]
    tn = 1024
    return pl.pallas_call(
        _ada_kernel,
        grid=(N // tn,),
        in_specs=[pl.BlockSpec((B, D), lambda j: (0, 0)),
                  pl.BlockSpec((D, tn), lambda j: (0, j)),
                  pl.BlockSpec((1, tn), lambda j: (0, j))],
        out_specs=pl.BlockSpec((B, tn), lambda j: (0, j)),
        out_shape=jax.ShapeDtypeStruct((B, N), F32),
        name="ada",
    )(c, w, b.reshape(1, N))


def _inproj_kernel(x_ref, gain_ref, sc_ref, sh_ref, w_ref, cw_ref, cb_ref, wa_ref, ba_ref, wx_ref, bx_ref,
                   lam_ref, grnn_ref,
                   rnn_ref, qT_ref, ks_ref, vs_ref, kw_ref, vw_ref, kc_ref, vc_ref, g_ref,
                   xbuf, hcar, a_s, u_s, h_s):
    tm = x_ref.shape[1]
    hd = HEAD_DIM

    @pl.when(pl.program_id(1) == 0)
    def _():
        xbuf[0:8, :] = jnp.zeros((8, D_RNN), F32)
        hcar[...] = jnp.zeros((1, D_RNN), F32)

    x = x_ref[0]
    h = _rms(x) * (gain_ref[...] * (1.0 + sc_ref[0])) + sh_ref[0]
    hb = h.astype(BF16)

    xr = jnp.dot(hb, w_ref[:, D_RNN:OFF_Q], preferred_element_type=F32)
    xbuf[8:8 + tm, :] = xr
    y = (cw_ref[3:4, :] * xr + cw_ref[2:3, :] * xbuf[7:7 + tm, :]
         + cw_ref[1:2, :] * xbuf[6:6 + tm, :] + cw_ref[0:1, :] * xbuf[5:5 + tm, :]) + cb_ref[...]
    xbuf[0:8, :] = xbuf[tm:tm + 8, :]

    yb = y.astype(BF16)
    r = jax.nn.sigmoid(jnp.dot(yb, wa_ref[...], preferred_element_type=F32) + ba_ref[...])
    i = jax.nn.sigmoid(jnp.dot(yb, wx_ref[...], preferred_element_type=F32) + bx_ref[...])
    nl = -lam_ref[...]
    softplus = jnp.maximum(nl, 0.0) + jnp.log(1.0 + jnp.exp(-jnp.abs(nl)))
    a = jnp.exp((-LRU_C) * r * softplus)
    a_s[...] = a
    u_s[...] = jnp.sqrt(1.0 - a * a) * (i * y)

    qT_ref[0] = (jnp.dot(hb, w_ref[:, OFF_Q:OFF_KV], preferred_element_type=F32)
                 * (HEAD_DIM ** -0.5 * LOG2E)).T.astype(BF16)
    kv = jnp.dot(hb, w_ref[:, OFF_KV:OFF_GATE], preferred_element_type=F32)
    width = N_KV * hd
    vsT = kv[:, 3 * width:4 * width].T
    vwT = kv[:, 5 * width:6 * width].T
    for gi in range(N_KV):
        kc_ref[0, gi] = kv[:, gi * hd:(gi + 1) * hd]
        vc_ref[0, gi] = kv[:, width + gi * hd:width + (gi + 1) * hd]
        ks_ref[0, gi] = kv[:, 2 * width + gi * hd:2 * width + (gi + 1) * hd].astype(BF16)
        kw_ref[0, gi] = kv[:, 4 * width + gi * hd:4 * width + (gi + 1) * hd].astype(BF16)
        for j in range(tm // TK):
            vs_ref[0, gi, j] = vsT[gi * hd:(gi + 1) * hd, j * TK:(j + 1) * TK].astype(BF16)
            vw_ref[0, gi, j] = vwT[gi * hd:(gi + 1) * hd, j * TK:(j + 1) * TK].astype(BF16)
    gates = jax.nn.sigmoid(jnp.dot(hb, w_ref[:, OFF_GATE:D_IN_PAD], preferred_element_type=F32)).T
    for gi in range(N_KV):
        g_ref[0, gi] = gates[gi * GATE_ROWS:(gi + 1) * GATE_ROWS, :]
    g = jnp.dot(hb, w_ref[:, 0:D_RNN], preferred_element_type=F32)


    rows = lax.broadcasted_iota(I32, (8, D_RNN), 0)
    hprev = hcar[...]
    for gi in range(tm // 8):
        ag = a_s[gi * 8:gi * 8 + 8, :]
        ug = u_s[gi * 8:gi * 8 + 8, :]
        for k in (1, 2, 4):
            a_sh = jnp.where(rows >= k, pltpu.roll(ag, k, 0), 1.0)
            u_sh = jnp.where(rows >= k, pltpu.roll(ug, k, 0), 0.0)
            ug = ag * u_sh + ug
            ag = ag * a_sh
        hg = ag * hprev + ug
        h_s[gi * 8:gi * 8 + 8, :] = hg
        hprev = hg[7:8, :]
    hcar[...] = hprev

    rnn = _gelu_tanh(g) * h_s[...]
    rnn_ref[0] = (_rms(rnn) * grnn_ref[...]).astype(BF16)


def _inproj(x, gain, sc, sh, w_in, conv_w, conv_b, wa, ba, wx, bx, lam, grnn):
    B, S, D = x.shape
    tm = min(TM_IN, S)
    row = lambda n: pl.BlockSpec((1, n), lambda b, s: (0, 0))
    per_b = lambda n: pl.BlockSpec((1, 1, n), lambda b, s: (b, 0, 0))
    full = lambda a: pl.BlockSpec(a.shape, lambda b, s: (0,) * a.ndim)
    tok = lambda n: pl.BlockSpec((1, tm, n), lambda b, s: (b, s, 0))
    G = N_KV
    grp = pl.BlockSpec((1, G, tm, HEAD_DIM), lambda b, s: (b, 0, s, 0))
    grpT = pl.BlockSpec((1, G, tm // TK, HEAD_DIM, TK), lambda b, s: (b, 0, s, 0, 0))
    return pl.pallas_call(
        _inproj_kernel,
        grid=(B, S // tm),
        in_specs=[tok(D), row(D), per_b(D), per_b(D), full(w_in), full(conv_w), row(D_RNN),
                  full(wa), row(D_RNN), full(wx), row(D_RNN), row(D_RNN), row(D_RNN)],
        out_specs=[tok(D_RNN),
                   pl.BlockSpec((1, D_ATT, tm), lambda b, s: (b, 0, s)),
                   grp, grpT, grp, grpT, grp, grp,
                   pl.BlockSpec((1, G, GATE_ROWS, tm), lambda b, s: (b, 0, 0, s))],
        out_shape=[jax.ShapeDtypeStruct((B, S, D_RNN), BF16),
                   jax.ShapeDtypeStruct((B, D_ATT, S), BF16),
                   jax.ShapeDtypeStruct((B, G, S, HEAD_DIM), BF16),
                   jax.ShapeDtypeStruct((B, G, S // TK, HEAD_DIM, TK), BF16),
                   jax.ShapeDtypeStruct((B, G, S, HEAD_DIM), BF16),
                   jax.ShapeDtypeStruct((B, G, S // TK, HEAD_DIM, TK), BF16),
                   jax.ShapeDtypeStruct((B, G, S, HEAD_DIM), F32),
                   jax.ShapeDtypeStruct((B, G, S, HEAD_DIM), F32),
                   jax.ShapeDtypeStruct((B, G, GATE_ROWS, S), F32)],
        scratch_shapes=[pltpu.VMEM((tm + 8, D_RNN), F32), pltpu.VMEM((1, D_RNN), F32),
                        pltpu.VMEM((tm, D_RNN), F32), pltpu.VMEM((tm, D_RNN), F32),
                        pltpu.VMEM((tm, D_RNN), F32)],
        compiler_params=pltpu.CompilerParams(dimension_semantics=("arbitrary", "arbitrary"),
                                             vmem_limit_bytes=VMEM_LIMIT),
        name="inproj",
    )(x, gain, sc, sh, w_in, conv_w, conv_b, wa, ba, wx, bx, lam, grnn)


def _compress_kernel(rk_ref, rv_ref, w1k_ref, w2k_ref, pk_ref, w1v_ref, w2v_ref, pv_ref, kc_ref, vc_ref):
    def one(r_ref, w1_ref, w2_ref, pos_ref, o_ref):
        n = r_ref.shape[2] // CMP_STRIDE
        p = jnp.zeros((n, 2 * CMP_HIDDEN), F32)
        for l in range(CMP_STRIDE):
            rows = r_ref[0, 0, pl.ds(l, n, stride=CMP_STRIDE), :].astype(BF16)
            p = p + jnp.dot(rows, w1_ref[l * HEAD_DIM:(l + 1) * HEAD_DIM, :], preferred_element_type=F32)
        posb = jnp.dot(pos_ref[...], w1_ref[...], preferred_element_type=F32)
        bias = posb[0:1, 0:CMP_HIDDEN] + posb[1:2, CMP_HIDDEN:]
        pre = p[:, 0:CMP_HIDDEN] + pltpu.roll(p[:, CMP_HIDDEN:], n - 1, 0) + bias
        hid = _gelu_tanh(pre).astype(BF16)
        o_ref[0, 0] = jnp.dot(hid, w2_ref[...], preferred_element_type=F32).astype(BF16)

    one(rk_ref, w1k_ref, w2k_ref, pk_ref, kc_ref)
    one(rv_ref, w1v_ref, w2v_ref, pv_ref, vc_ref)


def _compress(rk, rv, w1k, w2k, posk, w1v, w2v, posv):
    B, G, S, width = rk.shape
    n = S // CMP_STRIDE
    blk = pl.BlockSpec((1, 1, S, width), lambda b, g: (b, g, 0, 0))
    full = lambda a: pl.BlockSpec(a.shape, lambda b, g: (0,) * a.ndim)
    out = pl.BlockSpec((1, 1, n, HEAD_DIM), lambda b, g: (b, g, 0, 0))
    return pl.pallas_call(
        _compress_kernel,
        grid=(B, G),
        in_specs=[blk, blk, full(w1k), full(w2k), full(posk), full(w1v), full(w2v), full(posv)],
        out_specs=[out, out],
        out_shape=[jax.ShapeDtypeStruct((B, G, n, HEAD_DIM), BF16)] * 2,
        compiler_params=pltpu.CompilerParams(vmem_limit_bytes=VMEM_LIMIT),
        name="compress",
    )(rk, rv, w1k, w2k, posk, w1v, w2v, posv)


def _attn_kernel(qT_ref, ks_ref, vs_ref, kw_ref, vw_ref, kc_ref, vcT_ref, g_ref,
                 o_ref,
                 kall, vall, qall, kca, m_all, acc_all, al_s, sbuf, pbuf, oacc, val_s, rank_s, selbias_s,
                 bias_tbl, cmask,
                 oT_s, st_s, *, n_sel):
    S = ks_ref.shape[2]
    nkt = S // TK
    ncmp = kc_ref.shape[2]
    nblk = S // SLC_BLOCK
    cmp_per_tile = TQ // CMP_STRIDE
    b = pl.program_id(0)
    g = pl.program_id(1)
    qi = pl.program_id(2)
    q0 = qi * TQ
    W4 = REP * TQ

    @pl.when((b == 0) & (g == 0) & (qi == 0))
    def _():
        col = lax.broadcasted_iota(I32, (TK, KAUG), 1)
        is_blk = (col == POS_COL) | (col == POS_COL + 2)
        is_off = (col == POS_COL + 1) | (col == POS_COL + 3)
        ones_row = (lax.broadcasted_iota(I32, (VROWS - HEAD_DIM, TK), 0) == 0).astype(BF16)
        for kt in range(nkt):
            key = kt * TK + lax.broadcasted_iota(I32, (TK, KAUG), 0)
            pos = jnp.where(is_blk, (key // SLC_BLOCK).astype(F32),
                            jnp.where(is_off, (key % SLC_BLOCK).astype(F32), 0.0))
            onehot = (col - HEAD_DIM == key // SLC_BLOCK).astype(F32)
            kall[SEL, kt * TK:(kt + 1) * TK, :] = (pos + onehot).astype(BF16)
            kall[WIN, kt * TK:(kt + 1) * TK, :] = pos.astype(BF16)
            vall[SEL, kt, HEAD_DIM:VROWS, :] = ones_row
            vall[WIN, kt, HEAD_DIM:VROWS, :] = ones_row
        ko = lax.broadcasted_iota(I32, (TK, W4), 0)
        to = lax.broadcasted_iota(I32, (TK, W4), 1) % TQ
        bias_tbl[MASK_NONE] = jnp.zeros((TK, W4), F32)
        bias_tbl[MASK_CAUSAL] = jnp.where(ko <= to, 0.0, NEG)
        bias_tbl[MASK_WINDOW_LOW] = jnp.where(ko > to, 0.0, NEG)
        bias_tbl[MASK_ALL] = jnp.full((TK, W4), NEG, F32)
        u = lax.broadcasted_iota(I32, (2 * ncmp, W4), 0)
        tc = lax.broadcasted_iota(I32, (2 * ncmp, W4), 1) % TQ
        cmask[...] = jnp.where(u <= ncmp + ((tc + 1) // CMP_STRIDE) - 2, 0.0, NEG)
        cc = lax.broadcasted_iota(I32, (ncmp, KAUG), 0)
        colc = lax.broadcasted_iota(I32, (ncmp, KAUG), 1)
        kca[...] = jnp.where((colc == POS_COL) | (colc == POS_COL + 2), (cc // 4).astype(F32),
                             jnp.where((colc == POS_COL + 1) | (colc == POS_COL + 3),
                                       (CMP_STRIDE * (cc % 4)).astype(F32), 0.0)).astype(BF16)
        qall[...] = jnp.zeros(qall.shape, BF16)

    lane = lax.broadcasted_iota(I32, (1, W4), 1)
    head = g * REP + lane // TQ
    slope = lax.bitcast_convert_type((126 - head) << 23, F32)

    @pl.when(qi == 0)
    def _():
        for kt in range(nkt):
            kall[SEL, kt * TK:(kt + 1) * TK, 0:HEAD_DIM] = ks_ref[0, 0, kt * TK:(kt + 1) * TK, :]
            kall[WIN, kt * TK:(kt + 1) * TK, 0:HEAD_DIM] = kw_ref[0, 0, kt * TK:(kt + 1) * TK, :]
            vall[SEL, kt, 0:HEAD_DIM, :] = vs_ref[0, 0, kt]
            vall[WIN, kt, 0:HEAD_DIM, :] = vw_ref[0, 0, kt]
        kca[:, 0:HEAD_DIM] = kc_ref[0, 0]
        c_hi = (slope * LOG2E).astype(BF16).astype(F32)
        c_lo = slope * LOG2E - c_hi
        r16 = lax.broadcasted_iota(I32, (16, W4), 0)
        alibi = jnp.where(r16 == 0, c_hi * SLC_BLOCK, jnp.where(r16 == 1, c_hi, jnp.where(
            r16 == 2, c_lo * SLC_BLOCK, jnp.where(r16 == 3, c_lo, 0.0)))).astype(BF16)
        qall[SEL, POS_COL:POS_COL + 16, :] = alibi
        qall[WIN, POS_COL:POS_COL + 16, :] = alibi

    q = qT_ref[0]
    for r in range(REP):
        qr = q[r * HEAD_DIM:(r + 1) * HEAD_DIM, :]
        qall[SEL, 0:HEAD_DIM, r * TQ:(r + 1) * TQ] = qr
        qall[WIN, 0:HEAD_DIM, r * TQ:(r + 1) * TQ] = qr

    gall = g_ref[0, 0]
    gates = [gall[br * REP:(br + 1) * REP, :] for br in range(3)]

    def stage_scores(br, tiles_masks):
        col_max = None
        for h, (tile, mask) in enumerate(tiles_masks):
            r0 = pl.multiple_of(tile * TK, TK)
            s = jnp.dot(kall[br, pl.ds(r0, TK), :], qall[br], preferred_element_type=F32)
            if mask is not None:
                s = s + bias_tbl[mask]
            sbuf[h * TK:(h + 1) * TK, :] = s
            c = jnp.max(s, axis=0, keepdims=True)
            col_max = c if col_max is None else jnp.maximum(col_max, c)
        m_old = m_all[br]
        m_new = jnp.maximum(m_old, col_max)
        m_all[br] = m_new
        return jnp.exp2(m_old - m_new)

    def stage_probs(br, ntile):
        pbuf[0:ntile * TK, :] = jnp.exp2(sbuf[0:ntile * TK, :] - m_all[br]).astype(BF16)

    def stage_values(br, tiles, al_row):
        acc = acc_all[br] * al_row
        for h, tile in enumerate(tiles):
            acc = acc + jnp.dot(vall[br, tile], pbuf[h * TK:(h + 1) * TK, :], preferred_element_type=F32)
        acc_all[br] = acc

    def start_branch(br, tl, mask1):
        al_s[...] = stage_scores(br, [(tl[0], MASK_CAUSAL), (tl[1], mask1)])

    def finish_branch(br, n, tl):
        rest = jnp.maximum(n - 2, 0)
        npair = rest // 2

        def body(j, carry):
            al_prev = al_s[...]
            stage_probs(br, 2)
            al_s[...] = stage_scores(br, [(tl[2 * j], None), (tl[2 * j + 1], None)])
            stage_values(br, [tl[2 * j - 2], tl[2 * j - 1]], al_prev)
            return carry

        lax.fori_loop(1, 1 + npair, body, 0)
        last = 2 * npair

        @pl.when(rest % 2 == 1)
        def _():
            al_prev = al_s[...]
            stage_probs(br, 2)
            al_k = stage_scores(br, [(tl[n - 1], None)])
            stage_values(br, [tl[last], tl[last + 1]], al_prev)
            stage_probs(br, 1)
            stage_values(br, [tl[n - 1]], al_k)

        @pl.when(rest % 2 == 0)
        def _():
            stage_probs(br, 2)
            stage_values(br, [tl[last], tl[last + 1]], al_s[...])

    m_all[...] = jnp.full(m_all.shape, NEG, F32)
    acc_all[...] = jnp.zeros(acc_all.shape, F32)
    w1 = jnp.maximum(qi - 2, 0)
    w2 = jnp.maximum(qi - 1, 0)
    al_w0 = stage_scores(WIN, [(qi, MASK_CAUSAL),
                               (w1, jnp.where(qi >= 2, MASK_WINDOW_LOW, jnp.where(qi >= 1, MASK_NONE, MASK_ALL)))])

    c0 = pl.multiple_of(ncmp - cmp_per_tile * qi, cmp_per_tile)
    sc = jnp.dot(kca[...], qall[WIN], preferred_element_type=F32) + cmask[pl.ds(c0, ncmp), :]
    e = jnp.exp2(sc - jnp.max(sc, axis=0, keepdims=True))
    tq = q0 + lane % TQ
    p = e * ((1.0 / jnp.sum(e, axis=0, keepdims=True)) * (tq >= CMP_LEN - 1).astype(F32))
    ocT = jnp.dot(vcT_ref[0, 0], p.astype(BF16), preferred_element_type=F32)
    for r in range(REP):
        oacc[:, r * TQ:(r + 1) * TQ] = gates[0][r:r + 1, :] * ocT[:, r * TQ:(r + 1) * TQ]

    stage_probs(WIN, 2)
    al_w1 = stage_scores(WIN, [(w2, jnp.where(qi >= 2, MASK_NONE, MASK_ALL))])
    stage_values(WIN, [qi, w1], al_w0)

    psum = p[:, 0:TQ]
    for r in range(1, REP):
        psum = psum + p[:, r * TQ:(r + 1) * TQ]
    jj = lax.broadcasted_iota(I32, (nblk, ncmp), 0)
    cc = lax.broadcasted_iota(I32, (nblk, ncmp), 1)
    ovT = ((CMP_STRIDE * cc < SLC_BLOCK * jj + SLC_BLOCK)
           & (CMP_STRIDE * cc + CMP_LEN > SLC_BLOCK * jj)).astype(F32)
    impT = jnp.dot(ovT, psum, preferred_element_type=F32, precision=lax.Precision.HIGHEST)

    j_i = lax.broadcasted_iota(I32, (nblk, TQ), 0)
    t1 = q0 + lax.broadcasted_iota(I32, (nblk, TQ), 1)
    cur = t1 // SLC_BLOCK
    forced = (j_i == 0) | (j_i == cur) | (j_i == cur - 1)
    visible = SLC_BLOCK * j_i <= t1
    val_s[...] = jnp.where(forced, FORCED, jnp.where(visible, impT, NEG))
    ngrp = nblk // 8
    rank_s[...] = jnp.zeros(rank_s.shape, F32)
    j8 = lax.broadcasted_iota(I32, (8, TQ), 0)
    for ib in range(ngrp):
        @pl.when(8 * ib * SLC_BLOCK < q0 + TQ)
        def _(ib=ib):
            vals = [val_s[8 * jb:8 * jb + 8, :] for jb in range(ngrp)]
            ranks = [rank_s[8 * jb:8 * jb + 8, :] for jb in range(ngrp)]
            for i in range(8 * ib, 8 * ib + 8):
                row = jnp.broadcast_to(val_s[i:i + 1, :], (8, TQ))
                for jb in range(ngrp):
                    if jb > ib:
                        hit = row >= vals[jb]
                    elif jb < ib:
                        hit = row > vals[jb]
                    else:
                        hit = jnp.where(j8 > i - 8 * jb, jnp.where(row >= vals[jb], 1.0, 0.0),
                                        jnp.where(row > vals[jb], 1.0, 0.0)) > 0.5
                    ranks[jb] = ranks[jb] + jnp.where(hit, 1.0, 0.0)
            for jb in range(ngrp):
                rank_s[8 * jb:8 * jb + 8, :] = ranks[jb]
    ranks = [rank_s[8 * jb:8 * jb + 8, :] for jb in range(ngrp)]
    blocks_per_tile = TK // SLC_BLOCK
    st_s[0] = qi
    n_selt = jnp.int32(1)
    for jb in range(ngrp):
        chosen = ranks[jb] < n_sel
        selb = jnp.where(chosen, 0.0, NEG)
        for r in range(REP):
            selbias_s[8 * jb:8 * jb + 8, r * TQ:(r + 1) * TQ] = selb
        any_q = jnp.max(jnp.where(chosen, 1.0, 0.0), axis=1, keepdims=True)
        for hh in range(8 // blocks_per_tile):
            kt = (8 * jb) // blocks_per_tile + hh
            hit = jnp.max(any_q[hh * blocks_per_tile:(hh + 1) * blocks_per_tile, :]) > 0.5
            st_s[n_selt] = kt
            n_selt = n_selt + jnp.where(hit & (kt < qi), 1, 0)
    qall[SEL, HEAD_DIM:HEAD_DIM + nblk, :] = selbias_s[...].astype(BF16)

    stage_probs(WIN, 1)
    stage_values(WIN, [w2], al_w1)
    start_branch(SEL, st_s, jnp.where(n_selt >= 2, MASK_NONE, MASK_ALL))
    finish_branch(SEL, n_selt, st_s)

    def normalised(br):
        acc = acc_all[br]
        return acc[0:HEAD_DIM, :] / acc[HEAD_DIM:HEAD_DIM + 1, :]

    o_sel = normalised(SEL)
    o_win = normalised(WIN)
    for r in range(REP):
        lanes = slice(r * TQ, (r + 1) * TQ)
        oT_s[r * HEAD_DIM:(r + 1) * HEAD_DIM, :] = (oacc[:, lanes] + gates[1][r:r + 1, :] * o_sel[:, lanes]
                                                    + gates[2][r:r + 1, :] * o_win[:, lanes])
    o_ref[0] = oT_s[...].T.astype(BF16)


def _attention(qT, ks, vs, kw, vw, kc, vcT, gT):
    B, _, S = qT.shape
    G = N_KV
    nkt = S // TK
    ncmp = kc.shape[2]
    nblk = S // SLC_BLOCK
    n_sel = min(SLC_TOPK, nblk)
    W4 = REP * TQ
    res4 = lambda a: pl.BlockSpec((1, 1) + a.shape[2:], lambda b, g, i: (b, g) + (0,) * (a.ndim - 2))
    return pl.pallas_call(
        functools.partial(_attn_kernel, n_sel=n_sel),
        grid=(B, G, S // TQ),
        in_specs=[pl.BlockSpec((1, REP * HEAD_DIM, TQ), lambda b, g, i: (b, g, i)),
                  res4(ks), res4(vs), res4(kw), res4(vw), res4(kc), res4(vcT),
                  pl.BlockSpec((1, 1, GATE_ROWS, TQ), lambda b, g, i: (b, g, 0, i))],
        out_specs=pl.BlockSpec((1, TQ, REP * HEAD_DIM), lambda b, g, i: (b, i, g)),
        out_shape=jax.ShapeDtypeStruct((B, S, D_ATT), BF16),
        scratch_shapes=[pltpu.VMEM((2, S, KAUG), BF16),
                        pltpu.VMEM((2, nkt, VROWS, TK), BF16),
                        pltpu.VMEM((2, KAUG, W4), BF16),
                        pltpu.VMEM((ncmp, KAUG), BF16),
                        pltpu.VMEM((2, 1, W4), F32),
                        pltpu.VMEM((2, VROWS, W4), F32),
                        pltpu.VMEM((1, W4), F32),
                        pltpu.VMEM((2 * TK, W4), F32),
                        pltpu.VMEM((2 * TK, W4), BF16),
                        pltpu.VMEM((HEAD_DIM, W4), F32),
                        pltpu.VMEM((nblk, TQ), F32),
                        pltpu.VMEM((nblk, TQ), F32),
                        pltpu.VMEM((nblk, W4), F32),
                        pltpu.VMEM((4, TK, W4), F32),
                        pltpu.VMEM((2 * ncmp, W4), F32),
                        pltpu.VMEM((REP * HEAD_DIM, TQ), F32),
                        pltpu.SMEM((nkt + 2,), I32)],
        compiler_params=pltpu.CompilerParams(dimension_semantics=("arbitrary", "arbitrary", "arbitrary"),
                                             vmem_limit_bytes=VMEM_LIMIT),
        name="attn",
    )(qT, ks, vs, kw, vw, kc, vcT, gT)


def _outmlp_kernel(x_ref, rnn_ref, att_ref, gatt_ref, wo_ref, gpost_ref, g1_ref, gpre_ref, sc2_ref, sh2_ref,
                   w1_ref, w2_ref, gpost2_ref, g2_ref, o_ref):
    tm = x_ref.shape[1]
    halves = [slice(0, tm // 2), slice(tm // 2, tm)]
    x1s, h2s = [], []
    for rows in halves:
        att_n = (_rms(att_ref[0, rows, :].astype(F32)) * gatt_ref[...]).astype(BF16)
        y = (jnp.dot(rnn_ref[0, rows, :], wo_ref[0:D_RNN, :], preferred_element_type=F32)
             + jnp.dot(att_n, wo_ref[D_RNN:, :], preferred_element_type=F32))
        x1 = x_ref[0, rows, :] + (1.0 + g1_ref[0]) * (_rms(y) * gpost_ref[...])
        x1s.append(x1)
        h2s.append((_rms(x1) * (gpre_ref[...] * (1.0 + sc2_ref[0])) + sh2_ref[0]).astype(BF16))
    fc = 1024
    for rows, x1, h2 in zip(halves, x1s, h2s):
        ff = jnp.zeros(x1.shape, F32)
        for c in range(D_FF // fc):
            hid = jnp.maximum(jnp.dot(h2, w1_ref[:, c * fc:(c + 1) * fc], preferred_element_type=F32), 0.0)
            ff = ff + jnp.dot((hid * hid).astype(BF16), w2_ref[c * fc:(c + 1) * fc, :],
                              preferred_element_type=F32)
        o_ref[0, rows, :] = x1 + (1.0 + g2_ref[0]) * (_rms(ff) * gpost2_ref[...])


def _outmlp(x, rnn_n, att, gatt, wo, gpost, g1, gpre, sc2, sh2, w1, w2, gpost2, g2):
    B, S, D = x.shape
    tm = min(TM_OUT, S)
    row = lambda n: pl.BlockSpec((1, n), lambda b, s: (0, 0))
    per_b = lambda n: pl.BlockSpec((1, 1, n), lambda b, s: (b, 0, 0))
    const = lambda a: pl.BlockSpec(a.shape, lambda b, s: (0,) * a.ndim, pipeline_mode=pl.Buffered(1))
    tok = lambda n: pl.BlockSpec((1, tm, n), lambda b, s: (b, s, 0))
    return pl.pallas_call(
        _outmlp_kernel,
        grid=(B, S // tm),
        in_specs=[tok(D), tok(D_RNN), tok(D_ATT), row(D_ATT), const(wo), row(D), per_b(D), row(D),
                  per_b(D), per_b(D), const(w1), const(w2), row(D), per_b(D)],
        out_specs=tok(D),
        out_shape=jax.ShapeDtypeStruct((B, S, D), F32),
        compiler_params=pltpu.CompilerParams(dimension_semantics=("arbitrary", "arbitrary"),
                                             vmem_limit_bytes=VMEM_LIMIT),
        name="outmlp",
    )(x, rnn_n, att, gatt, wo, gpost, g1, gpre, sc2, sh2, w1, w2, gpost2, g2)


def _block_diag(w):
    n, k, _ = w.shape
    return jnp.einsum('nij,nm->nimj', w, jnp.eye(n, dtype=w.dtype)).reshape(n * k, n * k)


def _layer(x, c, ada_w, ada_b, pre_norm_mix, w_in, conv_w, conv_b, lru_wa, lru_ba, lru_wx, lru_bx, lru_lambda,
           cmp_pos_k, cmp_w1_k, cmp_w2_k, cmp_pos_v, cmp_w1_v, cmp_w2_v, norm_rnn_out, norm_att_out, w_out,
           post_norm_mix, pre_norm_mlp, w_ff1, w_ff2, post_norm_mlp):
    B, S, D = x.shape
    G = N_KV
    row = lambda v: v.reshape(1, -1)

    mod = _ada(c, ada_w, ada_b)
    sh1, sc1, g1, sh2, sc2, g2 = [m.reshape(B, 1, D) for m in jnp.split(mod, 6, axis=-1)]

    gate_cols = [OFF_GATE + br * N_HEADS + g * REP + r for g in range(G) for br in range(3) for r in range(REP)]
    w_gate = w_in[:, jnp.asarray(gate_cols)].reshape(D, G, 3 * REP)
    w_gate = jnp.pad(w_gate, ((0, 0), (0, 0), (0, GATE_ROWS - 3 * REP))).reshape(D, G * GATE_ROWS)
    w_in_p = jnp.concatenate([w_in[:, :OFF_GATE], jnp.pad(w_gate, ((0, 0), (0, GATE_PAD - G * GATE_ROWS)))],
                             axis=1).astype(BF16)
    wa = _block_diag(lru_wa).astype(BF16)
    wx = _block_diag(lru_wx).astype(BF16)
    half = CMP_LEN // 2 * HEAD_DIM

    def w1_cat(w1):
        return jnp.concatenate([w1[:half], w1[half:]], axis=1).astype(BF16)

    def pos_rows(pos):
        return jnp.pad(pos.reshape(2, half), ((0, 14), (0, 0))).astype(BF16)

    rnn_n, qT, ks, vsT, kw, vwT, kc_in, vc_in, gT = _inproj(
        x, row(pre_norm_mix), sc1, sh1, w_in_p, conv_w, row(conv_b), wa, row(lru_ba), wx, row(lru_bx),
        row(lru_lambda), row(norm_rnn_out))
    kc, vc = _compress(kc_in, vc_in, w1_cat(cmp_w1_k), cmp_w2_k.astype(BF16), pos_rows(cmp_pos_k),
                       w1_cat(cmp_w1_v), cmp_w2_v.astype(BF16), pos_rows(cmp_pos_v))
    att = _attention(qT, ks, vsT, kw, vwT, kc, vc.transpose(0, 1, 3, 2), gT)

    return _outmlp(x, rnn_n, att, row(norm_att_out), w_out.astype(BF16), row(post_norm_mix), g1,
                   row(pre_norm_mlp), sc2, sh2, w_ff1.astype(BF16), w_ff2.astype(BF16), row(post_norm_mlp), g2)


def kernel(x, c, ada_w, ada_b, pre_norm_mix, w_in, conv_w, conv_b, lru_wa, lru_ba, lru_wx, lru_bx, lru_lambda,
           cmp_pos_k, cmp_w1_k, cmp_w2_k, cmp_pos_v, cmp_w1_v, cmp_w2_v, norm_rnn_out, norm_att_out, w_out,
           post_norm_mix, pre_norm_mlp, w_ff1, w_ff2, post_norm_mlp):
    for l in range(ada_w.shape[0]):
        x = _layer(x, c, ada_w[l], ada_b[l], pre_norm_mix[l], w_in[l], conv_w[l], conv_b[l], lru_wa[l], lru_ba[l],
                   lru_wx[l], lru_bx[l], lru_lambda[l], cmp_pos_k[l], cmp_w1_k[l], cmp_w2_k[l], cmp_pos_v[l],
                   cmp_w1_v[l], cmp_w2_v[l], norm_rnn_out[l], norm_att_out[l], w_out[l], post_norm_mix[l],
                   pre_norm_mlp[l], w_ff1[l], w_ff2[l], post_norm_mlp[l])
    return x
```

```python
import functools

import jax
import jax.numpy as jnp
from jax import lax
from jax.experimental import pallas as pl
from jax.experimental.pallas import tpu as pltpu

F32 = jnp.float32
BF16 = jnp.bfloat16
I32 = jnp.int32

D_MODEL = 1024
D_RNN = 512
RNN_BLOCKS = 8
CONV_WIDTH = 4
LRU_C = 8.0
N_HEADS = 8
HEAD_DIM = 64
N_KV = 2
REP = N_HEADS // N_KV
D_ATT = N_HEADS * HEAD_DIM
CMP_LEN = 32
CMP_STRIDE = 16
CMP_HIDDEN = 256
SLC_BLOCK = 64
SLC_TOPK = 16
WINDOW = 512
D_FF = 4 * D_MODEL
EPS = 1e-6
NEG = -1e30
FORCED = 1e4
LOG2E = 1.4426950408889634

KV_COLS = 6 * N_KV * HEAD_DIM
N_GATE = 3 * N_HEADS
GATE_PAD = 128
GATE_ROWS = 16
OFF_Q = 2 * D_RNN
OFF_KV = OFF_Q + D_ATT
OFF_GATE = OFF_KV + KV_COLS
D_IN_PAD = OFF_GATE + GATE_PAD

TM_IN = 512
TM_OUT = 1024
TQ = 256
TK = 256
NCHAIN = 2
KAUG = 256
VROWS = 80
VMEM_LIMIT = 56 * 1024 * 1024

SEL, WIN = 0, 1
MASK_NONE, MASK_CAUSAL, MASK_WINDOW_LOW, MASK_ALL = 0, 1, 2, 3
POS_COL = 2 * HEAD_DIM


def _gelu_tanh(x):
    return 0.5 * x * (1.0 + jnp.tanh(0.7978845608028654 * (x + 0.044715 * (x * x * x))))


def _rms(x):
    return x * lax.rsqrt(jnp.mean(x * x, axis=-1, keepdims=True) + EPS)


def _ada_kernel(c_ref, w_ref, b_ref, o_ref):
    c = c_ref[...]
    a = c * jax.nn.sigmoid(c)
    o_ref[...] = jnp.dot(a, w_ref[...], preferred_element_type=F32,
                         precision=lax.Precision.HIGHEST) + b_ref[...]


def _ada(c, w, b):
    B, D = c.shape
    N = w.shape[1]
    tn = 1024
    return pl.pallas_call(
        _ada_kernel,
        grid=(N // tn,),
        in_specs=[pl.BlockSpec((B, D), lambda j: (0, 0)),
                  pl.BlockSpec((D, tn), lambda j: (0, j)),
                  pl.BlockSpec((1, tn), lambda j: (0, j))],
        out_specs=pl.BlockSpec((B, tn), lambda j: (0, j)),
        out_shape=jax.ShapeDtypeStruct((B, N), F32),
        name="ada",
    )(c, w, b.reshape(1, N))


def _inproj_kernel(x_ref, gain_ref, sc_ref, sh_ref, w_ref, cw_ref, cb_ref, wa_ref, ba_ref, wx_ref, bx_ref,
                   lam_ref, grnn_ref,
                   rnn_ref, qT_ref, ks_ref, vs_ref, kw_ref, vw_ref, kc_ref, vc_ref, g_ref,
                   xbuf, hcar, a_s, u_s, h_s):
    tm = x_ref.shape[1]
    hd = HEAD_DIM

    @pl.when(pl.program_id(1) == 0)
    def _():
        xbuf[0:8, :] = jnp.zeros((8, D_RNN), F32)
        hcar[...] = jnp.zeros((1, D_RNN), F32)

    x = x_ref[0]
    h = _rms(x) * (gain_ref[...] * (1.0 + sc_ref[0])) + sh_ref[0]
    hb = h.astype(BF16)

    xr = jnp.dot(hb, w_ref[:, D_RNN:OFF_Q], preferred_element_type=F32)
    xbuf[8:8 + tm, :] = xr
    y = (cw_ref[3:4, :] * xr + cw_ref[2:3, :] * xbuf[7:7 + tm, :]
         + cw_ref[1:2, :] * xbuf[6:6 + tm, :] + cw_ref[0:1, :] * xbuf[5:5 + tm, :]) + cb_ref[...]
    xbuf[0:8, :] = xbuf[tm:tm + 8, :]

    yb = y.astype(BF16)
    r = jax.nn.sigmoid(jnp.dot(yb, wa_ref[...], preferred_element_type=F32) + ba_ref[...])
    i = jax.nn.sigmoid(jnp.dot(yb, wx_ref[...], preferred_element_type=F32) + bx_ref[...])
    nl = -lam_ref[...]
    softplus = jnp.maximum(nl, 0.0) + jnp.log(1.0 + jnp.exp(-jnp.abs(nl)))
    a = jnp.exp((-LRU_C) * r * softplus)
    a_s[...] = a
    u_s[...] = jnp.sqrt(1.0 - a * a) * (i * y)

    qT_ref[0] = (jnp.dot(hb, w_ref[:, OFF_Q:OFF_KV], preferred_element_type=F32)
                 * (HEAD_DIM ** -0.5 * LOG2E)).T.astype(BF16)
    kv = jnp.dot(hb, w_ref[:, OFF_KV:OFF_GATE], preferred_element_type=F32)
    width = N_KV * hd
    vsT = kv[:, 3 * width:4 * width].T
    vwT = kv[:, 5 * width:6 * width].T
    for gi in range(N_KV):
        kc_ref[0, gi] = kv[:, gi * hd:(gi + 1) * hd]
        vc_ref[0, gi] = kv[:, width + gi * hd:width + (gi + 1) * hd]
        ks_ref[0, gi] = kv[:, 2 * width + gi * hd:2 * width + (gi + 1) * hd].astype(BF16)
        kw_ref[0, gi] = kv[:, 4 * width + gi * hd:4 * width + (gi + 1) * hd].astype(BF16)
        for j in range(tm // TK):
            vs_ref[0, gi, j] = vsT[gi * hd:(gi + 1) * hd, j * TK:(j + 1) * TK].astype(BF16)
            vw_ref[0, gi, j] = vwT[gi * hd:(gi + 1) * hd, j * TK:(j + 1) * TK].astype(BF16)
    gates = jax.nn.sigmoid(jnp.dot(hb, w_ref[:, OFF_GATE:D_IN_PAD], preferred_element_type=F32)).T
    for gi in range(N_KV):
        g_ref[0, gi] = gates[gi * GATE_ROWS:(gi + 1) * GATE_ROWS, :]
    g = jnp.dot(hb, w_ref[:, 0:D_RNN], preferred_element_type=F32)


    rows = lax.broadcasted_iota(I32, (8, D_RNN), 0)
    hprev = hcar[...]
    for gi in range(tm // 8):
        ag = a_s[gi * 8:gi * 8 + 8, :]
        ug = u_s[gi * 8:gi * 8 + 8, :]
        for k in (1, 2, 4):
            a_sh = jnp.where(rows >= k, pltpu.roll(ag, k, 0), 1.0)
            u_sh = jnp.where(rows >= k, pltpu.roll(ug, k, 0), 0.0)
            ug = ag * u_sh + ug
            ag = ag * a_sh
        hg = ag * hprev + ug
        h_s[gi * 8:gi * 8 + 8, :] = hg
        hprev = hg[7:8, :]
    hcar[...] = hprev

    rnn = _gelu_tanh(g) * h_s[...]
    rnn_ref[0] = (_rms(rnn) * grnn_ref[...]).astype(BF16)


def _inproj(x, gain, sc, sh, w_in, conv_w, conv_b, wa, ba, wx, bx, lam, grnn):
    B, S, D = x.shape
    tm = min(TM_IN, S)
    row = lambda n: pl.BlockSpec((1, n), lambda b, s: (0, 0))
    per_b = lambda n: pl.BlockSpec((1, 1, n), lambda b, s: (b, 0, 0))
    full = lambda a: pl.BlockSpec(a.shape, lambda b, s: (0,) * a.ndim)
    tok = lambda n: pl.BlockSpec((1, tm, n), lambda b, s: (b, s, 0))
    G = N_KV
    grp = pl.BlockSpec((1, G, tm, HEAD_DIM), lambda b, s: (b, 0, s, 0))
    grpT = pl.BlockSpec((1, G, tm // TK, HEAD_DIM, TK), lambda b, s: (b, 0, s, 0, 0))
    return pl.pallas_call(
        _inproj_kernel,
        grid=(B, S // tm),
        in_specs=[tok(D), row(D), per_b(D), per_b(D), full(w_in), full(conv_w), row(D_RNN),
                  full(wa), row(D_RNN), full(wx), row(D_RNN), row(D_RNN), row(D_RNN)],
        out_specs=[tok(D_RNN),
                   pl.BlockSpec((1, D_ATT, tm), lambda b, s: (b, 0, s)),
                   grp, grpT, grp, grpT, grp, grp,
                   pl.BlockSpec((1, G, GATE_ROWS, tm), lambda b, s: (b, 0, 0, s))],
        out_shape=[jax.ShapeDtypeStruct((B, S, D_RNN), BF16),
                   jax.ShapeDtypeStruct((B, D_ATT, S), BF16),
                   jax.ShapeDtypeStruct((B, G, S, HEAD_DIM), BF16),
                   jax.ShapeDtypeStruct((B, G, S // TK, HEAD_DIM, TK), BF16),
                   jax.ShapeDtypeStruct((B, G, S, HEAD_DIM), BF16),
                   jax.ShapeDtypeStruct((B, G, S // TK, HEAD_DIM, TK), BF16),
                   jax.ShapeDtypeStruct((B, G, S, HEAD_DIM), F32),
                   jax.ShapeDtypeStruct((B, G, S, HEAD_DIM), F32),
                   jax.ShapeDtypeStruct((B, G, GATE_ROWS, S), F32)],
        scratch_shapes=[pltpu.VMEM((tm + 8, D_RNN), F32), pltpu.VMEM((1, D_RNN), F32),
                        pltpu.VMEM((tm, D_RNN), F32), pltpu.VMEM((tm, D_RNN), F32),
                        pltpu.VMEM((tm, D_RNN), F32)],
        compiler_params=pltpu.CompilerParams(dimension_semantics=("arbitrary", "arbitrary"),
                                             vmem_limit_bytes=VMEM_LIMIT),
        name="inproj",
    )(x, gain, sc, sh, w_in, conv_w, conv_b, wa, ba, wx, bx, lam, grnn)


def _compress_kernel(rk_ref, rv_ref, w1k_ref, w2k_ref, pk_ref, w1v_ref, w2v_ref, pv_ref, kc_ref, vc_ref):
    def one(r_ref, w1_ref, w2_ref, pos_ref, o_ref):
        n = r_ref.shape[2] // CMP_STRIDE
        p = jnp.zeros((n, 2 * CMP_HIDDEN), F32)
        for l in range(CMP_STRIDE):
            rows = r_ref[0, 0, pl.ds(l, n, stride=CMP_STRIDE), :].astype(BF16)
            p = p + jnp.dot(rows, w1_ref[l * HEAD_DIM:(l + 1) * HEAD_DIM, :], preferred_element_type=F32)
        posb = jnp.dot(pos_ref[...], w1_ref[...], preferred_element_type=F32)
        bias = posb[0:1, 0:CMP_HIDDEN] + posb[1:2, CMP_HIDDEN:]
        pre = p[:, 0:CMP_HIDDEN] + pltpu.roll(p[:, CMP_HIDDEN:], n - 1, 0) + bias
        hid = _gelu_tanh(pre).astype(BF16)
        o_ref[0, 0] = jnp.dot(hid, w2_ref[...], preferred_element_type=F32).astype(BF16)

    one(rk_ref, w1k_ref, w2k_ref, pk_ref, kc_ref)
    one(rv_ref, w1v_ref, w2v_ref, pv_ref, vc_ref)


def _compress(rk, rv, w1k, w2k, posk, w1v, w2v, posv):
    B, G, S, width = rk.shape
    n = S // CMP_STRIDE
    blk = pl.BlockSpec((1, 1, S, width), lambda b, g: (b, g, 0, 0))
    full = lambda a: pl.BlockSpec(a.shape, lambda b, g: (0,) * a.ndim)
    out = pl.BlockSpec((1, 1, n, HEAD_DIM), lambda b, g: (b, g, 0, 0))
    return pl.pallas_call(
        _compress_kernel,
        grid=(B, G),
        in_specs=[blk, blk, full(w1k), full(w2k), full(posk), full(w1v), full(w2v), full(posv)],
        out_specs=[out, out],
        out_shape=[jax.ShapeDtypeStruct((B, G, n, HEAD_DIM), BF16)] * 2,
        compiler_params=pltpu.CompilerParams(vmem_limit_bytes=VMEM_LIMIT),
        name="compress",
    )(rk, rv, w1k, w2k, posk, w1v, w2v, posv)


def _attn_kernel(qT_ref, ks_ref, vs_ref, kw_ref, vw_ref, kc_ref, vcT_ref, g_ref,
                 o_ref,
                 kall, vall, qall, kca, m_all, acc_all, al_s, sbuf, pbuf, oacc, val_s, rank_s, selbias_s,
                 bias_tbl, cmask,
                 oT_s, st_s, *, n_sel):
    S = ks_ref.shape[2]
    nkt = S // TK
    ncmp = kc_ref.shape[2]
    b = pl.program_id(0)
    g = pl.program_id(1)
    step = pl.program_id(2)
    W4 = REP * TQ
    cmp_per_tile = TQ // CMP_STRIDE

    @pl.when((b == 0) & (g == 0) & (step == 0))
    def _():
        col = lax.broadcasted_iota(I32, (TK, KAUG), 1)
        is_blk = (col == POS_COL) | (col == POS_COL + 2)
        is_off = (col == POS_COL + 1) | (col == POS_COL + 3)
        ones_row = (lax.broadcasted_iota(I32, (VROWS - HEAD_DIM, TK), 0) == 0).astype(BF16)
        for kt in range(nkt):
            key = kt * TK + lax.broadcasted_iota(I32, (TK, KAUG), 0)
            pos = jnp.where(is_blk, (key // SLC_BLOCK).astype(F32),
                            jnp.where(is_off, (key % SLC_BLOCK).astype(F32), 0.0))
            onehot = (col - HEAD_DIM == key // SLC_BLOCK).astype(F32)
            kall[SEL, kt * TK:(kt + 1) * TK, :] = (pos + onehot).astype(BF16)
            kall[WIN, kt * TK:(kt + 1) * TK, :] = pos.astype(BF16)
            vall[SEL, kt, HEAD_DIM:VROWS, :] = ones_row
            vall[WIN, kt, HEAD_DIM:VROWS, :] = ones_row
        ko = lax.broadcasted_iota(I32, (TK, W4), 0)
        to = lax.broadcasted_iota(I32, (TK, W4), 1) % TQ
        bias_tbl[MASK_NONE] = jnp.zeros((TK, W4), F32)
        bias_tbl[MASK_CAUSAL] = jnp.where(ko <= to, 0.0, NEG)
        bias_tbl[MASK_WINDOW_LOW] = jnp.where(ko > to, 0.0, NEG)
        bias_tbl[MASK_ALL] = jnp.full((TK, W4), NEG, F32)
        u = lax.broadcasted_iota(I32, (2 * ncmp, W4), 0)
        tc = lax.broadcasted_iota(I32, (2 * ncmp, W4), 1) % TQ
        cmask[...] = jnp.where(u <= ncmp + ((tc + 1) // CMP_STRIDE) - 2, 0.0, NEG)
        cc = lax.broadcasted_iota(I32, (ncmp, KAUG), 0)
        colc = lax.broadcasted_iota(I32, (ncmp, KAUG), 1)
        kca[...] = jnp.where((colc == POS_COL) | (colc == POS_COL + 2), (cc // 4).astype(F32),
                             jnp.where((colc == POS_COL + 1) | (colc == POS_COL + 3),
                                       (CMP_STRIDE * (cc % 4)).astype(F32), 0.0)).astype(BF16)
        qall[...] = jnp.zeros(qall.shape, BF16)

    lane = lax.broadcasted_iota(I32, (1, W4), 1)
    head = g * REP + lane // TQ
    slope = lax.bitcast_convert_type((126 - head) << 23, F32)

    @pl.when(step == 0)
    def _():
        for kt in range(nkt):
            kall[SEL, kt * TK:(kt + 1) * TK, 0:HEAD_DIM] = ks_ref[0, 0, kt * TK:(kt + 1) * TK, :]
            kall[WIN, kt * TK:(kt + 1) * TK, 0:HEAD_DIM] = kw_ref[0, 0, kt * TK:(kt + 1) * TK, :]
            vall[SEL, kt, 0:HEAD_DIM, :] = vs_ref[0, 0, kt]
            vall[WIN, kt, 0:HEAD_DIM, :] = vw_ref[0, 0, kt]
        kca[:, 0:HEAD_DIM] = kc_ref[0, 0]
        c_hi = (slope * LOG2E).astype(BF16).astype(F32)
        c_lo = slope * LOG2E - c_hi
        r16 = lax.broadcasted_iota(I32, (16, W4), 0)
        alibi = jnp.where(r16 == 0, c_hi * SLC_BLOCK, jnp.where(r16 == 1, c_hi, jnp.where(
            r16 == 2, c_lo * SLC_BLOCK, jnp.where(r16 == 3, c_lo, 0.0)))).astype(BF16)
        for c in range(NCHAIN):
            qall[c, SEL, POS_COL:POS_COL + 16, :] = alibi
            qall[c, WIN, POS_COL:POS_COL + 16, :] = alibi

    chains = [_attn_chain(c, NCHAIN * step + c, qT_ref, kc_ref, vcT_ref, g_ref, o_ref, kall, vall, qall.at[c], kca,
                          m_all.at[c], acc_all.at[c], al_s.at[c], sbuf.at[c], pbuf.at[c], oacc.at[c], val_s.at[c],
                          rank_s.at[c], selbias_s.at[c], bias_tbl, cmask, oT_s.at[c], st_s.at[c], n_sel)
              for c in range(NCHAIN)]
    live = list(chains)
    while live:
        live = [ch for ch in live if next(ch, _DONE) is not _DONE]


_DONE = object()


def _attn_chain(c, qi, qT_ref, kc_ref, vcT_ref, g_ref, o_ref, kall, vall, qall, kca, m_all, acc_all, al_s, sbuf,
                pbuf, oacc, val_s, rank_s, selbias_s, bias_tbl, cmask, oT_s, st_s, n_sel):
    S = kall.shape[1]
    ncmp = kc_ref.shape[2]
    nblk = S // SLC_BLOCK
    cmp_per_tile = TQ // CMP_STRIDE
    q0 = qi * TQ
    W4 = REP * TQ
    lane = lax.broadcasted_iota(I32, (1, W4), 1)
    lanes_c = slice(c * TQ, (c + 1) * TQ)

    q = qT_ref[0, :, lanes_c]
    for r in range(REP):
        qr = q[r * HEAD_DIM:(r + 1) * HEAD_DIM, :]
        qall[SEL, 0:HEAD_DIM, r * TQ:(r + 1) * TQ] = qr
        qall[WIN, 0:HEAD_DIM, r * TQ:(r + 1) * TQ] = qr

    gall = g_ref[0, 0, :, lanes_c]
    gates = [gall[br * REP:(br + 1) * REP, :] for br in range(3)]

    def stage_scores(br, tiles_masks):
        col_max = None
        for h, (tile, mask) in enumerate(tiles_masks):
            r0 = pl.multiple_of(tile * TK, TK)
            s = jnp.dot(kall[br, pl.ds(r0, TK), :], qall[br], preferred_element_type=F32)
            if mask is not None:
                s = s + bias_tbl[mask]
            sbuf[h * TK:(h + 1) * TK, :] = s
            c = jnp.max(s, axis=0, keepdims=True)
            col_max = c if col_max is None else jnp.maximum(col_max, c)
        m_old = m_all[br]
        m_new = jnp.maximum(m_old, col_max)
        m_all[br] = m_new
        return jnp.exp2(m_old - m_new)

    def stage_probs(br, ntile):
        pbuf[0:ntile * TK, :] = jnp.exp2(sbuf[0:ntile * TK, :] - m_all[br]).astype(BF16)

    def stage_values(br, tiles, al_row):
        acc = acc_all[br] * al_row
        for h, tile in enumerate(tiles):
            acc = acc + jnp.dot(vall[br, tile], pbuf[h * TK:(h + 1) * TK, :], preferred_element_type=F32)
        acc_all[br] = acc

    def start_branch(br, tl, mask1):
        al_s[...] = stage_scores(br, [(tl[0], MASK_CAUSAL), (tl[1], mask1)])

    def finish_branch(br, n, tl):
        rest = jnp.maximum(n - 2, 0)
        npair = rest // 2

        def body(j, carry):
            al_prev = al_s[...]
            stage_probs(br, 2)
            al_s[...] = stage_scores(br, [(tl[2 * j], None), (tl[2 * j + 1], None)])
            stage_values(br, [tl[2 * j - 2], tl[2 * j - 1]], al_prev)
            return carry

        lax.fori_loop(1, 1 + npair, body, 0)
        last = 2 * npair

        @pl.when(rest % 2 == 1)
        def _():
            al_prev = al_s[...]
            stage_probs(br, 2)
            al_k = stage_scores(br, [(tl[n - 1], None)])
            stage_values(br, [tl[last], tl[last + 1]], al_prev)
            stage_probs(br, 1)
            stage_values(br, [tl[n - 1]], al_k)

        @pl.when(rest % 2 == 0)
        def _():
            stage_probs(br, 2)
            stage_values(br, [tl[last], tl[last + 1]], al_s[...])

    m_all[...] = jnp.full(m_all.shape, NEG, F32)
    acc_all[...] = jnp.zeros(acc_all.shape, F32)
    w1 = jnp.maximum(qi - 2, 0)
    w2 = jnp.maximum(qi - 1, 0)
    al_w0 = stage_scores(WIN, [(qi, MASK_CAUSAL),
                               (w1, jnp.where(qi >= 2, MASK_WINDOW_LOW, jnp.where(qi >= 1, MASK_NONE, MASK_ALL)))])
    yield

    c0 = pl.multiple_of(ncmp - cmp_per_tile * qi, cmp_per_tile)
    sc = jnp.dot(kca[...], qall[WIN], preferred_element_type=F32) + cmask[pl.ds(c0, ncmp), :]
    yield
    e = jnp.exp2(sc - jnp.max(sc, axis=0, keepdims=True))
    tq = q0 + lane % TQ
    p = e * ((1.0 / jnp.sum(e, axis=0, keepdims=True)) * (tq >= CMP_LEN - 1).astype(F32))
    ocT = jnp.dot(vcT_ref[0, 0], p.astype(BF16), preferred_element_type=F32)
    for r in range(REP):
        oacc[:, r * TQ:(r + 1) * TQ] = gates[0][r:r + 1, :] * ocT[:, r * TQ:(r + 1) * TQ]
    yield

    stage_probs(WIN, 2)
    al_w1 = stage_scores(WIN, [(w2, jnp.where(qi >= 2, MASK_NONE, MASK_ALL))])
    stage_values(WIN, [qi, w1], al_w0)
    yield

    psum = p[:, 0:TQ]
    for r in range(1, REP):
        psum = psum + p[:, r * TQ:(r + 1) * TQ]
    jj = lax.broadcasted_iota(I32, (nblk, ncmp), 0)
    cc = lax.broadcasted_iota(I32, (nblk, ncmp), 1)
    ovT = ((CMP_STRIDE * cc < SLC_BLOCK * jj + SLC_BLOCK)
           & (CMP_STRIDE * cc + CMP_LEN > SLC_BLOCK * jj)).astype(F32)
    impT = jnp.dot(ovT, psum, preferred_element_type=F32, precision=lax.Precision.HIGHEST)

    j_i = lax.broadcasted_iota(I32, (nblk, TQ), 0)
    t1 = q0 + lax.broadcasted_iota(I32, (nblk, TQ), 1)
    cur = t1 // SLC_BLOCK
    forced = (j_i == 0) | (j_i == cur) | (j_i == cur - 1)
    visible = SLC_BLOCK * j_i <= t1
    val_s[...] = jnp.where(forced, FORCED, jnp.where(visible, impT, NEG))
    yield
    ngrp = nblk // 8
    rank_s[...] = jnp.zeros(rank_s.shape, F32)
    j8 = lax.broadcasted_iota(I32, (8, TQ), 0)
    for ib in range(ngrp):
        @pl.when(8 * ib * SLC_BLOCK < q0 + TQ)
        def _(ib=ib):
            vals = [val_s[8 * jb:8 * jb + 8, :] for jb in range(ngrp)]
            ranks = [rank_s[8 * jb:8 * jb + 8, :] for jb in range(ngrp)]
            for i in range(8 * ib, 8 * ib + 8):
                row = jnp.broadcast_to(val_s[i:i + 1, :], (8, TQ))
                for jb in range(ngrp):
                    if jb > ib:
                        hit = row >= vals[jb]
                    elif jb < ib:
                        hit = row > vals[jb]
                    else:
                        hit = jnp.where(j8 > i - 8 * jb, jnp.where(row >= vals[jb], 1.0, 0.0),
                                        jnp.where(row > vals[jb], 1.0, 0.0)) > 0.5
                    ranks[jb] = ranks[jb] + jnp.where(hit, 1.0, 0.0)
            for jb in range(ngrp):
                rank_s[8 * jb:8 * jb + 8, :] = ranks[jb]
    yield
    ranks = [rank_s[8 * jb:8 * jb + 8, :] for jb in range(ngrp)]
    blocks_per_tile = TK // SLC_BLOCK
    st_s[0] = qi
    n_selt = jnp.int32(1)
    for jb in range(ngrp):
        chosen = ranks[jb] < n_sel
        selb = jnp.where(chosen, 0.0, NEG)
        for r in range(REP):
            selbias_s[8 * jb:8 * jb + 8, r * TQ:(r + 1) * TQ] = selb
        any_q = jnp.max(jnp.where(chosen, 1.0, 0.0), axis=1, keepdims=True)
        for hh in range(8 // blocks_per_tile):
            kt = (8 * jb) // blocks_per_tile + hh
            hit = jnp.max(any_q[hh * blocks_per_tile:(hh + 1) * blocks_per_tile, :]) > 0.5
            st_s[n_selt] = kt
            n_selt = n_selt + jnp.where(hit & (kt < qi), 1, 0)
    qall[SEL, HEAD_DIM:HEAD_DIM + nblk, :] = selbias_s[...].astype(BF16)
    yield

    stage_probs(WIN, 1)
    stage_values(WIN, [w2], al_w1)
    start_branch(SEL, st_s, jnp.where(n_selt >= 2, MASK_NONE, MASK_ALL))
    yield
    finish_branch(SEL, n_selt, st_s)

    def normalised(br):
        acc = acc_all[br]
        return acc[0:HEAD_DIM, :] / acc[HEAD_DIM:HEAD_DIM + 1, :]

    o_sel = normalised(SEL)
    o_win = normalised(WIN)
    for r in range(REP):
        lanes = slice(r * TQ, (r + 1) * TQ)
        oT_s[r * HEAD_DIM:(r + 1) * HEAD_DIM, :] = (oacc[:, lanes] + gates[1][r:r + 1, :] * o_sel[:, lanes]
                                                    + gates[2][r:r + 1, :] * o_win[:, lanes])
    o_ref[0, lanes_c, :] = oT_s[...].T.astype(BF16)


def _attention(qT, ks, vs, kw, vw, kc, vcT, gT):
    B, _, S = qT.shape
    G = N_KV
    nkt = S // TK
    ncmp = kc.shape[2]
    nblk = S // SLC_BLOCK
    n_sel = min(SLC_TOPK, nblk)
    W4 = REP * TQ
    res4 = lambda a: pl.BlockSpec((1, 1) + a.shape[2:], lambda b, g, i: (b, g) + (0,) * (a.ndim - 2))
    return pl.pallas_call(
        functools.partial(_attn_kernel, n_sel=n_sel),
        grid=(B, G, S // (NCHAIN * TQ)),
        in_specs=[pl.BlockSpec((1, REP * HEAD_DIM, NCHAIN * TQ), lambda b, g, i: (b, g, i)),
                  res4(ks), res4(vs), res4(kw), res4(vw), res4(kc), res4(vcT),
                  pl.BlockSpec((1, 1, GATE_ROWS, NCHAIN * TQ), lambda b, g, i: (b, g, 0, i))],
        out_specs=pl.BlockSpec((1, NCHAIN * TQ, REP * HEAD_DIM), lambda b, g, i: (b, i, g)),
        out_shape=jax.ShapeDtypeStruct((B, S, D_ATT), BF16),
        scratch_shapes=[pltpu.VMEM((2, S, KAUG), BF16),
                        pltpu.VMEM((2, nkt, VROWS, TK), BF16),
                        pltpu.VMEM((NCHAIN, 2, KAUG, W4), BF16),
                        pltpu.VMEM((ncmp, KAUG), BF16),
                        pltpu.VMEM((NCHAIN, 2, 1, W4), F32),
                        pltpu.VMEM((NCHAIN, 2, VROWS, W4), F32),
                        pltpu.VMEM((NCHAIN, 1, W4), F32),
                        pltpu.VMEM((NCHAIN, 2 * TK, W4), F32),
                        pltpu.VMEM((NCHAIN, 2 * TK, W4), BF16),
                        pltpu.VMEM((NCHAIN, HEAD_DIM, W4), F32),
                        pltpu.VMEM((NCHAIN, nblk, TQ), F32),
                        pltpu.VMEM((NCHAIN, nblk, TQ), F32),
                        pltpu.VMEM((NCHAIN, nblk, W4), F32),
                        pltpu.VMEM((4, TK, W4), F32),
                        pltpu.VMEM((2 * ncmp, W4), F32),
                        pltpu.VMEM((NCHAIN, REP * HEAD_DIM, TQ), F32),
                        pltpu.SMEM((NCHAIN, nkt + 2), I32)],
        compiler_params=pltpu.CompilerParams(dimension_semantics=("arbitrary", "arbitrary", "arbitrary"),
                                             vmem_limit_bytes=VMEM_LIMIT),
        name="attn",
    )(qT, ks, vs, kw, vw, kc, vcT, gT)


def _outmlp_kernel(x_ref, rnn_ref, att_ref, gatt_ref, wo_ref, gpost_ref, g1_ref, gpre_ref, sc2_ref, sh2_ref,
                   w1_ref, w2_ref, gpost2_ref, g2_ref, o_ref):
    tm = x_ref.shape[1]
    halves = [slice(0, tm // 2), slice(tm // 2, tm)]
    x1s, h2s = [], []
    for rows in halves:
        att_n = (_rms(att_ref[0, rows, :].astype(F32)) * gatt_ref[...]).astype(BF16)
        y = (jnp.dot(rnn_ref[0, rows, :], wo_ref[0:D_RNN, :], preferred_element_type=F32)
             + jnp.dot(att_n, wo_ref[D_RNN:, :], preferred_element_type=F32))
        x1 = x_ref[0, rows, :] + (1.0 + g1_ref[0]) * (_rms(y) * gpost_ref[...])
        x1s.append(x1)
        h2s.append((_rms(x1) * (gpre_ref[...] * (1.0 + sc2_ref[0])) + sh2_ref[0]).astype(BF16))
    fc = 1024
    for rows, x1, h2 in zip(halves, x1s, h2s):
        ff = jnp.zeros(x1.shape, F32)
        for c in range(D_FF // fc):
            hid = jnp.maximum(jnp.dot(h2, w1_ref[:, c * fc:(c + 1) * fc], preferred_element_type=F32), 0.0)
            ff = ff + jnp.dot((hid * hid).astype(BF16), w2_ref[c * fc:(c + 1) * fc, :],
                              preferred_element_type=F32)
        o_ref[0, rows, :] = x1 + (1.0 + g2_ref[0]) * (_rms(ff) * gpost2_ref[...])


def _outmlp(x, rnn_n, att, gatt, wo, gpost, g1, gpre, sc2, sh2, w1, w2, gpost2, g2):
    B, S, D = x.shape
    tm = min(TM_OUT, S)
    row = lambda n: pl.BlockSpec((1, n), lambda b, s: (0, 0))
    per_b = lambda n: pl.BlockSpec((1, 1, n), lambda b, s: (b, 0, 0))
    const = lambda a: pl.BlockSpec(a.shape, lambda b, s: (0,) * a.ndim, pipeline_mode=pl.Buffered(1))
    tok = lambda n: pl.BlockSpec((1, tm, n), lambda b, s: (b, s, 0))
    return pl.pallas_call(
        _outmlp_kernel,
        grid=(B, S // tm),
        in_specs=[tok(D), tok(D_RNN), tok(D_ATT), row(D_ATT), const(wo), row(D), per_b(D), row(D),
                  per_b(D), per_b(D), const(w1), const(w2), row(D), per_b(D)],
        out_specs=tok(D),
        out_shape=jax.ShapeDtypeStruct((B, S, D), F32),
        compiler_params=pltpu.CompilerParams(dimension_semantics=("arbitrary", "arbitrary"),
                                             vmem_limit_bytes=VMEM_LIMIT),
        name="outmlp",
    )(x, rnn_n, att, gatt, wo, gpost, g1, gpre, sc2, sh2, w1, w2, gpost2, g2)


def _block_diag(w):
    n, k, _ = w.shape
    return jnp.einsum('nij,nm->nimj', w, jnp.eye(n, dtype=w.dtype)).reshape(n * k, n * k)


def _layer(x, c, ada_w, ada_b, pre_norm_mix, w_in, conv_w, conv_b, lru_wa, lru_ba, lru_wx, lru_bx, lru_lambda,
           cmp_pos_k, cmp_w1_k, cmp_w2_k, cmp_pos_v, cmp_w1_v, cmp_w2_v, norm_rnn_out, norm_att_out, w_out,
           post_norm_mix, pre_norm_mlp, w_ff1, w_ff2, post_norm_mlp):
    B, S, D = x.shape
    G = N_KV
    row = lambda v: v.reshape(1, -1)

    mod = _ada(c, ada_w, ada_b)
    sh1, sc1, g1, sh2, sc2, g2 = [m.reshape(B, 1, D) for m in jnp.split(mod, 6, axis=-1)]

    gate_cols = [OFF_GATE + br * N_HEADS + g * REP + r for g in range(G) for br in range(3) for r in range(REP)]
    w_gate = w_in[:, jnp.asarray(gate_cols)].reshape(D, G, 3 * REP)
    w_gate = jnp.pad(w_gate, ((0, 0), (0, 0), (0, GATE_ROWS - 3 * REP))).reshape(D, G * GATE_ROWS)
    w_in_p = jnp.concatenate([w_in[:, :OFF_GATE], jnp.pad(w_gate, ((0, 0), (0, GATE_PAD - G * GATE_ROWS)))],
                             axis=1).astype(BF16)
    wa = _block_diag(lru_wa).astype(BF16)
    wx = _block_diag(lru_wx).astype(BF16)
    half = CMP_LEN // 2 * HEAD_DIM

    def w1_cat(w1):
        return jnp.concatenate([w1[:half], w1[half:]], axis=1).astype(BF16)

    def pos_rows(pos):
        return jnp.pad(pos.reshape(2, half), ((0, 14), (0, 0))).astype(BF16)

    rnn_n, qT, ks, vsT, kw, vwT, kc_in, vc_in, gT = _inproj(
        x, row(pre_norm_mix), sc1, sh1, w_in_p, conv_w, row(conv_b), wa, row(lru_ba), wx, row(lru_bx),
        row(lru_lambda), row(norm_rnn_out))
    kc, vc = _compress(kc_in, vc_in, w1_cat(cmp_w1_k), cmp_w2_k.astype(BF16), pos_rows(cmp_pos_k),
                       w1_cat(cmp_w1_v), cmp_w2_v.astype(BF16), pos_rows(cmp_pos_v))
    att = _attention(qT, ks, vsT, kw, vwT, kc, vc.transpose(0, 1, 3, 2), gT)

    return _outmlp(x, rnn_n, att, row(norm_att_out), w_out.astype(BF16), row(post_norm_mix), g1,
                   row(pre_norm_mlp), sc2, sh2, w_ff1.astype(BF16), w_ff2.astype(BF16), row(post_norm_mlp), g2)


def kernel(x, c, ada_w, ada_b, pre_norm_mix, w_in, conv_w, conv_b, lru_wa, lru_ba, lru_wx, lru_bx, lru_lambda,
           cmp_pos_k, cmp_w1_k, cmp_w2_k, cmp_pos_v, cmp_w1_v, cmp_w2_v, norm_rnn_out, norm_att_out, w_out,
           post_norm_mix, pre_norm_mlp, w_ff1, w_ff2, post_norm_mlp):
    for l in range(ada_w.shape[0]):
        x = _layer(x, c, ada_w[l], ada_b[l], pre_norm_mix[l], w_in[l], conv_w[l], conv_b[l], lru_wa[l], lru_ba[l],
                   lru_wx[l], lru_bx[l], lru_lambda[l], cmp_pos_k[l], cmp_w1_k[l], cmp_w2_k[l], cmp_pos_v[l],
                   cmp_w1_v[l], cmp_w2_v[l], norm_rnn_out[l], norm_att_out[l], w_out[l], post_norm_mix[l],
                   pre_norm_mlp[l], w_ff1[l], w_ff2[l], post_norm_mlp[l])
    return x
```

```python
import functools

import jax
import jax.numpy as jnp
from jax import lax
from jax.experimental import pallas as pl
from jax.experimental.pallas import tpu as pltpu

F32 = jnp.float32
BF16 = jnp.bfloat16
I32 = jnp.int32

D_MODEL = 1024
D_RNN = 512
RNN_BLOCKS = 8
CONV_WIDTH = 4
LRU_C = 8.0
N_HEADS = 8
HEAD_DIM = 64
N_KV = 2
REP = N_HEADS // N_KV
D_ATT = N_HEADS * HEAD_DIM
CMP_LEN = 32
CMP_STRIDE = 16
CMP_HIDDEN = 256
SLC_BLOCK = 64
SLC_TOPK = 16
WINDOW = 512
D_FF = 4 * D_MODEL
EPS = 1e-6
NEG = -1e30
FORCED = 1e4
LOG2E = 1.4426950408889634

KV_COLS = 6 * N_KV * HEAD_DIM
N_GATE = 3 * N_HEADS
GATE_PAD = 128
GATE_ROWS = 16
OFF_Q = 2 * D_RNN
OFF_KV = OFF_Q + D_ATT
OFF_GATE = OFF_KV + KV_COLS
D_IN_PAD = OFF_GATE + GATE_PAD

TM_IN = 512
TM_OUT = 1024
TQ = 256
TK = 256
NCHAIN = 4
KAUG = 256
VROWS = 80
VMEM_LIMIT = 56 * 1024 * 1024

SEL, WIN = 0, 1
MASK_NONE, MASK_CAUSAL, MASK_WINDOW_LOW, MASK_ALL = 0, 1, 2, 3
POS_COL = 2 * HEAD_DIM


def _gelu_tanh(x):
    return 0.5 * x * (1.0 + jnp.tanh(0.7978845608028654 * (x + 0.044715 * (x * x * x))))


def _rms(x):
    return x * lax.rsqrt(jnp.mean(x * x, axis=-1, keepdims=True) + EPS)


def _ada_kernel(c_ref, w_ref, b_ref, o_ref):
    c = c_ref[...]
    a = c * jax.nn.sigmoid(c)
    o_ref[...] = jnp.dot(a, w_ref[...], preferred_element_type=F32,
                         precision=lax.Precision.HIGHEST) + b_ref[...]


def _ada(c, w, b):
    B, D = c.shape
    N = w.shape[1]
    tn = 1024
    return pl.pallas_call(
        _ada_kernel,
        grid=(N // tn,),
        in_specs=[pl.BlockSpec((B, D), lambda j: (0, 0)),
                  pl.BlockSpec((D, tn), lambda j: (0, j)),
                  pl.BlockSpec((1, tn), lambda j: (0, j))],
        out_specs=pl.BlockSpec((B, tn), lambda j: (0, j)),
        out_shape=jax.ShapeDtypeStruct((B, N), F32),
        name="ada",
    )(c, w, b.reshape(1, N))


def _inproj_kernel(x_ref, gain_ref, sc_ref, sh_ref, w_ref, cw_ref, cb_ref, wa_ref, ba_ref, wx_ref, bx_ref,
                   lam_ref, grnn_ref,
                   rnn_ref, qT_ref, ks_ref, vs_ref, kw_ref, vw_ref, kc_ref, vc_ref, g_ref,
                   xbuf, hcar, a_s, u_s, h_s):
    tm = x_ref.shape[1]
    hd = HEAD_DIM

    @pl.when(pl.program_id(1) == 0)
    def _():
        xbuf[0:8, :] = jnp.zeros((8, D_RNN), F32)
        hcar[...] = jnp.zeros((1, D_RNN), F32)

    x = x_ref[0]
    h = _rms(x) * (gain_ref[...] * (1.0 + sc_ref[0])) + sh_ref[0]
    hb = h.astype(BF16)

    xr = jnp.dot(hb, w_ref[:, D_RNN:OFF_Q], preferred_element_type=F32)
    xbuf[8:8 + tm, :] = xr
    y = (cw_ref[3:4, :] * xr + cw_ref[2:3, :] * xbuf[7:7 + tm, :]
         + cw_ref[1:2, :] * xbuf[6:6 + tm, :] + cw_ref[0:1, :] * xbuf[5:5 + tm, :]) + cb_ref[...]
    xbuf[0:8, :] = xbuf[tm:tm + 8, :]

    yb = y.astype(BF16)
    r = jax.nn.sigmoid(jnp.dot(yb, wa_ref[...], preferred_element_type=F32) + ba_ref[...])
    i = jax.nn.sigmoid(jnp.dot(yb, wx_ref[...], preferred_element_type=F32) + bx_ref[...])
    nl = -lam_ref[...]
    softplus = jnp.maximum(nl, 0.0) + jnp.log(1.0 + jnp.exp(-jnp.abs(nl)))
    a = jnp.exp((-LRU_C) * r * softplus)
    a_s[...] = a
    u_s[...] = jnp.sqrt(1.0 - a * a) * (i * y)

    qT_ref[0] = (jnp.dot(hb, w_ref[:, OFF_Q:OFF_KV], preferred_element_type=F32)
                 * (HEAD_DIM ** -0.5 * LOG2E)).T.astype(BF16)
    kv = jnp.dot(hb, w_ref[:, OFF_KV:OFF_GATE], preferred_element_type=F32)
    width = N_KV * hd
    vsT = kv[:, 3 * width:4 * width].T
    vwT = kv[:, 5 * width:6 * width].T
    for gi in range(N_KV):
        kc_ref[0, gi] = kv[:, gi * hd:(gi + 1) * hd]
        vc_ref[0, gi] = kv[:, width + gi * hd:width + (gi + 1) * hd]
        ks_ref[0, gi] = kv[:, 2 * width + gi * hd:2 * width + (gi + 1) * hd].astype(BF16)
        kw_ref[0, gi] = kv[:, 4 * width + gi * hd:4 * width + (gi + 1) * hd].astype(BF16)
        for j in range(tm // TK):
            vs_ref[0, gi, j] = vsT[gi * hd:(gi + 1) * hd, j * TK:(j + 1) * TK].astype(BF16)
            vw_ref[0, gi, j] = vwT[gi * hd:(gi + 1) * hd, j * TK:(j + 1) * TK].astype(BF16)
    gates = jax.nn.sigmoid(jnp.dot(hb, w_ref[:, OFF_GATE:D_IN_PAD], preferred_element_type=F32)).T
    for gi in range(N_KV):
        g_ref[0, gi] = gates[gi * GATE_ROWS:(gi + 1) * GATE_ROWS, :]
    g = jnp.dot(hb, w_ref[:, 0:D_RNN], preferred_element_type=F32)


    rows = lax.broadcasted_iota(I32, (8, D_RNN), 0)
    hprev = hcar[...]
    for gi in range(tm // 8):
        ag = a_s[gi * 8:gi * 8 + 8, :]
        ug = u_s[gi * 8:gi * 8 + 8, :]
        for k in (1, 2, 4):
            a_sh = jnp.where(rows >= k, pltpu.roll(ag, k, 0), 1.0)
            u_sh = jnp.where(rows >= k, pltpu.roll(ug, k, 0), 0.0)
            ug = ag * u_sh + ug
            ag = ag * a_sh
        hg = ag * hprev + ug
        h_s[gi * 8:gi * 8 + 8, :] = hg
        hprev = hg[7:8, :]
    hcar[...] = hprev

    rnn = _gelu_tanh(g) * h_s[...]
    rnn_ref[0] = (_rms(rnn) * grnn_ref[...]).astype(BF16)


def _inproj(x, gain, sc, sh, w_in, conv_w, conv_b, wa, ba, wx, bx, lam, grnn):
    B, S, D = x.shape
    tm = min(TM_IN, S)
    row = lambda n: pl.BlockSpec((1, n), lambda b, s: (0, 0))
    per_b = lambda n: pl.BlockSpec((1, 1, n), lambda b, s: (b, 0, 0))
    full = lambda a: pl.BlockSpec(a.shape, lambda b, s: (0,) * a.ndim)
    tok = lambda n: pl.BlockSpec((1, tm, n), lambda b, s: (b, s, 0))
    G = N_KV
    grp = pl.BlockSpec((1, G, tm, HEAD_DIM), lambda b, s: (b, 0, s, 0))
    grpT = pl.BlockSpec((1, G, tm // TK, HEAD_DIM, TK), lambda b, s: (b, 0, s, 0, 0))
    return pl.pallas_call(
        _inproj_kernel,
        grid=(B, S // tm),
        in_specs=[tok(D), row(D), per_b(D), per_b(D), full(w_in), full(conv_w), row(D_RNN),
                  full(wa), row(D_RNN), full(wx), row(D_RNN), row(D_RNN), row(D_RNN)],
        out_specs=[tok(D_RNN),
                   pl.BlockSpec((1, D_ATT, tm), lambda b, s: (b, 0, s)),
                   grp, grpT, grp, grpT, grp, grp,
                   pl.BlockSpec((1, G, GATE_ROWS, tm), lambda b, s: (b, 0, 0, s))],
        out_shape=[jax.ShapeDtypeStruct((B, S, D_RNN), BF16),
                   jax.ShapeDtypeStruct((B, D_ATT, S), BF16),
                   jax.ShapeDtypeStruct((B, G, S, HEAD_DIM), BF16),
                   jax.ShapeDtypeStruct((B, G, S // TK, HEAD_DIM, TK), BF16),
                   jax.ShapeDtypeStruct((B, G, S, HEAD_DIM), BF16),
                   jax.ShapeDtypeStruct((B, G, S // TK, HEAD_DIM, TK), BF16),
                   jax.ShapeDtypeStruct((B, G, S, HEAD_DIM), F32),
                   jax.ShapeDtypeStruct((B, G, S, HEAD_DIM), F32),
                   jax.ShapeDtypeStruct((B, G, GATE_ROWS, S), F32)],
        scratch_shapes=[pltpu.VMEM((tm + 8, D_RNN), F32), pltpu.VMEM((1, D_RNN), F32),
                        pltpu.VMEM((tm, D_RNN), F32), pltpu.VMEM((tm, D_RNN), F32),
                        pltpu.VMEM((tm, D_RNN), F32)],
        compiler_params=pltpu.CompilerParams(dimension_semantics=("arbitrary", "arbitrary"),
                                             vmem_limit_bytes=VMEM_LIMIT),
        name="inproj",
    )(x, gain, sc, sh, w_in, conv_w, conv_b, wa, ba, wx, bx, lam, grnn)


def _compress_kernel(rk_ref, rv_ref, w1k_ref, w2k_ref, pk_ref, w1v_ref, w2v_ref, pv_ref, kc_ref, vc_ref):
    def one(r_ref, w1_ref, w2_ref, pos_ref, o_ref):
        n = r_ref.shape[2] // CMP_STRIDE
        p = jnp.zeros((n, 2 * CMP_HIDDEN), F32)
        for l in range(CMP_STRIDE):
            rows = r_ref[0, 0, pl.ds(l, n, stride=CMP_STRIDE), :].astype(BF16)
            p = p + jnp.dot(rows, w1_ref[l * HEAD_DIM:(l + 1) * HEAD_DIM, :], preferred_element_type=F32)
        posb = jnp.dot(pos_ref[...], w1_ref[...], preferred_element_type=F32)
        bias = posb[0:1, 0:CMP_HIDDEN] + posb[1:2, CMP_HIDDEN:]
        pre = p[:, 0:CMP_HIDDEN] + pltpu.roll(p[:, CMP_HIDDEN:], n - 1, 0) + bias
        hid = _gelu_tanh(pre).astype(BF16)
        o_ref[0, 0] = jnp.dot(hid, w2_ref[...], preferred_element_type=F32).astype(BF16)

    one(rk_ref, w1k_ref, w2k_ref, pk_ref, kc_ref)
    one(rv_ref, w1v_ref, w2v_ref, pv_ref, vc_ref)


def _compress(rk, rv, w1k, w2k, posk, w1v, w2v, posv):
    B, G, S, width = rk.shape
    n = S // CMP_STRIDE
    blk = pl.BlockSpec((1, 1, S, width), lambda b, g: (b, g, 0, 0))
    full = lambda a: pl.BlockSpec(a.shape, lambda b, g: (0,) * a.ndim)
    out = pl.BlockSpec((1, 1, n, HEAD_DIM), lambda b, g: (b, g, 0, 0))
    return pl.pallas_call(
        _compress_kernel,
        grid=(B, G),
        in_specs=[blk, blk, full(w1k), full(w2k), full(posk), full(w1v), full(w2v), full(posv)],
        out_specs=[out, out],
        out_shape=[jax.ShapeDtypeStruct((B, G, n, HEAD_DIM), BF16)] * 2,
        compiler_params=pltpu.CompilerParams(vmem_limit_bytes=VMEM_LIMIT),
        name="compress",
    )(rk, rv, w1k, w2k, posk, w1v, w2v, posv)


def _attn_kernel(qT_ref, ks_ref, vs_ref, kw_ref, vw_ref, kc_ref, vcT_ref, g_ref,
                 o_ref,
                 kall, vall, qall, kca, m_all, acc_all, al_s, sbuf, pbuf, oacc, val_s, rank_s, selbias_s,
                 bias_tbl, cmask,
                 oT_s, st_s, *, n_sel):
    S = ks_ref.shape[2]
    nkt = S // TK
    ncmp = kc_ref.shape[2]
    b = pl.program_id(0)
    g = pl.program_id(1)
    step = pl.program_id(2)
    W4 = REP * TQ
    cmp_per_tile = TQ // CMP_STRIDE

    @pl.when((b == 0) & (g == 0) & (step == 0))
    def _():
        col = lax.broadcasted_iota(I32, (TK, KAUG), 1)
        is_blk = (col == POS_COL) | (col == POS_COL + 2)
        is_off = (col == POS_COL + 1) | (col == POS_COL + 3)
        ones_row = (lax.broadcasted_iota(I32, (VROWS - HEAD_DIM, TK), 0) == 0).astype(BF16)
        for kt in range(nkt):
            key = kt * TK + lax.broadcasted_iota(I32, (TK, KAUG), 0)
            pos = jnp.where(is_blk, (key // SLC_BLOCK).astype(F32),
                            jnp.where(is_off, (key % SLC_BLOCK).astype(F32), 0.0))
            onehot = (col - HEAD_DIM == key // SLC_BLOCK).astype(F32)
            kall[SEL, kt * TK:(kt + 1) * TK, :] = (pos + onehot).astype(BF16)
            kall[WIN, kt * TK:(kt + 1) * TK, :] = pos.astype(BF16)
            vall[SEL, kt, HEAD_DIM:VROWS, :] = ones_row
            vall[WIN, kt, HEAD_DIM:VROWS, :] = ones_row
        ko = lax.broadcasted_iota(I32, (TK, W4), 0)
        to = lax.broadcasted_iota(I32, (TK, W4), 1) % TQ
        bias_tbl[MASK_NONE] = jnp.zeros((TK, W4), F32)
        bias_tbl[MASK_CAUSAL] = jnp.where(ko <= to, 0.0, NEG)
        bias_tbl[MASK_WINDOW_LOW] = jnp.where(ko > to, 0.0, NEG)
        bias_tbl[MASK_ALL] = jnp.full((TK, W4), NEG, F32)
        u = lax.broadcasted_iota(I32, (2 * ncmp, W4), 0)
        tc = lax.broadcasted_iota(I32, (2 * ncmp, W4), 1) % TQ
        cmask[...] = jnp.where(u <= ncmp + ((tc + 1) // CMP_STRIDE) - 2, 0.0, NEG)
        cc = lax.broadcasted_iota(I32, (ncmp, KAUG), 0)
        colc = lax.broadcasted_iota(I32, (ncmp, KAUG), 1)
        kca[...] = jnp.where((colc == POS_COL) | (colc == POS_COL + 2), (cc // 4).astype(F32),
                             jnp.where((colc == POS_COL + 1) | (colc == POS_COL + 3),
                                       (CMP_STRIDE * (cc % 4)).astype(F32), 0.0)).astype(BF16)
        qall[...] = jnp.zeros(qall.shape, BF16)

    lane = lax.broadcasted_iota(I32, (1, W4), 1)
    head = g * REP + lane // TQ
    slope = lax.bitcast_convert_type((126 - head) << 23, F32)

    @pl.when(step == 0)
    def _():
        for kt in range(nkt):
            kall[SEL, kt * TK:(kt + 1) * TK, 0:HEAD_DIM] = ks_ref[0, 0, kt * TK:(kt + 1) * TK, :]
            kall[WIN, kt * TK:(kt + 1) * TK, 0:HEAD_DIM] = kw_ref[0, 0, kt * TK:(kt + 1) * TK, :]
            vall[SEL, kt, 0:HEAD_DIM, :] = vs_ref[0, 0, kt]
            vall[WIN, kt, 0:HEAD_DIM, :] = vw_ref[0, 0, kt]
        kca[:, 0:HEAD_DIM] = kc_ref[0, 0]
        c_hi = (slope * LOG2E).astype(BF16).astype(F32)
        c_lo = slope * LOG2E - c_hi
        r16 = lax.broadcasted_iota(I32, (16, W4), 0)
        alibi = jnp.where(r16 == 0, c_hi * SLC_BLOCK, jnp.where(r16 == 1, c_hi, jnp.where(
            r16 == 2, c_lo * SLC_BLOCK, jnp.where(r16 == 3, c_lo, 0.0)))).astype(BF16)
        for c in range(NCHAIN):
            qall[c, SEL, POS_COL:POS_COL + 16, :] = alibi
            qall[c, WIN, POS_COL:POS_COL + 16, :] = alibi

    chains = [_attn_chain(c, NCHAIN * step + c, qT_ref, kc_ref, vcT_ref, g_ref, o_ref, kall, vall, qall.at[c], kca,
                          m_all.at[c], acc_all.at[c], al_s.at[c], sbuf.at[c], pbuf.at[c], oacc.at[c], val_s.at[c],
                          rank_s.at[c], selbias_s.at[c], bias_tbl, cmask, oT_s.at[c], st_s.at[c], n_sel)
              for c in range(NCHAIN)]
    live = list(chains)
    while live:
        live = [ch for ch in live if next(ch, _DONE) is not _DONE]


_DONE = object()


def _attn_chain(c, qi, qT_ref, kc_ref, vcT_ref, g_ref, o_ref, kall, vall, qall, kca, m_all, acc_all, al_s, sbuf,
                pbuf, oacc, val_s, rank_s, selbias_s, bias_tbl, cmask, oT_s, st_s, n_sel):
    S = kall.shape[1]
    ncmp = kc_ref.shape[2]
    nblk = S // SLC_BLOCK
    cmp_per_tile = TQ // CMP_STRIDE
    q0 = qi * TQ
    W4 = REP * TQ
    lane = lax.broadcasted_iota(I32, (1, W4), 1)
    lanes_c = slice(c * TQ, (c + 1) * TQ)

    q = qT_ref[0, :, lanes_c]
    for r in range(REP):
        qr = q[r * HEAD_DIM:(r + 1) * HEAD_DIM, :]
        qall[SEL, 0:HEAD_DIM, r * TQ:(r + 1) * TQ] = qr
        qall[WIN, 0:HEAD_DIM, r * TQ:(r + 1) * TQ] = qr

    gall = g_ref[0, 0, :, lanes_c]
    gates = [gall[br * REP:(br + 1) * REP, :] for br in range(3)]

    def stage_scores(br, tiles_masks):
        col_max = None
        for h, (tile, mask) in enumerate(tiles_masks):
            r0 = pl.multiple_of(tile * TK, TK)
            s = jnp.dot(kall[br, pl.ds(r0, TK), :], qall[br], preferred_element_type=F32)
            if mask is not None:
                s = s + bias_tbl[mask]
            sbuf[h * TK:(h + 1) * TK, :] = s
            c = jnp.max(s, axis=0, keepdims=True)
            col_max = c if col_max is None else jnp.maximum(col_max, c)
        m_old = m_all[br]
        m_new = jnp.maximum(m_old, col_max)
        m_all[br] = m_new
        return jnp.exp2(m_old - m_new)

    def stage_probs(br, ntile):
        pbuf[0:ntile * TK, :] = jnp.exp2(sbuf[0:ntile * TK, :] - m_all[br]).astype(BF16)

    def stage_values(br, tiles, al_row):
        acc = acc_all[br] * al_row
        for h, tile in enumerate(tiles):
            acc = acc + jnp.dot(vall[br, tile], pbuf[h * TK:(h + 1) * TK, :], preferred_element_type=F32)
        acc_all[br] = acc

    def start_branch(br, tl, mask1):
        al_s[...] = stage_scores(br, [(tl[0], MASK_CAUSAL), (tl[1], mask1)])

    def finish_branch(br, n, tl):
        rest = jnp.maximum(n - 2, 0)
        npair = rest // 2

        def body(j, carry):
            al_prev = al_s[...]
            stage_probs(br, 2)
            al_s[...] = stage_scores(br, [(tl[2 * j], None), (tl[2 * j + 1], None)])
            stage_values(br, [tl[2 * j - 2], tl[2 * j - 1]], al_prev)
            return carry

        lax.fori_loop(1, 1 + npair, body, 0)
        last = 2 * npair

        @pl.when(rest % 2 == 1)
        def _():
            al_prev = al_s[...]
            stage_probs(br, 2)
            al_k = stage_scores(br, [(tl[n - 1], None)])
            stage_values(br, [tl[last], tl[last + 1]], al_prev)
            stage_probs(br, 1)
            stage_values(br, [tl[n - 1]], al_k)

        @pl.when(rest % 2 == 0)
        def _():
            stage_probs(br, 2)
            stage_values(br, [tl[last], tl[last + 1]], al_s[...])

    m_all[...] = jnp.full(m_all.shape, NEG, F32)
    acc_all[...] = jnp.zeros(acc_all.shape, F32)
    w1 = jnp.maximum(qi - 2, 0)
    w2 = jnp.maximum(qi - 1, 0)
    al_w0 = stage_scores(WIN, [(qi, MASK_CAUSAL),
                               (w1, jnp.where(qi >= 2, MASK_WINDOW_LOW, jnp.where(qi >= 1, MASK_NONE, MASK_ALL)))])
    yield

    c0 = pl.multiple_of(ncmp - cmp_per_tile * qi, cmp_per_tile)
    sc = jnp.dot(kca[...], qall[WIN], preferred_element_type=F32) + cmask[pl.ds(c0, ncmp), :]
    yield
    e = jnp.exp2(sc - jnp.max(sc, axis=0, keepdims=True))
    tq = q0 + lane % TQ
    p = e * ((1.0 / jnp.sum(e, axis=0, keepdims=True)) * (tq >= CMP_LEN - 1).astype(F32))
    ocT = jnp.dot(vcT_ref[0, 0], p.astype(BF16), preferred_element_type=F32)
    for r in range(REP):
        oacc[:, r * TQ:(r + 1) * TQ] = gates[0][r:r + 1, :] * ocT[:, r * TQ:(r + 1) * TQ]
    yield

    stage_probs(WIN, 2)
    al_w1 = stage_scores(WIN, [(w2, jnp.where(qi >= 2, MASK_NONE, MASK_ALL))])
    stage_values(WIN, [qi, w1], al_w0)
    yield

    psum = p[:, 0:TQ]
    for r in range(1, REP):
        psum = psum + p[:, r * TQ:(r + 1) * TQ]
    jj = lax.broadcasted_iota(I32, (nblk, ncmp), 0)
    cc = lax.broadcasted_iota(I32, (nblk, ncmp), 1)
    ovT = ((CMP_STRIDE * cc < SLC_BLOCK * jj + SLC_BLOCK)
           & (CMP_STRIDE * cc + CMP_LEN > SLC_BLOCK * jj)).astype(F32)
    impT = jnp.dot(ovT, psum, preferred_element_type=F32, precision=lax.Precision.HIGHEST)

    j_i = lax.broadcasted_iota(I32, (nblk, TQ), 0)
    t1 = q0 + lax.broadcasted_iota(I32, (nblk, TQ), 1)
    cur = t1 // SLC_BLOCK
    forced = (j_i == 0) | (j_i == cur) | (j_i == cur - 1)
    visible = SLC_BLOCK * j_i <= t1
    val_s[...] = jnp.where(forced, FORCED, jnp.where(visible, impT, NEG))
    yield
    ngrp = nblk // 8
    rank_s[...] = jnp.zeros(rank_s.shape, F32)
    j8 = lax.broadcasted_iota(I32, (8, TQ), 0)
    for ib in range(ngrp):
        @pl.when(8 * ib * SLC_BLOCK < q0 + TQ)
        def _(ib=ib):
            vals = [val_s[8 * jb:8 * jb + 8, :] for jb in range(ngrp)]
            ranks = [rank_s[8 * jb:8 * jb + 8, :] for jb in range(ngrp)]
            for i in range(8 * ib, 8 * ib + 8):
                row = jnp.broadcast_to(val_s[i:i + 1, :], (8, TQ))
                for jb in range(ngrp):
                    if jb > ib:
                        hit = row >= vals[jb]
                    elif jb < ib:
                        hit = row > vals[jb]
                    else:
                        hit = jnp.where(j8 > i - 8 * jb, jnp.where(row >= vals[jb], 1.0, 0.0),
                                        jnp.where(row > vals[jb], 1.0, 0.0)) > 0.5
                    ranks[jb] = ranks[jb] + jnp.where(hit, 1.0, 0.0)
            for jb in range(ngrp):
                rank_s[8 * jb:8 * jb + 8, :] = ranks[jb]
    yield
    ranks = [rank_s[8 * jb:8 * jb + 8, :] for jb in range(ngrp)]
    blocks_per_tile = TK // SLC_BLOCK
    st_s[0] = qi
    n_selt = jnp.int32(1)
    for jb in range(ngrp):
        chosen = ranks[jb] < n_sel
        selb = jnp.where(chosen, 0.0, NEG)
        for r in range(REP):
            selbias_s[8 * jb:8 * jb + 8, r * TQ:(r + 1) * TQ] = selb
        any_q = jnp.max(jnp.where(chosen, 1.0, 0.0), axis=1, keepdims=True)
        for hh in range(8 // blocks_per_tile):
            kt = (8 * jb) // blocks_per_tile + hh
            hit = jnp.max(any_q[hh * blocks_per_tile:(hh + 1) * blocks_per_tile, :]) > 0.5
            st_s[n_selt] = kt
            n_selt = n_selt + jnp.where(hit & (kt < qi), 1, 0)
    qall[SEL, HEAD_DIM:HEAD_DIM + nblk, :] = selbias_s[...].astype(BF16)
    yield

    stage_probs(WIN, 1)
    stage_values(WIN, [w2], al_w1)
    start_branch(SEL, st_s, jnp.where(n_selt >= 2, MASK_NONE, MASK_ALL))
    yield
    finish_branch(SEL, n_selt, st_s)

    def normalised(br):
        acc = acc_all[br]
        return acc[0:HEAD_DIM, :] / acc[HEAD_DIM:HEAD_DIM + 1, :]

    o_sel = normalised(SEL)
    o_win = normalised(WIN)
    for r in range(REP):
        lanes = slice(r * TQ, (r + 1) * TQ)
        oT_s[r * HEAD_DIM:(r + 1) * HEAD_DIM, :] = (oacc[:, lanes] + gates[1][r:r + 1, :] * o_sel[:, lanes]
                                                    + gates[2][r:r + 1, :] * o_win[:, lanes])
    o_ref[0, lanes_c, :] = oT_s[...].T.astype(BF16)


def _attention(qT, ks, vs, kw, vw, kc, vcT, gT):
    B, _, S = qT.shape
    G = N_KV
    nkt = S // TK
    ncmp = kc.shape[2]
    nblk = S // SLC_BLOCK
    n_sel = min(SLC_TOPK, nblk)
    W4 = REP * TQ
    res4 = lambda a: pl.BlockSpec((1, 1) + a.shape[2:], lambda b, g, i: (b, g) + (0,) * (a.ndim - 2))
    return pl.pallas_call(
        functools.partial(_attn_kernel, n_sel=n_sel),
        grid=(B, G, S // (NCHAIN * TQ)),
        in_specs=[pl.BlockSpec((1, REP * HEAD_DIM, NCHAIN * TQ), lambda b, g, i: (b, g, i)),
                  res4(ks), res4(vs), res4(kw), res4(vw), res4(kc), res4(vcT),
                  pl.BlockSpec((1, 1, GATE_ROWS, NCHAIN * TQ), lambda b, g, i: (b, g, 0, i))],
        out_specs=pl.BlockSpec((1, NCHAIN * TQ, REP * HEAD_DIM), lambda b, g, i: (b, i, g)),
        out_shape=jax.ShapeDtypeStruct((B, S, D_ATT), BF16),
        scratch_shapes=[pltpu.VMEM((2, S, KAUG), BF16),
                        pltpu.VMEM((2, nkt, VROWS, TK), BF16),
                        pltpu.VMEM((NCHAIN, 2, KAUG, W4), BF16),
                        pltpu.VMEM((ncmp, KAUG), BF16),
                        pltpu.VMEM((NCHAIN, 2, 1, W4), F32),
                        pltpu.VMEM((NCHAIN, 2, VROWS, W4), F32),
                        pltpu.VMEM((NCHAIN, 1, W4), F32),
                        pltpu.VMEM((NCHAIN, 2 * TK, W4), F32),
                        pltpu.VMEM((NCHAIN, 2 * TK, W4), BF16),
                        pltpu.VMEM((NCHAIN, HEAD_DIM, W4), F32),
                        pltpu.VMEM((NCHAIN, nblk, TQ), F32),
                        pltpu.VMEM((NCHAIN, nblk, TQ), F32),
                        pltpu.VMEM((NCHAIN, nblk, W4), F32),
                        pltpu.VMEM((4, TK, W4), F32),
                        pltpu.VMEM((2 * ncmp, W4), F32),
                        pltpu.VMEM((NCHAIN, REP * HEAD_DIM, TQ), F32),
                        pltpu.SMEM((NCHAIN, nkt + 2), I32)],
        compiler_params=pltpu.CompilerParams(dimension_semantics=("arbitrary", "arbitrary", "arbitrary"),
                                             vmem_limit_bytes=VMEM_LIMIT),
        name="attn",
    )(qT, ks, vs, kw, vw, kc, vcT, gT)


def _outmlp_kernel(x_ref, rnn_ref, att_ref, gatt_ref, wo_ref, gpost_ref, g1_ref, gpre_ref, sc2_ref, sh2_ref,
                   w1_ref, w2_ref, gpost2_ref, g2_ref, o_ref):
    tm = x_ref.shape[1]
    halves = [slice(0, tm // 2), slice(tm // 2, tm)]
    x1s, h2s = [], []
    for rows in halves:
        att_n = (_rms(att_ref[0, rows, :].astype(F32)) * gatt_ref[...]).astype(BF16)
        y = (jnp.dot(rnn_ref[0, rows, :], wo_ref[0:D_RNN, :], preferred_element_type=F32)
             + jnp.dot(att_n, wo_ref[D_RNN:, :], preferred_element_type=F32))
        x1 = x_ref[0, rows, :] + (1.0 + g1_ref[0]) * (_rms(y) * gpost_ref[...])
        x1s.append(x1)
        h2s.append((_rms(x1) * (gpre_ref[...] * (1.0 + sc2_ref[0])) + sh2_ref[0]).astype(BF16))
    fc = 1024
    for rows, x1, h2 in zip(halves, x1s, h2s):
        ff = jnp.zeros(x1.shape, F32)
        for c in range(D_FF // fc):
            hid = jnp.maximum(jnp.dot(h2, w1_ref[:, c * fc:(c + 1) * fc], preferred_element_type=F32), 0.0)
            ff = ff + jnp.dot((hid * hid).astype(BF16), w2_ref[c * fc:(c + 1) * fc, :],
                              preferred_element_type=F32)
        o_ref[0, rows, :] = x1 + (1.0 + g2_ref[0]) * (_rms(ff) * gpost2_ref[...])


def _outmlp(x, rnn_n, att, gatt, wo, gpost, g1, gpre, sc2, sh2, w1, w2, gpost2, g2):
    B, S, D = x.shape
    tm = min(TM_OUT, S)
    row = lambda n: pl.BlockSpec((1, n), lambda b, s: (0, 0))
    per_b = lambda n: pl.BlockSpec((1, 1, n), lambda b, s: (b, 0, 0))
    const = lambda a: pl.BlockSpec(a.shape, lambda b, s: (0,) * a.ndim, pipeline_mode=pl.Buffered(1))
    tok = lambda n: pl.BlockSpec((1, tm, n), lambda b, s: (b, s, 0))
    return pl.pallas_call(
        _outmlp_kernel,
        grid=(B, S // tm),
        in_specs=[tok(D), tok(D_RNN), tok(D_ATT), row(D_ATT), const(wo), row(D), per_b(D), row(D),
                  per_b(D), per_b(D), const(w1), const(w2), row(D), per_b(D)],
        out_specs=tok(D),
        out_shape=jax.ShapeDtypeStruct((B, S, D), F32),
        compiler_params=pltpu.CompilerParams(dimension_semantics=("arbitrary", "arbitrary"),
                                             vmem_limit_bytes=VMEM_LIMIT),
        name="outmlp",
    )(x, rnn_n, att, gatt, wo, gpost, g1, gpre, sc2, sh2, w1, w2, gpost2, g2)


def _block_diag(w):
    n, k, _ = w.shape
    return jnp.einsum('nij,nm->nimj', w, jnp.eye(n, dtype=w.dtype)).reshape(n * k, n * k)


def _layer(x, c, ada_w, ada_b, pre_norm_mix, w_in, conv_w, conv_b, lru_wa, lru_ba, lru_wx, lru_bx, lru_lambda,
           cmp_pos_k, cmp_w1_k, cmp_w2_k, cmp_pos_v, cmp_w1_v, cmp_w2_v, norm_rnn_out, norm_att_out, w_out,
           post_norm_mix, pre_norm_mlp, w_ff1, w_ff2, post_norm_mlp):
    B, S, D = x.shape
    G = N_KV
    row = lambda v: v.reshape(1, -1)

    mod = _ada(c, ada_w, ada_b)
    sh1, sc1, g1, sh2, sc2, g2 = [m.reshape(B, 1, D) for m in jnp.split(mod, 6, axis=-1)]

    gate_cols = [OFF_GATE + br * N_HEADS + g * REP + r for g in range(G) for br in range(3) for r in range(REP)]
    w_gate = w_in[:, jnp.asarray(gate_cols)].reshape(D, G, 3 * REP)
    w_gate = jnp.pad(w_gate, ((0, 0), (0, 0), (0, GATE_ROWS - 3 * REP))).reshape(D, G * GATE_ROWS)
    w_in_p = jnp.concatenate([w_in[:, :OFF_GATE], jnp.pad(w_gate, ((0, 0), (0, GATE_PAD - G * GATE_ROWS)))],
                             axis=1).astype(BF16)
    wa = _block_diag(lru_wa).astype(BF16)
    wx = _block_diag(lru_wx).astype(BF16)
    half = CMP_LEN // 2 * HEAD_DIM

    def w1_cat(w1):
        return jnp.concatenate([w1[:half], w1[half:]], axis=1).astype(BF16)

    def pos_rows(pos):
        return jnp.pad(pos.reshape(2, half), ((0, 14), (0, 0))).astype(BF16)

    rnn_n, qT, ks, vsT, kw, vwT, kc_in, vc_in, gT = _inproj(
        x, row(pre_norm_mix), sc1, sh1, w_in_p, conv_w, row(conv_b), wa, row(lru_ba), wx, row(lru_bx),
        row(lru_lambda), row(norm_rnn_out))
    kc, vc = _compress(kc_in, vc_in, w1_cat(cmp_w1_k), cmp_w2_k.astype(BF16), pos_rows(cmp_pos_k),
                       w1_cat(cmp_w1_v), cmp_w2_v.astype(BF16), pos_rows(cmp_pos_v))
    att = _attention(qT, ks, vsT, kw, vwT, kc, vc.transpose(0, 1, 3, 2), gT)

    return _outmlp(x, rnn_n, att, row(norm_att_out), w_out.astype(BF16), row(post_norm_mix), g1,
                   row(pre_norm_mlp), sc2, sh2, w_ff1.astype(BF16), w_ff2.astype(BF16), row(post_norm_mlp), g2)


def kernel(x, c, ada_w, ada_b, pre_norm_mix, w_in, conv_w, conv_b, lru_wa, lru_ba, lru_wx, lru_bx, lru_lambda,
           cmp_pos_k, cmp_w1_k, cmp_w2_k, cmp_pos_v, cmp_w1_v, cmp_w2_v, norm_rnn_out, norm_att_out, w_out,
           post_norm_mix, pre_norm_mlp, w_ff1, w_ff2, post_norm_mlp):
    for l in range(ada_w.shape[0]):
        x = _layer(x, c, ada_w[l], ada_b[l], pre_norm_mix[l], w_in[l], conv_w[l], conv_b[l], lru_wa[l], lru_ba[l],
                   lru_wx[l], lru_bx[l], lru_lambda[l], cmp_pos_k[l], cmp_w1_k[l], cmp_w2_k[l], cmp_pos_v[l],
                   cmp_w1_v[l], cmp_w2_v[l], norm_rnn_out[l], norm_att_out[l], w_out[l], post_norm_mix[l],
                   pre_norm_mlp[l], w_ff1[l], w_ff2[l], post_norm_mlp[l])
    return x
```

```python
import functools

import jax
import jax.numpy as jnp
from jax import lax
from jax.experimental import pallas as pl
from jax.experimental.pallas import tpu as pltpu

F32 = jnp.float32
BF16 = jnp.bfloat16
I32 = jnp.int32

D_MODEL = 1024
D_RNN = 512
RNN_BLOCKS = 8
CONV_WIDTH = 4
LRU_C = 8.0
N_HEADS = 8
HEAD_DIM = 64
N_KV = 2
REP = N_HEADS // N_KV
D_ATT = N_HEADS * HEAD_DIM
CMP_LEN = 32
CMP_STRIDE = 16
CMP_HIDDEN = 256
SLC_BLOCK = 64
SLC_TOPK = 16
WINDOW = 512
D_FF = 4 * D_MODEL
EPS = 1e-6
NEG = -1e30
FORCED = 1e4
LOG2E = 1.4426950408889634

KV_COLS = 6 * N_KV * HEAD_DIM
N_GATE = 3 * N_HEADS
GATE_PAD = 128
GATE_ROWS = 16
OFF_Q = 2 * D_RNN
OFF_KV = OFF_Q + D_ATT
OFF_GATE = OFF_KV + KV_COLS
D_IN_PAD = OFF_GATE + GATE_PAD

TM_IN = 512
TM_OUT = 1024
TQ = 256
TK = 256
NCHAIN = 4
KAUG = 256
VROWS = 80
VMEM_LIMIT = 56 * 1024 * 1024

SEL, WIN = 0, 1
MASK_NONE, MASK_CAUSAL, MASK_WINDOW_LOW, MASK_ALL = 0, 1, 2, 3
POS_COL = 2 * HEAD_DIM


def _gelu_tanh(x):
    return 0.5 * x * (1.0 + jnp.tanh(0.7978845608028654 * (x + 0.044715 * (x * x * x))))


def _rms(x):
    return x * lax.rsqrt(jnp.mean(x * x, axis=-1, keepdims=True) + EPS)


def _ada_kernel(c_ref, w_ref, b_ref, o_ref):
    c = c_ref[...]
    a = c * jax.nn.sigmoid(c)
    o_ref[...] = jnp.dot(a, w_ref[...], preferred_element_type=F32,
                         precision=lax.Precision.HIGHEST) + b_ref[...]


def _ada(c, w, b):
    B, D = c.shape
    N = w.shape[1]
    tn = 1024
    return pl.pallas_call(
        _ada_kernel,
        grid=(N // tn,),
        in_specs=[pl.BlockSpec((B, D), lambda j: (0, 0)),
                  pl.BlockSpec((D, tn), lambda j: (0, j)),
                  pl.BlockSpec((1, tn), lambda j: (0, j))],
        out_specs=pl.BlockSpec((B, tn), lambda j: (0, j)),
        out_shape=jax.ShapeDtypeStruct((B, N), F32),
        name="ada",
    )(c, w, b.reshape(1, N))


def _inproj_kernel(x_ref, gain_ref, sc_ref, sh_ref, w_ref, cw_ref, cb_ref, wa_ref, ba_ref, wx_ref, bx_ref,
                   lam_ref, grnn_ref,
                   rnn_ref, qT_ref, ks_ref, vs_ref, kw_ref, vw_ref, kc_ref, vc_ref, g_ref,
                   xbuf, hcar, a_s, u_s, h_s):
    tm = x_ref.shape[1]
    hd = HEAD_DIM

    @pl.when(pl.program_id(1) == 0)
    def _():
        xbuf[0:8, :] = jnp.zeros((8, D_RNN), F32)
        hcar[...] = jnp.zeros((1, D_RNN), F32)

    x = x_ref[0]
    h = _rms(x) * (gain_ref[...] * (1.0 + sc_ref[0])) + sh_ref[0]
    hb = h.astype(BF16)

    xr = jnp.dot(hb, w_ref[:, D_RNN:OFF_Q], preferred_element_type=F32)
    xbuf[8:8 + tm, :] = xr
    y = (cw_ref[3:4, :] * xr + cw_ref[2:3, :] * xbuf[7:7 + tm, :]
         + cw_ref[1:2, :] * xbuf[6:6 + tm, :] + cw_ref[0:1, :] * xbuf[5:5 + tm, :]) + cb_ref[...]
    xbuf[0:8, :] = xbuf[tm:tm + 8, :]

    yb = y.astype(BF16)
    r = jax.nn.sigmoid(jnp.dot(yb, wa_ref[...], preferred_element_type=F32) + ba_ref[...])
    i = jax.nn.sigmoid(jnp.dot(yb, wx_ref[...], preferred_element_type=F32) + bx_ref[...])
    nl = -lam_ref[...]
    softplus = jnp.maximum(nl, 0.0) + jnp.log(1.0 + jnp.exp(-jnp.abs(nl)))
    a = jnp.exp((-LRU_C) * r * softplus)
    a_s[...] = a
    u_s[...] = jnp.sqrt(1.0 - a * a) * (i * y)

    qT_ref[0] = (jnp.dot(hb, w_ref[:, OFF_Q:OFF_KV], preferred_element_type=F32)
                 * (HEAD_DIM ** -0.5 * LOG2E)).T.astype(BF16)
    kv = jnp.dot(hb, w_ref[:, OFF_KV:OFF_GATE], preferred_element_type=F32)
    width = N_KV * hd
    vsT = kv[:, 3 * width:4 * width].T
    vwT = kv[:, 5 * width:6 * width].T
    for gi in range(N_KV):
        kc_ref[0, gi] = kv[:, gi * hd:(gi + 1) * hd]
        vc_ref[0, gi] = kv[:, width + gi * hd:width + (gi + 1) * hd]
        ks_ref[0, gi] = kv[:, 2 * width + gi * hd:2 * width + (gi + 1) * hd].astype(BF16)
        kw_ref[0, gi] = kv[:, 4 * width + gi * hd:4 * width + (gi + 1) * hd].astype(BF16)
        for j in range(tm // TK):
            vs_ref[0, gi, j] = vsT[gi * hd:(gi + 1) * hd, j * TK:(j + 1) * TK].astype(BF16)
            vw_ref[0, gi, j] = vwT[gi * hd:(gi + 1) * hd, j * TK:(j + 1) * TK].astype(BF16)
    gates = jax.nn.sigmoid(jnp.dot(hb, w_ref[:, OFF_GATE:D_IN_PAD], preferred_element_type=F32)).T
    for gi in range(N_KV):
        g_ref[0, gi] = gates[gi * GATE_ROWS:(gi + 1) * GATE_ROWS, :]
    g = jnp.dot(hb, w_ref[:, 0:D_RNN], preferred_element_type=F32)


    rows = lax.broadcasted_iota(I32, (8, D_RNN), 0)
    hprev = hcar[...]
    for gi in range(tm // 8):
        ag = a_s[gi * 8:gi * 8 + 8, :]
        ug = u_s[gi * 8:gi * 8 + 8, :]
        for k in (1, 2, 4):
            a_sh = jnp.where(rows >= k, pltpu.roll(ag, k, 0), 1.0)
            u_sh = jnp.where(rows >= k, pltpu.roll(ug, k, 0), 0.0)
            ug = ag * u_sh + ug
            ag = ag * a_sh
        hg = ag * hprev + ug
        h_s[gi * 8:gi * 8 + 8, :] = hg
        hprev = hg[7:8, :]
    hcar[...] = hprev

    rnn = _gelu_tanh(g) * h_s[...]
    rnn_ref[0] = (_rms(rnn) * grnn_ref[...]).astype(BF16)


def _inproj(x, gain, sc, sh, w_in, conv_w, conv_b, wa, ba, wx, bx, lam, grnn):
    B, S, D = x.shape
    tm = min(TM_IN, S)
    row = lambda n: pl.BlockSpec((1, n), lambda b, s: (0, 0))
    per_b = lambda n: pl.BlockSpec((1, 1, n), lambda b, s: (b, 0, 0))
    full = lambda a: pl.BlockSpec(a.shape, lambda b, s: (0,) * a.ndim)
    tok = lambda n: pl.BlockSpec((1, tm, n), lambda b, s: (b, s, 0))
    G = N_KV
    grp = pl.BlockSpec((1, G, tm, HEAD_DIM), lambda b, s: (b, 0, s, 0))
    grpT = pl.BlockSpec((1, G, tm // TK, HEAD_DIM, TK), lambda b, s: (b, 0, s, 0, 0))
    return pl.pallas_call(
        _inproj_kernel,
        grid=(B, S // tm),
        in_specs=[tok(D), row(D), per_b(D), per_b(D), full(w_in), full(conv_w), row(D_RNN),
                  full(wa), row(D_RNN), full(wx), row(D_RNN), row(D_RNN), row(D_RNN)],
        out_specs=[tok(D_RNN),
                   pl.BlockSpec((1, D_ATT, tm), lambda b, s: (b, 0, s)),
                   grp, grpT, grp, grpT, grp, grp,
                   pl.BlockSpec((1, G, GATE_ROWS, tm), lambda b, s: (b, 0, 0, s))],
        out_shape=[jax.ShapeDtypeStruct((B, S, D_RNN), BF16),
                   jax.ShapeDtypeStruct((B, D_ATT, S), BF16),
                   jax.ShapeDtypeStruct((B, G, S, HEAD_DIM), BF16),
                   jax.ShapeDtypeStruct((B, G, S // TK, HEAD_DIM, TK), BF16),
                   jax.ShapeDtypeStruct((B, G, S, HEAD_DIM), BF16),
                   jax.ShapeDtypeStruct((B, G, S // TK, HEAD_DIM, TK), BF16),
                   jax.ShapeDtypeStruct((B, G, S, HEAD_DIM), F32),
                   jax.ShapeDtypeStruct((B, G, S, HEAD_DIM), F32),
                   jax.ShapeDtypeStruct((B, G, GATE_ROWS, S), F32)],
        scratch_shapes=[pltpu.VMEM((tm + 8, D_RNN), F32), pltpu.VMEM((1, D_RNN), F32),
                        pltpu.VMEM((tm, D_RNN), F32), pltpu.VMEM((tm, D_RNN), F32),
                        pltpu.VMEM((tm, D_RNN), F32)],
        compiler_params=pltpu.CompilerParams(dimension_semantics=("arbitrary", "arbitrary"),
                                             vmem_limit_bytes=VMEM_LIMIT),
        name="inproj",
    )(x, gain, sc, sh, w_in, conv_w, conv_b, wa, ba, wx, bx, lam, grnn)


def _compress_kernel(rk_ref, rv_ref, w1k_ref, w2k_ref, pk_ref, w1v_ref, w2v_ref, pv_ref, kc_ref, vc_ref):
    def one(r_ref, w1_ref, w2_ref, pos_ref, o_ref):
        n = r_ref.shape[2] // CMP_STRIDE
        p = jnp.zeros((n, 2 * CMP_HIDDEN), F32)
        for l in range(CMP_STRIDE):
            rows = r_ref[0, 0, pl.ds(l, n, stride=CMP_STRIDE), :].astype(BF16)
            p = p + jnp.dot(rows, w1_ref[l * HEAD_DIM:(l + 1) * HEAD_DIM, :], preferred_element_type=F32)
        posb = jnp.dot(pos_ref[...], w1_ref[...], preferred_element_type=F32)
        bias = posb[0:1, 0:CMP_HIDDEN] + posb[1:2, CMP_HIDDEN:]
        pre = p[:, 0:CMP_HIDDEN] + pltpu.roll(p[:, CMP_HIDDEN:], n - 1, 0) + bias
        hid = _gelu_tanh(pre).astype(BF16)
        o_ref[0, 0] = jnp.dot(hid, w2_ref[...], preferred_element_type=F32).astype(BF16)

    one(rk_ref, w1k_ref, w2k_ref, pk_ref, kc_ref)
    one(rv_ref, w1v_ref, w2v_ref, pv_ref, vc_ref)


def _compress(rk, rv, w1k, w2k, posk, w1v, w2v, posv):
    B, G, S, width = rk.shape
    n = S // CMP_STRIDE
    blk = pl.BlockSpec((1, 1, S, width), lambda b, g: (b, g, 0, 0))
    full = lambda a: pl.BlockSpec(a.shape, lambda b, g: (0,) * a.ndim)
    out = pl.BlockSpec((1, 1, n, HEAD_DIM), lambda b, g: (b, g, 0, 0))
    return pl.pallas_call(
        _compress_kernel,
        grid=(B, G),
        in_specs=[blk, blk, full(w1k), full(w2k), full(posk), full(w1v), full(w2v), full(posv)],
        out_specs=[out, out],
        out_shape=[jax.ShapeDtypeStruct((B, G, n, HEAD_DIM), BF16)] * 2,
        compiler_params=pltpu.CompilerParams(vmem_limit_bytes=VMEM_LIMIT),
        name="compress",
    )(rk, rv, w1k, w2k, posk, w1v, w2v, posv)


def _attn_kernel(qT_ref, ks_ref, vs_ref, kw_ref, vw_ref, kc_ref, vcT_ref, g_ref,
                 o_ref,
                 kall, vall, qall, kca, m_all, acc_all, al_s, sbuf, pbuf, oacc, val_s, rank_s, selbias_s,
                 bias_tbl, cmask,
                 oT_s, st_s, *, n_sel):
    S = ks_ref.shape[2]
    nkt = S // TK
    ncmp = kc_ref.shape[2]
    b = pl.program_id(0)
    g = pl.program_id(1)
    step = pl.program_id(2)
    W4 = REP * TQ
    cmp_per_tile = TQ // CMP_STRIDE

    @pl.when((b == 0) & (g == 0) & (step == 0))
    def _():
        col = lax.broadcasted_iota(I32, (TK, KAUG), 1)
        is_blk = (col == POS_COL) | (col == POS_COL + 2)
        is_off = (col == POS_COL + 1) | (col == POS_COL + 3)
        ones_row = (lax.broadcasted_iota(I32, (VROWS - HEAD_DIM, TK), 0) == 0).astype(BF16)
        for kt in range(nkt):
            key = kt * TK + lax.broadcasted_iota(I32, (TK, KAUG), 0)
            pos = jnp.where(is_blk, (key // SLC_BLOCK).astype(F32),
                            jnp.where(is_off, (key % SLC_BLOCK).astype(F32), 0.0))
            onehot = (col - HEAD_DIM == key // SLC_BLOCK).astype(F32)
            kall[SEL, kt * TK:(kt + 1) * TK, :] = (pos + onehot).astype(BF16)
            kall[WIN, kt * TK:(kt + 1) * TK, :] = pos.astype(BF16)
            vall[SEL, kt, HEAD_DIM:VROWS, :] = ones_row
            vall[WIN, kt, HEAD_DIM:VROWS, :] = ones_row
        ko = lax.broadcasted_iota(I32, (TK, W4), 0)
        to = lax.broadcasted_iota(I32, (TK, W4), 1) % TQ
        bias_tbl[MASK_NONE] = jnp.zeros((TK, W4), F32)
        bias_tbl[MASK_CAUSAL] = jnp.where(ko <= to, 0.0, NEG)
        bias_tbl[MASK_WINDOW_LOW] = jnp.where(ko > to, 0.0, NEG)
        bias_tbl[MASK_ALL] = jnp.full((TK, W4), NEG, F32)
        u = lax.broadcasted_iota(I32, (2 * ncmp, W4), 0)
        tc = lax.broadcasted_iota(I32, (2 * ncmp, W4), 1) % TQ
        cmask[...] = jnp.where(u <= ncmp + ((tc + 1) // CMP_STRIDE) - 2, 0.0, NEG)
        cc = lax.broadcasted_iota(I32, (ncmp, KAUG), 0)
        colc = lax.broadcasted_iota(I32, (ncmp, KAUG), 1)
        kca[...] = jnp.where((colc == POS_COL) | (colc == POS_COL + 2), (cc // 4).astype(F32),
                             jnp.where((colc == POS_COL + 1) | (colc == POS_COL + 3),
                                       (CMP_STRIDE * (cc % 4)).astype(F32), 0.0)).astype(BF16)
        qall[...] = jnp.zeros(qall.shape, BF16)

    lane = lax.broadcasted_iota(I32, (1, W4), 1)
    head = g * REP + lane // TQ
    slope = lax.bitcast_convert_type((126 - head) << 23, F32)

    @pl.when(step == 0)
    def _():
        for kt in range(nkt):
            kall[SEL, kt * TK:(kt + 1) * TK, 0:HEAD_DIM] = ks_ref[0, 0, kt * TK:(kt + 1) * TK, :]
            kall[WIN, kt * TK:(kt + 1) * TK, 0:HEAD_DIM] = kw_ref[0, 0, kt * TK:(kt + 1) * TK, :]
            vall[SEL, kt, 0:HEAD_DIM, :] = vs_ref[0, 0, kt]
            vall[WIN, kt, 0:HEAD_DIM, :] = vw_ref[0, 0, kt]
        kca[:, 0:HEAD_DIM] = kc_ref[0, 0]
        c_hi = (slope * LOG2E).astype(BF16).astype(F32)
        c_lo = slope * LOG2E - c_hi
        r16 = lax.broadcasted_iota(I32, (16, W4), 0)
        alibi = jnp.where(r16 == 0, c_hi * SLC_BLOCK, jnp.where(r16 == 1, c_hi, jnp.where(
            r16 == 2, c_lo * SLC_BLOCK, jnp.where(r16 == 3, c_lo, 0.0)))).astype(BF16)
        for c in range(NCHAIN):
            qall[c, SEL, POS_COL:POS_COL + 16, :] = alibi
            qall[c, WIN, POS_COL:POS_COL + 16, :] = alibi

    chains = [_attn_chain(c, NCHAIN * step + c, qT_ref, kc_ref, vcT_ref, g_ref, o_ref, kall, vall, qall.at[c], kca,
                          m_all.at[c], acc_all.at[c], al_s.at[c], sbuf.at[c], pbuf.at[c], oacc.at[c], val_s.at[c],
                          rank_s.at[c], selbias_s.at[c], bias_tbl, cmask, oT_s.at[c], st_s.at[c], n_sel)
              for c in range(NCHAIN)]
    live = list(chains)
    while live:
        handed = [next(ch, _DONE) for ch in live]
        live = [ch for ch, r in zip(live, handed) if r is not _DONE]
        loops = [r for r in handed if r is not _DONE and r is not None]
        if loops:
            joint = functools.reduce(jnp.minimum, [npair for npair, _, _ in loops])

            def trip(j, group):
                als = [head(j) for _, head, _ in group]
                for (_, _, tail), al in zip(group, als):
                    tail(j, al)

            lax.fori_loop(1, 1 + joint, lambda j, carry: (trip(j, loops), carry)[1], 0)
            for item in loops:
                lax.fori_loop(1 + joint, 1 + item[0], lambda j, carry, item=item: (trip(j, [item]), carry)[1], 0)


_DONE = object()


def _attn_chain(c, qi, qT_ref, kc_ref, vcT_ref, g_ref, o_ref, kall, vall, qall, kca, m_all, acc_all, al_s, sbuf,
                pbuf, oacc, val_s, rank_s, selbias_s, bias_tbl, cmask, oT_s, st_s, n_sel):
    S = kall.shape[1]
    ncmp = kc_ref.shape[2]
    nblk = S // SLC_BLOCK
    cmp_per_tile = TQ // CMP_STRIDE
    q0 = qi * TQ
    W4 = REP * TQ
    lane = lax.broadcasted_iota(I32, (1, W4), 1)
    lanes_c = slice(c * TQ, (c + 1) * TQ)

    q = qT_ref[0, :, lanes_c]
    for r in range(REP):
        qr = q[r * HEAD_DIM:(r + 1) * HEAD_DIM, :]
        qall[SEL, 0:HEAD_DIM, r * TQ:(r + 1) * TQ] = qr
        qall[WIN, 0:HEAD_DIM, r * TQ:(r + 1) * TQ] = qr

    gall = g_ref[0, 0, :, lanes_c]
    gates = [gall[br * REP:(br + 1) * REP, :] for br in range(3)]

    def stage_scores(br, tiles_masks):
        col_max = None
        for h, (tile, mask) in enumerate(tiles_masks):
            r0 = pl.multiple_of(tile * TK, TK)
            s = jnp.dot(kall[br, pl.ds(r0, TK), :], qall[br], preferred_element_type=F32)
            if mask is not None:
                s = s + bias_tbl[mask]
            sbuf[h * TK:(h + 1) * TK, :] = s
            c = jnp.max(s, axis=0, keepdims=True)
            col_max = c if col_max is None else jnp.maximum(col_max, c)
        m_old = m_all[br]
        m_new = jnp.maximum(m_old, col_max)
        m_all[br] = m_new
        return jnp.exp2(m_old - m_new)

    def stage_probs(br, ntile):
        pbuf[0:ntile * TK, :] = jnp.exp2(sbuf[0:ntile * TK, :] - m_all[br]).astype(BF16)

    def stage_values(br, tiles, al_row):
        acc = acc_all[br] * al_row
        for h, tile in enumerate(tiles):
            acc = acc + jnp.dot(vall[br, tile], pbuf[h * TK:(h + 1) * TK, :], preferred_element_type=F32)
        acc_all[br] = acc

    def start_branch(br, tl, mask1):
        al_s[...] = stage_scores(br, [(tl[0], MASK_CAUSAL), (tl[1], mask1)])

    def pair_items(br, n, tl):
        npair = jnp.maximum(n - 2, 0) // 2

        def head(j):
            al_prev = al_s[...]
            stage_probs(br, 2)
            al_s[...] = stage_scores(br, [(tl[2 * j], None), (tl[2 * j + 1], None)])
            return al_prev

        def tail(j, al_prev):
            stage_values(br, [tl[2 * j - 2], tl[2 * j - 1]], al_prev)

        return npair, head, tail

    def finish_branch(br, n, tl):
        rest = jnp.maximum(n - 2, 0)
        last = 2 * (rest // 2)

        @pl.when(rest % 2 == 1)
        def _():
            al_prev = al_s[...]
            stage_probs(br, 2)
            al_k = stage_scores(br, [(tl[n - 1], None)])
            stage_values(br, [tl[last], tl[last + 1]], al_prev)
            stage_probs(br, 1)
            stage_values(br, [tl[n - 1]], al_k)

        @pl.when(rest % 2 == 0)
        def _():
            stage_probs(br, 2)
            stage_values(br, [tl[last], tl[last + 1]], al_s[...])

    m_all[...] = jnp.full(m_all.shape, NEG, F32)
    acc_all[...] = jnp.zeros(acc_all.shape, F32)
    w1 = jnp.maximum(qi - 2, 0)
    w2 = jnp.maximum(qi - 1, 0)
    al_w0 = stage_scores(WIN, [(qi, MASK_CAUSAL),
                               (w1, jnp.where(qi >= 2, MASK_WINDOW_LOW, jnp.where(qi >= 1, MASK_NONE, MASK_ALL)))])
    yield

    c0 = pl.multiple_of(ncmp - cmp_per_tile * qi, cmp_per_tile)
    sc = jnp.dot(kca[...], qall[WIN], preferred_element_type=F32) + cmask[pl.ds(c0, ncmp), :]
    yield
    e = jnp.exp2(sc - jnp.max(sc, axis=0, keepdims=True))
    tq = q0 + lane % TQ
    p = e * ((1.0 / jnp.sum(e, axis=0, keepdims=True)) * (tq >= CMP_LEN - 1).astype(F32))
    ocT = jnp.dot(vcT_ref[0, 0], p.astype(BF16), preferred_element_type=F32)
    for r in range(REP):
        oacc[:, r * TQ:(r + 1) * TQ] = gates[0][r:r + 1, :] * ocT[:, r * TQ:(r + 1) * TQ]
    yield

    stage_probs(WIN, 2)
    al_w1 = stage_scores(WIN, [(w2, jnp.where(qi >= 2, MASK_NONE, MASK_ALL))])
    stage_values(WIN, [qi, w1], al_w0)
    yield

    psum = p[:, 0:TQ]
    for r in range(1, REP):
        psum = psum + p[:, r * TQ:(r + 1) * TQ]
    jj = lax.broadcasted_iota(I32, (nblk, ncmp), 0)
    cc = lax.broadcasted_iota(I32, (nblk, ncmp), 1)
    ovT = ((CMP_STRIDE * cc < SLC_BLOCK * jj + SLC_BLOCK)
           & (CMP_STRIDE * cc + CMP_LEN > SLC_BLOCK * jj)).astype(F32)
    impT = jnp.dot(ovT, psum, preferred_element_type=F32, precision=lax.Precision.HIGHEST)

    j_i = lax.broadcasted_iota(I32, (nblk, TQ), 0)
    t1 = q0 + lax.broadcasted_iota(I32, (nblk, TQ), 1)
    cur = t1 // SLC_BLOCK
    forced = (j_i == 0) | (j_i == cur) | (j_i == cur - 1)
    visible = SLC_BLOCK * j_i <= t1
    val_s[...] = jnp.where(forced, FORCED, jnp.where(visible, impT, NEG))
    yield
    ngrp = nblk // 8
    rank_s[...] = jnp.zeros(rank_s.shape, F32)
    j8 = lax.broadcasted_iota(I32, (8, TQ), 0)
    for ib in range(ngrp):
        @pl.when(8 * ib * SLC_BLOCK < q0 + TQ)
        def _(ib=ib):
            vals = [val_s[8 * jb:8 * jb + 8, :] for jb in range(ngrp)]
            ranks = [rank_s[8 * jb:8 * jb + 8, :] for jb in range(ngrp)]
            for i in range(8 * ib, 8 * ib + 8):
                row = jnp.broadcast_to(val_s[i:i + 1, :], (8, TQ))
                for jb in range(ngrp):
                    if jb > ib:
                        hit = row >= vals[jb]
                    elif jb < ib:
                        hit = row > vals[jb]
                    else:
                        hit = jnp.where(j8 > i - 8 * jb, jnp.where(row >= vals[jb], 1.0, 0.0),
                                        jnp.where(row > vals[jb], 1.0, 0.0)) > 0.5
                    ranks[jb] = ranks[jb] + jnp.where(hit, 1.0, 0.0)
            for jb in range(ngrp):
                rank_s[8 * jb:8 * jb + 8, :] = ranks[jb]
    yield
    ranks = [rank_s[8 * jb:8 * jb + 8, :] for jb in range(ngrp)]
    blocks_per_tile = TK // SLC_BLOCK
    st_s[0] = qi
    n_selt = jnp.int32(1)
    for jb in range(ngrp):
        chosen = ranks[jb] < n_sel
        selb = jnp.where(chosen, 0.0, NEG)
        for r in range(REP):
            selbias_s[8 * jb:8 * jb + 8, r * TQ:(r + 1) * TQ] = selb
        any_q = jnp.max(jnp.where(chosen, 1.0, 0.0), axis=1, keepdims=True)
        for hh in range(8 // blocks_per_tile):
            kt = (8 * jb) // blocks_per_tile + hh
            hit = jnp.max(any_q[hh * blocks_per_tile:(hh + 1) * blocks_per_tile, :]) > 0.5
            st_s[n_selt] = kt
            n_selt = n_selt + jnp.where(hit & (kt < qi), 1, 0)
    qall[SEL, HEAD_DIM:HEAD_DIM + nblk, :] = selbias_s[...].astype(BF16)
    yield

    stage_probs(WIN, 1)
    stage_values(WIN, [w2], al_w1)
    start_branch(SEL, st_s, jnp.where(n_selt >= 2, MASK_NONE, MASK_ALL))
    yield pair_items(SEL, n_selt, st_s)
    finish_branch(SEL, n_selt, st_s)

    def normalised(br):
        acc = acc_all[br]
        return acc[0:HEAD_DIM, :] / acc[HEAD_DIM:HEAD_DIM + 1, :]

    o_sel = normalised(SEL)
    o_win = normalised(WIN)
    for r in range(REP):
        lanes = slice(r * TQ, (r + 1) * TQ)
        oT_s[r * HEAD_DIM:(r + 1) * HEAD_DIM, :] = (oacc[:, lanes] + gates[1][r:r + 1, :] * o_sel[:, lanes]
                                                    + gates[2][r:r + 1, :] * o_win[:, lanes])
    o_ref[0, lanes_c, :] = oT_s[...].T.astype(BF16)


def _attention(qT, ks, vs, kw, vw, kc, vcT, gT):
    B, _, S = qT.shape
    G = N_KV
    nkt = S // TK
    ncmp = kc.shape[2]
    nblk = S // SLC_BLOCK
    n_sel = min(SLC_TOPK, nblk)
    W4 = REP * TQ
    res4 = lambda a: pl.BlockSpec((1, 1) + a.shape[2:], lambda b, g, i: (b, g) + (0,) * (a.ndim - 2))
    return pl.pallas_call(
        functools.partial(_attn_kernel, n_sel=n_sel),
        grid=(B, G, S // (NCHAIN * TQ)),
        in_specs=[pl.BlockSpec((1, REP * HEAD_DIM, NCHAIN * TQ), lambda b, g, i: (b, g, i)),
                  res4(ks), res4(vs), res4(kw), res4(vw), res4(kc), res4(vcT),
                  pl.BlockSpec((1, 1, GATE_ROWS, NCHAIN * TQ), lambda b, g, i: (b, g, 0, i))],
        out_specs=pl.BlockSpec((1, NCHAIN * TQ, REP * HEAD_DIM), lambda b, g, i: (b, i, g)),
        out_shape=jax.ShapeDtypeStruct((B, S, D_ATT), BF16),
        scratch_shapes=[pltpu.VMEM((2, S, KAUG), BF16),
                        pltpu.VMEM((2, nkt, VROWS, TK), BF16),
                        pltpu.VMEM((NCHAIN, 2, KAUG, W4), BF16),
                        pltpu.VMEM((ncmp, KAUG), BF16),
                        pltpu.VMEM((NCHAIN, 2, 1, W4), F32),
                        pltpu.VMEM((NCHAIN, 2, VROWS, W4), F32),
                        pltpu.VMEM((NCHAIN, 1, W4), F32),
                        pltpu.VMEM((NCHAIN, 2 * TK, W4), F32),
                        pltpu.VMEM((NCHAIN, 2 * TK, W4), BF16),
                        pltpu.VMEM((NCHAIN, HEAD_DIM, W4), F32),
                        pltpu.VMEM((NCHAIN, nblk, TQ), F32),
                        pltpu.VMEM((NCHAIN, nblk, TQ), F32),
                        pltpu.VMEM((NCHAIN, nblk, W4), F32),
                        pltpu.VMEM((4, TK, W4), F32),
                        pltpu.VMEM((2 * ncmp, W4), F32),
                        pltpu.VMEM((NCHAIN, REP * HEAD_DIM, TQ), F32),
                        pltpu.SMEM((NCHAIN, nkt + 2), I32)],
        compiler_params=pltpu.CompilerParams(dimension_semantics=("arbitrary", "arbitrary", "arbitrary"),
                                             vmem_limit_bytes=VMEM_LIMIT),
        name="attn",
    )(qT, ks, vs, kw, vw, kc, vcT, gT)


def _outmlp_kernel(x_ref, rnn_ref, att_ref, gatt_ref, wo_ref, gpost_ref, g1_ref, gpre_ref, sc2_ref, sh2_ref,
                   w1_ref, w2_ref, gpost2_ref, g2_ref, o_ref):
    tm = x_ref.shape[1]
    halves = [slice(0, tm // 2), slice(tm // 2, tm)]
    x1s, h2s = [], []
    for rows in halves:
        att_n = (_rms(att_ref[0, rows, :].astype(F32)) * gatt_ref[...]).astype(BF16)
        y = (jnp.dot(rnn_ref[0, rows, :], wo_ref[0:D_RNN, :], preferred_element_type=F32)
             + jnp.dot(att_n, wo_ref[D_RNN:, :], preferred_element_type=F32))
        x1 = x_ref[0, rows, :] + (1.0 + g1_ref[0]) * (_rms(y) * gpost_ref[...])
        x1s.append(x1)
        h2s.append((_rms(x1) * (gpre_ref[...] * (1.0 + sc2_ref[0])) + sh2_ref[0]).astype(BF16))
    fc = 1024
    for rows, x1, h2 in zip(halves, x1s, h2s):
        ff = jnp.zeros(x1.shape, F32)
        for c in range(D_FF // fc):
            hid = jnp.maximum(jnp.dot(h2, w1_ref[:, c * fc:(c + 1) * fc], preferred_element_type=F32), 0.0)
            ff = ff + jnp.dot((hid * hid).astype(BF16), w2_ref[c * fc:(c + 1) * fc, :],
                              preferred_element_type=F32)
        o_ref[0, rows, :] = x1 + (1.0 + g2_ref[0]) * (_rms(ff) * gpost2_ref[...])


def _outmlp(x, rnn_n, att, gatt, wo, gpost, g1, gpre, sc2, sh2, w1, w2, gpost2, g2):
    B, S, D = x.shape
    tm = min(TM_OUT, S)
    row = lambda n: pl.BlockSpec((1, n), lambda b, s: (0, 0))
    per_b = lambda n: pl.BlockSpec((1, 1, n), lambda b, s: (b, 0, 0))
    const = lambda a: pl.BlockSpec(a.shape, lambda b, s: (0,) * a.ndim, pipeline_mode=pl.Buffered(1))
    tok = lambda n: pl.BlockSpec((1, tm, n), lambda b, s: (b, s, 0))
    return pl.pallas_call(
        _outmlp_kernel,
        grid=(B, S // tm),
        in_specs=[tok(D), tok(D_RNN), tok(D_ATT), row(D_ATT), const(wo), row(D), per_b(D), row(D),
                  per_b(D), per_b(D), const(w1), const(w2), row(D), per_b(D)],
        out_specs=tok(D),
        out_shape=jax.ShapeDtypeStruct((B, S, D), F32),
        compiler_params=pltpu.CompilerParams(dimension_semantics=("arbitrary", "arbitrary"),
                                             vmem_limit_bytes=VMEM_LIMIT),
        name="outmlp",
    )(x, rnn_n, att, gatt, wo, gpost, g1, gpre, sc2, sh2, w1, w2, gpost2, g2)


def _block_diag(w):
    n, k, _ = w.shape
    return jnp.einsum('nij,nm->nimj', w, jnp.eye(n, dtype=w.dtype)).reshape(n * k, n * k)


def _layer(x, c, ada_w, ada_b, pre_norm_mix, w_in, conv_w, conv_b, lru_wa, lru_ba, lru_wx, lru_bx, lru_lambda,
           cmp_pos_k, cmp_w1_k, cmp_w2_k, cmp_pos_v, cmp_w1_v, cmp_w2_v, norm_rnn_out, norm_att_out, w_out,
           post_norm_mix, pre_norm_mlp, w_ff1, w_ff2, post_norm_mlp):
    B, S, D = x.shape
    G = N_KV
    row = lambda v: v.reshape(1, -1)

    mod = _ada(c, ada_w, ada_b)
    sh1, sc1, g1, sh2, sc2, g2 = [m.reshape(B, 1, D) for m in jnp.split(mod, 6, axis=-1)]

    gate_cols = [OFF_GATE + br * N_HEADS + g * REP + r for g in range(G) for br in range(3) for r in range(REP)]
    w_gate = w_in[:, jnp.asarray(gate_cols)].reshape(D, G, 3 * REP)
    w_gate = jnp.pad(w_gate, ((0, 0), (0, 0), (0, GATE_ROWS - 3 * REP))).reshape(D, G * GATE_ROWS)
    w_in_p = jnp.concatenate([w_in[:, :OFF_GATE], jnp.pad(w_gate, ((0, 0), (0, GATE_PAD - G * GATE_ROWS)))],
                             axis=1).astype(BF16)
    wa = _block_diag(lru_wa).astype(BF16)
    wx = _block_diag(lru_wx).astype(BF16)
    half = CMP_LEN // 2 * HEAD_DIM

    def w1_cat(w1):
        return jnp.concatenate([w1[:half], w1[half:]], axis=1).astype(BF16)

    def pos_rows(pos):
        return jnp.pad(pos.reshape(2, half), ((0, 14), (0, 0))).astype(BF16)

    rnn_n, qT, ks, vsT, kw, vwT, kc_in, vc_in, gT = _inproj(
        x, row(pre_norm_mix), sc1, sh1, w_in_p, conv_w, row(conv_b), wa, row(lru_ba), wx, row(lru_bx),
        row(lru_lambda), row(norm_rnn_out))
    kc, vc = _compress(kc_in, vc_in, w1_cat(cmp_w1_k), cmp_w2_k.astype(BF16), pos_rows(cmp_pos_k),
                       w1_cat(cmp_w1_v), cmp_w2_v.astype(BF16), pos_rows(cmp_pos_v))
    att = _attention(qT, ks, vsT, kw, vwT, kc, vc.transpose(0, 1, 3, 2), gT)

    return _outmlp(x, rnn_n, att, row(norm_att_out), w_out.astype(BF16), row(post_norm_mix), g1,
                   row(pre_norm_mlp), sc2, sh2, w_ff1.astype(BF16), w_ff2.astype(BF16), row(post_norm_mlp), g2)


def kernel(x, c, ada_w, ada_b, pre_norm_mix, w_in, conv_w, conv_b, lru_wa, lru_ba, lru_wx, lru_bx, lru_lambda,
           cmp_pos_k, cmp_w1_k, cmp_w2_k, cmp_pos_v, cmp_w1_v, cmp_w2_v, norm_rnn_out, norm_att_out, w_out,
           post_norm_mix, pre_norm_mlp, w_ff1, w_ff2, post_norm_mlp):
    for l in range(ada_w.shape[0]):
        x = _layer(x, c, ada_w[l], ada_b[l], pre_norm_mix[l], w_in[l], conv_w[l], conv_b[l], lru_wa[l], lru_ba[l],
                   lru_wx[l], lru_bx[l], lru_lambda[l], cmp_pos_k[l], cmp_w1_k[l], cmp_w2_k[l], cmp_pos_v[l],
                   cmp_w1_v[l], cmp_w2_v[l], norm_rnn_out[l], norm_att_out[l], w_out[l], post_norm_mix[l],
                   pre_norm_mlp[l], w_ff1[l], w_ff2[l], post_norm_mlp[l])
    return x
```

```python
import functools

import jax
import jax.numpy as jnp
from jax import lax
from jax.experimental import pallas as pl
from jax.experimental.pallas import tpu as pltpu

F32 = jnp.float32
BF16 = jnp.bfloat16
I32 = jnp.int32

D_MODEL = 1024
D_RNN = 512
RNN_BLOCKS = 8
CONV_WIDTH = 4
LRU_C = 8.0
N_HEADS = 8
HEAD_DIM = 64
N_KV = 2
REP = N_HEADS // N_KV
D_ATT = N_HEADS * HEAD_DIM
CMP_LEN = 32
CMP_STRIDE = 16
CMP_HIDDEN = 256
SLC_BLOCK = 64
SLC_TOPK = 16
WINDOW = 512
D_FF = 4 * D_MODEL
EPS = 1e-6
NEG = -1e30
FORCED = 1e4
LOG2E = 1.4426950408889634

KV_COLS = 6 * N_KV * HEAD_DIM
N_GATE = 3 * N_HEADS
GATE_PAD = 128
GATE_ROWS = 16
OFF_Q = 2 * D_RNN
OFF_KV = OFF_Q + D_ATT
OFF_GATE = OFF_KV + KV_COLS
D_IN_PAD = OFF_GATE + GATE_PAD

TM_IN = 512
TM_OUT = 1024
TQ = 256
TK = 256
NCHAIN = 4
KAUG = 256
VROWS = 80
VMEM_LIMIT = 56 * 1024 * 1024

SEL, WIN = 0, 1
MASK_NONE, MASK_CAUSAL, MASK_WINDOW_LOW, MASK_ALL = 0, 1, 2, 3
POS_COL = 2 * HEAD_DIM


def _gelu_tanh(x):
    return 0.5 * x * (1.0 + jnp.tanh(0.7978845608028654 * (x + 0.044715 * (x * x * x))))


def _rms(x):
    return x * lax.rsqrt(jnp.mean(x * x, axis=-1, keepdims=True) + EPS)


def _ada_kernel(c_ref, w_ref, b_ref, o_ref):
    c = c_ref[...]
    a = c * jax.nn.sigmoid(c)
    o_ref[...] = jnp.dot(a, w_ref[...], preferred_element_type=F32,
                         precision=lax.Precision.HIGHEST) + b_ref[...]


def _ada(c, w, b):
    B, D = c.shape
    N = w.shape[1]
    tn = 1024
    return pl.pallas_call(
        _ada_kernel,
        grid=(N // tn,),
        in_specs=[pl.BlockSpec((B, D), lambda j: (0, 0)),
                  pl.BlockSpec((D, tn), lambda j: (0, j)),
                  pl.BlockSpec((1, tn), lambda j: (0, j))],
        out_specs=pl.BlockSpec((B, tn), lambda j: (0, j)),
        out_shape=jax.ShapeDtypeStruct((B, N), F32),
        name="ada",
    )(c, w, b.reshape(1, N))


def _inproj_kernel(x_ref, gain_ref, sc_ref, sh_ref, w_ref, cw_ref, cb_ref, wa_ref, ba_ref, wx_ref, bx_ref,
                   lam_ref, grnn_ref,
                   rnn_ref, qT_ref, ks_ref, vs_ref, kw_ref, vw_ref, kc_ref, vc_ref, g_ref,
                   xbuf, hcar, a_s, u_s, h_s):
    tm = x_ref.shape[1]
    hd = HEAD_DIM

    @pl.when(pl.program_id(1) == 0)
    def _():
        xbuf[0:8, :] = jnp.zeros((8, D_RNN), F32)
        hcar[...] = jnp.zeros((1, D_RNN), F32)

    x = x_ref[0]
    h = _rms(x) * (gain_ref[...] * (1.0 + sc_ref[0])) + sh_ref[0]
    hb = h.astype(BF16)

    xr = jnp.dot(hb, w_ref[:, D_RNN:OFF_Q], preferred_element_type=F32)
    xbuf[8:8 + tm, :] = xr
    y = (cw_ref[3:4, :] * xr + cw_ref[2:3, :] * xbuf[7:7 + tm, :]
         + cw_ref[1:2, :] * xbuf[6:6 + tm, :] + cw_ref[0:1, :] * xbuf[5:5 + tm, :]) + cb_ref[...]
    xbuf[0:8, :] = xbuf[tm:tm + 8, :]

    yb = y.astype(BF16)
    r = jax.nn.sigmoid(jnp.dot(yb, wa_ref[...], preferred_element_type=F32) + ba_ref[...])
    i = jax.nn.sigmoid(jnp.dot(yb, wx_ref[...], preferred_element_type=F32) + bx_ref[...])
    nl = -lam_ref[...]
    softplus = jnp.maximum(nl, 0.0) + jnp.log(1.0 + jnp.exp(-jnp.abs(nl)))
    a = jnp.exp((-LRU_C) * r * softplus)
    a_s[...] = a
    u_s[...] = jnp.sqrt(1.0 - a * a) * (i * y)

    qT_ref[0] = (jnp.dot(hb, w_ref[:, OFF_Q:OFF_KV], preferred_element_type=F32)
                 * (HEAD_DIM ** -0.5 * LOG2E)).T.astype(BF16)
    kv = jnp.dot(hb, w_ref[:, OFF_KV:OFF_GATE], preferred_element_type=F32)
    width = N_KV * hd
    vsT = kv[:, 3 * width:4 * width].T
    vwT = kv[:, 5 * width:6 * width].T
    for gi in range(N_KV):
        kc_ref[0, gi] = kv[:, gi * hd:(gi + 1) * hd]
        vc_ref[0, gi] = kv[:, width + gi * hd:width + (gi + 1) * hd]
        ks_ref[0, gi] = kv[:, 2 * width + gi * hd:2 * width + (gi + 1) * hd].astype(BF16)
        kw_ref[0, gi] = kv[:, 4 * width + gi * hd:4 * width + (gi + 1) * hd].astype(BF16)
        for j in range(tm // TK):
            vs_ref[0, gi, j] = vsT[gi * hd:(gi + 1) * hd, j * TK:(j + 1) * TK].astype(BF16)
            vw_ref[0, gi, j] = vwT[gi * hd:(gi + 1) * hd, j * TK:(j + 1) * TK].astype(BF16)
    gates = jax.nn.sigmoid(jnp.dot(hb, w_ref[:, OFF_GATE:D_IN_PAD], preferred_element_type=F32)).T
    for gi in range(N_KV):
        g_ref[0, gi] = gates[gi * GATE_ROWS:(gi + 1) * GATE_ROWS, :]
    g = jnp.dot(hb, w_ref[:, 0:D_RNN], preferred_element_type=F32)


    rows = lax.broadcasted_iota(I32, (8, D_RNN), 0)
    hprev = hcar[...]
    for gi in range(tm // 8):
        ag = a_s[gi * 8:gi * 8 + 8, :]
        ug = u_s[gi * 8:gi * 8 + 8, :]
        for k in (1, 2, 4):
            a_sh = jnp.where(rows >= k, pltpu.roll(ag, k, 0), 1.0)
            u_sh = jnp.where(rows >= k, pltpu.roll(ug, k, 0), 0.0)
            ug = ag * u_sh + ug
            ag = ag * a_sh
        hg = ag * hprev + ug
        h_s[gi * 8:gi * 8 + 8, :] = hg
        hprev = hg[7:8, :]
    hcar[...] = hprev

    rnn = _gelu_tanh(g) * h_s[...]
    rnn_ref[0] = (_rms(rnn) * grnn_ref[...]).astype(BF16)


def _inproj(x, gain, sc, sh, w_in, conv_w, conv_b, wa, ba, wx, bx, lam, grnn):
    B, S, D = x.shape
    tm = min(TM_IN, S)
    row = lambda n: pl.BlockSpec((1, n), lambda b, s: (0, 0))
    per_b = lambda n: pl.BlockSpec((1, 1, n), lambda b, s: (b, 0, 0))
    full = lambda a: pl.BlockSpec(a.shape, lambda b, s: (0,) * a.ndim)
    tok = lambda n: pl.BlockSpec((1, tm, n), lambda b, s: (b, s, 0))
    G = N_KV
    grp = pl.BlockSpec((1, G, tm, HEAD_DIM), lambda b, s: (b, 0, s, 0))
    grpT = pl.BlockSpec((1, G, tm // TK, HEAD_DIM, TK), lambda b, s: (b, 0, s, 0, 0))
    return pl.pallas_call(
        _inproj_kernel,
        grid=(B, S // tm),
        in_specs=[tok(D), row(D), per_b(D), per_b(D), full(w_in), full(conv_w), row(D_RNN),
                  full(wa), row(D_RNN), full(wx), row(D_RNN), row(D_RNN), row(D_RNN)],
        out_specs=[tok(D_RNN),
                   pl.BlockSpec((1, D_ATT, tm), lambda b, s: (b, 0, s)),
                   grp, grpT, grp, grpT, grp, grp,
                   pl.BlockSpec((1, G, GATE_ROWS, tm), lambda b, s: (b, 0, 0, s))],
        out_shape=[jax.ShapeDtypeStruct((B, S, D_RNN), BF16),
                   jax.ShapeDtypeStruct((B, D_ATT, S), BF16),
                   jax.ShapeDtypeStruct((B, G, S, HEAD_DIM), BF16),
                   jax.ShapeDtypeStruct((B, G, S // TK, HEAD_DIM, TK), BF16),
                   jax.ShapeDtypeStruct((B, G, S, HEAD_DIM), BF16),
                   jax.ShapeDtypeStruct((B, G, S // TK, HEAD_DIM, TK), BF16),
                   jax.ShapeDtypeStruct((B, G, S, HEAD_DIM), F32),
                   jax.ShapeDtypeStruct((B, G, S, HEAD_DIM), F32),
                   jax.ShapeDtypeStruct((B, G, GATE_ROWS, S), F32)],
        scratch_shapes=[pltpu.VMEM((tm + 8, D_RNN), F32), pltpu.VMEM((1, D_RNN), F32),
                        pltpu.VMEM((tm, D_RNN), F32), pltpu.VMEM((tm, D_RNN), F32),
                        pltpu.VMEM((tm, D_RNN), F32)],
        compiler_params=pltpu.CompilerParams(dimension_semantics=("arbitrary", "arbitrary"),
                                             vmem_limit_bytes=VMEM_LIMIT),
        name="inproj",
    )(x, gain, sc, sh, w_in, conv_w, conv_b, wa, ba, wx, bx, lam, grnn)


def _compress_kernel(rk_ref, rv_ref, w1k_ref, w2k_ref, pk_ref, w1v_ref, w2v_ref, pv_ref, kc_ref, vc_ref):
    def one(r_ref, w1_ref, w2_ref, pos_ref, o_ref):
        n = r_ref.shape[2] // CMP_STRIDE
        p = jnp.zeros((n, 2 * CMP_HIDDEN), F32)
        for l in range(CMP_STRIDE):
            rows = r_ref[0, 0, pl.ds(l, n, stride=CMP_STRIDE), :].astype(BF16)
            p = p + jnp.dot(rows, w1_ref[l * HEAD_DIM:(l + 1) * HEAD_DIM, :], preferred_element_type=F32)
        posb = jnp.dot(pos_ref[...], w1_ref[...], preferred_element_type=F32)
        bias = posb[0:1, 0:CMP_HIDDEN] + posb[1:2, CMP_HIDDEN:]
        pre = p[:, 0:CMP_HIDDEN] + pltpu.roll(p[:, CMP_HIDDEN:], n - 1, 0) + bias
        hid = _gelu_tanh(pre).astype(BF16)
        o_ref[0, 0] = jnp.dot(hid, w2_ref[...], preferred_element_type=F32).astype(BF16)

    one(rk_ref, w1k_ref, w2k_ref, pk_ref, kc_ref)
    one(rv_ref, w1v_ref, w2v_ref, pv_ref, vc_ref)


def _compress(rk, rv, w1k, w2k, posk, w1v, w2v, posv):
    B, G, S, width = rk.shape
    n = S // CMP_STRIDE
    blk = pl.BlockSpec((1, 1, S, width), lambda b, g: (b, g, 0, 0))
    full = lambda a: pl.BlockSpec(a.shape, lambda b, g: (0,) * a.ndim)
    out = pl.BlockSpec((1, 1, n, HEAD_DIM), lambda b, g: (b, g, 0, 0))
    return pl.pallas_call(
        _compress_kernel,
        grid=(B, G),
        in_specs=[blk, blk, full(w1k), full(w2k), full(posk), full(w1v), full(w2v), full(posv)],
        out_specs=[out, out],
        out_shape=[jax.ShapeDtypeStruct((B, G, n, HEAD_DIM), BF16)] * 2,
        compiler_params=pltpu.CompilerParams(vmem_limit_bytes=VMEM_LIMIT),
        name="compress",
    )(rk, rv, w1k, w2k, posk, w1v, w2v, posv)


def _attn_kernel(qT_ref, ks_ref, vs_ref, kw_ref, vw_ref, kc_ref, vcT_ref, g_ref,
                 o_ref,
                 kall, vall, qall, kca, m_all, acc_all, al_s, sbuf, pbuf, oacc, val_s, rank_s, selbias_s,
                 bias_tbl, cmask,
                 oT_s, st_s, *, n_sel):
    S = ks_ref.shape[2]
    nkt = S // TK
    ncmp = kc_ref.shape[2]
    b = pl.program_id(0)
    g = pl.program_id(1)
    step = pl.program_id(2)
    W4 = REP * TQ
    cmp_per_tile = TQ // CMP_STRIDE

    @pl.when((b == 0) & (g == 0) & (step == 0))
    def _():
        col = lax.broadcasted_iota(I32, (TK, KAUG), 1)
        is_blk = (col == POS_COL) | (col == POS_COL + 2)
        is_off = (col == POS_COL + 1) | (col == POS_COL + 3)
        ones_row = (lax.broadcasted_iota(I32, (VROWS - HEAD_DIM, TK), 0) == 0).astype(BF16)
        for kt in range(nkt):
            key = kt * TK + lax.broadcasted_iota(I32, (TK, KAUG), 0)
            pos = jnp.where(is_blk, (key // SLC_BLOCK).astype(F32),
                            jnp.where(is_off, (key % SLC_BLOCK).astype(F32), 0.0))
            onehot = (col - HEAD_DIM == key // SLC_BLOCK).astype(F32)
            kall[SEL, kt * TK:(kt + 1) * TK, :] = (pos + onehot).astype(BF16)
            kall[WIN, kt * TK:(kt + 1) * TK, :] = pos.astype(BF16)
            vall[SEL, kt, HEAD_DIM:VROWS, :] = ones_row
            vall[WIN, kt, HEAD_DIM:VROWS, :] = ones_row
        ko = lax.broadcasted_iota(I32, (TK, W4), 0)
        to = lax.broadcasted_iota(I32, (TK, W4), 1) % TQ
        bias_tbl[MASK_NONE] = jnp.zeros((TK, W4), F32)
        bias_tbl[MASK_CAUSAL] = jnp.where(ko <= to, 0.0, NEG)
        bias_tbl[MASK_WINDOW_LOW] = jnp.where(ko > to, 0.0, NEG)
        bias_tbl[MASK_ALL] = jnp.full((TK, W4), NEG, F32)
        u = lax.broadcasted_iota(I32, (2 * ncmp, W4), 0)
        tc = lax.broadcasted_iota(I32, (2 * ncmp, W4), 1) % TQ
        cmask[...] = jnp.where(u <= ncmp + ((tc + 1) // CMP_STRIDE) - 2, 0.0, NEG)
        cc = lax.broadcasted_iota(I32, (ncmp, KAUG), 0)
        colc = lax.broadcasted_iota(I32, (ncmp, KAUG), 1)
        kca[...] = jnp.where((colc == POS_COL) | (colc == POS_COL + 2), (cc // 4).astype(F32),
                             jnp.where((colc == POS_COL + 1) | (colc == POS_COL + 3),
                                       (CMP_STRIDE * (cc % 4)).astype(F32), 0.0)).astype(BF16)
        qall[...] = jnp.zeros(qall.shape, BF16)

    lane = lax.broadcasted_iota(I32, (1, W4), 1)
    head = g * REP + lane // TQ
    slope = lax.bitcast_convert_type((126 - head) << 23, F32)

    @pl.when(step == 0)
    def _():
        for kt in range(nkt):
            kall[SEL, kt * TK:(kt + 1) * TK, 0:HEAD_DIM] = ks_ref[0, 0, kt * TK:(kt + 1) * TK, :]
            kall[WIN, kt * TK:(kt + 1) * TK, 0:HEAD_DIM] = kw_ref[0, 0, kt * TK:(kt + 1) * TK, :]
            vall[SEL, kt, 0:HEAD_DIM, :] = vs_ref[0, 0, kt]
            vall[WIN, kt, 0:HEAD_DIM, :] = vw_ref[0, 0, kt]
        kca[:, 0:HEAD_DIM] = kc_ref[0, 0]
        c_hi = (slope * LOG2E).astype(BF16).astype(F32)
        c_lo = slope * LOG2E - c_hi
        r16 = lax.broadcasted_iota(I32, (16, W4), 0)
        alibi = jnp.where(r16 == 0, c_hi * SLC_BLOCK, jnp.where(r16 == 1, c_hi, jnp.where(
            r16 == 2, c_lo * SLC_BLOCK, jnp.where(r16 == 3, c_lo, 0.0)))).astype(BF16)
        for c in range(NCHAIN):
            qall[c, SEL, POS_COL:POS_COL + 16, :] = alibi
            qall[c, WIN, POS_COL:POS_COL + 16, :] = alibi

    chains = [_attn_chain(c, NCHAIN * step + c, qT_ref, kc_ref, vcT_ref, g_ref, o_ref, kall, vall, qall.at[c], kca,
                          m_all.at[c], acc_all.at[c], al_s.at[c], sbuf.at[c], pbuf.at[c], oacc.at[c], val_s.at[c],
                          rank_s.at[c], selbias_s.at[c], bias_tbl, cmask, oT_s.at[c], st_s.at[c], n_sel)
              for c in range(NCHAIN)]
    live = list(chains)
    while live:
        handed = [next(ch, _DONE) for ch in live]
        live = [ch for ch, r in zip(live, handed) if r is not _DONE]
        loops = [r for r in handed if r is not _DONE and r is not None]
        if loops:
            joint = functools.reduce(jnp.minimum, [npair for npair, _, _ in loops])

            def trip(j, group):
                als = [head(j) for _, head, _ in group]
                for (_, _, tail), al in zip(group, als):
                    tail(j, al)

            lax.fori_loop(1, 1 + joint, lambda j, carry: (trip(j, loops), carry)[1], 0)
            for item in loops:
                lax.fori_loop(1 + joint, 1 + item[0], lambda j, carry, item=item: (trip(j, [item]), carry)[1], 0)


_DONE = object()


def _attn_chain(c, qi, qT_ref, kc_ref, vcT_ref, g_ref, o_ref, kall, vall, qall, kca, m_all, acc_all, al_s, sbuf,
                pbuf, oacc, val_s, rank_s, selbias_s, bias_tbl, cmask, oT_s, st_s, n_sel):
    S = kall.shape[1]
    ncmp = kc_ref.shape[2]
    nblk = S // SLC_BLOCK
    cmp_per_tile = TQ // CMP_STRIDE
    q0 = qi * TQ
    W4 = REP * TQ
    lane = lax.broadcasted_iota(I32, (1, W4), 1)
    lanes_c = slice(c * TQ, (c + 1) * TQ)

    q = qT_ref[0, :, lanes_c]
    for r in range(REP):
        qr = q[r * HEAD_DIM:(r + 1) * HEAD_DIM, :]
        qall[SEL, 0:HEAD_DIM, r * TQ:(r + 1) * TQ] = qr
        qall[WIN, 0:HEAD_DIM, r * TQ:(r + 1) * TQ] = qr

    gall = g_ref[0, 0, :, lanes_c]
    gates = [gall[br * REP:(br + 1) * REP, :] for br in range(3)]

    def stage_scores(br, tiles_masks):
        col_max = None
        for h, (tile, mask) in enumerate(tiles_masks):
            r0 = pl.multiple_of(tile * TK, TK)
            s = jnp.dot(kall[br, pl.ds(r0, TK), :], qall[br], preferred_element_type=F32)
            if mask is not None:
                s = s + bias_tbl[mask]
            sbuf[h * TK:(h + 1) * TK, :] = s
            c = jnp.max(s, axis=0, keepdims=True)
            col_max = c if col_max is None else jnp.maximum(col_max, c)
        m_old = m_all[br]
        m_new = jnp.maximum(m_old, col_max)
        m_all[br] = m_new
        return jnp.exp2(m_old - m_new)

    def stage_probs(br, ntile):
        pbuf[0:ntile * TK, :] = jnp.exp2(sbuf[0:ntile * TK, :] - m_all[br]).astype(BF16)

    def stage_values(br, tiles, al_row):
        acc = acc_all[br] * al_row
        for h, tile in enumerate(tiles):
            acc = acc + jnp.dot(vall[br, tile], pbuf[h * TK:(h + 1) * TK, :], preferred_element_type=F32)
        acc_all[br] = acc

    def start_branch(br, tl, mask1):
        al_s[...] = stage_scores(br, [(tl[0], MASK_CAUSAL), (tl[1], mask1)])

    def pair_items(br, n, tl):
        npair = jnp.maximum(n - 2, 0) // 2

        def head(j):
            al_prev = al_s[...]
            stage_probs(br, 2)
            al_s[...] = stage_scores(br, [(tl[2 * j], None), (tl[2 * j + 1], None)])
            return al_prev

        def tail(j, al_prev):
            stage_values(br, [tl[2 * j - 2], tl[2 * j - 1]], al_prev)

        return npair, head, tail

    def finish_branch(br, n, tl):
        rest = jnp.maximum(n - 2, 0)
        last = 2 * (rest // 2)
        odd = rest % 2 == 1
        t_single = tl[jnp.where(odd, n - 1, 0)]
        al_prev = al_s[...]
        stage_probs(br, 2)
        al_k = stage_scores(br, [(t_single, jnp.where(odd, MASK_NONE, MASK_ALL))])
        yield
        stage_values(br, [tl[last], tl[last + 1]], al_prev)
        stage_probs(br, 1)
        yield
        stage_values(br, [t_single], al_k)

    m_all[...] = jnp.full(m_all.shape, NEG, F32)
    acc_all[...] = jnp.zeros(acc_all.shape, F32)
    w1 = jnp.maximum(qi - 2, 0)
    w2 = jnp.maximum(qi - 1, 0)
    al_w0 = stage_scores(WIN, [(qi, MASK_CAUSAL),
                               (w1, jnp.where(qi >= 2, MASK_WINDOW_LOW, jnp.where(qi >= 1, MASK_NONE, MASK_ALL)))])
    yield

    c0 = pl.multiple_of(ncmp - cmp_per_tile * qi, cmp_per_tile)
    sc = jnp.dot(kca[...], qall[WIN], preferred_element_type=F32) + cmask[pl.ds(c0, ncmp), :]
    yield
    e = jnp.exp2(sc - jnp.max(sc, axis=0, keepdims=True))
    tq = q0 + lane % TQ
    p = e * ((1.0 / jnp.sum(e, axis=0, keepdims=True)) * (tq >= CMP_LEN - 1).astype(F32))
    ocT = jnp.dot(vcT_ref[0, 0], p.astype(BF16), preferred_element_type=F32)
    for r in range(REP):
        oacc[:, r * TQ:(r + 1) * TQ] = gates[0][r:r + 1, :] * ocT[:, r * TQ:(r + 1) * TQ]
    yield

    stage_probs(WIN, 2)
    al_w1 = stage_scores(WIN, [(w2, jnp.where(qi >= 2, MASK_NONE, MASK_ALL))])
    stage_values(WIN, [qi, w1], al_w0)
    yield

    psum = p[:, 0:TQ]
    for r in range(1, REP):
        psum = psum + p[:, r * TQ:(r + 1) * TQ]
    jj = lax.broadcasted_iota(I32, (nblk, ncmp), 0)
    cc = lax.broadcasted_iota(I32, (nblk, ncmp), 1)
    ovT = ((CMP_STRIDE * cc < SLC_BLOCK * jj + SLC_BLOCK)
           & (CMP_STRIDE * cc + CMP_LEN > SLC_BLOCK * jj)).astype(F32)
    impT = jnp.dot(ovT, psum, preferred_element_type=F32, precision=lax.Precision.HIGHEST)

    j_i = lax.broadcasted_iota(I32, (nblk, TQ), 0)
    t1 = q0 + lax.broadcasted_iota(I32, (nblk, TQ), 1)
    cur = t1 // SLC_BLOCK
    forced = (j_i == 0) | (j_i == cur) | (j_i == cur - 1)
    visible = SLC_BLOCK * j_i <= t1
    val_s[...] = jnp.where(forced, FORCED, jnp.where(visible, impT, NEG))
    yield
    ngrp = nblk // 8
    rank_s[...] = jnp.zeros(rank_s.shape, F32)
    j8 = lax.broadcasted_iota(I32, (8, TQ), 0)
    for ib in range(ngrp):
        @pl.when(8 * ib * SLC_BLOCK < q0 + TQ)
        def _(ib=ib):
            vals = [val_s[8 * jb:8 * jb + 8, :] for jb in range(ngrp)]
            ranks = [rank_s[8 * jb:8 * jb + 8, :] for jb in range(ngrp)]
            for i in range(8 * ib, 8 * ib + 8):
                row = jnp.broadcast_to(val_s[i:i + 1, :], (8, TQ))
                for jb in range(ngrp):
                    if jb > ib:
                        hit = row >= vals[jb]
                    elif jb < ib:
                        hit = row > vals[jb]
                    else:
                        hit = jnp.where(j8 > i - 8 * jb, jnp.where(row >= vals[jb], 1.0, 0.0),
                                        jnp.where(row > vals[jb], 1.0, 0.0)) > 0.5
                    ranks[jb] = ranks[jb] + jnp.where(hit, 1.0, 0.0)
            for jb in range(ngrp):
                rank_s[8 * jb:8 * jb + 8, :] = ranks[jb]
    yield
    ranks = [rank_s[8 * jb:8 * jb + 8, :] for jb in range(ngrp)]
    blocks_per_tile = TK // SLC_BLOCK
    st_s[0] = qi
    n_selt = jnp.int32(1)
    for jb in range(ngrp):
        chosen = ranks[jb] < n_sel
        selb = jnp.where(chosen, 0.0, NEG)
        for r in range(REP):
            selbias_s[8 * jb:8 * jb + 8, r * TQ:(r + 1) * TQ] = selb
        any_q = jnp.max(jnp.where(chosen, 1.0, 0.0), axis=1, keepdims=True)
        for hh in range(8 // blocks_per_tile):
            kt = (8 * jb) // blocks_per_tile + hh
            hit = jnp.max(any_q[hh * blocks_per_tile:(hh + 1) * blocks_per_tile, :]) > 0.5
            st_s[n_selt] = kt
            n_selt = n_selt + jnp.where(hit & (kt < qi), 1, 0)
    qall[SEL, HEAD_DIM:HEAD_DIM + nblk, :] = selbias_s[...].astype(BF16)
    yield

    stage_probs(WIN, 1)
    stage_values(WIN, [w2], al_w1)
    start_branch(SEL, st_s, jnp.where(n_selt >= 2, MASK_NONE, MASK_ALL))
    yield pair_items(SEL, n_selt, st_s)
    yield from finish_branch(SEL, n_selt, st_s)
    yield

    def normalised(br):
        acc = acc_all[br]
        return acc[0:HEAD_DIM, :] / acc[HEAD_DIM:HEAD_DIM + 1, :]

    o_sel = normalised(SEL)
    o_win = normalised(WIN)
    for r in range(REP):
        lanes = slice(r * TQ, (r + 1) * TQ)
        oT_s[r * HEAD_DIM:(r + 1) * HEAD_DIM, :] = (oacc[:, lanes] + gates[1][r:r + 1, :] * o_sel[:, lanes]
                                                    + gates[2][r:r + 1, :] * o_win[:, lanes])
    o_ref[0, lanes_c, :] = oT_s[...].T.astype(BF16)


def _attention(qT, ks, vs, kw, vw, kc, vcT, gT):
    B, _, S = qT.shape
    G = N_KV
    nkt = S // TK
    ncmp = kc.shape[2]
    nblk = S // SLC_BLOCK
    n_sel = min(SLC_TOPK, nblk)
    W4 = REP * TQ
    res4 = lambda a: pl.BlockSpec((1, 1) + a.shape[2:], lambda b, g, i: (b, g) + (0,) * (a.ndim - 2))
    return pl.pallas_call(
        functools.partial(_attn_kernel, n_sel=n_sel),
        grid=(B, G, S // (NCHAIN * TQ)),
        in_specs=[pl.BlockSpec((1, REP * HEAD_DIM, NCHAIN * TQ), lambda b, g, i: (b, g, i)),
                  res4(ks), res4(vs), res4(kw), res4(vw), res4(kc), res4(vcT),
                  pl.BlockSpec((1, 1, GATE_ROWS, NCHAIN * TQ), lambda b, g, i: (b, g, 0, i))],
        out_specs=pl.BlockSpec((1, NCHAIN * TQ, REP * HEAD_DIM), lambda b, g, i: (b, i, g)),
        out_shape=jax.ShapeDtypeStruct((B, S, D_ATT), BF16),
        scratch_shapes=[pltpu.VMEM((2, S, KAUG), BF16),
                        pltpu.VMEM((2, nkt, VROWS, TK), BF16),
                        pltpu.VMEM((NCHAIN, 2, KAUG, W4), BF16),
                        pltpu.VMEM((ncmp, KAUG), BF16),
                        pltpu.VMEM((NCHAIN, 2, 1, W4), F32),
                        pltpu.VMEM((NCHAIN, 2, VROWS, W4), F32),
                        pltpu.VMEM((NCHAIN, 1, W4), F32),
                        pltpu.VMEM((NCHAIN, 2 * TK, W4), F32),
                        pltpu.VMEM((NCHAIN, 2 * TK, W4), BF16),
                        pltpu.VMEM((NCHAIN, HEAD_DIM, W4), F32),
                        pltpu.VMEM((NCHAIN, nblk, TQ), F32),
                        pltpu.VMEM((NCHAIN, nblk, TQ), F32),
                        pltpu.VMEM((NCHAIN, nblk, W4), F32),
                        pltpu.VMEM((4, TK, W4), F32),
                        pltpu.VMEM((2 * ncmp, W4), F32),
                        pltpu.VMEM((NCHAIN, REP * HEAD_DIM, TQ), F32),
                        pltpu.SMEM((NCHAIN, nkt + 2), I32)],
        compiler_params=pltpu.CompilerParams(dimension_semantics=("arbitrary", "arbitrary", "arbitrary"),
                                             vmem_limit_bytes=VMEM_LIMIT),
        name="attn",
    )(qT, ks, vs, kw, vw, kc, vcT, gT)


def _outmlp_kernel(x_ref, rnn_ref, att_ref, gatt_ref, wo_ref, gpost_ref, g1_ref, gpre_ref, sc2_ref, sh2_ref,
                   w1_ref, w2_ref, gpost2_ref, g2_ref, o_ref):
    tm = x_ref.shape[1]
    halves = [slice(0, tm // 2), slice(tm // 2, tm)]
    x1s, h2s = [], []
    for rows in halves:
        att_n = (_rms(att_ref[0, rows, :].astype(F32)) * gatt_ref[...]).astype(BF16)
        y = (jnp.dot(rnn_ref[0, rows, :], wo_ref[0:D_RNN, :], preferred_element_type=F32)
             + jnp.dot(att_n, wo_ref[D_RNN:, :], preferred_element_type=F32))
        x1 = x_ref[0, rows, :] + (1.0 + g1_ref[0]) * (_rms(y) * gpost_ref[...])
        x1s.append(x1)
        h2s.append((_rms(x1) * (gpre_ref[...] * (1.0 + sc2_ref[0])) + sh2_ref[0]).astype(BF16))
    fc = 1024
    for rows, x1, h2 in zip(halves, x1s, h2s):
        ff = jnp.zeros(x1.shape, F32)
        for c in range(D_FF // fc):
            hid = jnp.maximum(jnp.dot(h2, w1_ref[:, c * fc:(c + 1) * fc], preferred_element_type=F32), 0.0)
            ff = ff + jnp.dot((hid * hid).astype(BF16), w2_ref[c * fc:(c + 1) * fc, :],
                              preferred_element_type=F32)
        o_ref[0, rows, :] = x1 + (1.0 + g2_ref[0]) * (_rms(ff) * gpost2_ref[...])


def _outmlp(x, rnn_n, att, gatt, wo, gpost, g1, gpre, sc2, sh2, w1, w2, gpost2, g2):
    B, S, D = x.shape
    tm = min(TM_OUT, S)
    row = lambda n: pl.BlockSpec((1, n), lambda b, s: (0, 0))
    per_b = lambda n: pl.BlockSpec((1, 1, n), lambda b, s: (b, 0, 0))
    const = lambda a: pl.BlockSpec(a.shape, lambda b, s: (0,) * a.ndim, pipeline_mode=pl.Buffered(1))
    tok = lambda n: pl.BlockSpec((1, tm, n), lambda b, s: (b, s, 0))
    return pl.pallas_call(
        _outmlp_kernel,
        grid=(B, S // tm),
        in_specs=[tok(D), tok(D_RNN), tok(D_ATT), row(D_ATT), const(wo), row(D), per_b(D), row(D),
                  per_b(D), per_b(D), const(w1), const(w2), row(D), per_b(D)],
        out_specs=tok(D),
        out_shape=jax.ShapeDtypeStruct((B, S, D), F32),
        compiler_params=pltpu.CompilerParams(dimension_semantics=("arbitrary", "arbitrary"),
                                             vmem_limit_bytes=VMEM_LIMIT),
        name="outmlp",
    )(x, rnn_n, att, gatt, wo, gpost, g1, gpre, sc2, sh2, w1, w2, gpost2, g2)


def _block_diag(w):
    n, k, _ = w.shape
    return jnp.einsum('nij,nm->nimj', w, jnp.eye(n, dtype=w.dtype)).reshape(n * k, n * k)


def _layer(x, c, ada_w, ada_b, pre_norm_mix, w_in, conv_w, conv_b, lru_wa, lru_ba, lru_wx, lru_bx, lru_lambda,
           cmp_pos_k, cmp_w1_k, cmp_w2_k, cmp_pos_v, cmp_w1_v, cmp_w2_v, norm_rnn_out, norm_att_out, w_out,
           post_norm_mix, pre_norm_mlp, w_ff1, w_ff2, post_norm_mlp):
    B, S, D = x.shape
    G = N_KV
    row = lambda v: v.reshape(1, -1)

    mod = _ada(c, ada_w, ada_b)
    sh1, sc1, g1, sh2, sc2, g2 = [m.reshape(B, 1, D) for m in jnp.split(mod, 6, axis=-1)]

    gate_cols = [OFF_GATE + br * N_HEADS + g * REP + r for g in range(G) for br in range(3) for r in range(REP)]
    w_gate = w_in[:, jnp.asarray(gate_cols)].reshape(D, G, 3 * REP)
    w_gate = jnp.pad(w_gate, ((0, 0), (0, 0), (0, GATE_ROWS - 3 * REP))).reshape(D, G * GATE_ROWS)
    w_in_p = jnp.concatenate([w_in[:, :OFF_GATE], jnp.pad(w_gate, ((0, 0), (0, GATE_PAD - G * GATE_ROWS)))],
                             axis=1).astype(BF16)
    wa = _block_diag(lru_wa).astype(BF16)
    wx = _block_diag(lru_wx).astype(BF16)
    half = CMP_LEN // 2 * HEAD_DIM

    def w1_cat(w1):
        return jnp.concatenate([w1[:half], w1[half:]], axis=1).astype(BF16)

    def pos_rows(pos):
        return jnp.pad(pos.reshape(2, half), ((0, 14), (0, 0))).astype(BF16)

    rnn_n, qT, ks, vsT, kw, vwT, kc_in, vc_in, gT = _inproj(
        x, row(pre_norm_mix), sc1, sh1, w_in_p, conv_w, row(conv_b), wa, row(lru_ba), wx, row(lru_bx),
        row(lru_lambda), row(norm_rnn_out))
    kc, vc = _compress(kc_in, vc_in, w1_cat(cmp_w1_k), cmp_w2_k.astype(BF16), pos_rows(cmp_pos_k),
                       w1_cat(cmp_w1_v), cmp_w2_v.astype(BF16), pos_rows(cmp_pos_v))
    att = _attention(qT, ks, vsT, kw, vwT, kc, vc.transpose(0, 1, 3, 2), gT)

    return _outmlp(x, rnn_n, att, row(norm_att_out), w_out.astype(BF16), row(post_norm_mix), g1,
                   row(pre_norm_mlp), sc2, sh2, w_ff1.astype(BF16), w_ff2.astype(BF16), row(post_norm_mlp), g2)


def kernel(x, c, ada_w, ada_b, pre_norm_mix, w_in, conv_w, conv_b, lru_wa, lru_ba, lru_wx, lru_bx, lru_lambda,
           cmp_pos_k, cmp_w1_k, cmp_w2_k, cmp_pos_v, cmp_w1_v, cmp_w2_v, norm_rnn_out, norm_att_out, w_out,
           post_norm_mix, pre_norm_mlp, w_ff1, w_ff2, post_norm_mlp):
    for l in range(ada_w.shape[0]):
        x = _layer(x, c, ada_w[l], ada_b[l], pre_norm_mix[l], w_in[l], conv_w[l], conv_b[l], lru_wa[l], lru_ba[l],
                   lru_wx[l], lru_bx[l], lru_lambda[l], cmp_pos_k[l], cmp_w1_k[l], cmp_w2_k[l], cmp_pos_v[l],
                   cmp_w1_v[l], cmp_w2_v[l], norm_rnn_out[l], norm_att_out[l], w_out[l], post_norm_mix[l],
                   pre_norm_mlp[l], w_ff1[l], w_ff2[l], post_norm_mlp[l])
    return x
```

```python
import functools

import jax
import jax.numpy as jnp
from jax import lax
from jax.experimental import pallas as pl
from jax.experimental.pallas import tpu as pltpu

F32 = jnp.float32
BF16 = jnp.bfloat16
I32 = jnp.int32

D_MODEL = 1024
D_RNN = 512
RNN_BLOCKS = 8
CONV_WIDTH = 4
LRU_C = 8.0
N_HEADS = 8
HEAD_DIM = 64
N_KV = 2
REP = N_HEADS // N_KV
D_ATT = N_HEADS * HEAD_DIM
CMP_LEN = 32
CMP_STRIDE = 16
CMP_HIDDEN = 256
SLC_BLOCK = 64
SLC_TOPK = 16
WINDOW = 512
D_FF = 4 * D_MODEL
EPS = 1e-6
NEG = -1e30
FORCED = 1e4
LOG2E = 1.4426950408889634

KV_COLS = 6 * N_KV * HEAD_DIM
N_GATE = 3 * N_HEADS
GATE_PAD = 128
GATE_ROWS = 16
OFF_Q = 2 * D_RNN
OFF_KV = OFF_Q + D_ATT
OFF_GATE = OFF_KV + KV_COLS
D_IN_PAD = OFF_GATE + GATE_PAD

TM_IN = 1024
TM_OUT = 1024
TQ = 256
TK = 256
NCHAIN = 4
KAUG = 256
VROWS = 80
VMEM_LIMIT = 56 * 1024 * 1024

SEL, WIN = 0, 1
MASK_NONE, MASK_CAUSAL, MASK_WINDOW_LOW, MASK_ALL = 0, 1, 2, 3
POS_COL = 2 * HEAD_DIM


def _gelu_tanh(x):
    return 0.5 * x * (1.0 + jnp.tanh(0.7978845608028654 * (x + 0.044715 * (x * x * x))))


def _sigmoid(x):
    return 0.5 * jnp.tanh(0.5 * x) + 0.5


def _rms(x):
    return x * lax.rsqrt(jnp.mean(x * x, axis=-1, keepdims=True) + EPS)


def _ada_kernel(c_ref, w_ref, b_ref, o_ref):
    c = c_ref[...]
    a = c * jax.nn.sigmoid(c)
    o_ref[...] = jnp.dot(a, w_ref[...], preferred_element_type=F32,
                         precision=lax.Precision.HIGHEST) + b_ref[...]


def _ada(c, w, b):
    B, D = c.shape
    N = w.shape[1]
    tn = 1024
    return pl.pallas_call(
        _ada_kernel,
        grid=(N // tn,),
        in_specs=[pl.BlockSpec((B, D), lambda j: (0, 0)),
                  pl.BlockSpec((D, tn), lambda j: (0, j)),
                  pl.BlockSpec((1, tn), lambda j: (0, j))],
        out_specs=pl.BlockSpec((B, tn), lambda j: (0, j)),
        out_shape=jax.ShapeDtypeStruct((B, N), F32),
        name="ada",
    )(c, w, b.reshape(1, N))


def _inproj_kernel(x_ref, gain_ref, sc_ref, sh_ref, w_ref, cw_ref, cb_ref, wa_ref, ba_ref, wx_ref, bx_ref,
                   lam_ref, grnn_ref,
                   rnn_ref, qT_ref, ks_ref, vs_ref, kw_ref, vw_ref, kc_ref, vc_ref, g_ref,
                   xbuf, hcar, a_s, u_s, h_s):
    tm = x_ref.shape[1]
    hd = HEAD_DIM

    @pl.when(pl.program_id(1) == 0)
    def _():
        xbuf[0:8, :] = jnp.zeros((8, D_RNN), F32)
        hcar[...] = jnp.zeros((1, D_RNN), F32)

    x = x_ref[0]
    h = _rms(x) * (gain_ref[...] * (1.0 + sc_ref[0])) + sh_ref[0]
    hb = h.astype(BF16)

    xr = jnp.dot(hb, w_ref[:, D_RNN:OFF_Q], preferred_element_type=F32)
    xbuf[8:8 + tm, :] = xr
    y = (cw_ref[3:4, :] * xr + cw_ref[2:3, :] * xbuf[7:7 + tm, :]
         + cw_ref[1:2, :] * xbuf[6:6 + tm, :] + cw_ref[0:1, :] * xbuf[5:5 + tm, :]) + cb_ref[...]
    xbuf[0:8, :] = xbuf[tm:tm + 8, :]

    yb = y.astype(BF16)
    r = _sigmoid(jnp.dot(yb, wa_ref[...], preferred_element_type=F32) + ba_ref[...])
    i = _sigmoid(jnp.dot(yb, wx_ref[...], preferred_element_type=F32) + bx_ref[...])
    nl = -lam_ref[...]
    softplus = jnp.maximum(nl, 0.0) + jnp.log(1.0 + jnp.exp(-jnp.abs(nl)))
    a = jnp.exp((-LRU_C) * r * softplus)
    a_s[...] = a
    u_s[...] = jnp.sqrt(1.0 - a * a) * (i * y)

    qT_ref[0] = (jnp.dot(hb, w_ref[:, OFF_Q:OFF_KV], preferred_element_type=F32)
                 * (HEAD_DIM ** -0.5 * LOG2E)).T.astype(BF16)
    kv = jnp.dot(hb, w_ref[:, OFF_KV:OFF_GATE], preferred_element_type=F32)
    width = N_KV * hd
    vsT = kv[:, 3 * width:4 * width].T
    vwT = kv[:, 5 * width:6 * width].T
    for gi in range(N_KV):
        kc_ref[0, gi] = kv[:, gi * hd:(gi + 1) * hd]
        vc_ref[0, gi] = kv[:, width + gi * hd:width + (gi + 1) * hd]
        ks_ref[0, gi] = kv[:, 2 * width + gi * hd:2 * width + (gi + 1) * hd].astype(BF16)
        kw_ref[0, gi] = kv[:, 4 * width + gi * hd:4 * width + (gi + 1) * hd].astype(BF16)
        for j in range(tm // TK):
            vs_ref[0, gi, j] = vsT[gi * hd:(gi + 1) * hd, j * TK:(j + 1) * TK].astype(BF16)
            vw_ref[0, gi, j] = vwT[gi * hd:(gi + 1) * hd, j * TK:(j + 1) * TK].astype(BF16)
    gates = _sigmoid(jnp.dot(hb, w_ref[:, OFF_GATE:D_IN_PAD], preferred_element_type=F32)).T
    for gi in range(N_KV):
        g_ref[0, gi] = gates[gi * GATE_ROWS:(gi + 1) * GATE_ROWS, :]
    g = jnp.dot(hb, w_ref[:, 0:D_RNN], preferred_element_type=F32)


    rows = lax.broadcasted_iota(I32, (8, D_RNN), 0)
    hprev = hcar[...]
    for gi in range(tm // 8):
        ag = a_s[gi * 8:gi * 8 + 8, :]
        ug = u_s[gi * 8:gi * 8 + 8, :]
        for k in (1, 2, 4):
            a_sh = jnp.where(rows >= k, pltpu.roll(ag, k, 0), 1.0)
            u_sh = jnp.where(rows >= k, pltpu.roll(ug, k, 0), 0.0)
            ug = ag * u_sh + ug
            ag = ag * a_sh
        hg = ag * hprev + ug
        h_s[gi * 8:gi * 8 + 8, :] = hg
        hprev = hg[7:8, :]
    hcar[...] = hprev

    rnn = _gelu_tanh(g) * h_s[...]
    rnn_ref[0] = (_rms(rnn) * grnn_ref[...]).astype(BF16)


def _inproj(x, gain, sc, sh, w_in, conv_w, conv_b, wa, ba, wx, bx, lam, grnn):
    B, S, D = x.shape
    tm = min(TM_IN, S)
    row = lambda n: pl.BlockSpec((1, n), lambda b, s: (0, 0))
    per_b = lambda n: pl.BlockSpec((1, 1, n), lambda b, s: (b, 0, 0))
    full = lambda a: pl.BlockSpec(a.shape, lambda b, s: (0,) * a.ndim)
    tok = lambda n: pl.BlockSpec((1, tm, n), lambda b, s: (b, s, 0))
    G = N_KV
    grp = pl.BlockSpec((1, G, tm, HEAD_DIM), lambda b, s: (b, 0, s, 0))
    grpT = pl.BlockSpec((1, G, tm // TK, HEAD_DIM, TK), lambda b, s: (b, 0, s, 0, 0))
    return pl.pallas_call(
        _inproj_kernel,
        grid=(B, S // tm),
        in_specs=[tok(D), row(D), per_b(D), per_b(D), full(w_in), full(conv_w), row(D_RNN),
                  full(wa), row(D_RNN), full(wx), row(D_RNN), row(D_RNN), row(D_RNN)],
        out_specs=[tok(D_RNN),
                   pl.BlockSpec((1, D_ATT, tm), lambda b, s: (b, 0, s)),
                   grp, grpT, grp, grpT, grp, grp,
                   pl.BlockSpec((1, G, GATE_ROWS, tm), lambda b, s: (b, 0, 0, s))],
        out_shape=[jax.ShapeDtypeStruct((B, S, D_RNN), BF16),
                   jax.ShapeDtypeStruct((B, D_ATT, S), BF16),
                   jax.ShapeDtypeStruct((B, G, S, HEAD_DIM), BF16),
                   jax.ShapeDtypeStruct((B, G, S // TK, HEAD_DIM, TK), BF16),
                   jax.ShapeDtypeStruct((B, G, S, HEAD_DIM), BF16),
                   jax.ShapeDtypeStruct((B, G, S // TK, HEAD_DIM, TK), BF16),
                   jax.ShapeDtypeStruct((B, G, S, HEAD_DIM), F32),
                   jax.ShapeDtypeStruct((B, G, S, HEAD_DIM), F32),
                   jax.ShapeDtypeStruct((B, G, GATE_ROWS, S), F32)],
        scratch_shapes=[pltpu.VMEM((tm + 8, D_RNN), F32), pltpu.VMEM((1, D_RNN), F32),
                        pltpu.VMEM((tm, D_RNN), F32), pltpu.VMEM((tm, D_RNN), F32),
                        pltpu.VMEM((tm, D_RNN), F32)],
        compiler_params=pltpu.CompilerParams(dimension_semantics=("arbitrary", "arbitrary"),
                                             vmem_limit_bytes=VMEM_LIMIT),
        name="inproj",
    )(x, gain, sc, sh, w_in, conv_w, conv_b, wa, ba, wx, bx, lam, grnn)


def _compress_kernel(rk_ref, rv_ref, w1k_ref, w2k_ref, pk_ref, w1v_ref, w2v_ref, pv_ref, kc_ref, vc_ref):
    def one(r_ref, w1_ref, w2_ref, pos_ref, o_ref):
        n = r_ref.shape[2] // CMP_STRIDE
        p = jnp.zeros((n, 2 * CMP_HIDDEN), F32)
        for l in range(CMP_STRIDE):
            rows = r_ref[0, 0, pl.ds(l, n, stride=CMP_STRIDE), :].astype(BF16)
            p = p + jnp.dot(rows, w1_ref[l * HEAD_DIM:(l + 1) * HEAD_DIM, :], preferred_element_type=F32)
        posb = jnp.dot(pos_ref[...], w1_ref[...], preferred_element_type=F32)
        bias = posb[0:1, 0:CMP_HIDDEN] + posb[1:2, CMP_HIDDEN:]
        pre = p[:, 0:CMP_HIDDEN] + pltpu.roll(p[:, CMP_HIDDEN:], n - 1, 0) + bias
        hid = _gelu_tanh(pre).astype(BF16)
        o_ref[0, 0] = jnp.dot(hid, w2_ref[...], preferred_element_type=F32).astype(BF16)

    one(rk_ref, w1k_ref, w2k_ref, pk_ref, kc_ref)
    one(rv_ref, w1v_ref, w2v_ref, pv_ref, vc_ref)


def _compress(rk, rv, w1k, w2k, posk, w1v, w2v, posv):
    B, G, S, width = rk.shape
    n = S // CMP_STRIDE
    blk = pl.BlockSpec((1, 1, S, width), lambda b, g: (b, g, 0, 0))
    full = lambda a: pl.BlockSpec(a.shape, lambda b, g: (0,) * a.ndim)
    out = pl.BlockSpec((1, 1, n, HEAD_DIM), lambda b, g: (b, g, 0, 0))
    return pl.pallas_call(
        _compress_kernel,
        grid=(B, G),
        in_specs=[blk, blk, full(w1k), full(w2k), full(posk), full(w1v), full(w2v), full(posv)],
        out_specs=[out, out],
        out_shape=[jax.ShapeDtypeStruct((B, G, n, HEAD_DIM), BF16)] * 2,
        compiler_params=pltpu.CompilerParams(vmem_limit_bytes=VMEM_LIMIT),
        name="compress",
    )(rk, rv, w1k, w2k, posk, w1v, w2v, posv)


def _attn_kernel(qT_ref, ks_ref, vs_ref, kw_ref, vw_ref, kc_ref, vcT_ref, g_ref,
                 o_ref,
                 kall, vall, qall, kca, m_all, acc_all, al_s, sbuf, pbuf, oacc, val_s, rank_s, selbias_s,
                 bias_tbl, cmask,
                 oT_s, st_s, *, n_sel):
    S = ks_ref.shape[2]
    nkt = S // TK
    ncmp = kc_ref.shape[2]
    b = pl.program_id(0)
    g = pl.program_id(1)
    step = pl.program_id(2)
    W4 = REP * TQ
    cmp_per_tile = TQ // CMP_STRIDE

    @pl.when((b == 0) & (g == 0) & (step == 0))
    def _():
        col = lax.broadcasted_iota(I32, (TK, KAUG), 1)
        is_blk = (col == POS_COL) | (col == POS_COL + 2)
        is_off = (col == POS_COL + 1) | (col == POS_COL + 3)
        ones_row = (lax.broadcasted_iota(I32, (VROWS - HEAD_DIM, TK), 0) == 0).astype(BF16)
        for kt in range(nkt):
            key = kt * TK + lax.broadcasted_iota(I32, (TK, KAUG), 0)
            pos = jnp.where(is_blk, (key // SLC_BLOCK).astype(F32),
                            jnp.where(is_off, (key % SLC_BLOCK).astype(F32), 0.0))
            onehot = (col - HEAD_DIM == key // SLC_BLOCK).astype(F32)
            kall[SEL, kt * TK:(kt + 1) * TK, :] = (pos + onehot).astype(BF16)
            kall[WIN, kt * TK:(kt + 1) * TK, :] = pos.astype(BF16)
            vall[SEL, kt, HEAD_DIM:VROWS, :] = ones_row
            vall[WIN, kt, HEAD_DIM:VROWS, :] = ones_row
        ko = lax.broadcasted_iota(I32, (TK, W4), 0)
        to = lax.broadcasted_iota(I32, (TK, W4), 1) % TQ
        bias_tbl[MASK_NONE] = jnp.zeros((TK, W4), F32)
        bias_tbl[MASK_CAUSAL] = jnp.where(ko <= to, 0.0, NEG)
        bias_tbl[MASK_WINDOW_LOW] = jnp.where(ko > to, 0.0, NEG)
        bias_tbl[MASK_ALL] = jnp.full((TK, W4), NEG, F32)
        u = lax.broadcasted_iota(I32, (2 * ncmp, W4), 0)
        tc = lax.broadcasted_iota(I32, (2 * ncmp, W4), 1) % TQ
        cmask[...] = jnp.where(u <= ncmp + ((tc + 1) // CMP_STRIDE) - 2, 0.0, NEG)
        cc = lax.broadcasted_iota(I32, (ncmp, KAUG), 0)
        colc = lax.broadcasted_iota(I32, (ncmp, KAUG), 1)
        kca[...] = jnp.where((colc == POS_COL) | (colc == POS_COL + 2), (cc // 4).astype(F32),
                             jnp.where((colc == POS_COL + 1) | (colc == POS_COL + 3),
                                       (CMP_STRIDE * (cc % 4)).astype(F32), 0.0)).astype(BF16)
        qall[...] = jnp.zeros(qall.shape, BF16)

    lane = lax.broadcasted_iota(I32, (1, W4), 1)
    head = g * REP + lane // TQ
    slope = lax.bitcast_convert_type((126 - head) << 23, F32)

    @pl.when(step == 0)
    def _():
        for kt in range(nkt):
            kall[SEL, kt * TK:(kt + 1) * TK, 0:HEAD_DIM] = ks_ref[0, 0, kt * TK:(kt + 1) * TK, :]
            kall[WIN, kt * TK:(kt + 1) * TK, 0:HEAD_DIM] = kw_ref[0, 0, kt * TK:(kt + 1) * TK, :]
            vall[SEL, kt, 0:HEAD_DIM, :] = vs_ref[0, 0, kt]
            vall[WIN, kt, 0:HEAD_DIM, :] = vw_ref[0, 0, kt]
        kca[:, 0:HEAD_DIM] = kc_ref[0, 0]
        c_hi = (slope * LOG2E).astype(BF16).astype(F32)
        c_lo = slope * LOG2E - c_hi
        r16 = lax.broadcasted_iota(I32, (16, W4), 0)
        alibi = jnp.where(r16 == 0, c_hi * SLC_BLOCK, jnp.where(r16 == 1, c_hi, jnp.where(
            r16 == 2, c_lo * SLC_BLOCK, jnp.where(r16 == 3, c_lo, 0.0)))).astype(BF16)
        for c in range(NCHAIN):
            qall[c, SEL, POS_COL:POS_COL + 16, :] = alibi
            qall[c, WIN, POS_COL:POS_COL + 16, :] = alibi

    chains = [_attn_chain(c, NCHAIN * step + c, qT_ref, kc_ref, vcT_ref, g_ref, o_ref, kall, vall, qall.at[c], kca,
                          m_all.at[c], acc_all.at[c], al_s.at[c], sbuf.at[c], pbuf.at[c], oacc.at[c], val_s.at[c],
                          rank_s.at[c], selbias_s.at[c], bias_tbl, cmask, oT_s.at[c], st_s.at[c], n_sel)
              for c in range(NCHAIN)]
    live = list(chains)
    while live:
        handed = [next(ch, _DONE) for ch in live]
        live = [ch for ch, r in zip(live, handed) if r is not _DONE]
        loops = [r for r in handed if r is not _DONE and r is not None]
        if loops:
            joint = functools.reduce(jnp.minimum, [npair for npair, _, _ in loops])

            def trip(j, group):
                als = [head(j) for _, head, _ in group]
                for (_, _, tail), al in zip(group, als):
                    tail(j, al)

            lax.fori_loop(1, 1 + joint, lambda j, carry: (trip(j, loops), carry)[1], 0)
            for item in loops:
                lax.fori_loop(1 + joint, 1 + item[0], lambda j, carry, item=item: (trip(j, [item]), carry)[1], 0)


_DONE = object()


def _attn_chain(c, qi, qT_ref, kc_ref, vcT_ref, g_ref, o_ref, kall, vall, qall, kca, m_all, acc_all, al_s, sbuf,
                pbuf, oacc, val_s, rank_s, selbias_s, bias_tbl, cmask, oT_s, st_s, n_sel):
    S = kall.shape[1]
    ncmp = kc_ref.shape[2]
    nblk = S // SLC_BLOCK
    cmp_per_tile = TQ // CMP_STRIDE
    q0 = qi * TQ
    W4 = REP * TQ
    lane = lax.broadcasted_iota(I32, (1, W4), 1)
    lanes_c = slice(c * TQ, (c + 1) * TQ)

    q = qT_ref[0, :, lanes_c]
    for r in range(REP):
        qr = q[r * HEAD_DIM:(r + 1) * HEAD_DIM, :]
        qall[SEL, 0:HEAD_DIM, r * TQ:(r + 1) * TQ] = qr
        qall[WIN, 0:HEAD_DIM, r * TQ:(r + 1) * TQ] = qr

    gall = g_ref[0, 0, :, lanes_c]
    gates = [gall[br * REP:(br + 1) * REP, :] for br in range(3)]

    def stage_scores(br, tiles_masks):
        col_max = None
        for h, (tile, mask) in enumerate(tiles_masks):
            r0 = pl.multiple_of(tile * TK, TK)
            s = jnp.dot(kall[br, pl.ds(r0, TK), :], qall[br], preferred_element_type=F32)
            if mask is not None:
                s = s + bias_tbl[mask]
            sbuf[h * TK:(h + 1) * TK, :] = s
            c = jnp.max(s, axis=0, keepdims=True)
            col_max = c if col_max is None else jnp.maximum(col_max, c)
        m_old = m_all[br]
        m_new = jnp.maximum(m_old, col_max)
        m_all[br] = m_new
        return jnp.exp2(m_old - m_new)

    def stage_probs(br, ntile):
        pbuf[0:ntile * TK, :] = jnp.exp2(sbuf[0:ntile * TK, :] - m_all[br]).astype(BF16)

    def stage_values(br, tiles, al_row):
        acc = acc_all[br] * al_row
        for h, tile in enumerate(tiles):
            acc = acc + jnp.dot(vall[br, tile], pbuf[h * TK:(h + 1) * TK, :], preferred_element_type=F32)
        acc_all[br] = acc

    def start_branch(br, tl, mask1):
        al_s[...] = stage_scores(br, [(tl[0], MASK_CAUSAL), (tl[1], mask1)])

    def pair_items(br, n, tl):
        npair = jnp.maximum(n - 2, 0) // 2

        def head(j):
            al_prev = al_s[...]
            stage_probs(br, 2)
            al_s[...] = stage_scores(br, [(tl[2 * j], None), (tl[2 * j + 1], None)])
            return al_prev

        def tail(j, al_prev):
            stage_values(br, [tl[2 * j - 2], tl[2 * j - 1]], al_prev)

        return npair, head, tail

    def finish_branch(br, n, tl):
        rest = jnp.maximum(n - 2, 0)
        last = 2 * (rest // 2)
        odd = rest % 2 == 1
        t_single = tl[jnp.where(odd, n - 1, 0)]
        al_prev = al_s[...]
        stage_probs(br, 2)
        al_k = stage_scores(br, [(t_single, jnp.where(odd, MASK_NONE, MASK_ALL))])
        yield
        stage_values(br, [tl[last], tl[last + 1]], al_prev)
        stage_probs(br, 1)
        yield
        stage_values(br, [t_single], al_k)

    m_all[...] = jnp.full(m_all.shape, NEG, F32)
    acc_all[...] = jnp.zeros(acc_all.shape, F32)
    w1 = jnp.maximum(qi - 2, 0)
    w2 = jnp.maximum(qi - 1, 0)
    al_w0 = stage_scores(WIN, [(qi, MASK_CAUSAL),
                               (w1, jnp.where(qi >= 2, MASK_WINDOW_LOW, jnp.where(qi >= 1, MASK_NONE, MASK_ALL)))])
    yield

    c0 = pl.multiple_of(ncmp - cmp_per_tile * qi, cmp_per_tile)
    sc = jnp.dot(kca[...], qall[WIN], preferred_element_type=F32) + cmask[pl.ds(c0, ncmp), :]
    yield
    e = jnp.exp2(sc - jnp.max(sc, axis=0, keepdims=True))
    tq = q0 + lane % TQ
    p = e * ((1.0 / jnp.sum(e, axis=0, keepdims=True)) * (tq >= CMP_LEN - 1).astype(F32))
    ocT = jnp.dot(vcT_ref[0, 0], p.astype(BF16), preferred_element_type=F32)
    for r in range(REP):
        oacc[:, r * TQ:(r + 1) * TQ] = gates[0][r:r + 1, :] * ocT[:, r * TQ:(r + 1) * TQ]
    yield

    stage_probs(WIN, 2)
    al_w1 = stage_scores(WIN, [(w2, jnp.where(qi >= 2, MASK_NONE, MASK_ALL))])
    stage_values(WIN, [qi, w1], al_w0)
    yield

    psum = p[:, 0:TQ]
    for r in range(1, REP):
        psum = psum + p[:, r * TQ:(r + 1) * TQ]
    jj = lax.broadcasted_iota(I32, (nblk, ncmp), 0)
    cc = lax.broadcasted_iota(I32, (nblk, ncmp), 1)
    ovT = ((CMP_STRIDE * cc < SLC_BLOCK * jj + SLC_BLOCK)
           & (CMP_STRIDE * cc + CMP_LEN > SLC_BLOCK * jj)).astype(F32)
    impT = jnp.dot(ovT, psum, preferred_element_type=F32, precision=lax.Precision.HIGHEST)

    j_i = lax.broadcasted_iota(I32, (nblk, TQ), 0)
    t1 = q0 + lax.broadcasted_iota(I32, (nblk, TQ), 1)
    cur = t1 // SLC_BLOCK
    forced = (j_i == 0) | (j_i == cur) | (j_i == cur - 1)
    visible = SLC_BLOCK * j_i <= t1
    val_s[...] = jnp.where(forced, FORCED, jnp.where(visible, impT, NEG))
    yield
    ngrp = nblk // 8
    rank_s[...] = jnp.zeros(rank_s.shape, F32)
    j8 = lax.broadcasted_iota(I32, (8, TQ), 0)
    for ib in range(ngrp):
        @pl.when(8 * ib * SLC_BLOCK < q0 + TQ)
        def _(ib=ib):
            vals = [val_s[8 * jb:8 * jb + 8, :] for jb in range(ngrp)]
            ranks = [rank_s[8 * jb:8 * jb + 8, :] for jb in range(ngrp)]
            for i in range(8 * ib, 8 * ib + 8):
                row = jnp.broadcast_to(val_s[i:i + 1, :], (8, TQ))
                for jb in range(ngrp):
                    if jb > ib:
                        hit = row >= vals[jb]
                    elif jb < ib:
                        hit = row > vals[jb]
                    else:
                        hit = jnp.where(j8 > i - 8 * jb, jnp.where(row >= vals[jb], 1.0, 0.0),
                                        jnp.where(row > vals[jb], 1.0, 0.0)) > 0.5
                    ranks[jb] = ranks[jb] + jnp.where(hit, 1.0, 0.0)
            for jb in range(ngrp):
                rank_s[8 * jb:8 * jb + 8, :] = ranks[jb]
    yield
    ranks = [rank_s[8 * jb:8 * jb + 8, :] for jb in range(ngrp)]
    blocks_per_tile = TK // SLC_BLOCK
    st_s[0] = qi
    n_selt = jnp.int32(1)
    for jb in range(ngrp):
        chosen = ranks[jb] < n_sel
        selb = jnp.where(chosen, 0.0, NEG)
        for r in range(REP):
            selbias_s[8 * jb:8 * jb + 8, r * TQ:(r + 1) * TQ] = selb
        any_q = jnp.max(jnp.where(chosen, 1.0, 0.0), axis=1, keepdims=True)
        for hh in range(8 // blocks_per_tile):
            kt = (8 * jb) // blocks_per_tile + hh
            hit = jnp.max(any_q[hh * blocks_per_tile:(hh + 1) * blocks_per_tile, :]) > 0.5
            st_s[n_selt] = kt
            n_selt = n_selt + jnp.where(hit & (kt < qi), 1, 0)
    qall[SEL, HEAD_DIM:HEAD_DIM + nblk, :] = selbias_s[...].astype(BF16)
    yield

    stage_probs(WIN, 1)
    stage_values(WIN, [w2], al_w1)
    start_branch(SEL, st_s, jnp.where(n_selt >= 2, MASK_NONE, MASK_ALL))
    yield pair_items(SEL, n_selt, st_s)
    yield from finish_branch(SEL, n_selt, st_s)
    yield

    def normalised(br):
        acc = acc_all[br]
        return acc[0:HEAD_DIM, :] / acc[HEAD_DIM:HEAD_DIM + 1, :]

    o_sel = normalised(SEL)
    o_win = normalised(WIN)
    for r in range(REP):
        lanes = slice(r * TQ, (r + 1) * TQ)
        oT_s[r * HEAD_DIM:(r + 1) * HEAD_DIM, :] = (oacc[:, lanes] + gates[1][r:r + 1, :] * o_sel[:, lanes]
                                                    + gates[2][r:r + 1, :] * o_win[:, lanes])
    o_ref[0, lanes_c, :] = oT_s[...].T.astype(BF16)


def _attention(qT, ks, vs, kw, vw, kc, vcT, gT):
    B, _, S = qT.shape
    G = N_KV
    nkt = S // TK
    ncmp = kc.shape[2]
    nblk = S // SLC_BLOCK
    n_sel = min(SLC_TOPK, nblk)
    W4 = REP * TQ
    res4 = lambda a: pl.BlockSpec((1, 1) + a.shape[2:], lambda b, g, i: (b, g) + (0,) * (a.ndim - 2))
    return pl.pallas_call(
        functools.partial(_attn_kernel, n_sel=n_sel),
        grid=(B, G, S // (NCHAIN * TQ)),
        in_specs=[pl.BlockSpec((1, REP * HEAD_DIM, NCHAIN * TQ), lambda b, g, i: (b, g, i)),
                  res4(ks), res4(vs), res4(kw), res4(vw), res4(kc), res4(vcT),
                  pl.BlockSpec((1, 1, GATE_ROWS, NCHAIN * TQ), lambda b, g, i: (b, g, 0, i))],
        out_specs=pl.BlockSpec((1, NCHAIN * TQ, REP * HEAD_DIM), lambda b, g, i: (b, i, g)),
        out_shape=jax.ShapeDtypeStruct((B, S, D_ATT), BF16),
        scratch_shapes=[pltpu.VMEM((2, S, KAUG), BF16),
                        pltpu.VMEM((2, nkt, VROWS, TK), BF16),
                        pltpu.VMEM((NCHAIN, 2, KAUG, W4), BF16),
                        pltpu.VMEM((ncmp, KAUG), BF16),
                        pltpu.VMEM((NCHAIN, 2, 1, W4), F32),
                        pltpu.VMEM((NCHAIN, 2, VROWS, W4), F32),
                        pltpu.VMEM((NCHAIN, 1, W4), F32),
                        pltpu.VMEM((NCHAIN, 2 * TK, W4), F32),
                        pltpu.VMEM((NCHAIN, 2 * TK, W4), BF16),
                        pltpu.VMEM((NCHAIN, HEAD_DIM, W4), F32),
                        pltpu.VMEM((NCHAIN, nblk, TQ), F32),
                        pltpu.VMEM((NCHAIN, nblk, TQ), F32),
                        pltpu.VMEM((NCHAIN, nblk, W4), F32),
                        pltpu.VMEM((4, TK, W4), F32),
                        pltpu.VMEM((2 * ncmp, W4), F32),
                        pltpu.VMEM((NCHAIN, REP * HEAD_DIM, TQ), F32),
                        pltpu.SMEM((NCHAIN, nkt + 2), I32)],
        compiler_params=pltpu.CompilerParams(dimension_semantics=("arbitrary", "arbitrary", "arbitrary"),
                                             vmem_limit_bytes=VMEM_LIMIT),
        name="attn",
    )(qT, ks, vs, kw, vw, kc, vcT, gT)


def _outmlp_kernel(x_ref, rnn_ref, att_ref, gatt_ref, wo_ref, gpost_ref, g1_ref, gpre_ref, sc2_ref, sh2_ref,
                   w1_ref, w2_ref, gpost2_ref, g2_ref, o_ref):
    tm = x_ref.shape[1]
    halves = [slice(0, tm // 2), slice(tm // 2, tm)]
    x1s, h2s = [], []
    for rows in halves:
        att_n = (_rms(att_ref[0, rows, :].astype(F32)) * gatt_ref[...]).astype(BF16)
        y = (jnp.dot(rnn_ref[0, rows, :], wo_ref[0:D_RNN, :], preferred_element_type=F32)
             + jnp.dot(att_n, wo_ref[D_RNN:, :], preferred_element_type=F32))
        x1 = x_ref[0, rows, :] + (1.0 + g1_ref[0]) * (_rms(y) * gpost_ref[...])
        x1s.append(x1)
        h2s.append((_rms(x1) * (gpre_ref[...] * (1.0 + sc2_ref[0])) + sh2_ref[0]).astype(BF16))
    fc = 1024
    for rows, x1, h2 in zip(halves, x1s, h2s):
        ff = jnp.zeros(x1.shape, F32)
        for c in range(D_FF // fc):
            hid = jnp.maximum(jnp.dot(h2, w1_ref[:, c * fc:(c + 1) * fc], preferred_element_type=F32), 0.0)
            ff = ff + jnp.dot((hid * hid).astype(BF16), w2_ref[c * fc:(c + 1) * fc, :],
                              preferred_element_type=F32)
        o_ref[0, rows, :] = x1 + (1.0 + g2_ref[0]) * (_rms(ff) * gpost2_ref[...])


def _outmlp(x, rnn_n, att, gatt, wo, gpost, g1, gpre, sc2, sh2, w1, w2, gpost2, g2):
    B, S, D = x.shape
    tm = min(TM_OUT, S)
    row = lambda n: pl.BlockSpec((1, n), lambda b, s: (0, 0))
    per_b = lambda n: pl.BlockSpec((1, 1, n), lambda b, s: (b, 0, 0))
    const = lambda a: pl.BlockSpec(a.shape, lambda b, s: (0,) * a.ndim, pipeline_mode=pl.Buffered(1))
    tok = lambda n: pl.BlockSpec((1, tm, n), lambda b, s: (b, s, 0))
    return pl.pallas_call(
        _outmlp_kernel,
        grid=(B, S // tm),
        in_specs=[tok(D), tok(D_RNN), tok(D_ATT), row(D_ATT), const(wo), row(D), per_b(D), row(D),
                  per_b(D), per_b(D), const(w1), const(w2), row(D), per_b(D)],
        out_specs=tok(D),
        out_shape=jax.ShapeDtypeStruct((B, S, D), F32),
        compiler_params=pltpu.CompilerParams(dimension_semantics=("arbitrary", "arbitrary"),
                                             vmem_limit_bytes=VMEM_LIMIT),
        name="outmlp",
    )(x, rnn_n, att, gatt, wo, gpost, g1, gpre, sc2, sh2, w1, w2, gpost2, g2)


def _block_diag(w):
    n, k, _ = w.shape
    return jnp.einsum('nij,nm->nimj', w, jnp.eye(n, dtype=w.dtype)).reshape(n * k, n * k)


def _layer(x, c, ada_w, ada_b, pre_norm_mix, w_in, conv_w, conv_b, lru_wa, lru_ba, lru_wx, lru_bx, lru_lambda,
           cmp_pos_k, cmp_w1_k, cmp_w2_k, cmp_pos_v, cmp_w1_v, cmp_w2_v, norm_rnn_out, norm_att_out, w_out,
           post_norm_mix, pre_norm_mlp, w_ff1, w_ff2, post_norm_mlp):
    B, S, D = x.shape
    G = N_KV
    row = lambda v: v.reshape(1, -1)

    mod = _ada(c, ada_w, ada_b)
    sh1, sc1, g1, sh2, sc2, g2 = [m.reshape(B, 1, D) for m in jnp.split(mod, 6, axis=-1)]

    gate_cols = [OFF_GATE + br * N_HEADS + g * REP + r for g in range(G) for br in range(3) for r in range(REP)]
    w_gate = w_in[:, jnp.asarray(gate_cols)].reshape(D, G, 3 * REP)
    w_gate = jnp.pad(w_gate, ((0, 0), (0, 0), (0, GATE_ROWS - 3 * REP))).reshape(D, G * GATE_ROWS)
    w_in_p = jnp.concatenate([w_in[:, :OFF_GATE], jnp.pad(w_gate, ((0, 0), (0, GATE_PAD - G * GATE_ROWS)))],
                             axis=1).astype(BF16)
    wa = _block_diag(lru_wa).astype(BF16)
    wx = _block_diag(lru_wx).astype(BF16)
    half = CMP_LEN // 2 * HEAD_DIM

    def w1_cat(w1):
        return jnp.concatenate([w1[:half], w1[half:]], axis=1).astype(BF16)

    def pos_rows(pos):
        return jnp.pad(pos.reshape(2, half), ((0, 14), (0, 0))).astype(BF16)

    rnn_n, qT, ks, vsT, kw, vwT, kc_in, vc_in, gT = _inproj(
        x, row(pre_norm_mix), sc1, sh1, w_in_p, conv_w, row(conv_b), wa, row(lru_ba), wx, row(lru_bx),
        row(lru_lambda), row(norm_rnn_out))
    kc, vc = _compress(kc_in, vc_in, w1_cat(cmp_w1_k), cmp_w2_k.astype(BF16), pos_rows(cmp_pos_k),
                       w1_cat(cmp_w1_v), cmp_w2_v.astype(BF16), pos_rows(cmp_pos_v))
    att = _attention(qT, ks, vsT, kw, vwT, kc, vc.transpose(0, 1, 3, 2), gT)

    return _outmlp(x, rnn_n, att, row(norm_att_out), w_out.astype(BF16), row(post_norm_mix), g1,
                   row(pre_norm_mlp), sc2, sh2, w_ff1.astype(BF16), w_ff2.astype(BF16), row(post_norm_mlp), g2)


def kernel(x, c, ada_w, ada_b, pre_norm_mix, w_in, conv_w, conv_b, lru_wa, lru_ba, lru_wx, lru_bx, lru_lambda,
           cmp_pos_k, cmp_w1_k, cmp_w2_k, cmp_pos_v, cmp_w1_v, cmp_w2_v, norm_rnn_out, norm_att_out, w_out,
           post_norm_mix, pre_norm_mlp, w_ff1, w_ff2, post_norm_mlp):
    for l in range(ada_w.shape[0]):
        x = _layer(x, c, ada_w[l], ada_b[l], pre_norm_mix[l], w_in[l], conv_w[l], conv_b[l], lru_wa[l], lru_ba[l],
                   lru_wx[l], lru_bx[l], lru_lambda[l], cmp_pos_k[l], cmp_w1_k[l], cmp_w2_k[l], cmp_pos_v[l],
                   cmp_w1_v[l], cmp_w2_v[l], norm_rnn_out[l], norm_att_out[l], w_out[l], post_norm_mix[l],
                   pre_norm_mlp[l], w_ff1[l], w_ff2[l], post_norm_mlp[l])
    return x
```

```python
import functools

import jax
import jax.numpy as jnp
from jax import lax
from jax.experimental import pallas as pl
from jax.experimental.pallas import tpu as pltpu

F32 = jnp.float32
BF16 = jnp.bfloat16
I32 = jnp.int32

D_MODEL = 1024
D_RNN = 512
RNN_BLOCKS = 8
CONV_WIDTH = 4
LRU_C = 8.0
N_HEADS = 8
HEAD_DIM = 64
N_KV = 2
REP = N_HEADS // N_KV
D_ATT = N_HEADS * HEAD_DIM
CMP_LEN = 32
CMP_STRIDE = 16
CMP_HIDDEN = 256
SLC_BLOCK = 64
SLC_TOPK = 16
WINDOW = 512
D_FF = 4 * D_MODEL
EPS = 1e-6
NEG = -1e30
FORCED = 1e4
LOG2E = 1.4426950408889634

KV_COLS = 6 * N_KV * HEAD_DIM
N_GATE = 3 * N_HEADS
GATE_PAD = 128
GATE_ROWS = 16
OFF_Q = 2 * D_RNN
OFF_KV = OFF_Q + D_ATT
OFF_GATE = OFF_KV + KV_COLS
D_IN_PAD = OFF_GATE + GATE_PAD

TM_IN = 1024
TM_OUT = 1024
TQ = 256
TK = 256
NCHAIN = 4
KAUG = 256
VROWS = 80
VMEM_LIMIT = 56 * 1024 * 1024

SEL, WIN = 0, 1
MASK_NONE, MASK_CAUSAL, MASK_WINDOW_LOW, MASK_ALL = 0, 1, 2, 3
POS_COL = 2 * HEAD_DIM


def _gelu_tanh(x):
    return 0.5 * x * (1.0 + jnp.tanh(0.7978845608028654 * (x + 0.044715 * (x * x * x))))


def _sigmoid(x):
    return 0.5 * jnp.tanh(0.5 * x) + 0.5


def _rms(x):
    return x * lax.rsqrt(jnp.mean(x * x, axis=-1, keepdims=True) + EPS)


def _ada_kernel(c_ref, w_ref, b_ref, o_ref):
    c = c_ref[...]
    a = c * jax.nn.sigmoid(c)
    o_ref[...] = jnp.dot(a, w_ref[...], preferred_element_type=F32,
                         precision=lax.Precision.HIGHEST) + b_ref[...]


def _ada(c, w, b):
    B, D = c.shape
    N = w.shape[1]
    tn = 1024
    return pl.pallas_call(
        _ada_kernel,
        grid=(N // tn,),
        in_specs=[pl.BlockSpec((B, D), lambda j: (0, 0)),
                  pl.BlockSpec((D, tn), lambda j: (0, j)),
                  pl.BlockSpec((1, tn), lambda j: (0, j))],
        out_specs=pl.BlockSpec((B, tn), lambda j: (0, j)),
        out_shape=jax.ShapeDtypeStruct((B, N), F32),
        name="ada",
    )(c, w, b.reshape(1, N))


def _inproj_kernel(x_ref, gain_ref, sc_ref, sh_ref, w_ref, cw_ref, cb_ref, wa_ref, ba_ref, wx_ref, bx_ref,
                   lam_ref, grnn_ref,
                   rnn_ref, qT_ref, ks_ref, vs_ref, kw_ref, vw_ref, kc_ref, vc_ref, g_ref,
                   xbuf, hcar, a_s, u_s, h_s):
    tm = x_ref.shape[1]
    hd = HEAD_DIM

    @pl.when(pl.program_id(1) == 0)
    def _():
        xbuf[0:8, :] = jnp.zeros((8, D_RNN), F32)
        hcar[...] = jnp.zeros((1, D_RNN), F32)

    x = x_ref[0]
    h = _rms(x) * (gain_ref[...] * (1.0 + sc_ref[0])) + sh_ref[0]
    hb = h.astype(BF16)

    xr = jnp.dot(hb, w_ref[:, D_RNN:OFF_Q], preferred_element_type=F32)
    xbuf[8:8 + tm, :] = xr
    y = (cw_ref[3:4, :] * xr + cw_ref[2:3, :] * xbuf[7:7 + tm, :]
         + cw_ref[1:2, :] * xbuf[6:6 + tm, :] + cw_ref[0:1, :] * xbuf[5:5 + tm, :]) + cb_ref[...]
    xbuf[0:8, :] = xbuf[tm:tm + 8, :]

    yb = y.astype(BF16)
    r = _sigmoid(jnp.dot(yb, wa_ref[...], preferred_element_type=F32) + ba_ref[...])
    i = _sigmoid(jnp.dot(yb, wx_ref[...], preferred_element_type=F32) + bx_ref[...])
    nl = -lam_ref[...]
    softplus = jnp.maximum(nl, 0.0) + jnp.log(1.0 + jnp.exp(-jnp.abs(nl)))
    a = jnp.exp((-LRU_C) * r * softplus)
    a_s[...] = a
    u_s[...] = jnp.sqrt(1.0 - a * a) * (i * y)

    qT_ref[0] = (jnp.dot(hb, w_ref[:, OFF_Q:OFF_KV], preferred_element_type=F32)
                 * (HEAD_DIM ** -0.5 * LOG2E)).T.astype(BF16)
    kv = jnp.dot(hb, w_ref[:, OFF_KV:OFF_GATE], preferred_element_type=F32)
    width = N_KV * hd
    vsT = kv[:, 3 * width:4 * width].T
    vwT = kv[:, 5 * width:6 * width].T
    for gi in range(N_KV):
        kc_ref[0, gi] = kv[:, gi * hd:(gi + 1) * hd]
        vc_ref[0, gi] = kv[:, width + gi * hd:width + (gi + 1) * hd]
        ks_ref[0, gi] = kv[:, 2 * width + gi * hd:2 * width + (gi + 1) * hd].astype(BF16)
        kw_ref[0, gi] = kv[:, 4 * width + gi * hd:4 * width + (gi + 1) * hd].astype(BF16)
        for j in range(tm // TK):
            vs_ref[0, gi, j] = vsT[gi * hd:(gi + 1) * hd, j * TK:(j + 1) * TK].astype(BF16)
            vw_ref[0, gi, j] = vwT[gi * hd:(gi + 1) * hd, j * TK:(j + 1) * TK].astype(BF16)
    gates = _sigmoid(jnp.dot(hb, w_ref[:, OFF_GATE:D_IN_PAD], preferred_element_type=F32)).T
    for gi in range(N_KV):
        g_ref[0, gi] = gates[gi * GATE_ROWS:(gi + 1) * GATE_ROWS, :]
    g = jnp.dot(hb, w_ref[:, 0:D_RNN], preferred_element_type=F32)


    rows = lax.broadcasted_iota(I32, (8, D_RNN), 0)
    hprev = hcar[...]
    for gi in range(tm // 8):
        ag = a_s[gi * 8:gi * 8 + 8, :]
        ug = u_s[gi * 8:gi * 8 + 8, :]
        for k in (1, 2, 4):
            a_sh = jnp.where(rows >= k, pltpu.roll(ag, k, 0), 1.0)
            u_sh = jnp.where(rows >= k, pltpu.roll(ug, k, 0), 0.0)
            ug = ag * u_sh + ug
            ag = ag * a_sh
        hg = ag * hprev + ug
        h_s[gi * 8:gi * 8 + 8, :] = hg
        hprev = hg[7:8, :]
    hcar[...] = hprev

    rnn = _gelu_tanh(g) * h_s[...]
    rnn_ref[0] = (_rms(rnn) * grnn_ref[...]).astype(BF16)


def _inproj(x, gain, sc, sh, w_in, conv_w, conv_b, wa, ba, wx, bx, lam, grnn):
    B, S, D = x.shape
    tm = min(TM_IN, S)
    row = lambda n: pl.BlockSpec((1, n), lambda b, s: (0, 0))
    per_b = lambda n: pl.BlockSpec((1, 1, n), lambda b, s: (b, 0, 0))
    full = lambda a: pl.BlockSpec(a.shape, lambda b, s: (0,) * a.ndim)
    tok = lambda n: pl.BlockSpec((1, tm, n), lambda b, s: (b, s, 0))
    G = N_KV
    grp = pl.BlockSpec((1, G, tm, HEAD_DIM), lambda b, s: (b, 0, s, 0))
    grpT = pl.BlockSpec((1, G, tm // TK, HEAD_DIM, TK), lambda b, s: (b, 0, s, 0, 0))
    return pl.pallas_call(
        _inproj_kernel,
        grid=(B, S // tm),
        in_specs=[tok(D), row(D), per_b(D), per_b(D), full(w_in), full(conv_w), row(D_RNN),
                  full(wa), row(D_RNN), full(wx), row(D_RNN), row(D_RNN), row(D_RNN)],
        out_specs=[tok(D_RNN),
                   pl.BlockSpec((1, D_ATT, tm), lambda b, s: (b, 0, s)),
                   grp, grpT, grp, grpT, grp, grp,
                   pl.BlockSpec((1, G, GATE_ROWS, tm), lambda b, s: (b, 0, 0, s))],
        out_shape=[jax.ShapeDtypeStruct((B, S, D_RNN), BF16),
                   jax.ShapeDtypeStruct((B, D_ATT, S), BF16),
                   jax.ShapeDtypeStruct((B, G, S, HEAD_DIM), BF16),
                   jax.ShapeDtypeStruct((B, G, S // TK, HEAD_DIM, TK), BF16),
                   jax.ShapeDtypeStruct((B, G, S, HEAD_DIM), BF16),
                   jax.ShapeDtypeStruct((B, G, S // TK, HEAD_DIM, TK), BF16),
                   jax.ShapeDtypeStruct((B, G, S, HEAD_DIM), F32),
                   jax.ShapeDtypeStruct((B, G, S, HEAD_DIM), F32),
                   jax.ShapeDtypeStruct((B, G, GATE_ROWS, S), F32)],
        scratch_shapes=[pltpu.VMEM((tm + 8, D_RNN), F32), pltpu.VMEM((1, D_RNN), F32),
                        pltpu.VMEM((tm, D_RNN), F32), pltpu.VMEM((tm, D_RNN), F32),
                        pltpu.VMEM((tm, D_RNN), F32)],
        compiler_params=pltpu.CompilerParams(dimension_semantics=("arbitrary", "arbitrary"),
                                             vmem_limit_bytes=VMEM_LIMIT),
        name="inproj",
    )(x, gain, sc, sh, w_in, conv_w, conv_b, wa, ba, wx, bx, lam, grnn)


def _compress_kernel(rk_ref, rv_ref, w1k_ref, w2k_ref, pk_ref, w1v_ref, w2v_ref, pv_ref, kc_ref, vc_ref):
    def one(r_ref, w1_ref, w2_ref, pos_ref, o_ref):
        n = r_ref.shape[2] // CMP_STRIDE
        p = jnp.zeros((n, 2 * CMP_HIDDEN), F32)
        for l in range(CMP_STRIDE):
            rows = r_ref[0, 0, pl.ds(l, n, stride=CMP_STRIDE), :].astype(BF16)
            p = p + jnp.dot(rows, w1_ref[l * HEAD_DIM:(l + 1) * HEAD_DIM, :], preferred_element_type=F32)
        posb = jnp.dot(pos_ref[...], w1_ref[...], preferred_element_type=F32)
        bias = posb[0:1, 0:CMP_HIDDEN] + posb[1:2, CMP_HIDDEN:]
        pre = p[:, 0:CMP_HIDDEN] + pltpu.roll(p[:, CMP_HIDDEN:], n - 1, 0) + bias
        hid = _gelu_tanh(pre).astype(BF16)
        o_ref[0, 0] = jnp.dot(hid, w2_ref[...], preferred_element_type=F32).astype(BF16)

    one(rk_ref, w1k_ref, w2k_ref, pk_ref, kc_ref)
    one(rv_ref, w1v_ref, w2v_ref, pv_ref, vc_ref)


def _compress(rk, rv, w1k, w2k, posk, w1v, w2v, posv):
    B, G, S, width = rk.shape
    n = S // CMP_STRIDE
    blk = pl.BlockSpec((1, 1, S, width), lambda b, g: (b, g, 0, 0))
    full = lambda a: pl.BlockSpec(a.shape, lambda b, g: (0,) * a.ndim)
    out = pl.BlockSpec((1, 1, n, HEAD_DIM), lambda b, g: (b, g, 0, 0))
    return pl.pallas_call(
        _compress_kernel,
        grid=(B, G),
        in_specs=[blk, blk, full(w1k), full(w2k), full(posk), full(w1v), full(w2v), full(posv)],
        out_specs=[out, out],
        out_shape=[jax.ShapeDtypeStruct((B, G, n, HEAD_DIM), BF16)] * 2,
        compiler_params=pltpu.CompilerParams(vmem_limit_bytes=VMEM_LIMIT),
        name="compress",
    )(rk, rv, w1k, w2k, posk, w1v, w2v, posv)


def _attn_kernel(qT_ref, ks_ref, vs_ref, kw_ref, vw_ref, kc_ref, vcT_ref, g_ref,
                 o_ref,
                 kall, vall, qall, kca, m_all, acc_all, al_s, sbuf, pbuf, oacc, val_s, rank_s, selbias_s,
                 bias_tbl, cmask,
                 oT_s, st_s, *, n_sel):
    S = ks_ref.shape[2]
    nkt = S // TK
    ncmp = kc_ref.shape[2]
    b = pl.program_id(0)
    g = pl.program_id(1)
    step = pl.program_id(2)
    W4 = REP * TQ
    cmp_per_tile = TQ // CMP_STRIDE

    @pl.when((b == 0) & (g == 0) & (step == 0))
    def _():
        col = lax.broadcasted_iota(I32, (TK, KAUG), 1)
        is_blk = (col == POS_COL) | (col == POS_COL + 2)
        is_off = (col == POS_COL + 1) | (col == POS_COL + 3)
        ones_row = (lax.broadcasted_iota(I32, (VROWS - HEAD_DIM, TK), 0) == 0).astype(BF16)
        for kt in range(nkt):
            key = kt * TK + lax.broadcasted_iota(I32, (TK, KAUG), 0)
            pos = jnp.where(is_blk, (key // SLC_BLOCK).astype(F32),
                            jnp.where(is_off, (key % SLC_BLOCK).astype(F32), 0.0))
            onehot = (col - HEAD_DIM == key // SLC_BLOCK).astype(F32)
            kall[SEL, kt * TK:(kt + 1) * TK, :] = (pos + onehot).astype(BF16)
            kall[WIN, kt * TK:(kt + 1) * TK, :] = pos.astype(BF16)
            vall[SEL, kt, HEAD_DIM:VROWS, :] = ones_row
            vall[WIN, kt, HEAD_DIM:VROWS, :] = ones_row
        ko = lax.broadcasted_iota(I32, (TK, W4), 0)
        to = lax.broadcasted_iota(I32, (TK, W4), 1) % TQ
        bias_tbl[MASK_NONE] = jnp.zeros((TK, W4), F32)
        bias_tbl[MASK_CAUSAL] = jnp.where(ko <= to, 0.0, NEG)
        bias_tbl[MASK_WINDOW_LOW] = jnp.where(ko > to, 0.0, NEG)
        bias_tbl[MASK_ALL] = jnp.full((TK, W4), NEG, F32)
        u = lax.broadcasted_iota(I32, (2 * ncmp, W4), 0)
        tc = lax.broadcasted_iota(I32, (2 * ncmp, W4), 1) % TQ
        cmask[...] = jnp.where(u <= ncmp + ((tc + 1) // CMP_STRIDE) - 2, 0.0, NEG)
        cc = lax.broadcasted_iota(I32, (ncmp, KAUG), 0)
        colc = lax.broadcasted_iota(I32, (ncmp, KAUG), 1)
        kca[...] = jnp.where((colc == POS_COL) | (colc == POS_COL + 2), (cc // 4).astype(F32),
                             jnp.where((colc == POS_COL + 1) | (colc == POS_COL + 3),
                                       (CMP_STRIDE * (cc % 4)).astype(F32), 0.0)).astype(BF16)
        qall[...] = jnp.zeros(qall.shape, BF16)

    lane = lax.broadcasted_iota(I32, (1, W4), 1)
    head = g * REP + lane // TQ
    slope = lax.bitcast_convert_type((126 - head) << 23, F32)

    @pl.when(step == 0)
    def _():
        for kt in range(nkt):
            kall[SEL, kt * TK:(kt + 1) * TK, 0:HEAD_DIM] = ks_ref[0, 0, kt * TK:(kt + 1) * TK, :]
            kall[WIN, kt * TK:(kt + 1) * TK, 0:HEAD_DIM] = kw_ref[0, 0, kt * TK:(kt + 1) * TK, :]
            vall[SEL, kt, 0:HEAD_DIM, :] = vs_ref[0, 0, kt]
            vall[WIN, kt, 0:HEAD_DIM, :] = vw_ref[0, 0, kt]
        kca[:, 0:HEAD_DIM] = kc_ref[0, 0]
        c_hi = (slope * LOG2E).astype(BF16).astype(F32)
        c_lo = slope * LOG2E - c_hi
        r16 = lax.broadcasted_iota(I32, (16, W4), 0)
        alibi = jnp.where(r16 == 0, c_hi * SLC_BLOCK, jnp.where(r16 == 1, c_hi, jnp.where(
            r16 == 2, c_lo * SLC_BLOCK, jnp.where(r16 == 3, c_lo, 0.0)))).astype(BF16)
        for c in range(NCHAIN):
            qall[c, SEL, POS_COL:POS_COL + 16, :] = alibi
            qall[c, WIN, POS_COL:POS_COL + 16, :] = alibi

    chains = [_attn_chain(c, NCHAIN * step + c, qT_ref, kc_ref, vcT_ref, g_ref, o_ref, kall, vall, qall.at[c], kca,
                          m_all.at[c], acc_all.at[c], al_s.at[c], sbuf.at[c], pbuf.at[c], oacc.at[c], val_s.at[c],
                          rank_s.at[c], selbias_s.at[c], bias_tbl, cmask, oT_s.at[c], st_s.at[c], n_sel)
              for c in range(NCHAIN)]
    live = list(chains)
    while live:
        handed = [next(ch, _DONE) for ch in live]
        live = [ch for ch, r in zip(live, handed) if r is not _DONE]
        loops = [r for r in handed if r is not _DONE and r is not None]
        if loops:
            joint = functools.reduce(jnp.minimum, [npair for npair, _, _ in loops])

            def trip(j, group):
                als = [head(j) for _, head, _ in group]
                for (_, _, tail), al in zip(group, als):
                    tail(j, al)

            lax.fori_loop(1, 1 + joint, lambda j, carry: (trip(j, loops), carry)[1], 0)
            for item in loops:
                lax.fori_loop(1 + joint, 1 + item[0], lambda j, carry, item=item: (trip(j, [item]), carry)[1], 0)


_DONE = object()


def _attn_chain(c, qi, qT_ref, kc_ref, vcT_ref, g_ref, o_ref, kall, vall, qall, kca, m_all, acc_all, al_s, sbuf,
                pbuf, oacc, val_s, rank_s, selbias_s, bias_tbl, cmask, oT_s, st_s, n_sel):
    S = kall.shape[1]
    ncmp = kc_ref.shape[2]
    nblk = S // SLC_BLOCK
    cmp_per_tile = TQ // CMP_STRIDE
    q0 = qi * TQ
    W4 = REP * TQ
    lane = lax.broadcasted_iota(I32, (1, W4), 1)
    lanes_c = slice(c * TQ, (c + 1) * TQ)

    q = qT_ref[0, :, lanes_c]
    for r in range(REP):
        qr = q[r * HEAD_DIM:(r + 1) * HEAD_DIM, :]
        qall[SEL, 0:HEAD_DIM, r * TQ:(r + 1) * TQ] = qr
        qall[WIN, 0:HEAD_DIM, r * TQ:(r + 1) * TQ] = qr

    gall = g_ref[0, 0, :, lanes_c]
    gates = [gall[br * REP:(br + 1) * REP, :] for br in range(3)]

    def stage_scores(br, tiles_masks):
        col_max = None
        for h, (tile, mask) in enumerate(tiles_masks):
            r0 = pl.multiple_of(tile * TK, TK)
            s = jnp.dot(kall[br, pl.ds(r0, TK), :], qall[br], preferred_element_type=F32)
            if mask is not None:
                s = s + bias_tbl[mask]
            sbuf[h * TK:(h + 1) * TK, :] = s
            c = jnp.max(s, axis=0, keepdims=True)
            col_max = c if col_max is None else jnp.maximum(col_max, c)
        m_old = m_all[br]
        m_new = jnp.maximum(m_old, col_max)
        m_all[br] = m_new
        return jnp.exp2(m_old - m_new)

    def stage_probs(br, ntile):
        pbuf[0:ntile * TK, :] = jnp.exp2(sbuf[0:ntile * TK, :] - m_all[br]).astype(BF16)

    def stage_values(br, tiles, al_row):
        acc = acc_all[br] * al_row
        for h, tile in enumerate(tiles):
            acc = acc + jnp.dot(vall[br, tile], pbuf[h * TK:(h + 1) * TK, :], preferred_element_type=F32)
        acc_all[br] = acc

    def start_branch(br, tl, mask1):
        al_s[...] = stage_scores(br, [(tl[0], MASK_CAUSAL), (tl[1], mask1)])

    def pair_items(br, n, tl):
        npair = jnp.maximum(n - 2, 0) // 2

        def head(j):
            al_prev = al_s[...]
            stage_probs(br, 2)
            al_s[...] = stage_scores(br, [(tl[2 * j], None), (tl[2 * j + 1], None)])
            return al_prev

        def tail(j, al_prev):
            stage_values(br, [tl[2 * j - 2], tl[2 * j - 1]], al_prev)

        return npair, head, tail

    def finish_branch(br, n, tl):
        rest = jnp.maximum(n - 2, 0)
        last = 2 * (rest // 2)
        odd = rest % 2 == 1
        t_single = tl[jnp.where(odd, n - 1, 0)]
        al_prev = al_s[...]
        stage_probs(br, 2)
        al_k = stage_scores(br, [(t_single, jnp.where(odd, MASK_NONE, MASK_ALL))])
        yield
        stage_values(br, [tl[last], tl[last + 1]], al_prev)
        stage_probs(br, 1)
        yield
        stage_values(br, [t_single], al_k)

    m_all[...] = jnp.full(m_all.shape, NEG, F32)
    acc_all[...] = jnp.zeros(acc_all.shape, F32)
    w1 = jnp.maximum(qi - 2, 0)
    w2 = jnp.maximum(qi - 1, 0)
    al_w0 = stage_scores(WIN, [(qi, MASK_CAUSAL),
                               (w1, jnp.where(qi >= 2, MASK_WINDOW_LOW, jnp.where(qi >= 1, MASK_NONE, MASK_ALL)))])
    yield

    c0 = pl.multiple_of(ncmp - cmp_per_tile * qi, cmp_per_tile)
    sc = jnp.dot(kca[...], qall[WIN], preferred_element_type=F32) + cmask[pl.ds(c0, ncmp), :]
    yield
    e = jnp.exp2(sc - jnp.max(sc, axis=0, keepdims=True))
    tq = q0 + lane % TQ
    p = e * ((1.0 / jnp.sum(e, axis=0, keepdims=True)) * (tq >= CMP_LEN - 1).astype(F32))
    ocT = jnp.dot(vcT_ref[0, 0], p.astype(BF16), preferred_element_type=F32)
    for r in range(REP):
        oacc[:, r * TQ:(r + 1) * TQ] = gates[0][r:r + 1, :] * ocT[:, r * TQ:(r + 1) * TQ]
    yield

    stage_probs(WIN, 2)
    yield
    al_w1 = stage_scores(WIN, [(w2, jnp.where(qi >= 2, MASK_NONE, MASK_ALL))])
    yield
    stage_values(WIN, [qi, w1], al_w0)
    yield

    psum = p[:, 0:TQ]
    for r in range(1, REP):
        psum = psum + p[:, r * TQ:(r + 1) * TQ]
    jj = lax.broadcasted_iota(I32, (nblk, ncmp), 0)
    cc = lax.broadcasted_iota(I32, (nblk, ncmp), 1)
    ovT = ((CMP_STRIDE * cc < SLC_BLOCK * jj + SLC_BLOCK)
           & (CMP_STRIDE * cc + CMP_LEN > SLC_BLOCK * jj)).astype(F32)
    impT = jnp.dot(ovT, psum, preferred_element_type=F32, precision=lax.Precision.HIGHEST)

    j_i = lax.broadcasted_iota(I32, (nblk, TQ), 0)
    t1 = q0 + lax.broadcasted_iota(I32, (nblk, TQ), 1)
    cur = t1 // SLC_BLOCK
    forced = (j_i == 0) | (j_i == cur) | (j_i == cur - 1)
    visible = SLC_BLOCK * j_i <= t1
    val_s[...] = jnp.where(forced, FORCED, jnp.where(visible, impT, NEG))
    yield
    ngrp = nblk // 8
    rank_s[...] = jnp.zeros(rank_s.shape, F32)
    j8 = lax.broadcasted_iota(I32, (8, TQ), 0)
    for ib in range(ngrp):
        @pl.when(8 * ib * SLC_BLOCK < q0 + TQ)
        def _(ib=ib):
            vals = [val_s[8 * jb:8 * jb + 8, :] for jb in range(ngrp)]
            ranks = [rank_s[8 * jb:8 * jb + 8, :] for jb in range(ngrp)]
            for i in range(8 * ib, 8 * ib + 8):
                row = jnp.broadcast_to(val_s[i:i + 1, :], (8, TQ))
                for jb in range(ngrp):
                    if jb > ib:
                        hit = row >= vals[jb]
                    elif jb < ib:
                        hit = row > vals[jb]
                    else:
                        hit = jnp.where(j8 > i - 8 * jb, jnp.where(row >= vals[jb], 1.0, 0.0),
                                        jnp.where(row > vals[jb], 1.0, 0.0)) > 0.5
                    ranks[jb] = ranks[jb] + jnp.where(hit, 1.0, 0.0)
            for jb in range(ngrp):
                rank_s[8 * jb:8 * jb + 8, :] = ranks[jb]
    yield
    ranks = [rank_s[8 * jb:8 * jb + 8, :] for jb in range(ngrp)]
    blocks_per_tile = TK // SLC_BLOCK
    st_s[0] = qi
    n_selt = jnp.int32(1)
    for jb in range(ngrp):
        chosen = ranks[jb] < n_sel
        selb = jnp.where(chosen, 0.0, NEG)
        for r in range(REP):
            selbias_s[8 * jb:8 * jb + 8, r * TQ:(r + 1) * TQ] = selb
        any_q = jnp.max(jnp.where(chosen, 1.0, 0.0), axis=1, keepdims=True)
        for hh in range(8 // blocks_per_tile):
            kt = (8 * jb) // blocks_per_tile + hh
            hit = jnp.max(any_q[hh * blocks_per_tile:(hh + 1) * blocks_per_tile, :]) > 0.5
            st_s[n_selt] = kt
            n_selt = n_selt + jnp.where(hit & (kt < qi), 1, 0)
    qall[SEL, HEAD_DIM:HEAD_DIM + nblk, :] = selbias_s[...].astype(BF16)
    yield

    stage_probs(WIN, 1)
    stage_values(WIN, [w2], al_w1)
    yield
    start_branch(SEL, st_s, jnp.where(n_selt >= 2, MASK_NONE, MASK_ALL))
    yield pair_items(SEL, n_selt, st_s)
    yield from finish_branch(SEL, n_selt, st_s)
    yield

    def normalised(br):
        acc = acc_all[br]
        return acc[0:HEAD_DIM, :] / acc[HEAD_DIM:HEAD_DIM + 1, :]

    o_sel = normalised(SEL)
    o_win = normalised(WIN)
    for r in range(REP):
        lanes = slice(r * TQ, (r + 1) * TQ)
        oT_s[r * HEAD_DIM:(r + 1) * HEAD_DIM, :] = (oacc[:, lanes] + gates[1][r:r + 1, :] * o_sel[:, lanes]
                                                    + gates[2][r:r + 1, :] * o_win[:, lanes])
    o_ref[0, lanes_c, :] = oT_s[...].T.astype(BF16)


def _attention(qT, ks, vs, kw, vw, kc, vcT, gT):
    B, _, S = qT.shape
    G = N_KV
    nkt = S // TK
    ncmp = kc.shape[2]
    nblk = S // SLC_BLOCK
    n_sel = min(SLC_TOPK, nblk)
    W4 = REP * TQ
    res4 = lambda a: pl.BlockSpec((1, 1) + a.shape[2:], lambda b, g, i: (b, g) + (0,) * (a.ndim - 2))
    return pl.pallas_call(
        functools.partial(_attn_kernel, n_sel=n_sel),
        grid=(B, G, S // (NCHAIN * TQ)),
        in_specs=[pl.BlockSpec((1, REP * HEAD_DIM, NCHAIN * TQ), lambda b, g, i: (b, g, i)),
                  res4(ks), res4(vs), res4(kw), res4(vw), res4(kc), res4(vcT),
                  pl.BlockSpec((1, 1, GATE_ROWS, NCHAIN * TQ), lambda b, g, i: (b, g, 0, i))],
        out_specs=pl.BlockSpec((1, NCHAIN * TQ, REP * HEAD_DIM), lambda b, g, i: (b, i, g)),
        out_shape=jax.ShapeDtypeStruct((B, S, D_ATT), BF16),
        scratch_shapes=[pltpu.VMEM((2, S, KAUG), BF16),
                        pltpu.VMEM((2, nkt, VROWS, TK), BF16),
                        pltpu.VMEM((NCHAIN, 2, KAUG, W4), BF16),
                        pltpu.VMEM((ncmp, KAUG), BF16),
                        pltpu.VMEM((NCHAIN, 2, 1, W4), F32),
                        pltpu.VMEM((NCHAIN, 2, VROWS, W4), F32),
                        pltpu.VMEM((NCHAIN, 1, W4), F32),
                        pltpu.VMEM((NCHAIN, 2 * TK, W4), F32),
                        pltpu.VMEM((NCHAIN, 2 * TK, W4), BF16),
                        pltpu.VMEM((NCHAIN, HEAD_DIM, W4), F32),
                        pltpu.VMEM((NCHAIN, nblk, TQ), F32),
                        pltpu.VMEM((NCHAIN, nblk, TQ), F32),
                        pltpu.VMEM((NCHAIN, nblk, W4), F32),
                        pltpu.VMEM((4, TK, W4), F32),
                        pltpu.VMEM((2 * ncmp, W4), F32),
                        pltpu.VMEM((NCHAIN, REP * HEAD_DIM, TQ), F32),
                        pltpu.SMEM((NCHAIN, nkt + 2), I32)],
        compiler_params=pltpu.CompilerParams(dimension_semantics=("arbitrary", "arbitrary", "arbitrary"),
                                             vmem_limit_bytes=VMEM_LIMIT),
        name="attn",
    )(qT, ks, vs, kw, vw, kc, vcT, gT)


def _outmlp_kernel(x_ref, rnn_ref, att_ref, gatt_ref, wo_ref, gpost_ref, g1_ref, gpre_ref, sc2_ref, sh2_ref,
                   w1_ref, w2_ref, gpost2_ref, g2_ref, o_ref):
    tm = x_ref.shape[1]
    halves = [slice(0, tm // 2), slice(tm // 2, tm)]
    x1s, h2s = [], []
    for rows in halves:
        att_n = (_rms(att_ref[0, rows, :].astype(F32)) * gatt_ref[...]).astype(BF16)
        y = (jnp.dot(rnn_ref[0, rows, :], wo_ref[0:D_RNN, :], preferred_element_type=F32)
             + jnp.dot(att_n, wo_ref[D_RNN:, :], preferred_element_type=F32))
        x1 = x_ref[0, rows, :] + (1.0 + g1_ref[0]) * (_rms(y) * gpost_ref[...])
        x1s.append(x1)
        h2s.append((_rms(x1) * (gpre_ref[...] * (1.0 + sc2_ref[0])) + sh2_ref[0]).astype(BF16))
    fc = 1024
    for rows, x1, h2 in zip(halves, x1s, h2s):
        ff = jnp.zeros(x1.shape, F32)
        for c in range(D_FF // fc):
            hid = jnp.maximum(jnp.dot(h2, w1_ref[:, c * fc:(c + 1) * fc], preferred_element_type=F32), 0.0)
            ff = ff + jnp.dot((hid * hid).astype(BF16), w2_ref[c * fc:(c + 1) * fc, :],
                              preferred_element_type=F32)
        o_ref[0, rows, :] = x1 + (1.0 + g2_ref[0]) * (_rms(ff) * gpost2_ref[...])


def _outmlp(x, rnn_n, att, gatt, wo, gpost, g1, gpre, sc2, sh2, w1, w2, gpost2, g2):
    B, S, D = x.shape
    tm = min(TM_OUT, S)
    row = lambda n: pl.BlockSpec((1, n), lambda b, s: (0, 0))
    per_b = lambda n: pl.BlockSpec((1, 1, n), lambda b, s: (b, 0, 0))
    const = lambda a: pl.BlockSpec(a.shape, lambda b, s: (0,) * a.ndim, pipeline_mode=pl.Buffered(1))
    tok = lambda n: pl.BlockSpec((1, tm, n), lambda b, s: (b, s, 0))
    return pl.pallas_call(
        _outmlp_kernel,
        grid=(B, S // tm),
        in_specs=[tok(D), tok(D_RNN), tok(D_ATT), row(D_ATT), const(wo), row(D), per_b(D), row(D),
                  per_b(D), per_b(D), const(w1), const(w2), row(D), per_b(D)],
        out_specs=tok(D),
        out_shape=jax.ShapeDtypeStruct((B, S, D), F32),
        compiler_params=pltpu.CompilerParams(dimension_semantics=("arbitrary", "arbitrary"),
                                             vmem_limit_bytes=VMEM_LIMIT),
        name="outmlp",
    )(x, rnn_n, att, gatt, wo, gpost, g1, gpre, sc2, sh2, w1, w2, gpost2, g2)


def _block_diag(w):
    n, k, _ = w.shape
    return jnp.einsum('nij,nm->nimj', w, jnp.eye(n, dtype=w.dtype)).reshape(n * k, n * k)


def _layer(x, c, ada_w, ada_b, pre_norm_mix, w_in, conv_w, conv_b, lru_wa, lru_ba, lru_wx, lru_bx, lru_lambda,
           cmp_pos_k, cmp_w1_k, cmp_w2_k, cmp_pos_v, cmp_w1_v, cmp_w2_v, norm_rnn_out, norm_att_out, w_out,
           post_norm_mix, pre_norm_mlp, w_ff1, w_ff2, post_norm_mlp):
    B, S, D = x.shape
    G = N_KV
    row = lambda v: v.reshape(1, -1)

    mod = _ada(c, ada_w, ada_b)
    sh1, sc1, g1, sh2, sc2, g2 = [m.reshape(B, 1, D) for m in jnp.split(mod, 6, axis=-1)]

    gate_cols = [OFF_GATE + br * N_HEADS + g * REP + r for g in range(G) for br in range(3) for r in range(REP)]
    w_gate = w_in[:, jnp.asarray(gate_cols)].reshape(D, G, 3 * REP)
    w_gate = jnp.pad(w_gate, ((0, 0), (0, 0), (0, GATE_ROWS - 3 * REP))).reshape(D, G * GATE_ROWS)
    w_in_p = jnp.concatenate([w_in[:, :OFF_GATE], jnp.pad(w_gate, ((0, 0), (0, GATE_PAD - G * GATE_ROWS)))],
                             axis=1).astype(BF16)
    wa = _block_diag(lru_wa).astype(BF16)
    wx = _block_diag(lru_wx).astype(BF16)
    half = CMP_LEN // 2 * HEAD_DIM

    def w1_cat(w1):
        return jnp.concatenate([w1[:half], w1[half:]], axis=1).astype(BF16)

    def pos_rows(pos):
        return jnp.pad(pos.reshape(2, half), ((0, 14), (0, 0))).astype(BF16)

    rnn_n, qT, ks, vsT, kw, vwT, kc_in, vc_in, gT = _inproj(
        x, row(pre_norm_mix), sc1, sh1, w_in_p, conv_w, row(conv_b), wa, row(lru_ba), wx, row(lru_bx),
        row(lru_lambda), row(norm_rnn_out))
    kc, vc = _compress(kc_in, vc_in, w1_cat(cmp_w1_k), cmp_w2_k.astype(BF16), pos_rows(cmp_pos_k),
                       w1_cat(cmp_w1_v), cmp_w2_v.astype(BF16), pos_rows(cmp_pos_v))
    att = _attention(qT, ks, vsT, kw, vwT, kc, vc.transpose(0, 1, 3, 2), gT)

    return _outmlp(x, rnn_n, att, row(norm_att_out), w_out.astype(BF16), row(post_norm_mix), g1,
                   row(pre_norm_mlp), sc2, sh2, w_ff1.astype(BF16), w_ff2.astype(BF16), row(post_norm_mlp), g2)


def kernel(x, c, ada_w, ada_b, pre_norm_mix, w_in, conv_w, conv_b, lru_wa, lru_ba, lru_wx, lru_bx, lru_lambda,
           cmp_pos_k, cmp_w1_k, cmp_w2_k, cmp_pos_v, cmp_w1_v, cmp_w2_v, norm_rnn_out, norm_att_out, w_out,
           post_norm_mix, pre_norm_mlp, w_ff1, w_ff2, post_norm_mlp):
    for l in range(ada_w.shape[0]):
        x = _layer(x, c, ada_w[l], ada_b[l], pre_norm_mix[l], w_in[l], conv_w[l], conv_b[l], lru_wa[l], lru_ba[l],
                   lru_wx[l], lru_bx[l], lru_lambda[l], cmp_pos_k[l], cmp_w1_k[l], cmp_w2_k[l], cmp_pos_v[l],
                   cmp_w1_v[l], cmp_w2_v[l], norm_rnn_out[l], norm_att_out[l], w_out[l], post_norm_mix[l],
                   pre_norm_mlp[l], w_ff1[l], w_ff2[l], post_norm_mlp[l])
    return x
```

```python
import functools

import jax
import jax.numpy as jnp
from jax import lax
from jax.experimental import pallas as pl
from jax.experimental.pallas import tpu as pltpu

F32 = jnp.float32
BF16 = jnp.bfloat16
I32 = jnp.int32

D_MODEL = 1024
D_RNN = 512
RNN_BLOCKS = 8
CONV_WIDTH = 4
LRU_C = 8.0
N_HEADS = 8
HEAD_DIM = 64
N_KV = 2
REP = N_HEADS // N_KV
D_ATT = N_HEADS * HEAD_DIM
CMP_LEN = 32
CMP_STRIDE = 16
CMP_HIDDEN = 256
SLC_BLOCK = 64
SLC_TOPK = 16
WINDOW = 512
D_FF = 4 * D_MODEL
EPS = 1e-6
NEG = -1e30
FORCED = 1e4
LOG2E = 1.4426950408889634

KV_COLS = 6 * N_KV * HEAD_DIM
N_GATE = 3 * N_HEADS
GATE_PAD = 128
GATE_ROWS = 16
OFF_Q = 2 * D_RNN
OFF_KV = OFF_Q + D_ATT
OFF_GATE = OFF_KV + KV_COLS
D_IN_PAD = OFF_GATE + GATE_PAD

TM_IN = 1024
TM_OUT = 1024
TQ = 256
TK = 256
NCHAIN = 4
KAUG = 256
VROWS = 80
VMEM_LIMIT = 56 * 1024 * 1024

SEL, WIN = 0, 1
MASK_NONE, MASK_CAUSAL, MASK_WINDOW_LOW, MASK_ALL = 0, 1, 2, 3
POS_COL = 2 * HEAD_DIM


def _gelu_tanh(x):
    return 0.5 * x * (1.0 + jnp.tanh(0.7978845608028654 * (x + 0.044715 * (x * x * x))))


def _sigmoid(x):
    return 0.5 * jnp.tanh(0.5 * x) + 0.5


def _rms(x):
    return x * lax.rsqrt(jnp.mean(x * x, axis=-1, keepdims=True) + EPS)


def _ada_kernel(c_ref, w_ref, b_ref, o_ref):
    c = c_ref[...]
    a = c * jax.nn.sigmoid(c)
    o_ref[...] = jnp.dot(a, w_ref[...], preferred_element_type=F32,
                         precision=lax.Precision.HIGHEST) + b_ref[...]


def _ada(c, w, b):
    B, D = c.shape
    N = w.shape[1]
    tn = 1024
    return pl.pallas_call(
        _ada_kernel,
        grid=(N // tn,),
        in_specs=[pl.BlockSpec((B, D), lambda j: (0, 0)),
                  pl.BlockSpec((D, tn), lambda j: (0, j)),
                  pl.BlockSpec((1, tn), lambda j: (0, j))],
        out_specs=pl.BlockSpec((B, tn), lambda j: (0, j)),
        out_shape=jax.ShapeDtypeStruct((B, N), F32),
        name="ada",
    )(c, w, b.reshape(1, N))


def _inproj_kernel(x_ref, gain_ref, sc_ref, sh_ref, w_ref, cw_ref, cb_ref, wa_ref, ba_ref, wx_ref, bx_ref,
                   lam_ref, grnn_ref,
                   rnn_ref, qT_ref, ks_ref, vs_ref, kw_ref, vw_ref, kc_ref, vc_ref, g_ref,
                   xbuf, hcar, a_s, u_s, h_s):
    tm = x_ref.shape[1]
    hd = HEAD_DIM

    @pl.when(pl.program_id(1) == 0)
    def _():
        xbuf[0:8, :] = jnp.zeros((8, D_RNN), F32)
        hcar[...] = jnp.zeros((1, D_RNN), F32)

    x = x_ref[0]
    h = _rms(x) * (gain_ref[...] * (1.0 + sc_ref[0])) + sh_ref[0]
    hb = h.astype(BF16)

    xr = jnp.dot(hb, w_ref[:, D_RNN:OFF_Q], preferred_element_type=F32)
    xbuf[8:8 + tm, :] = xr
    y = (cw_ref[3:4, :] * xr + cw_ref[2:3, :] * xbuf[7:7 + tm, :]
         + cw_ref[1:2, :] * xbuf[6:6 + tm, :] + cw_ref[0:1, :] * xbuf[5:5 + tm, :]) + cb_ref[...]
    xbuf[0:8, :] = xbuf[tm:tm + 8, :]

    yb = y.astype(BF16)
    r = _sigmoid(jnp.dot(yb, wa_ref[...], preferred_element_type=F32) + ba_ref[...])
    i = _sigmoid(jnp.dot(yb, wx_ref[...], preferred_element_type=F32) + bx_ref[...])
    nl = -lam_ref[...]
    softplus = jnp.maximum(nl, 0.0) + jnp.log(1.0 + jnp.exp(-jnp.abs(nl)))
    a = jnp.exp((-LRU_C) * r * softplus)
    a_s[...] = a
    u_s[...] = jnp.sqrt(1.0 - a * a) * (i * y)

    qT_ref[0] = (jnp.dot(hb, w_ref[:, OFF_Q:OFF_KV], preferred_element_type=F32)
                 * (HEAD_DIM ** -0.5 * LOG2E)).T.astype(BF16)
    kv = jnp.dot(hb, w_ref[:, OFF_KV:OFF_GATE], preferred_element_type=F32)
    width = N_KV * hd
    vsT = kv[:, 3 * width:4 * width].T
    vwT = kv[:, 5 * width:6 * width].T
    for gi in range(N_KV):
        kc_ref[0, gi] = kv[:, gi * hd:(gi + 1) * hd]
        vc_ref[0, gi] = kv[:, width + gi * hd:width + (gi + 1) * hd]
        ks_ref[0, gi] = kv[:, 2 * width + gi * hd:2 * width + (gi + 1) * hd].astype(BF16)
        kw_ref[0, gi] = kv[:, 4 * width + gi * hd:4 * width + (gi + 1) * hd].astype(BF16)
        for j in range(tm // TK):
            vs_ref[0, gi, j] = vsT[gi * hd:(gi + 1) * hd, j * TK:(j + 1) * TK].astype(BF16)
            vw_ref[0, gi, j] = vwT[gi * hd:(gi + 1) * hd, j * TK:(j + 1) * TK].astype(BF16)
    gates = _sigmoid(jnp.dot(hb, w_ref[:, OFF_GATE:D_IN_PAD], preferred_element_type=F32)).T
    for gi in range(N_KV):
        g_ref[0, gi] = gates[gi * GATE_ROWS:(gi + 1) * GATE_ROWS, :]
    g = jnp.dot(hb, w_ref[:, 0:D_RNN], preferred_element_type=F32)


    rows = lax.broadcasted_iota(I32, (8, D_RNN), 0)
    hprev = hcar[...]
    for gi in range(tm // 8):
        ag = a_s[gi * 8:gi * 8 + 8, :]
        ug = u_s[gi * 8:gi * 8 + 8, :]
        for k in (1, 2, 4):
            a_sh = jnp.where(rows >= k, pltpu.roll(ag, k, 0), 1.0)
            u_sh = jnp.where(rows >= k, pltpu.roll(ug, k, 0), 0.0)
            ug = ag * u_sh + ug
            ag = ag * a_sh
        hg = ag * hprev + ug
        h_s[gi * 8:gi * 8 + 8, :] = hg
        hprev = hg[7:8, :]
    hcar[...] = hprev

    rnn = _gelu_tanh(g) * h_s[...]
    rnn_ref[0] = (_rms(rnn) * grnn_ref[...]).astype(BF16)


def _inproj(x, gain, sc, sh, w_in, conv_w, conv_b, wa, ba, wx, bx, lam, grnn):
    B, S, D = x.shape
    tm = min(TM_IN, S)
    row = lambda n: pl.BlockSpec((1, n), lambda b, s: (0, 0))
    per_b = lambda n: pl.BlockSpec((1, 1, n), lambda b, s: (b, 0, 0))
    full = lambda a: pl.BlockSpec(a.shape, lambda b, s: (0,) * a.ndim)
    tok = lambda n: pl.BlockSpec((1, tm, n), lambda b, s: (b, s, 0))
    G = N_KV
    grp = pl.BlockSpec((1, G, tm, HEAD_DIM), lambda b, s: (b, 0, s, 0))
    grpT = pl.BlockSpec((1, G, tm // TK, HEAD_DIM, TK), lambda b, s: (b, 0, s, 0, 0))
    return pl.pallas_call(
        _inproj_kernel,
        grid=(B, S // tm),
        in_specs=[tok(D), row(D), per_b(D), per_b(D), full(w_in), full(conv_w), row(D_RNN),
                  full(wa), row(D_RNN), full(wx), row(D_RNN), row(D_RNN), row(D_RNN)],
        out_specs=[tok(D_RNN),
                   pl.BlockSpec((1, D_ATT, tm), lambda b, s: (b, 0, s)),
                   grp, grpT, grp, grpT, grp, grp,
                   pl.BlockSpec((1, G, GATE_ROWS, tm), lambda b, s: (b, 0, 0, s))],
        out_shape=[jax.ShapeDtypeStruct((B, S, D_RNN), BF16),
                   jax.ShapeDtypeStruct((B, D_ATT, S), BF16),
                   jax.ShapeDtypeStruct((B, G, S, HEAD_DIM), BF16),
                   jax.ShapeDtypeStruct((B, G, S // TK, HEAD_DIM, TK), BF16),
                   jax.ShapeDtypeStruct((B, G, S, HEAD_DIM), BF16),
                   jax.ShapeDtypeStruct((B, G, S // TK, HEAD_DIM, TK), BF16),
                   jax.ShapeDtypeStruct((B, G, S, HEAD_DIM), F32),
                   jax.ShapeDtypeStruct((B, G, S, HEAD_DIM), F32),
                   jax.ShapeDtypeStruct((B, G, GATE_ROWS, S), F32)],
        scratch_shapes=[pltpu.VMEM((tm + 8, D_RNN), F32), pltpu.VMEM((1, D_RNN), F32),
                        pltpu.VMEM((tm, D_RNN), F32), pltpu.VMEM((tm, D_RNN), F32),
                        pltpu.VMEM((tm, D_RNN), F32)],
        compiler_params=pltpu.CompilerParams(dimension_semantics=("arbitrary", "arbitrary"),
                                             vmem_limit_bytes=VMEM_LIMIT),
        name="inproj",
    )(x, gain, sc, sh, w_in, conv_w, conv_b, wa, ba, wx, bx, lam, grnn)


def _compress_kernel(rk_ref, rv_ref, w1k_ref, w2k_ref, pk_ref, w1v_ref, w2v_ref, pv_ref, kc_ref, vc_ref):
    def one(r_ref, w1_ref, w2_ref, pos_ref, o_ref):
        n = r_ref.shape[2] // CMP_STRIDE
        p = jnp.zeros((n, 2 * CMP_HIDDEN), F32)
        for l in range(CMP_STRIDE):
            rows = r_ref[0, 0, pl.ds(l, n, stride=CMP_STRIDE), :].astype(BF16)
            p = p + jnp.dot(rows, w1_ref[l * HEAD_DIM:(l + 1) * HEAD_DIM, :], preferred_element_type=F32)
        posb = jnp.dot(pos_ref[...], w1_ref[...], preferred_element_type=F32)
        bias = posb[0:1, 0:CMP_HIDDEN] + posb[1:2, CMP_HIDDEN:]
        pre = p[:, 0:CMP_HIDDEN] + pltpu.roll(p[:, CMP_HIDDEN:], n - 1, 0) + bias
        hid = _gelu_tanh(pre).astype(BF16)
        o_ref[0, 0] = jnp.dot(hid, w2_ref[...], preferred_element_type=F32).astype(BF16)

    one(rk_ref, w1k_ref, w2k_ref, pk_ref, kc_ref)
    one(rv_ref, w1v_ref, w2v_ref, pv_ref, vc_ref)


def _compress(rk, rv, w1k, w2k, posk, w1v, w2v, posv):
    B, G, S, width = rk.shape
    n = S // CMP_STRIDE
    blk = pl.BlockSpec((1, 1, S, width), lambda b, g: (b, g, 0, 0))
    full = lambda a: pl.BlockSpec(a.shape, lambda b, g: (0,) * a.ndim)
    out = pl.BlockSpec((1, 1, n, HEAD_DIM), lambda b, g: (b, g, 0, 0))
    return pl.pallas_call(
        _compress_kernel,
        grid=(B, G),
        in_specs=[blk, blk, full(w1k), full(w2k), full(posk), full(w1v), full(w2v), full(posv)],
        out_specs=[out, out],
        out_shape=[jax.ShapeDtypeStruct((B, G, n, HEAD_DIM), BF16)] * 2,
        compiler_params=pltpu.CompilerParams(vmem_limit_bytes=VMEM_LIMIT),
        name="compress",
    )(rk, rv, w1k, w2k, posk, w1v, w2v, posv)


def _attn_kernel(qT_ref, ks_ref, vs_ref, kw_ref, vw_ref, kc_ref, vcT_ref, g_ref,
                 o_ref,
                 kall, vall, qall, kca, m_all, acc_all, al_s, sbuf, pbuf, oacc, val_s, rank_s, selbias_s,
                 bias_tbl, cmask,
                 oT_s, st_s, *, n_sel):
    S = ks_ref.shape[2]
    nkt = S // TK
    ncmp = kc_ref.shape[2]
    b = pl.program_id(0)
    g = pl.program_id(1)
    step = pl.program_id(2)
    W4 = REP * TQ
    cmp_per_tile = TQ // CMP_STRIDE

    @pl.when((b == 0) & (g == 0) & (step == 0))
    def _():
        col = lax.broadcasted_iota(I32, (TK, KAUG), 1)
        is_blk = (col == POS_COL) | (col == POS_COL + 2)
        is_off = (col == POS_COL + 1) | (col == POS_COL + 3)
        ones_row = (lax.broadcasted_iota(I32, (VROWS - HEAD_DIM, TK), 0) == 0).astype(BF16)
        for kt in range(nkt):
            key = kt * TK + lax.broadcasted_iota(I32, (TK, KAUG), 0)
            pos = jnp.where(is_blk, (key // SLC_BLOCK).astype(F32),
                            jnp.where(is_off, (key % SLC_BLOCK).astype(F32), 0.0))
            onehot = (col - HEAD_DIM == key // SLC_BLOCK).astype(F32)
            kall[SEL, kt * TK:(kt + 1) * TK, :] = (pos + onehot).astype(BF16)
            kall[WIN, kt * TK:(kt + 1) * TK, :] = pos.astype(BF16)
            vall[SEL, kt, HEAD_DIM:VROWS, :] = ones_row
            vall[WIN, kt, HEAD_DIM:VROWS, :] = ones_row
        ko = lax.broadcasted_iota(I32, (TK, W4), 0)
        to = lax.broadcasted_iota(I32, (TK, W4), 1) % TQ
        bias_tbl[MASK_NONE] = jnp.zeros((TK, W4), F32)
        bias_tbl[MASK_CAUSAL] = jnp.where(ko <= to, 0.0, NEG)
        bias_tbl[MASK_WINDOW_LOW] = jnp.where(ko > to, 0.0, NEG)
        bias_tbl[MASK_ALL] = jnp.full((TK, W4), NEG, F32)
        u = lax.broadcasted_iota(I32, (2 * ncmp, W4), 0)
        tc = lax.broadcasted_iota(I32, (2 * ncmp, W4), 1) % TQ
        cmask[...] = jnp.where(u <= ncmp + ((tc + 1) // CMP_STRIDE) - 2, 0.0, NEG)
        cc = lax.broadcasted_iota(I32, (ncmp, KAUG), 0)
        colc = lax.broadcasted_iota(I32, (ncmp, KAUG), 1)
        kca[...] = jnp.where((colc == POS_COL) | (colc == POS_COL + 2), (cc // 4).astype(F32),
                             jnp.where((colc == POS_COL + 1) | (colc == POS_COL + 3),
                                       (CMP_STRIDE * (cc % 4)).astype(F32), 0.0)).astype(BF16)
        qall[...] = jnp.zeros(qall.shape, BF16)

    lane = lax.broadcasted_iota(I32, (1, W4), 1)
    head = g * REP + lane // TQ
    slope = lax.bitcast_convert_type((126 - head) << 23, F32)

    @pl.when(step == 0)
    def _():
        for kt in range(nkt):
            kall[SEL, kt * TK:(kt + 1) * TK, 0:HEAD_DIM] = ks_ref[0, 0, kt * TK:(kt + 1) * TK, :]
            kall[WIN, kt * TK:(kt + 1) * TK, 0:HEAD_DIM] = kw_ref[0, 0, kt * TK:(kt + 1) * TK, :]
            vall[SEL, kt, 0:HEAD_DIM, :] = vs_ref[0, 0, kt]
            vall[WIN, kt, 0:HEAD_DIM, :] = vw_ref[0, 0, kt]
        kca[:, 0:HEAD_DIM] = kc_ref[0, 0]
        c_hi = (slope * LOG2E).astype(BF16).astype(F32)
        c_lo = slope * LOG2E - c_hi
        r16 = lax.broadcasted_iota(I32, (16, W4), 0)
        alibi = jnp.where(r16 == 0, c_hi * SLC_BLOCK, jnp.where(r16 == 1, c_hi, jnp.where(
            r16 == 2, c_lo * SLC_BLOCK, jnp.where(r16 == 3, c_lo, 0.0)))).astype(BF16)
        for c in range(NCHAIN):
            qall[c, SEL, POS_COL:POS_COL + 16, :] = alibi
            qall[c, WIN, POS_COL:POS_COL + 16, :] = alibi

    chains = [_attn_chain(c, NCHAIN * step + c, qT_ref, kc_ref, vcT_ref, g_ref, o_ref, kall, vall, qall.at[c], kca,
                          m_all.at[c], acc_all.at[c], al_s.at[c], sbuf.at[c], pbuf.at[c], oacc.at[c], val_s.at[c],
                          rank_s.at[c], selbias_s.at[c], bias_tbl, cmask, oT_s.at[c], st_s.at[c], n_sel)
              for c in range(NCHAIN)]
    live = list(chains)
    while live:
        handed = [next(ch, _DONE) for ch in live]
        live = [ch for ch, r in zip(live, handed) if r is not _DONE]
        loops = [r for r in handed if r is not _DONE and r is not None]
        if loops:
            joint = functools.reduce(jnp.minimum, [npair for npair, _, _ in loops])

            def trip(j, group):
                als = [head(j) for _, head, _ in group]
                for (_, _, tail), al in zip(group, als):
                    tail(j, al)

            lax.fori_loop(1, 1 + joint, lambda j, carry: (trip(j, loops), carry)[1], 0)
            for item in loops:
                lax.fori_loop(1 + joint, 1 + item[0], lambda j, carry, item=item: (trip(j, [item]), carry)[1], 0)


_DONE = object()


def _drain(gen):
    try:
        while True:
            next(gen)
    except StopIteration as stop:
        return stop.value


def _attn_chain(c, qi, qT_ref, kc_ref, vcT_ref, g_ref, o_ref, kall, vall, qall, kca, m_all, acc_all, al_s, sbuf,
                pbuf, oacc, val_s, rank_s, selbias_s, bias_tbl, cmask, oT_s, st_s, n_sel):
    S = kall.shape[1]
    ncmp = kc_ref.shape[2]
    nblk = S // SLC_BLOCK
    cmp_per_tile = TQ // CMP_STRIDE
    q0 = qi * TQ
    W4 = REP * TQ
    lane = lax.broadcasted_iota(I32, (1, W4), 1)
    lanes_c = slice(c * TQ, (c + 1) * TQ)

    q = qT_ref[0, :, lanes_c]
    for r in range(REP):
        qr = q[r * HEAD_DIM:(r + 1) * HEAD_DIM, :]
        qall[SEL, 0:HEAD_DIM, r * TQ:(r + 1) * TQ] = qr
        qall[WIN, 0:HEAD_DIM, r * TQ:(r + 1) * TQ] = qr

    gall = g_ref[0, 0, :, lanes_c]
    gates = [gall[br * REP:(br + 1) * REP, :] for br in range(3)]

    def stage_scores(br, tiles_masks):
        col_max = None
        for h, (tile, mask) in enumerate(tiles_masks):
            r0 = pl.multiple_of(tile * TK, TK)
            s = jnp.dot(kall[br, pl.ds(r0, TK), :], qall[br], preferred_element_type=F32)
            if mask is not None:
                s = s + bias_tbl[mask]
            sbuf[h * TK:(h + 1) * TK, :] = s
            c = jnp.max(s, axis=0, keepdims=True)
            col_max = c if col_max is None else jnp.maximum(col_max, c)
            yield
        m_old = m_all[br]
        m_new = jnp.maximum(m_old, col_max)
        m_all[br] = m_new
        return jnp.exp2(m_old - m_new)

    def stage_probs(br, ntile):
        for h in range(ntile):
            rows = slice(h * TK, (h + 1) * TK)
            pbuf[rows, :] = jnp.exp2(sbuf[rows, :] - m_all[br]).astype(BF16)
            yield

    def stage_values(br, tiles, al_row):
        acc = acc_all[br] * al_row
        for h, tile in enumerate(tiles):
            acc = acc + jnp.dot(vall[br, tile], pbuf[h * TK:(h + 1) * TK, :], preferred_element_type=F32)
            if h + 1 < len(tiles):
                yield
        acc_all[br] = acc
        yield

    def start_branch(br, tl, mask1):
        al_s[...] = yield from stage_scores(br, [(tl[0], MASK_CAUSAL), (tl[1], mask1)])

    def pair_items(br, n, tl):
        npair = jnp.maximum(n - 2, 0) // 2

        def head(j):
            al_prev = al_s[...]
            _drain(stage_probs(br, 2))
            al_s[...] = _drain(stage_scores(br, [(tl[2 * j], None), (tl[2 * j + 1], None)]))
            return al_prev

        def tail(j, al_prev):
            _drain(stage_values(br, [tl[2 * j - 2], tl[2 * j - 1]], al_prev))

        return npair, head, tail

    def finish_branch(br, n, tl):
        rest = jnp.maximum(n - 2, 0)
        last = 2 * (rest // 2)
        odd = rest % 2 == 1
        t_single = tl[jnp.where(odd, n - 1, 0)]
        al_prev = al_s[...]
        yield from stage_probs(br, 2)
        al_k = yield from stage_scores(br, [(t_single, jnp.where(odd, MASK_NONE, MASK_ALL))])
        yield from stage_values(br, [tl[last], tl[last + 1]], al_prev)
        yield from stage_probs(br, 1)
        yield from stage_values(br, [t_single], al_k)

    m_all[...] = jnp.full(m_all.shape, NEG, F32)
    acc_all[...] = jnp.zeros(acc_all.shape, F32)
    w1 = jnp.maximum(qi - 2, 0)
    w2 = jnp.maximum(qi - 1, 0)
    al_w0 = yield from stage_scores(
        WIN, [(qi, MASK_CAUSAL),
              (w1, jnp.where(qi >= 2, MASK_WINDOW_LOW, jnp.where(qi >= 1, MASK_NONE, MASK_ALL)))])

    c0 = pl.multiple_of(ncmp - cmp_per_tile * qi, cmp_per_tile)
    sc = jnp.dot(kca[...], qall[WIN], preferred_element_type=F32) + cmask[pl.ds(c0, ncmp), :]
    yield
    e = jnp.exp2(sc - jnp.max(sc, axis=0, keepdims=True))
    tq = q0 + lane % TQ
    p = e * ((1.0 / jnp.sum(e, axis=0, keepdims=True)) * (tq >= CMP_LEN - 1).astype(F32))
    ocT = jnp.dot(vcT_ref[0, 0], p.astype(BF16), preferred_element_type=F32)
    for r in range(REP):
        oacc[:, r * TQ:(r + 1) * TQ] = gates[0][r:r + 1, :] * ocT[:, r * TQ:(r + 1) * TQ]
    yield

    yield from stage_probs(WIN, 2)
    al_w1 = yield from stage_scores(WIN, [(w2, jnp.where(qi >= 2, MASK_NONE, MASK_ALL))])
    yield from stage_values(WIN, [qi, w1], al_w0)

    psum = p[:, 0:TQ]
    for r in range(1, REP):
        psum = psum + p[:, r * TQ:(r + 1) * TQ]
    jj = lax.broadcasted_iota(I32, (nblk, ncmp), 0)
    cc = lax.broadcasted_iota(I32, (nblk, ncmp), 1)
    ovT = ((CMP_STRIDE * cc < SLC_BLOCK * jj + SLC_BLOCK)
           & (CMP_STRIDE * cc + CMP_LEN > SLC_BLOCK * jj)).astype(F32)
    impT = jnp.dot(ovT, psum, preferred_element_type=F32, precision=lax.Precision.HIGHEST)

    j_i = lax.broadcasted_iota(I32, (nblk, TQ), 0)
    t1 = q0 + lax.broadcasted_iota(I32, (nblk, TQ), 1)
    cur = t1 // SLC_BLOCK
    forced = (j_i == 0) | (j_i == cur) | (j_i == cur - 1)
    visible = SLC_BLOCK * j_i <= t1
    val_s[...] = jnp.where(forced, FORCED, jnp.where(visible, impT, NEG))
    yield
    ngrp = nblk // 8
    rank_s[...] = jnp.zeros(rank_s.shape, F32)
    j8 = lax.broadcasted_iota(I32, (8, TQ), 0)
    for ib in range(ngrp):
        @pl.when(8 * ib * SLC_BLOCK < q0 + TQ)
        def _(ib=ib):
            vals = [val_s[8 * jb:8 * jb + 8, :] for jb in range(ngrp)]
            ranks = [rank_s[8 * jb:8 * jb + 8, :] for jb in range(ngrp)]
            for i in range(8 * ib, 8 * ib + 8):
                row = jnp.broadcast_to(val_s[i:i + 1, :], (8, TQ))
                for jb in range(ngrp):
                    if jb > ib:
                        hit = row >= vals[jb]
                    elif jb < ib:
                        hit = row > vals[jb]
                    else:
                        hit = jnp.where(j8 > i - 8 * jb, jnp.where(row >= vals[jb], 1.0, 0.0),
                                        jnp.where(row > vals[jb], 1.0, 0.0)) > 0.5
                    ranks[jb] = ranks[jb] + jnp.where(hit, 1.0, 0.0)
            for jb in range(ngrp):
                rank_s[8 * jb:8 * jb + 8, :] = ranks[jb]
    yield
    ranks = [rank_s[8 * jb:8 * jb + 8, :] for jb in range(ngrp)]
    blocks_per_tile = TK // SLC_BLOCK
    st_s[0] = qi
    n_selt = jnp.int32(1)
    for jb in range(ngrp):
        chosen = ranks[jb] < n_sel
        selb = jnp.where(chosen, 0.0, NEG)
        for r in range(REP):
            selbias_s[8 * jb:8 * jb + 8, r * TQ:(r + 1) * TQ] = selb
        any_q = jnp.max(jnp.where(chosen, 1.0, 0.0), axis=1, keepdims=True)
        for hh in range(8 // blocks_per_tile):
            kt = (8 * jb) // blocks_per_tile + hh
            hit = jnp.max(any_q[hh * blocks_per_tile:(hh + 1) * blocks_per_tile, :]) > 0.5
            st_s[n_selt] = kt
            n_selt = n_selt + jnp.where(hit & (kt < qi), 1, 0)
    qall[SEL, HEAD_DIM:HEAD_DIM + nblk, :] = selbias_s[...].astype(BF16)
    yield

    yield from stage_probs(WIN, 1)
    yield from stage_values(WIN, [w2], al_w1)
    yield from start_branch(SEL, st_s, jnp.where(n_selt >= 2, MASK_NONE, MASK_ALL))
    yield pair_items(SEL, n_selt, st_s)
    yield from finish_branch(SEL, n_selt, st_s)
    yield

    def normalised(br):
        acc = acc_all[br]
        return acc[0:HEAD_DIM, :] / acc[HEAD_DIM:HEAD_DIM + 1, :]

    o_sel = normalised(SEL)
    o_win = normalised(WIN)
    for r in range(REP):
        lanes = slice(r * TQ, (r + 1) * TQ)
        oT_s[r * HEAD_DIM:(r + 1) * HEAD_DIM, :] = (oacc[:, lanes] + gates[1][r:r + 1, :] * o_sel[:, lanes]
                                                    + gates[2][r:r + 1, :] * o_win[:, lanes])
    o_ref[0, lanes_c, :] = oT_s[...].T.astype(BF16)


def _attention(qT, ks, vs, kw, vw, kc, vcT, gT):
    B, _, S = qT.shape
    G = N_KV
    nkt = S // TK
    ncmp = kc.shape[2]
    nblk = S // SLC_BLOCK
    n_sel = min(SLC_TOPK, nblk)
    W4 = REP * TQ
    res4 = lambda a: pl.BlockSpec((1, 1) + a.shape[2:], lambda b, g, i: (b, g) + (0,) * (a.ndim - 2))
    return pl.pallas_call(
        functools.partial(_attn_kernel, n_sel=n_sel),
        grid=(B, G, S // (NCHAIN * TQ)),
        in_specs=[pl.BlockSpec((1, REP * HEAD_DIM, NCHAIN * TQ), lambda b, g, i: (b, g, i)),
                  res4(ks), res4(vs), res4(kw), res4(vw), res4(kc), res4(vcT),
                  pl.BlockSpec((1, 1, GATE_ROWS, NCHAIN * TQ), lambda b, g, i: (b, g, 0, i))],
        out_specs=pl.BlockSpec((1, NCHAIN * TQ, REP * HEAD_DIM), lambda b, g, i: (b, i, g)),
        out_shape=jax.ShapeDtypeStruct((B, S, D_ATT), BF16),
        scratch_shapes=[pltpu.VMEM((2, S, KAUG), BF16),
                        pltpu.VMEM((2, nkt, VROWS, TK), BF16),
                        pltpu.VMEM((NCHAIN, 2, KAUG, W4), BF16),
                        pltpu.VMEM((ncmp, KAUG), BF16),
                        pltpu.VMEM((NCHAIN, 2, 1, W4), F32),
                        pltpu.VMEM((NCHAIN, 2, VROWS, W4), F32),
                        pltpu.VMEM((NCHAIN, 1, W4), F32),
                        pltpu.VMEM((NCHAIN, 2 * TK, W4), F32),
                        pltpu.VMEM((NCHAIN, 2 * TK, W4), BF16),
                        pltpu.VMEM((NCHAIN, HEAD_DIM, W4), F32),
                        pltpu.VMEM((NCHAIN, nblk, TQ), F32),
                        pltpu.VMEM((NCHAIN, nblk, TQ), F32),
                        pltpu.VMEM((NCHAIN, nblk, W4), F32),
                        pltpu.VMEM((4, TK, W4), F32),
                        pltpu.VMEM((2 * ncmp, W4), F32),
                        pltpu.VMEM((NCHAIN, REP * HEAD_DIM, TQ), F32),
                        pltpu.SMEM((NCHAIN, nkt + 2), I32)],
        compiler_params=pltpu.CompilerParams(dimension_semantics=("arbitrary", "arbitrary", "arbitrary"),
                                             vmem_limit_bytes=VMEM_LIMIT),
        name="attn",
    )(qT, ks, vs, kw, vw, kc, vcT, gT)


def _outmlp_kernel(x_ref, rnn_ref, att_ref, gatt_ref, wo_ref, gpost_ref, g1_ref, gpre_ref, sc2_ref, sh2_ref,
                   w1_ref, w2_ref, gpost2_ref, g2_ref, o_ref):
    tm = x_ref.shape[1]
    halves = [slice(0, tm // 2), slice(tm // 2, tm)]
    x1s, h2s = [], []
    for rows in halves:
        att_n = (_rms(att_ref[0, rows, :].astype(F32)) * gatt_ref[...]).astype(BF16)
        y = (jnp.dot(rnn_ref[0, rows, :], wo_ref[0:D_RNN, :], preferred_element_type=F32)
             + jnp.dot(att_n, wo_ref[D_RNN:, :], preferred_element_type=F32))
        x1 = x_ref[0, rows, :] + (1.0 + g1_ref[0]) * (_rms(y) * gpost_ref[...])
        x1s.append(x1)
        h2s.append((_rms(x1) * (gpre_ref[...] * (1.0 + sc2_ref[0])) + sh2_ref[0]).astype(BF16))
    fc = 1024
    for rows, x1, h2 in zip(halves, x1s, h2s):
        ff = jnp.zeros(x1.shape, F32)
        for c in range(D_FF // fc):
            hid = jnp.maximum(jnp.dot(h2, w1_ref[:, c * fc:(c + 1) * fc], preferred_element_type=F32), 0.0)
            ff = ff + jnp.dot((hid * hid).astype(BF16), w2_ref[c * fc:(c + 1) * fc, :],
                              preferred_element_type=F32)
        o_ref[0, rows, :] = x1 + (1.0 + g2_ref[0]) * (_rms(ff) * gpost2_ref[...])


def _outmlp(x, rnn_n, att, gatt, wo, gpost, g1, gpre, sc2, sh2, w1, w2, gpost2, g2):
    B, S, D = x.shape
    tm = min(TM_OUT, S)
    row = lambda n: pl.BlockSpec((1, n), lambda b, s: (0, 0))
    per_b = lambda n: pl.BlockSpec((1, 1, n), lambda b, s: (b, 0, 0))
    const = lambda a: pl.BlockSpec(a.shape, lambda b, s: (0,) * a.ndim, pipeline_mode=pl.Buffered(1))
    tok = lambda n: pl.BlockSpec((1, tm, n), lambda b, s: (b, s, 0))
    return pl.pallas_call(
        _outmlp_kernel,
        grid=(B, S // tm),
        in_specs=[tok(D), tok(D_RNN), tok(D_ATT), row(D_ATT), const(wo), row(D), per_b(D), row(D),
                  per_b(D), per_b(D), const(w1), const(w2), row(D), per_b(D)],
        out_specs=tok(D),
        out_shape=jax.ShapeDtypeStruct((B, S, D), F32),
        compiler_params=pltpu.CompilerParams(dimension_semantics=("arbitrary", "arbitrary"),
                                             vmem_limit_bytes=VMEM_LIMIT),
        name="outmlp",
    )(x, rnn_n, att, gatt, wo, gpost, g1, gpre, sc2, sh2, w1, w2, gpost2, g2)


def _block_diag(w):
    n, k, _ = w.shape
    return jnp.einsum('nij,nm->nimj', w, jnp.eye(n, dtype=w.dtype)).reshape(n * k, n * k)


def _layer(x, c, ada_w, ada_b, pre_norm_mix, w_in, conv_w, conv_b, lru_wa, lru_ba, lru_wx, lru_bx, lru_lambda,
           cmp_pos_k, cmp_w1_k, cmp_w2_k, cmp_pos_v, cmp_w1_v, cmp_w2_v, norm_rnn_out, norm_att_out, w_out,
           post_norm_mix, pre_norm_mlp, w_ff1, w_ff2, post_norm_mlp):
    B, S, D = x.shape
    G = N_KV
    row = lambda v: v.reshape(1, -1)

    mod = _ada(c, ada_w, ada_b)
    sh1, sc1, g1, sh2, sc2, g2 = [m.reshape(B, 1, D) for m in jnp.split(mod, 6, axis=-1)]

    gate_cols = [OFF_GATE + br * N_HEADS + g * REP + r for g in range(G) for br in range(3) for r in range(REP)]
    w_gate = w_in[:, jnp.asarray(gate_cols)].reshape(D, G, 3 * REP)
    w_gate = jnp.pad(w_gate, ((0, 0), (0, 0), (0, GATE_ROWS - 3 * REP))).reshape(D, G * GATE_ROWS)
    w_in_p = jnp.concatenate([w_in[:, :OFF_GATE], jnp.pad(w_gate, ((0, 0), (0, GATE_PAD - G * GATE_ROWS)))],
                             axis=1).astype(BF16)
    wa = _block_diag(lru_wa).astype(BF16)
    wx = _block_diag(lru_wx).astype(BF16)
    half = CMP_LEN // 2 * HEAD_DIM

    def w1_cat(w1):
        return jnp.concatenate([w1[:half], w1[half:]], axis=1).astype(BF16)

    def pos_rows(pos):
        return jnp.pad(pos.reshape(2, half), ((0, 14), (0, 0))).astype(BF16)

    rnn_n, qT, ks, vsT, kw, vwT, kc_in, vc_in, gT = _inproj(
        x, row(pre_norm_mix), sc1, sh1, w_in_p, conv_w, row(conv_b), wa, row(lru_ba), wx, row(lru_bx),
        row(lru_lambda), row(norm_rnn_out))
    kc, vc = _compress(kc_in, vc_in, w1_cat(cmp_w1_k), cmp_w2_k.astype(BF16), pos_rows(cmp_pos_k),
                       w1_cat(cmp_w1_v), cmp_w2_v.astype(BF16), pos_rows(cmp_pos_v))
    att = _attention(qT, ks, vsT, kw, vwT, kc, vc.transpose(0, 1, 3, 2), gT)

    return _outmlp(x, rnn_n, att, row(norm_att_out), w_out.astype(BF16), row(post_norm_mix), g1,
                   row(pre_norm_mlp), sc2, sh2, w_ff1.astype(BF16), w_ff2.astype(BF16), row(post_norm_mlp), g2)


def kernel(x, c, ada_w, ada_b, pre_norm_mix, w_in, conv_w, conv_b, lru_wa, lru_ba, lru_wx, lru_bx, lru_lambda,
           cmp_pos_k, cmp_w1_k, cmp_w2_k, cmp_pos_v, cmp_w1_v, cmp_w2_v, norm_rnn_out, norm_att_out, w_out,
           post_norm_mix, pre_norm_mlp, w_ff1, w_ff2, post_norm_mlp):
    for l in range(ada_w.shape[0]):
        x = _layer(x, c, ada_w[l], ada_b[l], pre_norm_mix[l], w_in[l], conv_w[l], conv_b[l], lru_wa[l], lru_ba[l],
                   lru_wx[l], lru_bx[l], lru_lambda[l], cmp_pos_k[l], cmp_w1_k[l], cmp_w2_k[l], cmp_pos_v[l],
                   cmp_w1_v[l], cmp_w2_v[l], norm_rnn_out[l], norm_att_out[l], w_out[l], post_norm_mix[l],
                   pre_norm_mlp[l], w_ff1[l], w_ff2[l], post_norm_mlp[l])
    return x
```

```python
import functools

import jax
import jax.numpy as jnp
from jax import lax
from jax.experimental import pallas as pl
from jax.experimental.pallas import tpu as pltpu

F32 = jnp.float32
BF16 = jnp.bfloat16
I32 = jnp.int32

D_MODEL = 1024
D_RNN = 512
RNN_BLOCKS = 8
CONV_WIDTH = 4
LRU_C = 8.0
N_HEADS = 8
HEAD_DIM = 64
N_KV = 2
REP = N_HEADS // N_KV
D_ATT = N_HEADS * HEAD_DIM
CMP_LEN = 32
CMP_STRIDE = 16
CMP_HIDDEN = 256
SLC_BLOCK = 64
SLC_TOPK = 16
WINDOW = 512
D_FF = 4 * D_MODEL
EPS = 1e-6
NEG = -1e30
FORCED = 1e4
LOG2E = 1.4426950408889634

KV_COLS = 6 * N_KV * HEAD_DIM
N_GATE = 3 * N_HEADS
GATE_PAD = 128
GATE_ROWS = 16
OFF_Q = 2 * D_RNN
OFF_KV = OFF_Q + D_ATT
OFF_GATE = OFF_KV + KV_COLS
D_IN_PAD = OFF_GATE + GATE_PAD

TM_IN = 1024
TM_OUT = 1024
OUT_SUBTILES = 4
TQ = 256
TK = 256
NCHAIN = 4
KAUG = 256
VROWS = 80
VMEM_LIMIT = 56 * 1024 * 1024

SEL, WIN = 0, 1
MASK_NONE, MASK_CAUSAL, MASK_WINDOW_LOW, MASK_ALL = 0, 1, 2, 3
POS_COL = 2 * HEAD_DIM


def _gelu_tanh(x):
    return 0.5 * x * (1.0 + jnp.tanh(0.7978845608028654 * (x + 0.044715 * (x * x * x))))


def _sigmoid(x):
    return 0.5 * jnp.tanh(0.5 * x) + 0.5


def _rms(x):
    return x * lax.rsqrt(jnp.mean(x * x, axis=-1, keepdims=True) + EPS)


def _ada_kernel(c_ref, w_ref, b_ref, o_ref):
    c = c_ref[...]
    a = c * jax.nn.sigmoid(c)
    o_ref[...] = jnp.dot(a, w_ref[...], preferred_element_type=F32,
                         precision=lax.Precision.HIGHEST) + b_ref[...]


def _ada(c, w, b):
    B, D = c.shape
    N = w.shape[1]
    tn = 1024
    return pl.pallas_call(
        _ada_kernel,
        grid=(N // tn,),
        in_specs=[pl.BlockSpec((B, D), lambda j: (0, 0)),
                  pl.BlockSpec((D, tn), lambda j: (0, j)),
                  pl.BlockSpec((1, tn), lambda j: (0, j))],
        out_specs=pl.BlockSpec((B, tn), lambda j: (0, j)),
        out_shape=jax.ShapeDtypeStruct((B, N), F32),
        name="ada",
    )(c, w, b.reshape(1, N))


def _inproj_kernel(x_ref, gain_ref, sc_ref, sh_ref, w_ref, cw_ref, cb_ref, wa_ref, ba_ref, wx_ref, bx_ref,
                   lam_ref, grnn_ref,
                   rnn_ref, qT_ref, ks_ref, vs_ref, kw_ref, vw_ref, kc_ref, vc_ref, g_ref,
                   xbuf, hcar, a_s, u_s, h_s):
    tm = x_ref.shape[1]
    hd = HEAD_DIM

    @pl.when(pl.program_id(1) == 0)
    def _():
        xbuf[0:8, :] = jnp.zeros((8, D_RNN), F32)
        hcar[...] = jnp.zeros((1, D_RNN), F32)

    x = x_ref[0]
    h = _rms(x) * (gain_ref[...] * (1.0 + sc_ref[0])) + sh_ref[0]
    hb = h.astype(BF16)

    xr = jnp.dot(hb, w_ref[:, D_RNN:OFF_Q], preferred_element_type=F32)
    xbuf[8:8 + tm, :] = xr
    y = (cw_ref[3:4, :] * xr + cw_ref[2:3, :] * xbuf[7:7 + tm, :]
         + cw_ref[1:2, :] * xbuf[6:6 + tm, :] + cw_ref[0:1, :] * xbuf[5:5 + tm, :]) + cb_ref[...]
    xbuf[0:8, :] = xbuf[tm:tm + 8, :]

    yb = y.astype(BF16)
    r = _sigmoid(jnp.dot(yb, wa_ref[...], preferred_element_type=F32) + ba_ref[...])
    i = _sigmoid(jnp.dot(yb, wx_ref[...], preferred_element_type=F32) + bx_ref[...])
    nl = -lam_ref[...]
    softplus = jnp.maximum(nl, 0.0) + jnp.log(1.0 + jnp.exp(-jnp.abs(nl)))
    a = jnp.exp((-LRU_C) * r * softplus)
    a_s[...] = a
    u_s[...] = jnp.sqrt(1.0 - a * a) * (i * y)

    qT_ref[0] = (jnp.dot(hb, w_ref[:, OFF_Q:OFF_KV], preferred_element_type=F32)
                 * (HEAD_DIM ** -0.5 * LOG2E)).T.astype(BF16)
    kv = jnp.dot(hb, w_ref[:, OFF_KV:OFF_GATE], preferred_element_type=F32)
    width = N_KV * hd
    vsT = kv[:, 3 * width:4 * width].T
    vwT = kv[:, 5 * width:6 * width].T
    for gi in range(N_KV):
        kc_ref[0, gi] = kv[:, gi * hd:(gi + 1) * hd]
        vc_ref[0, gi] = kv[:, width + gi * hd:width + (gi + 1) * hd]
        ks_ref[0, gi] = kv[:, 2 * width + gi * hd:2 * width + (gi + 1) * hd].astype(BF16)
        kw_ref[0, gi] = kv[:, 4 * width + gi * hd:4 * width + (gi + 1) * hd].astype(BF16)
        for j in range(tm // TK):
            vs_ref[0, gi, j] = vsT[gi * hd:(gi + 1) * hd, j * TK:(j + 1) * TK].astype(BF16)
            vw_ref[0, gi, j] = vwT[gi * hd:(gi + 1) * hd, j * TK:(j + 1) * TK].astype(BF16)
    gates = _sigmoid(jnp.dot(hb, w_ref[:, OFF_GATE:D_IN_PAD], preferred_element_type=F32)).T
    for gi in range(N_KV):
        g_ref[0, gi] = gates[gi * GATE_ROWS:(gi + 1) * GATE_ROWS, :]
    g = jnp.dot(hb, w_ref[:, 0:D_RNN], preferred_element_type=F32)


    rows = lax.broadcasted_iota(I32, (8, D_RNN), 0)
    hprev = hcar[...]
    for gi in range(tm // 8):
        ag = a_s[gi * 8:gi * 8 + 8, :]
        ug = u_s[gi * 8:gi * 8 + 8, :]
        for k in (1, 2, 4):
            a_sh = jnp.where(rows >= k, pltpu.roll(ag, k, 0), 1.0)
            u_sh = jnp.where(rows >= k, pltpu.roll(ug, k, 0), 0.0)
            ug = ag * u_sh + ug
            ag = ag * a_sh
        hg = ag * hprev + ug
        h_s[gi * 8:gi * 8 + 8, :] = hg
        hprev = hg[7:8, :]
    hcar[...] = hprev

    rnn = _gelu_tanh(g) * h_s[...]
    rnn_ref[0] = (_rms(rnn) * grnn_ref[...]).astype(BF16)


def _inproj(x, gain, sc, sh, w_in, conv_w, conv_b, wa, ba, wx, bx, lam, grnn):
    B, S, D = x.shape
    tm = min(TM_IN, S)
    row = lambda n: pl.BlockSpec((1, n), lambda b, s: (0, 0))
    per_b = lambda n: pl.BlockSpec((1, 1, n), lambda b, s: (b, 0, 0))
    full = lambda a: pl.BlockSpec(a.shape, lambda b, s: (0,) * a.ndim)
    tok = lambda n: pl.BlockSpec((1, tm, n), lambda b, s: (b, s, 0))
    G = N_KV
    grp = pl.BlockSpec((1, G, tm, HEAD_DIM), lambda b, s: (b, 0, s, 0))
    grpT = pl.BlockSpec((1, G, tm // TK, HEAD_DIM, TK), lambda b, s: (b, 0, s, 0, 0))
    return pl.pallas_call(
        _inproj_kernel,
        grid=(B, S // tm),
        in_specs=[tok(D), row(D), per_b(D), per_b(D), full(w_in), full(conv_w), row(D_RNN),
                  full(wa), row(D_RNN), full(wx), row(D_RNN), row(D_RNN), row(D_RNN)],
        out_specs=[tok(D_RNN),
                   pl.BlockSpec((1, D_ATT, tm), lambda b, s: (b, 0, s)),
                   grp, grpT, grp, grpT, grp, grp,
                   pl.BlockSpec((1, G, GATE_ROWS, tm), lambda b, s: (b, 0, 0, s))],
        out_shape=[jax.ShapeDtypeStruct((B, S, D_RNN), BF16),
                   jax.ShapeDtypeStruct((B, D_ATT, S), BF16),
                   jax.ShapeDtypeStruct((B, G, S, HEAD_DIM), BF16),
                   jax.ShapeDtypeStruct((B, G, S // TK, HEAD_DIM, TK), BF16),
                   jax.ShapeDtypeStruct((B, G, S, HEAD_DIM), BF16),
                   jax.ShapeDtypeStruct((B, G, S // TK, HEAD_DIM, TK), BF16),
                   jax.ShapeDtypeStruct((B, G, S, HEAD_DIM), F32),
                   jax.ShapeDtypeStruct((B, G, S, HEAD_DIM), F32),
                   jax.ShapeDtypeStruct((B, G, GATE_ROWS, S), F32)],
        scratch_shapes=[pltpu.VMEM((tm + 8, D_RNN), F32), pltpu.VMEM((1, D_RNN), F32),
                        pltpu.VMEM((tm, D_RNN), F32), pltpu.VMEM((tm, D_RNN), F32),
                        pltpu.VMEM((tm, D_RNN), F32)],
        compiler_params=pltpu.CompilerParams(dimension_semantics=("arbitrary", "arbitrary"),
                                             vmem_limit_bytes=VMEM_LIMIT),
        name="inproj",
    )(x, gain, sc, sh, w_in, conv_w, conv_b, wa, ba, wx, bx, lam, grnn)


def _compress_kernel(rk_ref, rv_ref, w1k_ref, w2k_ref, pk_ref, w1v_ref, w2v_ref, pv_ref, kc_ref, vc_ref):
    def one(r_ref, w1_ref, w2_ref, pos_ref, o_ref):
        n = r_ref.shape[2] // CMP_STRIDE
        rows = jnp.concatenate([r_ref[0, 0, pl.ds(l, n, stride=CMP_STRIDE), :] for l in range(CMP_STRIDE)], axis=1)
        p = jnp.dot(rows.astype(BF16), w1_ref[...], preferred_element_type=F32)
        posb = jnp.dot(pos_ref[...], w1_ref[...], preferred_element_type=F32)
        bias = posb[0:1, 0:CMP_HIDDEN] + posb[1:2, CMP_HIDDEN:]
        pre = p[:, 0:CMP_HIDDEN] + pltpu.roll(p[:, CMP_HIDDEN:], n - 1, 0) + bias
        hid = _gelu_tanh(pre).astype(BF16)
        o_ref[0, 0] = jnp.dot(hid, w2_ref[...], preferred_element_type=F32).astype(BF16)

    one(rk_ref, w1k_ref, w2k_ref, pk_ref, kc_ref)
    one(rv_ref, w1v_ref, w2v_ref, pv_ref, vc_ref)


def _compress(rk, rv, w1k, w2k, posk, w1v, w2v, posv):
    B, G, S, width = rk.shape
    n = S // CMP_STRIDE
    blk = pl.BlockSpec((1, 1, S, width), lambda b, g: (b, g, 0, 0))
    full = lambda a: pl.BlockSpec(a.shape, lambda b, g: (0,) * a.ndim)
    out = pl.BlockSpec((1, 1, n, HEAD_DIM), lambda b, g: (b, g, 0, 0))
    return pl.pallas_call(
        _compress_kernel,
        grid=(B, G),
        in_specs=[blk, blk, full(w1k), full(w2k), full(posk), full(w1v), full(w2v), full(posv)],
        out_specs=[out, out],
        out_shape=[jax.ShapeDtypeStruct((B, G, n, HEAD_DIM), BF16)] * 2,
        compiler_params=pltpu.CompilerParams(vmem_limit_bytes=VMEM_LIMIT),
        name="compress",
    )(rk, rv, w1k, w2k, posk, w1v, w2v, posv)


def _attn_kernel(qT_ref, ks_ref, vs_ref, kw_ref, vw_ref, kc_ref, vcT_ref, g_ref,
                 o_ref,
                 kall, vall, qall, kca, m_all, acc_all, al_s, sbuf, pbuf, oacc, val_s, rank_s, selbias_s,
                 bias_tbl, cmask,
                 oT_s, st_s, *, n_sel):
    S = ks_ref.shape[2]
    nkt = S // TK
    ncmp = kc_ref.shape[2]
    b = pl.program_id(0)
    g = pl.program_id(1)
    step = pl.program_id(2)
    W4 = REP * TQ
    cmp_per_tile = TQ // CMP_STRIDE

    @pl.when((b == 0) & (g == 0) & (step == 0))
    def _():
        col = lax.broadcasted_iota(I32, (TK, KAUG), 1)
        is_blk = (col == POS_COL) | (col == POS_COL + 2)
        is_off = (col == POS_COL + 1) | (col == POS_COL + 3)
        ones_row = (lax.broadcasted_iota(I32, (VROWS - HEAD_DIM, TK), 0) == 0).astype(BF16)
        for kt in range(nkt):
            key = kt * TK + lax.broadcasted_iota(I32, (TK, KAUG), 0)
            pos = jnp.where(is_blk, (key // SLC_BLOCK).astype(F32),
                            jnp.where(is_off, (key % SLC_BLOCK).astype(F32), 0.0))
            onehot = (col - HEAD_DIM == key // SLC_BLOCK).astype(F32)
            kall[SEL, kt * TK:(kt + 1) * TK, :] = (pos + onehot).astype(BF16)
            kall[WIN, kt * TK:(kt + 1) * TK, :] = pos.astype(BF16)
            vall[SEL, kt, HEAD_DIM:VROWS, :] = ones_row
            vall[WIN, kt, HEAD_DIM:VROWS, :] = ones_row
        ko = lax.broadcasted_iota(I32, (TK, W4), 0)
        to = lax.broadcasted_iota(I32, (TK, W4), 1) % TQ
        bias_tbl[MASK_NONE] = jnp.zeros((TK, W4), F32)
        bias_tbl[MASK_CAUSAL] = jnp.where(ko <= to, 0.0, NEG)
        bias_tbl[MASK_WINDOW_LOW] = jnp.where(ko > to, 0.0, NEG)
        bias_tbl[MASK_ALL] = jnp.full((TK, W4), NEG, F32)
        u = lax.broadcasted_iota(I32, (2 * ncmp, W4), 0)
        tc = lax.broadcasted_iota(I32, (2 * ncmp, W4), 1) % TQ
        cmask[...] = jnp.where(u <= ncmp + ((tc + 1) // CMP_STRIDE) - 2, 0.0, NEG)
        cc = lax.broadcasted_iota(I32, (ncmp, KAUG), 0)
        colc = lax.broadcasted_iota(I32, (ncmp, KAUG), 1)
        kca[...] = jnp.where((colc == POS_COL) | (colc == POS_COL + 2), (cc // 4).astype(F32),
                             jnp.where((colc == POS_COL + 1) | (colc == POS_COL + 3),
                                       (CMP_STRIDE * (cc % 4)).astype(F32), 0.0)).astype(BF16)
        qall[...] = jnp.zeros(qall.shape, BF16)

    lane = lax.broadcasted_iota(I32, (1, W4), 1)
    head = g * REP + lane // TQ
    slope = lax.bitcast_convert_type((126 - head) << 23, F32)

    @pl.when(step == 0)
    def _():
        for kt in range(nkt):
            kall[SEL, kt * TK:(kt + 1) * TK, 0:HEAD_DIM] = ks_ref[0, 0, kt * TK:(kt + 1) * TK, :]
            kall[WIN, kt * TK:(kt + 1) * TK, 0:HEAD_DIM] = kw_ref[0, 0, kt * TK:(kt + 1) * TK, :]
            vall[SEL, kt, 0:HEAD_DIM, :] = vs_ref[0, 0, kt]
            vall[WIN, kt, 0:HEAD_DIM, :] = vw_ref[0, 0, kt]
        kca[:, 0:HEAD_DIM] = kc_ref[0, 0]
        c_hi = (slope * LOG2E).astype(BF16).astype(F32)
        c_lo = slope * LOG2E - c_hi
        r16 = lax.broadcasted_iota(I32, (16, W4), 0)
        alibi = jnp.where(r16 == 0, c_hi * SLC_BLOCK, jnp.where(r16 == 1, c_hi, jnp.where(
            r16 == 2, c_lo * SLC_BLOCK, jnp.where(r16 == 3, c_lo, 0.0)))).astype(BF16)
        for c in range(NCHAIN):
            qall[c, SEL, POS_COL:POS_COL + 16, :] = alibi
            qall[c, WIN, POS_COL:POS_COL + 16, :] = alibi

    chains = [_attn_chain(c, NCHAIN * step + c, qT_ref, kc_ref, vcT_ref, g_ref, o_ref, kall, vall, qall.at[c], kca,
                          m_all.at[c], acc_all.at[c], al_s.at[c], sbuf.at[c], pbuf.at[c], oacc.at[c], val_s.at[c],
                          rank_s.at[c], selbias_s.at[c], bias_tbl, cmask, oT_s.at[c], st_s.at[c], n_sel)
              for c in range(NCHAIN)]
    live = list(chains)
    while live:
        handed = [next(ch, _DONE) for ch in live]
        live = [ch for ch, r in zip(live, handed) if r is not _DONE]
        loops = [r for r in handed if r is not _DONE and r is not None]
        if loops:
            joint = functools.reduce(jnp.minimum, [npair for npair, _, _ in loops])

            def trip(j, group):
                als = [head(j) for _, head, _ in group]
                for (_, _, tail), al in zip(group, als):
                    tail(j, al)

            lax.fori_loop(1, 1 + joint, lambda j, carry: (trip(j, loops), carry)[1], 0)
            for item in loops:
                lax.fori_loop(1 + joint, 1 + item[0], lambda j, carry, item=item: (trip(j, [item]), carry)[1], 0)


_DONE = object()


def _drain(gen):
    try:
        while True:
            next(gen)
    except StopIteration as stop:
        return stop.value


def _attn_chain(c, qi, qT_ref, kc_ref, vcT_ref, g_ref, o_ref, kall, vall, qall, kca, m_all, acc_all, al_s, sbuf,
                pbuf, oacc, val_s, rank_s, selbias_s, bias_tbl, cmask, oT_s, st_s, n_sel):
    S = kall.shape[1]
    ncmp = kc_ref.shape[2]
    nblk = S // SLC_BLOCK
    cmp_per_tile = TQ // CMP_STRIDE
    q0 = qi * TQ
    W4 = REP * TQ
    lane = lax.broadcasted_iota(I32, (1, W4), 1)
    lanes_c = slice(c * TQ, (c + 1) * TQ)

    q = qT_ref[0, :, lanes_c]
    for r in range(REP):
        qr = q[r * HEAD_DIM:(r + 1) * HEAD_DIM, :]
        qall[SEL, 0:HEAD_DIM, r * TQ:(r + 1) * TQ] = qr
        qall[WIN, 0:HEAD_DIM, r * TQ:(r + 1) * TQ] = qr

    gall = g_ref[0, 0, :, lanes_c]
    gates = [gall[br * REP:(br + 1) * REP, :] for br in range(3)]

    def stage_scores(br, tiles_masks):
        col_max = None
        for h, (tile, mask) in enumerate(tiles_masks):
            r0 = pl.multiple_of(tile * TK, TK)
            s = jnp.dot(kall[br, pl.ds(r0, TK), :], qall[br], preferred_element_type=F32)
            if mask is not None:
                s = s + bias_tbl[mask]
            sbuf[h * TK:(h + 1) * TK, :] = s
            c = jnp.max(s, axis=0, keepdims=True)
            col_max = c if col_max is None else jnp.maximum(col_max, c)
            yield
        m_old = m_all[br]
        m_new = jnp.maximum(m_old, col_max)
        m_all[br] = m_new
        return jnp.exp2(m_old - m_new)

    def stage_probs(br, ntile):
        for h in range(ntile):
            rows = slice(h * TK, (h + 1) * TK)
            pbuf[rows, :] = jnp.exp2(sbuf[rows, :] - m_all[br]).astype(BF16)
            yield

    def stage_values(br, tiles, al_row):
        acc = acc_all[br] * al_row
        for h, tile in enumerate(tiles):
            acc = acc + jnp.dot(vall[br, tile], pbuf[h * TK:(h + 1) * TK, :], preferred_element_type=F32)
            if h + 1 < len(tiles):
                yield
        acc_all[br] = acc
        yield

    def start_branch(br, tl, mask1):
        al_s[...] = yield from stage_scores(br, [(tl[0], MASK_CAUSAL), (tl[1], mask1)])

    def pair_items(br, n, tl):
        npair = jnp.maximum(n - 2, 0) // 2

        def head(j):
            al_prev = al_s[...]
            _drain(stage_probs(br, 2))
            al_s[...] = _drain(stage_scores(br, [(tl[2 * j], None), (tl[2 * j + 1], None)]))
            return al_prev

        def tail(j, al_prev):
            _drain(stage_values(br, [tl[2 * j - 2], tl[2 * j - 1]], al_prev))

        return npair, head, tail

    def finish_branch(br, n, tl):
        rest = jnp.maximum(n - 2, 0)
        last = 2 * (rest // 2)
        odd = rest % 2 == 1
        t_single = tl[jnp.where(odd, n - 1, 0)]
        al_prev = al_s[...]
        yield from stage_probs(br, 2)
        al_k = yield from stage_scores(br, [(t_single, jnp.where(odd, MASK_NONE, MASK_ALL))])
        yield from stage_values(br, [tl[last], tl[last + 1]], al_prev)
        yield from stage_probs(br, 1)
        yield from stage_values(br, [t_single], al_k)

    m_all[...] = jnp.full(m_all.shape, NEG, F32)
    acc_all[...] = jnp.zeros(acc_all.shape, F32)
    w1 = jnp.maximum(qi - 2, 0)
    w2 = jnp.maximum(qi - 1, 0)
    al_w0 = yield from stage_scores(
        WIN, [(qi, MASK_CAUSAL),
              (w1, jnp.where(qi >= 2, MASK_WINDOW_LOW, jnp.where(qi >= 1, MASK_NONE, MASK_ALL)))])

    c0 = pl.multiple_of(ncmp - cmp_per_tile * qi, cmp_per_tile)
    sc = jnp.dot(kca[...], qall[WIN], preferred_element_type=F32) + cmask[pl.ds(c0, ncmp), :]
    yield
    e = jnp.exp2(sc - jnp.max(sc, axis=0, keepdims=True))
    tq = q0 + lane % TQ
    p = e * ((1.0 / jnp.sum(e, axis=0, keepdims=True)) * (tq >= CMP_LEN - 1).astype(F32))
    ocT = jnp.dot(vcT_ref[0, 0], p.astype(BF16), preferred_element_type=F32)
    for r in range(REP):
        oacc[:, r * TQ:(r + 1) * TQ] = gates[0][r:r + 1, :] * ocT[:, r * TQ:(r + 1) * TQ]
    yield

    yield from stage_probs(WIN, 2)
    al_w1 = yield from stage_scores(WIN, [(w2, jnp.where(qi >= 2, MASK_NONE, MASK_ALL))])
    yield from stage_values(WIN, [qi, w1], al_w0)

    psum = p[:, 0:TQ]
    for r in range(1, REP):
        psum = psum + p[:, r * TQ:(r + 1) * TQ]
    jj = lax.broadcasted_iota(I32, (nblk, ncmp), 0)
    cc = lax.broadcasted_iota(I32, (nblk, ncmp), 1)
    ovT = ((CMP_STRIDE * cc < SLC_BLOCK * jj + SLC_BLOCK)
           & (CMP_STRIDE * cc + CMP_LEN > SLC_BLOCK * jj)).astype(F32)
    impT = jnp.dot(ovT, psum, preferred_element_type=F32, precision=lax.Precision.HIGHEST)

    j_i = lax.broadcasted_iota(I32, (nblk, TQ), 0)
    t1 = q0 + lax.broadcasted_iota(I32, (nblk, TQ), 1)
    cur = t1 // SLC_BLOCK
    forced = (j_i == 0) | (j_i == cur) | (j_i == cur - 1)
    visible = SLC_BLOCK * j_i <= t1
    val_s[...] = jnp.where(forced, FORCED, jnp.where(visible, impT, NEG))
    yield
    ngrp = nblk // 8
    rank_s[...] = jnp.zeros(rank_s.shape, F32)
    j8 = lax.broadcasted_iota(I32, (8, TQ), 0)
    for ib in range(ngrp):
        @pl.when(8 * ib * SLC_BLOCK < q0 + TQ)
        def _(ib=ib):
            vals = [val_s[8 * jb:8 * jb + 8, :] for jb in range(ngrp)]
            ranks = [rank_s[8 * jb:8 * jb + 8, :] for jb in range(ngrp)]
            for i in range(8 * ib, 8 * ib + 8):
                row = jnp.broadcast_to(val_s[i:i + 1, :], (8, TQ))
                for jb in range(ngrp):
                    if jb > ib:
                        hit = row >= vals[jb]
                    elif jb < ib:
                        hit = row > vals[jb]
                    else:
                        hit = jnp.where(j8 > i - 8 * jb, jnp.where(row >= vals[jb], 1.0, 0.0),
                                        jnp.where(row > vals[jb], 1.0, 0.0)) > 0.5
                    ranks[jb] = ranks[jb] + jnp.where(hit, 1.0, 0.0)
            for jb in range(ngrp):
                rank_s[8 * jb:8 * jb + 8, :] = ranks[jb]
    yield
    ranks = [rank_s[8 * jb:8 * jb + 8, :] for jb in range(ngrp)]
    blocks_per_tile = TK // SLC_BLOCK
    st_s[0] = qi
    n_selt = jnp.int32(1)
    for jb in range(ngrp):
        chosen = ranks[jb] < n_sel
        selb = jnp.where(chosen, 0.0, NEG)
        for r in range(REP):
            selbias_s[8 * jb:8 * jb + 8, r * TQ:(r + 1) * TQ] = selb
        any_q = jnp.max(jnp.where(chosen, 1.0, 0.0), axis=1, keepdims=True)
        for hh in range(8 // blocks_per_tile):
            kt = (8 * jb) // blocks_per_tile + hh
            hit = jnp.max(any_q[hh * blocks_per_tile:(hh + 1) * blocks_per_tile, :]) > 0.5
            st_s[n_selt] = kt
            n_selt = n_selt + jnp.where(hit & (kt < qi), 1, 0)
    qall[SEL, HEAD_DIM:HEAD_DIM + nblk, :] = selbias_s[...].astype(BF16)
    yield

    yield from stage_probs(WIN, 1)
    yield from stage_values(WIN, [w2], al_w1)
    yield from start_branch(SEL, st_s, jnp.where(n_selt >= 2, MASK_NONE, MASK_ALL))
    yield pair_items(SEL, n_selt, st_s)
    yield from finish_branch(SEL, n_selt, st_s)
    yield

    def normalised(br):
        acc = acc_all[br]
        return acc[0:HEAD_DIM, :] / acc[HEAD_DIM:HEAD_DIM + 1, :]

    o_sel = normalised(SEL)
    o_win = normalised(WIN)
    for r in range(REP):
        lanes = slice(r * TQ, (r + 1) * TQ)
        oT_s[r * HEAD_DIM:(r + 1) * HEAD_DIM, :] = (oacc[:, lanes] + gates[1][r:r + 1, :] * o_sel[:, lanes]
                                                    + gates[2][r:r + 1, :] * o_win[:, lanes])
    o_ref[0, lanes_c, :] = oT_s[...].T.astype(BF16)


def _attention(qT, ks, vs, kw, vw, kc, vcT, gT):
    B, _, S = qT.shape
    G = N_KV
    nkt = S // TK
    ncmp = kc.shape[2]
    nblk = S // SLC_BLOCK
    n_sel = min(SLC_TOPK, nblk)
    W4 = REP * TQ
    res4 = lambda a: pl.BlockSpec((1, 1) + a.shape[2:], lambda b, g, i: (b, g) + (0,) * (a.ndim - 2))
    return pl.pallas_call(
        functools.partial(_attn_kernel, n_sel=n_sel),
        grid=(B, G, S // (NCHAIN * TQ)),
        in_specs=[pl.BlockSpec((1, REP * HEAD_DIM, NCHAIN * TQ), lambda b, g, i: (b, g, i)),
                  res4(ks), res4(vs), res4(kw), res4(vw), res4(kc), res4(vcT),
                  pl.BlockSpec((1, 1, GATE_ROWS, NCHAIN * TQ), lambda b, g, i: (b, g, 0, i))],
        out_specs=pl.BlockSpec((1, NCHAIN * TQ, REP * HEAD_DIM), lambda b, g, i: (b, i, g)),
        out_shape=jax.ShapeDtypeStruct((B, S, D_ATT), BF16),
        scratch_shapes=[pltpu.VMEM((2, S, KAUG), BF16),
                        pltpu.VMEM((2, nkt, VROWS, TK), BF16),
                        pltpu.VMEM((NCHAIN, 2, KAUG, W4), BF16),
                        pltpu.VMEM((ncmp, KAUG), BF16),
                        pltpu.VMEM((NCHAIN, 2, 1, W4), F32),
                        pltpu.VMEM((NCHAIN, 2, VROWS, W4), F32),
                        pltpu.VMEM((NCHAIN, 1, W4), F32),
                        pltpu.VMEM((NCHAIN, 2 * TK, W4), F32),
                        pltpu.VMEM((NCHAIN, 2 * TK, W4), BF16),
                        pltpu.VMEM((NCHAIN, HEAD_DIM, W4), F32),
                        pltpu.VMEM((NCHAIN, nblk, TQ), F32),
                        pltpu.VMEM((NCHAIN, nblk, TQ), F32),
                        pltpu.VMEM((NCHAIN, nblk, W4), F32),
                        pltpu.VMEM((4, TK, W4), F32),
                        pltpu.VMEM((2 * ncmp, W4), F32),
                        pltpu.VMEM((NCHAIN, REP * HEAD_DIM, TQ), F32),
                        pltpu.SMEM((NCHAIN, nkt + 2), I32)],
        compiler_params=pltpu.CompilerParams(dimension_semantics=("arbitrary", "arbitrary", "arbitrary"),
                                             vmem_limit_bytes=VMEM_LIMIT),
        name="attn",
    )(qT, ks, vs, kw, vw, kc, vcT, gT)


def _outmlp_kernel(x_ref, rnn_ref, att_ref, gatt_ref, wo_ref, gpost_ref, g1_ref, gpre_ref, sc2_ref, sh2_ref,
                   w1_ref, w2_ref, gpost2_ref, g2_ref, o_ref):
    tm = x_ref.shape[1]
    sub = tm // OUT_SUBTILES
    fc = 1024

    def head(rows, att_n):
        y = (jnp.dot(rnn_ref[0, rows, :], wo_ref[0:D_RNN, :], preferred_element_type=F32)
             + jnp.dot(att_n, wo_ref[D_RNN:, :], preferred_element_type=F32))
        x1 = x_ref[0, rows, :] + (1.0 + g1_ref[0]) * (_rms(y) * gpost_ref[...])
        return x1, (_rms(x1) * (gpre_ref[...] * (1.0 + sc2_ref[0])) + sh2_ref[0]).astype(BF16)

    def mlp(rows, x1, h2):
        ff = jnp.zeros(x1.shape, F32)
        for c in range(D_FF // fc):
            hid = jnp.maximum(jnp.dot(h2, w1_ref[:, c * fc:(c + 1) * fc], preferred_element_type=F32), 0.0)
            ff = ff + jnp.dot((hid * hid).astype(BF16), w2_ref[c * fc:(c + 1) * fc, :],
                              preferred_element_type=F32)
        o_ref[0, rows, :] = x1 + (1.0 + g2_ref[0]) * (_rms(ff) * gpost2_ref[...])

    tiles = [slice(k * sub, (k + 1) * sub) for k in range(OUT_SUBTILES)]
    att_ns = [(_rms(att_ref[0, rows, :].astype(F32)) * gatt_ref[...]).astype(BF16) for rows in tiles]
    heads = [head(tiles[0], att_ns[0])]
    for k in range(OUT_SUBTILES):
        if k + 1 < OUT_SUBTILES:
            heads.append(head(tiles[k + 1], att_ns[k + 1]))
        mlp(tiles[k], *heads[k])


def _outmlp(x, rnn_n, att, gatt, wo, gpost, g1, gpre, sc2, sh2, w1, w2, gpost2, g2):
    B, S, D = x.shape
    tm = min(TM_OUT, S)
    row = lambda n: pl.BlockSpec((1, n), lambda b, s: (0, 0))
    per_b = lambda n: pl.BlockSpec((1, 1, n), lambda b, s: (b, 0, 0))
    const = lambda a: pl.BlockSpec(a.shape, lambda b, s: (0,) * a.ndim, pipeline_mode=pl.Buffered(1))
    tok = lambda n: pl.BlockSpec((1, tm, n), lambda b, s: (b, s, 0))
    return pl.pallas_call(
        _outmlp_kernel,
        grid=(B, S // tm),
        in_specs=[tok(D), tok(D_RNN), tok(D_ATT), row(D_ATT), const(wo), row(D), per_b(D), row(D),
                  per_b(D), per_b(D), const(w1), const(w2), row(D), per_b(D)],
        out_specs=tok(D),
        out_shape=jax.ShapeDtypeStruct((B, S, D), F32),
        compiler_params=pltpu.CompilerParams(dimension_semantics=("arbitrary", "arbitrary"),
                                             vmem_limit_bytes=VMEM_LIMIT),
        name="outmlp",
    )(x, rnn_n, att, gatt, wo, gpost, g1, gpre, sc2, sh2, w1, w2, gpost2, g2)


def _block_diag(w):
    n, k, _ = w.shape
    return jnp.einsum('nij,nm->nimj', w, jnp.eye(n, dtype=w.dtype)).reshape(n * k, n * k)


def _layer(x, c, ada_w, ada_b, pre_norm_mix, w_in, conv_w, conv_b, lru_wa, lru_ba, lru_wx, lru_bx, lru_lambda,
           cmp_pos_k, cmp_w1_k, cmp_w2_k, cmp_pos_v, cmp_w1_v, cmp_w2_v, norm_rnn_out, norm_att_out, w_out,
           post_norm_mix, pre_norm_mlp, w_ff1, w_ff2, post_norm_mlp):
    B, S, D = x.shape
    G = N_KV
    row = lambda v: v.reshape(1, -1)

    mod = _ada(c, ada_w, ada_b)
    sh1, sc1, g1, sh2, sc2, g2 = [m.reshape(B, 1, D) for m in jnp.split(mod, 6, axis=-1)]

    gate_cols = [OFF_GATE + br * N_HEADS + g * REP + r for g in range(G) for br in range(3) for r in range(REP)]
    w_gate = w_in[:, jnp.asarray(gate_cols)].reshape(D, G, 3 * REP)
    w_gate = jnp.pad(w_gate, ((0, 0), (0, 0), (0, GATE_ROWS - 3 * REP))).reshape(D, G * GATE_ROWS)
    w_in_p = jnp.concatenate([w_in[:, :OFF_GATE], jnp.pad(w_gate, ((0, 0), (0, GATE_PAD - G * GATE_ROWS)))],
                             axis=1).astype(BF16)
    wa = _block_diag(lru_wa).astype(BF16)
    wx = _block_diag(lru_wx).astype(BF16)
    half = CMP_LEN // 2 * HEAD_DIM

    def w1_cat(w1):
        return jnp.concatenate([w1[:half], w1[half:]], axis=1).astype(BF16)

    def pos_rows(pos):
        return jnp.pad(pos.reshape(2, half), ((0, 14), (0, 0))).astype(BF16)

    rnn_n, qT, ks, vsT, kw, vwT, kc_in, vc_in, gT = _inproj(
        x, row(pre_norm_mix), sc1, sh1, w_in_p, conv_w, row(conv_b), wa, row(lru_ba), wx, row(lru_bx),
        row(lru_lambda), row(norm_rnn_out))
    kc, vc = _compress(kc_in, vc_in, w1_cat(cmp_w1_k), cmp_w2_k.astype(BF16), pos_rows(cmp_pos_k),
                       w1_cat(cmp_w1_v), cmp_w2_v.astype(BF16), pos_rows(cmp_pos_v))
    att = _attention(qT, ks, vsT, kw, vwT, kc, vc.transpose(0, 1, 3, 2), gT)

    return _outmlp(x, rnn_n, att, row(norm_att_out), w_out.astype(BF16), row(post_norm_mix), g1,
                   row(pre_norm_mlp), sc2, sh2, w_ff1.astype(BF16), w_ff2.astype(BF16), row(post_norm_mlp), g2)


def kernel(x, c, ada_w, ada_b, pre_norm_mix, w_in, conv_w, conv_b, lru_wa, lru_ba, lru_wx, lru_bx, lru_lambda,
           cmp_pos_k, cmp_w1_k, cmp_w2_k, cmp_pos_v, cmp_w1_v, cmp_w2_v, norm_rnn_out, norm_att_out, w_out,
           post_norm_mix, pre_norm_mlp, w_ff1, w_ff2, post_norm_mlp):
    for l in range(ada_w.shape[0]):
        x = _layer(x, c, ada_w[l], ada_b[l], pre_norm_mix[l], w_in[l], conv_w[l], conv_b[l], lru_wa[l], lru_ba[l],
                   lru_wx[l], lru_bx[l], lru_lambda[l], cmp_pos_k[l], cmp_w1_k[l], cmp_w2_k[l], cmp_pos_v[l],
                   cmp_w1_v[l], cmp_w2_v[l], norm_rnn_out[l], norm_att_out[l], w_out[l], post_norm_mix[l],
                   pre_norm_mlp[l], w_ff1[l], w_ff2[l], post_norm_mlp[l])
    return x
```

```python
import functools

import jax
import jax.numpy as jnp
from jax import lax
from jax.experimental import pallas as pl
from jax.experimental.pallas import tpu as pltpu

F32 = jnp.float32
BF16 = jnp.bfloat16
I32 = jnp.int32

D_MODEL = 1024
D_RNN = 512
RNN_BLOCKS = 8
CONV_WIDTH = 4
LRU_C = 8.0
N_HEADS = 8
HEAD_DIM = 64
N_KV = 2
REP = N_HEADS // N_KV
D_ATT = N_HEADS * HEAD_DIM
CMP_LEN = 32
CMP_STRIDE = 16
CMP_HIDDEN = 256
SLC_BLOCK = 64
SLC_TOPK = 16
WINDOW = 512
D_FF = 4 * D_MODEL
EPS = 1e-6
NEG = -1e30
FORCED = 1e4
LOG2E = 1.4426950408889634

KV_COLS = 6 * N_KV * HEAD_DIM
N_GATE = 3 * N_HEADS
GATE_PAD = 128
GATE_ROWS = 16
OFF_Q = 2 * D_RNN
OFF_KV = OFF_Q + D_ATT
OFF_GATE = OFF_KV + KV_COLS
D_IN_PAD = OFF_GATE + GATE_PAD

TM_IN = 1024
TM_OUT = 1024
OUT_SUBTILES = 4
TQ = 256
TK = 256
NCHAIN = 4
KAUG = 256
VROWS = 80
VMEM_LIMIT = 56 * 1024 * 1024

SEL, WIN = 0, 1
MASK_NONE, MASK_CAUSAL, MASK_WINDOW_LOW, MASK_ALL = 0, 1, 2, 3
POS_COL = 2 * HEAD_DIM


def _gelu_tanh(x):
    return 0.5 * x * (1.0 + jnp.tanh(0.7978845608028654 * (x + 0.044715 * (x * x * x))))


def _sigmoid(x):
    return 0.5 * jnp.tanh(0.5 * x) + 0.5


def _rms(x):
    return x * lax.rsqrt(jnp.mean(x * x, axis=-1, keepdims=True) + EPS)


def _ada_kernel(c_ref, w_ref, b_ref, o_ref):
    c = c_ref[...]
    a = c * jax.nn.sigmoid(c)
    o_ref[...] = jnp.dot(a, w_ref[...], preferred_element_type=F32,
                         precision=lax.Precision.HIGHEST) + b_ref[...]


def _ada(c, w, b):
    B, D = c.shape
    N = w.shape[1]
    tn = 1024
    return pl.pallas_call(
        _ada_kernel,
        grid=(N // tn,),
        in_specs=[pl.BlockSpec((B, D), lambda j: (0, 0)),
                  pl.BlockSpec((D, tn), lambda j: (0, j)),
                  pl.BlockSpec((1, tn), lambda j: (0, j))],
        out_specs=pl.BlockSpec((B, tn), lambda j: (0, j)),
        out_shape=jax.ShapeDtypeStruct((B, N), F32),
        name="ada",
    )(c, w, b.reshape(1, N))


def _inproj_kernel(x_ref, gain_ref, sc_ref, sh_ref, w_ref, cw_ref, cb_ref, wa_ref, ba_ref, wx_ref, bx_ref,
                   lam_ref, grnn_ref,
                   rnn_ref, qT_ref, ks_ref, vs_ref, kw_ref, vw_ref, kc_ref, vc_ref, g_ref,
                   xbuf, hcar, a_s, u_s, h_s):
    tm = x_ref.shape[1]
    hd = HEAD_DIM

    @pl.when(pl.program_id(1) == 0)
    def _():
        xbuf[0:8, :] = jnp.zeros((8, D_RNN), F32)
        hcar[...] = jnp.zeros((1, D_RNN), F32)

    x = x_ref[0]
    h = _rms(x) * (gain_ref[...] * (1.0 + sc_ref[0])) + sh_ref[0]
    hb = h.astype(BF16)

    xr = jnp.dot(hb, w_ref[:, D_RNN:OFF_Q], preferred_element_type=F32)
    xbuf[8:8 + tm, :] = xr
    y = (cw_ref[3:4, :] * xr + cw_ref[2:3, :] * xbuf[7:7 + tm, :]
         + cw_ref[1:2, :] * xbuf[6:6 + tm, :] + cw_ref[0:1, :] * xbuf[5:5 + tm, :]) + cb_ref[...]
    xbuf[0:8, :] = xbuf[tm:tm + 8, :]

    yb = y.astype(BF16)
    r = _sigmoid(jnp.dot(yb, wa_ref[...], preferred_element_type=F32) + ba_ref[...])
    i = _sigmoid(jnp.dot(yb, wx_ref[...], preferred_element_type=F32) + bx_ref[...])
    nl = -lam_ref[...]
    softplus = jnp.maximum(nl, 0.0) + jnp.log(1.0 + jnp.exp(-jnp.abs(nl)))
    a = jnp.exp((-LRU_C) * r * softplus)
    a_s[...] = a
    u_s[...] = jnp.sqrt(1.0 - a * a) * (i * y)

    qT_ref[0] = (jnp.dot(hb, w_ref[:, OFF_Q:OFF_KV], preferred_element_type=F32)
                 * (HEAD_DIM ** -0.5 * LOG2E)).T.astype(BF16)
    kv = jnp.dot(hb, w_ref[:, OFF_KV:OFF_GATE], preferred_element_type=F32)
    width = N_KV * hd
    vsT = kv[:, 3 * width:4 * width].T
    vwT = kv[:, 5 * width:6 * width].T
    for gi in range(N_KV):
        kc_ref[0, gi] = kv[:, gi * hd:(gi + 1) * hd]
        vc_ref[0, gi] = kv[:, width + gi * hd:width + (gi + 1) * hd]
        ks_ref[0, gi] = kv[:, 2 * width + gi * hd:2 * width + (gi + 1) * hd].astype(BF16)
        kw_ref[0, gi] = kv[:, 4 * width + gi * hd:4 * width + (gi + 1) * hd].astype(BF16)
        for j in range(tm // TK):
            vs_ref[0, gi, j] = vsT[gi * hd:(gi + 1) * hd, j * TK:(j + 1) * TK].astype(BF16)
            vw_ref[0, gi, j] = vwT[gi * hd:(gi + 1) * hd, j * TK:(j + 1) * TK].astype(BF16)
    gates = _sigmoid(jnp.dot(hb, w_ref[:, OFF_GATE:D_IN_PAD], preferred_element_type=F32)).T
    for gi in range(N_KV):
        g_ref[0, gi] = gates[gi * GATE_ROWS:(gi + 1) * GATE_ROWS, :]
    g = jnp.dot(hb, w_ref[:, 0:D_RNN], preferred_element_type=F32)


    rows = lax.broadcasted_iota(I32, (8, D_RNN), 0)
    hprev = hcar[...]
    for gi in range(tm // 8):
        ag = a_s[gi * 8:gi * 8 + 8, :]
        ug = u_s[gi * 8:gi * 8 + 8, :]
        for k in (1, 2, 4):
            a_sh = jnp.where(rows >= k, pltpu.roll(ag, k, 0), 1.0)
            u_sh = jnp.where(rows >= k, pltpu.roll(ug, k, 0), 0.0)
            ug = ag * u_sh + ug
            ag = ag * a_sh
        hg = ag * hprev + ug
        h_s[gi * 8:gi * 8 + 8, :] = hg
        hprev = hg[7:8, :]
    hcar[...] = hprev

    rnn = _gelu_tanh(g) * h_s[...]
    rnn_ref[0] = (_rms(rnn) * grnn_ref[...]).astype(BF16)


def _inproj(x, gain, sc, sh, w_in, conv_w, conv_b, wa, ba, wx, bx, lam, grnn):
    B, S, D = x.shape
    tm = min(TM_IN, S)
    row = lambda n: pl.BlockSpec((1, n), lambda b, s: (0, 0))
    per_b = lambda n: pl.BlockSpec((1, 1, n), lambda b, s: (b, 0, 0))
    full = lambda a: pl.BlockSpec(a.shape, lambda b, s: (0,) * a.ndim)
    tok = lambda n: pl.BlockSpec((1, tm, n), lambda b, s: (b, s, 0))
    G = N_KV
    grp = pl.BlockSpec((1, G, tm, HEAD_DIM), lambda b, s: (b, 0, s, 0))
    grpT = pl.BlockSpec((1, G, tm // TK, HEAD_DIM, TK), lambda b, s: (b, 0, s, 0, 0))
    return pl.pallas_call(
        _inproj_kernel,
        grid=(B, S // tm),
        in_specs=[tok(D), row(D), per_b(D), per_b(D), full(w_in), full(conv_w), row(D_RNN),
                  full(wa), row(D_RNN), full(wx), row(D_RNN), row(D_RNN), row(D_RNN)],
        out_specs=[tok(D_RNN),
                   pl.BlockSpec((1, D_ATT, tm), lambda b, s: (b, 0, s)),
                   grp, grpT, grp, grpT, grp, grp,
                   pl.BlockSpec((1, G, GATE_ROWS, tm), lambda b, s: (b, 0, 0, s))],
        out_shape=[jax.ShapeDtypeStruct((B, S, D_RNN), BF16),
                   jax.ShapeDtypeStruct((B, D_ATT, S), BF16),
                   jax.ShapeDtypeStruct((B, G, S, HEAD_DIM), BF16),
                   jax.ShapeDtypeStruct((B, G, S // TK, HEAD_DIM, TK), BF16),
                   jax.ShapeDtypeStruct((B, G, S, HEAD_DIM), BF16),
                   jax.ShapeDtypeStruct((B, G, S // TK, HEAD_DIM, TK), BF16),
                   jax.ShapeDtypeStruct((B, G, S, HEAD_DIM), F32),
                   jax.ShapeDtypeStruct((B, G, S, HEAD_DIM), F32),
                   jax.ShapeDtypeStruct((B, G, GATE_ROWS, S), F32)],
        scratch_shapes=[pltpu.VMEM((tm + 8, D_RNN), F32), pltpu.VMEM((1, D_RNN), F32),
                        pltpu.VMEM((tm, D_RNN), F32), pltpu.VMEM((tm, D_RNN), F32),
                        pltpu.VMEM((tm, D_RNN), F32)],
        compiler_params=pltpu.CompilerParams(dimension_semantics=("arbitrary", "arbitrary"),
                                             vmem_limit_bytes=VMEM_LIMIT),
        name="inproj",
    )(x, gain, sc, sh, w_in, conv_w, conv_b, wa, ba, wx, bx, lam, grnn)


def _compress_kernel(rk_ref, rv_ref, w1k_ref, w2k_ref, pk_ref, w1v_ref, w2v_ref, pv_ref, kc_ref, vc_ref):
    def one(r_ref, w1_ref, w2_ref, pos_ref, o_ref):
        n = r_ref.shape[2] // CMP_STRIDE
        rows = jnp.concatenate([r_ref[0, 0, pl.ds(l, n, stride=CMP_STRIDE), :] for l in range(CMP_STRIDE)], axis=1)
        p = jnp.dot(rows.astype(BF16), w1_ref[...], preferred_element_type=F32)
        posb = jnp.dot(pos_ref[...], w1_ref[...], preferred_element_type=F32)
        bias = posb[0:1, 0:CMP_HIDDEN] + posb[1:2, CMP_HIDDEN:]
        pre = p[:, 0:CMP_HIDDEN] + pltpu.roll(p[:, CMP_HIDDEN:], n - 1, 0) + bias
        hid = _gelu_tanh(pre).astype(BF16)
        o_ref[0, 0] = jnp.dot(hid, w2_ref[...], preferred_element_type=F32).astype(BF16)

    one(rk_ref, w1k_ref, w2k_ref, pk_ref, kc_ref)
    one(rv_ref, w1v_ref, w2v_ref, pv_ref, vc_ref)


def _compress(rk, rv, w1k, w2k, posk, w1v, w2v, posv):
    B, G, S, width = rk.shape
    n = S // CMP_STRIDE
    blk = pl.BlockSpec((1, 1, S, width), lambda b, g: (b, g, 0, 0))
    full = lambda a: pl.BlockSpec(a.shape, lambda b, g: (0,) * a.ndim)
    out = pl.BlockSpec((1, 1, n, HEAD_DIM), lambda b, g: (b, g, 0, 0))
    return pl.pallas_call(
        _compress_kernel,
        grid=(B, G),
        in_specs=[blk, blk, full(w1k), full(w2k), full(posk), full(w1v), full(w2v), full(posv)],
        out_specs=[out, out],
        out_shape=[jax.ShapeDtypeStruct((B, G, n, HEAD_DIM), BF16)] * 2,
        compiler_params=pltpu.CompilerParams(vmem_limit_bytes=VMEM_LIMIT),
        name="compress",
    )(rk, rv, w1k, w2k, posk, w1v, w2v, posv)


def _attn_kernel(qT_ref, ks_ref, vs_ref, kw_ref, vw_ref, kc_ref, vcT_ref, g_ref,
                 o_ref,
                 kall, vall, qall, kca, m_all, acc_all, al_s, sbuf, pbuf, oacc, val_s, rank_s, selbias_s,
                 bias_tbl, cmask,
                 oT_s, st_s, *, n_sel):
    S = ks_ref.shape[2]
    nkt = S // TK
    ncmp = kc_ref.shape[2]
    b = pl.program_id(0)
    g = pl.program_id(1)
    step = pl.program_id(2)
    W4 = REP * TQ
    cmp_per_tile = TQ // CMP_STRIDE

    @pl.when((b == 0) & (g == 0) & (step == 0))
    def _():
        col = lax.broadcasted_iota(I32, (TK, KAUG), 1)
        is_blk = (col == POS_COL) | (col == POS_COL + 2)
        is_off = (col == POS_COL + 1) | (col == POS_COL + 3)
        ones_row = (lax.broadcasted_iota(I32, (VROWS - HEAD_DIM, TK), 0) == 0).astype(BF16)
        for kt in range(nkt):
            key = kt * TK + lax.broadcasted_iota(I32, (TK, KAUG), 0)
            pos = jnp.where(is_blk, (key // SLC_BLOCK).astype(F32),
                            jnp.where(is_off, (key % SLC_BLOCK).astype(F32), 0.0))
            onehot = (col - HEAD_DIM == key // SLC_BLOCK).astype(F32)
            kall[SEL, kt * TK:(kt + 1) * TK, :] = (pos + onehot).astype(BF16)
            kall[WIN, kt * TK:(kt + 1) * TK, :] = pos.astype(BF16)
            vall[SEL, kt, HEAD_DIM:VROWS, :] = ones_row
            vall[WIN, kt, HEAD_DIM:VROWS, :] = ones_row
        ko = lax.broadcasted_iota(I32, (TK, W4), 0)
        to = lax.broadcasted_iota(I32, (TK, W4), 1) % TQ
        bias_tbl[MASK_NONE] = jnp.zeros((TK, W4), F32)
        bias_tbl[MASK_CAUSAL] = jnp.where(ko <= to, 0.0, NEG)
        bias_tbl[MASK_WINDOW_LOW] = jnp.where(ko > to, 0.0, NEG)
        bias_tbl[MASK_ALL] = jnp.full((TK, W4), NEG, F32)
        u = lax.broadcasted_iota(I32, (2 * ncmp, W4), 0)
        tc = lax.broadcasted_iota(I32, (2 * ncmp, W4), 1) % TQ
        cmask[...] = jnp.where(u <= ncmp + ((tc + 1) // CMP_STRIDE) - 2, 0.0, NEG)
        cc = lax.broadcasted_iota(I32, (ncmp, KAUG), 0)
        colc = lax.broadcasted_iota(I32, (ncmp, KAUG), 1)
        kca[...] = jnp.where((colc == POS_COL) | (colc == POS_COL + 2), (cc // 4).astype(F32),
                             jnp.where((colc == POS_COL + 1) | (colc == POS_COL + 3),
                                       (CMP_STRIDE * (cc % 4)).astype(F32), 0.0)).astype(BF16)
        qall[...] = jnp.zeros(qall.shape, BF16)

    lane = lax.broadcasted_iota(I32, (1, W4), 1)
    head = g * REP + lane // TQ
    slope = lax.bitcast_convert_type((126 - head) << 23, F32)

    @pl.when(step == 0)
    def _():
        for kt in range(nkt):
            kall[SEL, kt * TK:(kt + 1) * TK, 0:HEAD_DIM] = ks_ref[0, 0, kt * TK:(kt + 1) * TK, :]
            kall[WIN, kt * TK:(kt + 1) * TK, 0:HEAD_DIM] = kw_ref[0, 0, kt * TK:(kt + 1) * TK, :]
            vall[SEL, kt, 0:HEAD_DIM, :] = vs_ref[0, 0, kt]
            vall[WIN, kt, 0:HEAD_DIM, :] = vw_ref[0, 0, kt]
        kca[:, 0:HEAD_DIM] = kc_ref[0, 0]
        c_hi = (slope * LOG2E).astype(BF16).astype(F32)
        c_lo = slope * LOG2E - c_hi
        r16 = lax.broadcasted_iota(I32, (16, W4), 0)
        alibi = jnp.where(r16 == 0, c_hi * SLC_BLOCK, jnp.where(r16 == 1, c_hi, jnp.where(
            r16 == 2, c_lo * SLC_BLOCK, jnp.where(r16 == 3, c_lo, 0.0)))).astype(BF16)
        for c in range(NCHAIN):
            qall[c, SEL, POS_COL:POS_COL + 16, :] = alibi
            qall[c, WIN, POS_COL:POS_COL + 16, :] = alibi

    chains = [_attn_chain(c, NCHAIN * step + c, qT_ref, kc_ref, vcT_ref, g_ref, o_ref, kall, vall, qall.at[c], kca,
                          m_all.at[c], acc_all.at[c], al_s.at[c], sbuf.at[c], pbuf.at[c], oacc.at[c], val_s.at[c],
                          rank_s.at[c], selbias_s.at[c], bias_tbl, cmask, oT_s.at[c], st_s.at[c], n_sel)
              for c in range(NCHAIN)]
    live = list(chains)
    while live:
        handed = [next(ch, _DONE) for ch in live]
        live = [ch for ch, r in zip(live, handed) if r is not _DONE]
        loops = [r for r in handed if r is not _DONE and r is not None]
        if loops:
            joint = functools.reduce(jnp.minimum, [npair for npair, _, _ in loops])

            def trip(j, group):
                als = [head(j) for _, head, _ in group]
                for (_, _, tail), al in zip(group, als):
                    tail(j, al)

            lax.fori_loop(1, 1 + joint, lambda j, carry: (trip(j, loops), carry)[1], 0)
            for item in loops:
                lax.fori_loop(1 + joint, 1 + item[0], lambda j, carry, item=item: (trip(j, [item]), carry)[1], 0)


_DONE = object()


def _drain(gen):
    try:
        while True:
            next(gen)
    except StopIteration as stop:
        return stop.value


def _attn_chain(c, qi, qT_ref, kc_ref, vcT_ref, g_ref, o_ref, kall, vall, qall, kca, m_all, acc_all, al_s, sbuf,
                pbuf, oacc, val_s, rank_s, selbias_s, bias_tbl, cmask, oT_s, st_s, n_sel):
    S = kall.shape[1]
    ncmp = kc_ref.shape[2]
    nblk = S // SLC_BLOCK
    cmp_per_tile = TQ // CMP_STRIDE
    q0 = qi * TQ
    W4 = REP * TQ
    lane = lax.broadcasted_iota(I32, (1, W4), 1)
    lanes_c = slice(c * TQ, (c + 1) * TQ)

    q = qT_ref[0, :, lanes_c]
    for r in range(REP):
        qr = q[r * HEAD_DIM:(r + 1) * HEAD_DIM, :]
        qall[SEL, 0:HEAD_DIM, r * TQ:(r + 1) * TQ] = qr
        qall[WIN, 0:HEAD_DIM, r * TQ:(r + 1) * TQ] = qr

    gall = g_ref[0, 0, :, lanes_c]
    gates = [gall[br * REP:(br + 1) * REP, :] for br in range(3)]

    def stage_scores(br, tiles_masks):
        col_max = None
        for h, (tile, mask) in enumerate(tiles_masks):
            r0 = pl.multiple_of(tile * TK, TK)
            s = jnp.dot(kall[br, pl.ds(r0, TK), :], qall[br], preferred_element_type=F32)
            if mask is not None:
                s = s + bias_tbl[mask]
            sbuf[h * TK:(h + 1) * TK, :] = s
            c = jnp.max(s, axis=0, keepdims=True)
            col_max = c if col_max is None else jnp.maximum(col_max, c)
            yield
        m_old = m_all[br]
        m_new = jnp.maximum(m_old, col_max)
        m_all[br] = m_new
        return jnp.exp2(m_old - m_new)

    def stage_probs(br, ntile):
        for h in range(ntile):
            rows = slice(h * TK, (h + 1) * TK)
            pbuf[rows, :] = jnp.exp2(sbuf[rows, :] - m_all[br]).astype(BF16)
            yield

    def stage_values(br, tiles, al_row):
        acc = acc_all[br] * al_row
        for h, tile in enumerate(tiles):
            acc = acc + jnp.dot(vall[br, tile], pbuf[h * TK:(h + 1) * TK, :], preferred_element_type=F32)
            if h + 1 < len(tiles):
                yield
        acc_all[br] = acc
        yield

    def start_branch(br, tl, mask1):
        al_s[...] = yield from stage_scores(br, [(tl[0], MASK_CAUSAL), (tl[1], mask1)])

    def pair_items(br, n, tl):
        npair = jnp.maximum(n - 2, 0) // 2

        def head(j):
            al_prev = al_s[...]
            _drain(stage_probs(br, 2))
            al_s[...] = _drain(stage_scores(br, [(tl[2 * j], None), (tl[2 * j + 1], None)]))
            return al_prev

        def tail(j, al_prev):
            _drain(stage_values(br, [tl[2 * j - 2], tl[2 * j - 1]], al_prev))

        return npair, head, tail

    def finish_branch(br, n, tl):
        rest = jnp.maximum(n - 2, 0)
        last = 2 * (rest // 2)
        odd = rest % 2 == 1
        t_single = tl[jnp.where(odd, n - 1, 0)]
        al_prev = al_s[...]
        yield from stage_probs(br, 2)
        al_k = yield from stage_scores(br, [(t_single, jnp.where(odd, MASK_NONE, MASK_ALL))])
        yield from stage_values(br, [tl[last], tl[last + 1]], al_prev)
        yield from stage_probs(br, 1)
        yield from stage_values(br, [t_single], al_k)

    m_all[...] = jnp.full(m_all.shape, NEG, F32)
    acc_all[...] = jnp.zeros(acc_all.shape, F32)
    w1 = jnp.maximum(qi - 2, 0)
    w2 = jnp.maximum(qi - 1, 0)
    al_w0 = yield from stage_scores(
        WIN, [(qi, MASK_CAUSAL),
              (w1, jnp.where(qi >= 2, MASK_WINDOW_LOW, jnp.where(qi >= 1, MASK_NONE, MASK_ALL)))])

    c0 = pl.multiple_of(ncmp - cmp_per_tile * qi, cmp_per_tile)
    sc = jnp.dot(kca[...], qall[WIN], preferred_element_type=F32) + cmask[pl.ds(c0, ncmp), :]
    yield
    e = jnp.exp2(sc - jnp.max(sc, axis=0, keepdims=True))
    tq = q0 + lane % TQ
    p = e * ((1.0 / jnp.sum(e, axis=0, keepdims=True)) * (tq >= CMP_LEN - 1).astype(F32))
    ocT = jnp.dot(vcT_ref[0, 0], p.astype(BF16), preferred_element_type=F32)
    for r in range(REP):
        oacc[:, r * TQ:(r + 1) * TQ] = gates[0][r:r + 1, :] * ocT[:, r * TQ:(r + 1) * TQ]
    psum = p[:, 0:TQ]
    for r in range(1, REP):
        psum = psum + p[:, r * TQ:(r + 1) * TQ]
    yield

    yield from stage_probs(WIN, 2)
    al_w1 = yield from stage_scores(WIN, [(w2, jnp.where(qi >= 2, MASK_NONE, MASK_ALL))])
    yield from stage_values(WIN, [qi, w1], al_w0)

    jj = lax.broadcasted_iota(I32, (nblk, ncmp), 0)
    cc = lax.broadcasted_iota(I32, (nblk, ncmp), 1)
    ovT = ((CMP_STRIDE * cc < SLC_BLOCK * jj + SLC_BLOCK)
           & (CMP_STRIDE * cc + CMP_LEN > SLC_BLOCK * jj)).astype(F32)
    impT = jnp.dot(ovT, psum, preferred_element_type=F32, precision=lax.Precision.HIGHEST)

    j_i = lax.broadcasted_iota(I32, (nblk, TQ), 0)
    t1 = q0 + lax.broadcasted_iota(I32, (nblk, TQ), 1)
    cur = t1 // SLC_BLOCK
    forced = (j_i == 0) | (j_i == cur) | (j_i == cur - 1)
    visible = SLC_BLOCK * j_i <= t1
    val_s[...] = jnp.where(forced, FORCED, jnp.where(visible, impT, NEG))
    yield
    ngrp = nblk // 8
    rank_s[...] = jnp.zeros(rank_s.shape, F32)
    j8 = lax.broadcasted_iota(I32, (8, TQ), 0)
    for ib in range(ngrp):
        @pl.when(8 * ib * SLC_BLOCK < q0 + TQ)
        def _(ib=ib):
            vals = [val_s[8 * jb:8 * jb + 8, :] for jb in range(ngrp)]
            ranks = [rank_s[8 * jb:8 * jb + 8, :] for jb in range(ngrp)]
            for i in range(8 * ib, 8 * ib + 8):
                row = jnp.broadcast_to(val_s[i:i + 1, :], (8, TQ))
                for jb in range(ngrp):
                    if jb > ib:
                        hit = row >= vals[jb]
                    elif jb < ib:
                        hit = row > vals[jb]
                    else:
                        hit = jnp.where(j8 > i - 8 * jb, jnp.where(row >= vals[jb], 1.0, 0.0),
                                        jnp.where(row > vals[jb], 1.0, 0.0)) > 0.5
                    ranks[jb] = ranks[jb] + jnp.where(hit, 1.0, 0.0)
            for jb in range(ngrp):
                rank_s[8 * jb:8 * jb + 8, :] = ranks[jb]
    yield
    ranks = [rank_s[8 * jb:8 * jb + 8, :] for jb in range(ngrp)]
    blocks_per_tile = TK // SLC_BLOCK
    st_s[0] = qi
    n_selt = jnp.int32(1)
    for jb in range(ngrp):
        chosen = ranks[jb] < n_sel
        selb = jnp.where(chosen, 0.0, NEG)
        for r in range(REP):
            selbias_s[8 * jb:8 * jb + 8, r * TQ:(r + 1) * TQ] = selb
        any_q = jnp.max(jnp.where(chosen, 1.0, 0.0), axis=1, keepdims=True)
        for hh in range(8 // blocks_per_tile):
            kt = (8 * jb) // blocks_per_tile + hh
            hit = jnp.max(any_q[hh * blocks_per_tile:(hh + 1) * blocks_per_tile, :]) > 0.5
            st_s[n_selt] = kt
            n_selt = n_selt + jnp.where(hit & (kt < qi), 1, 0)
    qall[SEL, HEAD_DIM:HEAD_DIM + nblk, :] = selbias_s[...].astype(BF16)
    yield

    yield from stage_probs(WIN, 1)
    yield from stage_values(WIN, [w2], al_w1)
    yield from start_branch(SEL, st_s, jnp.where(n_selt >= 2, MASK_NONE, MASK_ALL))
    yield pair_items(SEL, n_selt, st_s)
    yield from finish_branch(SEL, n_selt, st_s)
    yield

    def normalised(br):
        acc = acc_all[br]
        return acc[0:HEAD_DIM, :] / acc[HEAD_DIM:HEAD_DIM + 1, :]

    o_sel = normalised(SEL)
    o_win = normalised(WIN)
    for r in range(REP):
        lanes = slice(r * TQ, (r + 1) * TQ)
        oT_s[r * HEAD_DIM:(r + 1) * HEAD_DIM, :] = (oacc[:, lanes] + gates[1][r:r + 1, :] * o_sel[:, lanes]
                                                    + gates[2][r:r + 1, :] * o_win[:, lanes])
    o_ref[0, lanes_c, :] = oT_s[...].T.astype(BF16)


def _attention(qT, ks, vs, kw, vw, kc, vcT, gT):
    B, _, S = qT.shape
    G = N_KV
    nkt = S // TK
    ncmp = kc.shape[2]
    nblk = S // SLC_BLOCK
    n_sel = min(SLC_TOPK, nblk)
    W4 = REP * TQ
    res4 = lambda a: pl.BlockSpec((1, 1) + a.shape[2:], lambda b, g, i: (b, g) + (0,) * (a.ndim - 2))
    return pl.pallas_call(
        functools.partial(_attn_kernel, n_sel=n_sel),
        grid=(B, G, S // (NCHAIN * TQ)),
        in_specs=[pl.BlockSpec((1, REP * HEAD_DIM, NCHAIN * TQ), lambda b, g, i: (b, g, i)),
                  res4(ks), res4(vs), res4(kw), res4(vw), res4(kc), res4(vcT),
                  pl.BlockSpec((1, 1, GATE_ROWS, NCHAIN * TQ), lambda b, g, i: (b, g, 0, i))],
        out_specs=pl.BlockSpec((1, NCHAIN * TQ, REP * HEAD_DIM), lambda b, g, i: (b, i, g)),
        out_shape=jax.ShapeDtypeStruct((B, S, D_ATT), BF16),
        scratch_shapes=[pltpu.VMEM((2, S, KAUG), BF16),
                        pltpu.VMEM((2, nkt, VROWS, TK), BF16),
                        pltpu.VMEM((NCHAIN, 2, KAUG, W4), BF16),
                        pltpu.VMEM((ncmp, KAUG), BF16),
                        pltpu.VMEM((NCHAIN, 2, 1, W4), F32),
                        pltpu.VMEM((NCHAIN, 2, VROWS, W4), F32),
                        pltpu.VMEM((NCHAIN, 1, W4), F32),
                        pltpu.VMEM((NCHAIN, 2 * TK, W4), F32),
                        pltpu.VMEM((NCHAIN, 2 * TK, W4), BF16),
                        pltpu.VMEM((NCHAIN, HEAD_DIM, W4), F32),
                        pltpu.VMEM((NCHAIN, nblk, TQ), F32),
                        pltpu.VMEM((NCHAIN, nblk, TQ), F32),
                        pltpu.VMEM((NCHAIN, nblk, W4), F32),
                        pltpu.VMEM((4, TK, W4), F32),
                        pltpu.VMEM((2 * ncmp, W4), F32),
                        pltpu.VMEM((NCHAIN, REP * HEAD_DIM, TQ), F32),
                        pltpu.SMEM((NCHAIN, nkt + 2), I32)],
        compiler_params=pltpu.CompilerParams(dimension_semantics=("arbitrary", "arbitrary", "arbitrary"),
                                             vmem_limit_bytes=VMEM_LIMIT),
        name="attn",
    )(qT, ks, vs, kw, vw, kc, vcT, gT)


def _outmlp_kernel(x_ref, rnn_ref, att_ref, gatt_ref, wo_ref, gpost_ref, g1_ref, gpre_ref, sc2_ref, sh2_ref,
                   w1_ref, w2_ref, gpost2_ref, g2_ref, o_ref):
    tm = x_ref.shape[1]
    sub = tm // OUT_SUBTILES
    fc = 1024

    def head(rows, att_n):
        y = (jnp.dot(rnn_ref[0, rows, :], wo_ref[0:D_RNN, :], preferred_element_type=F32)
             + jnp.dot(att_n, wo_ref[D_RNN:, :], preferred_element_type=F32))
        x1 = x_ref[0, rows, :] + (1.0 + g1_ref[0]) * (_rms(y) * gpost_ref[...])
        return x1, (_rms(x1) * (gpre_ref[...] * (1.0 + sc2_ref[0])) + sh2_ref[0]).astype(BF16)

    def mlp(rows, x1, h2):
        ff = jnp.zeros(x1.shape, F32)
        for c in range(D_FF // fc):
            hid = jnp.maximum(jnp.dot(h2, w1_ref[:, c * fc:(c + 1) * fc], preferred_element_type=F32), 0.0)
            ff = ff + jnp.dot((hid * hid).astype(BF16), w2_ref[c * fc:(c + 1) * fc, :],
                              preferred_element_type=F32)
        o_ref[0, rows, :] = x1 + (1.0 + g2_ref[0]) * (_rms(ff) * gpost2_ref[...])

    tiles = [slice(k * sub, (k + 1) * sub) for k in range(OUT_SUBTILES)]
    att_ns = [(_rms(att_ref[0, rows, :].astype(F32)) * gatt_ref[...]).astype(BF16) for rows in tiles]
    heads = [head(tiles[0], att_ns[0])]
    for k in range(OUT_SUBTILES):
        if k + 1 < OUT_SUBTILES:
            heads.append(head(tiles[k + 1], att_ns[k + 1]))
        mlp(tiles[k], *heads[k])


def _outmlp(x, rnn_n, att, gatt, wo, gpost, g1, gpre, sc2, sh2, w1, w2, gpost2, g2):
    B, S, D = x.shape
    tm = min(TM_OUT, S)
    row = lambda n: pl.BlockSpec((1, n), lambda b, s: (0, 0))
    per_b = lambda n: pl.BlockSpec((1, 1, n), lambda b, s: (b, 0, 0))
    const = lambda a: pl.BlockSpec(a.shape, lambda b, s: (0,) * a.ndim, pipeline_mode=pl.Buffered(1))
    tok = lambda n: pl.BlockSpec((1, tm, n), lambda b, s: (b, s, 0))
    return pl.pallas_call(
        _outmlp_kernel,
        grid=(B, S // tm),
        in_specs=[tok(D), tok(D_RNN), tok(D_ATT), row(D_ATT), const(wo), row(D), per_b(D), row(D),
                  per_b(D), per_b(D), const(w1), const(w2), row(D), per_b(D)],
        out_specs=tok(D),
        out_shape=jax.ShapeDtypeStruct((B, S, D), F32),
        compiler_params=pltpu.CompilerParams(dimension_semantics=("arbitrary", "arbitrary"),
                                             vmem_limit_bytes=VMEM_LIMIT),
        name="outmlp",
    )(x, rnn_n, att, gatt, wo, gpost, g1, gpre, sc2, sh2, w1, w2, gpost2, g2)


def _block_diag(w):
    n, k, _ = w.shape
    return jnp.einsum('nij,nm->nimj', w, jnp.eye(n, dtype=w.dtype)).reshape(n * k, n * k)


def _layer(x, c, ada_w, ada_b, pre_norm_mix, w_in, conv_w, conv_b, lru_wa, lru_ba, lru_wx, lru_bx, lru_lambda,
           cmp_pos_k, cmp_w1_k, cmp_w2_k, cmp_pos_v, cmp_w1_v, cmp_w2_v, norm_rnn_out, norm_att_out, w_out,
           post_norm_mix, pre_norm_mlp, w_ff1, w_ff2, post_norm_mlp):
    B, S, D = x.shape
    G = N_KV
    row = lambda v: v.reshape(1, -1)

    mod = _ada(c, ada_w, ada_b)
    sh1, sc1, g1, sh2, sc2, g2 = [m.reshape(B, 1, D) for m in jnp.split(mod, 6, axis=-1)]

    gate_cols = [OFF_GATE + br * N_HEADS + g * REP + r for g in range(G) for br in range(3) for r in range(REP)]
    w_gate = w_in[:, jnp.asarray(gate_cols)].reshape(D, G, 3 * REP)
    w_gate = jnp.pad(w_gate, ((0, 0), (0, 0), (0, GATE_ROWS - 3 * REP))).reshape(D, G * GATE_ROWS)
    w_in_p = jnp.concatenate([w_in[:, :OFF_GATE], jnp.pad(w_gate, ((0, 0), (0, GATE_PAD - G * GATE_ROWS)))],
                             axis=1).astype(BF16)
    wa = _block_diag(lru_wa).astype(BF16)
    wx = _block_diag(lru_wx).astype(BF16)
    half = CMP_LEN // 2 * HEAD_DIM

    def w1_cat(w1):
        return jnp.concatenate([w1[:half], w1[half:]], axis=1).astype(BF16)

    def pos_rows(pos):
        return jnp.pad(pos.reshape(2, half), ((0, 14), (0, 0))).astype(BF16)

    rnn_n, qT, ks, vsT, kw, vwT, kc_in, vc_in, gT = _inproj(
        x, row(pre_norm_mix), sc1, sh1, w_in_p, conv_w, row(conv_b), wa, row(lru_ba), wx, row(lru_bx),
        row(lru_lambda), row(norm_rnn_out))
    kc, vc = _compress(kc_in, vc_in, w1_cat(cmp_w1_k), cmp_w2_k.astype(BF16), pos_rows(cmp_pos_k),
                       w1_cat(cmp_w1_v), cmp_w2_v.astype(BF16), pos_rows(cmp_pos_v))
    att = _attention(qT, ks, vsT, kw, vwT, kc, vc.transpose(0, 1, 3, 2), gT)

    return _outmlp(x, rnn_n, att, row(norm_att_out), w_out.astype(BF16), row(post_norm_mix), g1,
                   row(pre_norm_mlp), sc2, sh2, w_ff1.astype(BF16), w_ff2.astype(BF16), row(post_norm_mlp), g2)


def kernel(x, c, ada_w, ada_b, pre_norm_mix, w_in, conv_w, conv_b, lru_wa, lru_ba, lru_wx, lru_bx, lru_lambda,
           cmp_pos_k, cmp_w1_k, cmp_w2_k, cmp_pos_v, cmp_w1_v, cmp_w2_v, norm_rnn_out, norm_att_out, w_out,
           post_norm_mix, pre_norm_mlp, w_ff1, w_ff2, post_norm_mlp):
    for l in range(ada_w.shape[0]):
        x = _layer(x, c, ada_w[l], ada_b[l], pre_norm_mix[l], w_in[l], conv_w[l], conv_b[l], lru_wa[l], lru_ba[l],
                   lru_wx[l], lru_bx[l], lru_lambda[l], cmp_pos_k[l], cmp_w1_k[l], cmp_w2_k[l], cmp_pos_v[l],
                   cmp_w1_v[l], cmp_w2_v[l], norm_rnn_out[l], norm_att_out[l], w_out[l], post_norm_mix[l],
                   pre_norm_mlp[l], w_ff1[l], w_ff2[l], post_norm_mlp[l])
    return x
```

```python
import functools

import jax
import jax.numpy as jnp
from jax import lax
from jax.experimental import pallas as pl
from jax.experimental.pallas import tpu as pltpu

F32 = jnp.float32
BF16 = jnp.bfloat16
I32 = jnp.int32

D_MODEL = 1024
D_RNN = 512
CONV_WIDTH = 4
LRU_C = 8.0
N_HEADS = 8
HEAD_DIM = 64
N_KV = 2
REP = N_HEADS // N_KV
D_ATT = N_HEADS * HEAD_DIM
CMP_LEN = 32
CMP_STRIDE = 16
CMP_HIDDEN = 256
SLC_BLOCK = 64
SLC_TOPK = 16
WINDOW = 512
D_FF = 4 * D_MODEL
EPS = 1e-6
NEG = -1e30
FORCED = 1e4
LOG2E = 1.4426950408889634

KV_COLS = 6 * N_KV * HEAD_DIM
N_GATE = 3 * N_HEADS
GATE_PAD = 128
GATE_ROWS = 16
OFF_Q = 2 * D_RNN
OFF_KV = OFF_Q + D_ATT
OFF_GATE = OFF_KV + KV_COLS
D_IN_PAD = OFF_GATE + GATE_PAD

TM_IN = 1024
TM_OUT = 1024
OUT_SUBTILES = 4
TQ = 256
TK = 256
NCHAIN = 4
KAUG = 256
VROWS = 80
VMEM_LIMIT = 56 * 1024 * 1024

assert TQ == TK and WINDOW == 2 * TK
assert TK % SLC_BLOCK == 0 and CMP_LEN == 2 * CMP_STRIDE and CONV_WIDTH == 4
assert N_GATE <= N_KV * GATE_ROWS <= GATE_PAD and D_IN_PAD % 128 == 0
assert 2 * HEAD_DIM + 4 <= KAUG and HEAD_DIM < VROWS

SEL, WIN = 0, 1
MASK_NONE, MASK_CAUSAL, MASK_WINDOW_LOW, MASK_ALL = 0, 1, 2, 3
POS_COL = 2 * HEAD_DIM


def _gelu_tanh(x):
    return 0.5 * x * (1.0 + jnp.tanh(0.7978845608028654 * (x + 0.044715 * (x * x * x))))


def _sigmoid(x):
    return 0.5 * jnp.tanh(0.5 * x) + 0.5


def _rms(x):
    return x * lax.rsqrt(jnp.mean(x * x, axis=-1, keepdims=True) + EPS)


def _ada_kernel(c_ref, w_ref, b_ref, o_ref):
    c = c_ref[...]
    a = c * jax.nn.sigmoid(c)
    o_ref[...] = jnp.dot(a, w_ref[...], preferred_element_type=F32,
                         precision=lax.Precision.HIGHEST) + b_ref[...]


def _ada(c, w, b):
    B, D = c.shape
    N = w.shape[1]
    tn = 1024
    return pl.pallas_call(
        _ada_kernel,
        grid=(N // tn,),
        in_specs=[pl.BlockSpec((B, D), lambda j: (0, 0)),
                  pl.BlockSpec((D, tn), lambda j: (0, j)),
                  pl.BlockSpec((1, tn), lambda j: (0, j))],
        out_specs=pl.BlockSpec((B, tn), lambda j: (0, j)),
        out_shape=jax.ShapeDtypeStruct((B, N), F32),
        name="ada",
    )(c, w, b.reshape(1, N))


def _inproj_kernel(x_ref, gain_ref, sc_ref, sh_ref, w_ref, cw_ref, cb_ref, wa_ref, ba_ref, wx_ref, bx_ref,
                   lam_ref, grnn_ref,
                   rnn_ref, qT_ref, ks_ref, vs_ref, kw_ref, vw_ref, kc_ref, vc_ref, g_ref,
                   xbuf, hcar, a_s, u_s, h_s):
    tm = x_ref.shape[1]
    hd = HEAD_DIM

    @pl.when(pl.program_id(1) == 0)
    def _():
        xbuf[0:8, :] = jnp.zeros((8, D_RNN), F32)
        hcar[...] = jnp.zeros((1, D_RNN), F32)

    x = x_ref[0]
    h = _rms(x) * (gain_ref[...] * (1.0 + sc_ref[0])) + sh_ref[0]
    hb = h.astype(BF16)

    xr = jnp.dot(hb, w_ref[:, D_RNN:OFF_Q], preferred_element_type=F32)
    xbuf[8:8 + tm, :] = xr
    y = (cw_ref[3:4, :] * xr + cw_ref[2:3, :] * xbuf[7:7 + tm, :]
         + cw_ref[1:2, :] * xbuf[6:6 + tm, :] + cw_ref[0:1, :] * xbuf[5:5 + tm, :]) + cb_ref[...]
    xbuf[0:8, :] = xbuf[tm:tm + 8, :]

    yb = y.astype(BF16)
    r = _sigmoid(jnp.dot(yb, wa_ref[...], preferred_element_type=F32) + ba_ref[...])
    i = _sigmoid(jnp.dot(yb, wx_ref[...], preferred_element_type=F32) + bx_ref[...])
    nl = -lam_ref[...]
    softplus = jnp.maximum(nl, 0.0) + jnp.log(1.0 + jnp.exp(-jnp.abs(nl)))
    a = jnp.exp((-LRU_C) * r * softplus)
    a_s[...] = a
    u_s[...] = jnp.sqrt(1.0 - a * a) * (i * y)

    qT_ref[0] = (jnp.dot(hb, w_ref[:, OFF_Q:OFF_KV], preferred_element_type=F32)
                 * (HEAD_DIM ** -0.5 * LOG2E)).T.astype(BF16)
    kv = jnp.dot(hb, w_ref[:, OFF_KV:OFF_GATE], preferred_element_type=F32)
    width = N_KV * hd
    vsT = kv[:, 3 * width:4 * width].T
    vwT = kv[:, 5 * width:6 * width].T
    for gi in range(N_KV):
        kc_ref[0, gi] = kv[:, gi * hd:(gi + 1) * hd]
        vc_ref[0, gi] = kv[:, width + gi * hd:width + (gi + 1) * hd]
        ks_ref[0, gi] = kv[:, 2 * width + gi * hd:2 * width + (gi + 1) * hd].astype(BF16)
        kw_ref[0, gi] = kv[:, 4 * width + gi * hd:4 * width + (gi + 1) * hd].astype(BF16)
        for j in range(tm // TK):
            vs_ref[0, gi, j] = vsT[gi * hd:(gi + 1) * hd, j * TK:(j + 1) * TK].astype(BF16)
            vw_ref[0, gi, j] = vwT[gi * hd:(gi + 1) * hd, j * TK:(j + 1) * TK].astype(BF16)
    gates = _sigmoid(jnp.dot(hb, w_ref[:, OFF_GATE:D_IN_PAD], preferred_element_type=F32)).T
    for gi in range(N_KV):
        g_ref[0, gi] = gates[gi * GATE_ROWS:(gi + 1) * GATE_ROWS, :]
    g = jnp.dot(hb, w_ref[:, 0:D_RNN], preferred_element_type=F32)


    rows = lax.broadcasted_iota(I32, (8, D_RNN), 0)
    hprev = hcar[...]
    for gi in range(tm // 8):
        ag = a_s[gi * 8:gi * 8 + 8, :]
        ug = u_s[gi * 8:gi * 8 + 8, :]
        for k in (1, 2, 4):
            a_sh = jnp.where(rows >= k, pltpu.roll(ag, k, 0), 1.0)
            u_sh = jnp.where(rows >= k, pltpu.roll(ug, k, 0), 0.0)
            ug = ag * u_sh + ug
            ag = ag * a_sh
        hg = ag * hprev + ug
        h_s[gi * 8:gi * 8 + 8, :] = hg
        hprev = hg[7:8, :]
    hcar[...] = hprev

    rnn = _gelu_tanh(g) * h_s[...]
    rnn_ref[0] = (_rms(rnn) * grnn_ref[...]).astype(BF16)


def _inproj(x, gain, sc, sh, w_in, conv_w, conv_b, wa, ba, wx, bx, lam, grnn):
    B, S, D = x.shape
    tm = min(TM_IN, S)
    row = lambda n: pl.BlockSpec((1, n), lambda b, s: (0, 0))
    per_b = lambda n: pl.BlockSpec((1, 1, n), lambda b, s: (b, 0, 0))
    full = lambda a: pl.BlockSpec(a.shape, lambda b, s: (0,) * a.ndim)
    tok = lambda n: pl.BlockSpec((1, tm, n), lambda b, s: (b, s, 0))
    G = N_KV
    grp = pl.BlockSpec((1, G, tm, HEAD_DIM), lambda b, s: (b, 0, s, 0))
    grpT = pl.BlockSpec((1, G, tm // TK, HEAD_DIM, TK), lambda b, s: (b, 0, s, 0, 0))
    return pl.pallas_call(
        _inproj_kernel,
        grid=(B, S // tm),
        in_specs=[tok(D), row(D), per_b(D), per_b(D), full(w_in), full(conv_w), row(D_RNN),
                  full(wa), row(D_RNN), full(wx), row(D_RNN), row(D_RNN), row(D_RNN)],
        out_specs=[tok(D_RNN),
                   pl.BlockSpec((1, D_ATT, tm), lambda b, s: (b, 0, s)),
                   grp, grpT, grp, grpT, grp, grp,
                   pl.BlockSpec((1, G, GATE_ROWS, tm), lambda b, s: (b, 0, 0, s))],
        out_shape=[jax.ShapeDtypeStruct((B, S, D_RNN), BF16),
                   jax.ShapeDtypeStruct((B, D_ATT, S), BF16),
                   jax.ShapeDtypeStruct((B, G, S, HEAD_DIM), BF16),
                   jax.ShapeDtypeStruct((B, G, S // TK, HEAD_DIM, TK), BF16),
                   jax.ShapeDtypeStruct((B, G, S, HEAD_DIM), BF16),
                   jax.ShapeDtypeStruct((B, G, S // TK, HEAD_DIM, TK), BF16),
                   jax.ShapeDtypeStruct((B, G, S, HEAD_DIM), F32),
                   jax.ShapeDtypeStruct((B, G, S, HEAD_DIM), F32),
                   jax.ShapeDtypeStruct((B, G, GATE_ROWS, S), F32)],
        scratch_shapes=[pltpu.VMEM((tm + 8, D_RNN), F32), pltpu.VMEM((1, D_RNN), F32),
                        pltpu.VMEM((tm, D_RNN), F32), pltpu.VMEM((tm, D_RNN), F32),
                        pltpu.VMEM((tm, D_RNN), F32)],
        compiler_params=pltpu.CompilerParams(dimension_semantics=("arbitrary", "arbitrary"),
                                             vmem_limit_bytes=VMEM_LIMIT),
        name="inproj",
    )(x, gain, sc, sh, w_in, conv_w, conv_b, wa, ba, wx, bx, lam, grnn)


def _compress_kernel(rk_ref, rv_ref, w1k_ref, w2k_ref, pk_ref, w1v_ref, w2v_ref, pv_ref, kc_ref, vc_ref):
    def one(r_ref, w1_ref, w2_ref, pos_ref, o_ref):
        n = r_ref.shape[2] // CMP_STRIDE
        rows = jnp.concatenate([r_ref[0, 0, pl.ds(l, n, stride=CMP_STRIDE), :] for l in range(CMP_STRIDE)], axis=1)
        p = jnp.dot(rows.astype(BF16), w1_ref[...], preferred_element_type=F32)
        posb = jnp.dot(pos_ref[...], w1_ref[...], preferred_element_type=F32)
        bias = posb[0:1, 0:CMP_HIDDEN] + posb[1:2, CMP_HIDDEN:]
        pre = p[:, 0:CMP_HIDDEN] + pltpu.roll(p[:, CMP_HIDDEN:], n - 1, 0) + bias
        hid = _gelu_tanh(pre).astype(BF16)
        o_ref[0, 0] = jnp.dot(hid, w2_ref[...], preferred_element_type=F32).astype(BF16)

    one(rk_ref, w1k_ref, w2k_ref, pk_ref, kc_ref)
    one(rv_ref, w1v_ref, w2v_ref, pv_ref, vc_ref)


def _compress(rk, rv, w1k, w2k, posk, w1v, w2v, posv):
    B, G, S, width = rk.shape
    n = S // CMP_STRIDE
    blk = pl.BlockSpec((1, 1, S, width), lambda b, g: (b, g, 0, 0))
    full = lambda a: pl.BlockSpec(a.shape, lambda b, g: (0,) * a.ndim)
    out = pl.BlockSpec((1, 1, n, HEAD_DIM), lambda b, g: (b, g, 0, 0))
    return pl.pallas_call(
        _compress_kernel,
        grid=(B, G),
        in_specs=[blk, blk, full(w1k), full(w2k), full(posk), full(w1v), full(w2v), full(posv)],
        out_specs=[out, out],
        out_shape=[jax.ShapeDtypeStruct((B, G, n, HEAD_DIM), BF16)] * 2,
        compiler_params=pltpu.CompilerParams(vmem_limit_bytes=VMEM_LIMIT),
        name="compress",
    )(rk, rv, w1k, w2k, posk, w1v, w2v, posv)


def _attn_kernel(qT_ref, ks_ref, vs_ref, kw_ref, vw_ref, kc_ref, vcT_ref, g_ref,
                 o_ref,
                 kall, vall, qall, kca, m_all, acc_all, al_s, sbuf, pbuf, oacc, val_s, rank_s, selbias_s,
                 bias_tbl, cmask,
                 oT_s, st_s, *, n_sel):
    S = ks_ref.shape[2]
    nkt = S // TK
    ncmp = kc_ref.shape[2]
    b = pl.program_id(0)
    g = pl.program_id(1)
    step = pl.program_id(2)
    W4 = REP * TQ

    @pl.when((b == 0) & (g == 0) & (step == 0))
    def _():
        col = lax.broadcasted_iota(I32, (TK, KAUG), 1)
        is_blk = (col == POS_COL) | (col == POS_COL + 2)
        is_off = (col == POS_COL + 1) | (col == POS_COL + 3)
        ones_row = (lax.broadcasted_iota(I32, (VROWS - HEAD_DIM, TK), 0) == 0).astype(BF16)
        for kt in range(nkt):
            key = kt * TK + lax.broadcasted_iota(I32, (TK, KAUG), 0)
            pos = jnp.where(is_blk, (key // SLC_BLOCK).astype(F32),
                            jnp.where(is_off, (key % SLC_BLOCK).astype(F32), 0.0))
            onehot = (col - HEAD_DIM == key // SLC_BLOCK).astype(F32)
            kall[SEL, kt * TK:(kt + 1) * TK, :] = (pos + onehot).astype(BF16)
            kall[WIN, kt * TK:(kt + 1) * TK, :] = pos.astype(BF16)
            vall[SEL, kt, HEAD_DIM:VROWS, :] = ones_row
            vall[WIN, kt, HEAD_DIM:VROWS, :] = ones_row
        ko = lax.broadcasted_iota(I32, (TK, W4), 0)
        to = lax.broadcasted_iota(I32, (TK, W4), 1) % TQ
        bias_tbl[MASK_NONE] = jnp.zeros((TK, W4), F32)
        bias_tbl[MASK_CAUSAL] = jnp.where(ko <= to, 0.0, NEG)
        bias_tbl[MASK_WINDOW_LOW] = jnp.where(ko > to, 0.0, NEG)
        bias_tbl[MASK_ALL] = jnp.full((TK, W4), NEG, F32)
        u = lax.broadcasted_iota(I32, (2 * ncmp, W4), 0)
        tc = lax.broadcasted_iota(I32, (2 * ncmp, W4), 1) % TQ
        cmask[...] = jnp.where(u <= ncmp + ((tc + 1) // CMP_STRIDE) - 2, 0.0, NEG)
        cc = lax.broadcasted_iota(I32, (ncmp, KAUG), 0)
        colc = lax.broadcasted_iota(I32, (ncmp, KAUG), 1)
        kca[...] = jnp.where((colc == POS_COL) | (colc == POS_COL + 2), (cc // 4).astype(F32),
                             jnp.where((colc == POS_COL + 1) | (colc == POS_COL + 3),
                                       (CMP_STRIDE * (cc % 4)).astype(F32), 0.0)).astype(BF16)
        qall[...] = jnp.zeros(qall.shape, BF16)

    lane = lax.broadcasted_iota(I32, (1, W4), 1)
    head = g * REP + lane // TQ
    slope = lax.bitcast_convert_type((126 - head) << 23, F32)

    @pl.when(step == 0)
    def _():
        for kt in range(nkt):
            kall[SEL, kt * TK:(kt + 1) * TK, 0:HEAD_DIM] = ks_ref[0, 0, kt * TK:(kt + 1) * TK, :]
            kall[WIN, kt * TK:(kt + 1) * TK, 0:HEAD_DIM] = kw_ref[0, 0, kt * TK:(kt + 1) * TK, :]
            vall[SEL, kt, 0:HEAD_DIM, :] = vs_ref[0, 0, kt]
            vall[WIN, kt, 0:HEAD_DIM, :] = vw_ref[0, 0, kt]
        kca[:, 0:HEAD_DIM] = kc_ref[0, 0]
        c_hi = (slope * LOG2E).astype(BF16).astype(F32)
        c_lo = slope * LOG2E - c_hi
        r16 = lax.broadcasted_iota(I32, (16, W4), 0)
        alibi = jnp.where(r16 == 0, c_hi * SLC_BLOCK, jnp.where(r16 == 1, c_hi, jnp.where(
            r16 == 2, c_lo * SLC_BLOCK, jnp.where(r16 == 3, c_lo, 0.0)))).astype(BF16)
        for c in range(NCHAIN):
            qall[c, SEL, POS_COL:POS_COL + 16, :] = alibi
            qall[c, WIN, POS_COL:POS_COL + 16, :] = alibi

    chains = [_attn_chain(c, NCHAIN * step + c, qT_ref, kc_ref, vcT_ref, g_ref, o_ref, kall, vall, qall.at[c], kca,
                          m_all.at[c], acc_all.at[c], al_s.at[c], sbuf.at[c], pbuf.at[c], oacc.at[c], val_s.at[c],
                          rank_s.at[c], selbias_s.at[c], bias_tbl, cmask, oT_s.at[c], st_s.at[c], n_sel)
              for c in range(NCHAIN)]
    live = list(chains)
    while live:
        handed = [next(ch, _DONE) for ch in live]
        live = [ch for ch, r in zip(live, handed) if r is not _DONE]
        loops = [r for r in handed if r is not _DONE and r is not None]
        if loops:
            joint = functools.reduce(jnp.minimum, [npair for npair, _, _ in loops])

            def trip(j, group):
                als = [head(j) for _, head, _ in group]
                for (_, _, tail), al in zip(group, als):
                    tail(j, al)

            lax.fori_loop(1, 1 + joint, lambda j, carry: (trip(j, loops), carry)[1], 0)
            for item in loops:
                lax.fori_loop(1 + joint, 1 + item[0], lambda j, carry, item=item: (trip(j, [item]), carry)[1], 0)


_DONE = object()


def _drain(gen):
    try:
        while True:
            next(gen)
    except StopIteration as stop:
        return stop.value


def _attn_chain(c, qi, qT_ref, kc_ref, vcT_ref, g_ref, o_ref, kall, vall, qall, kca, m_all, acc_all, al_s, sbuf,
                pbuf, oacc, val_s, rank_s, selbias_s, bias_tbl, cmask, oT_s, st_s, n_sel):
    S = kall.shape[1]
    ncmp = kc_ref.shape[2]
    nblk = S // SLC_BLOCK
    cmp_per_tile = TQ // CMP_STRIDE
    q0 = qi * TQ
    W4 = REP * TQ
    lane = lax.broadcasted_iota(I32, (1, W4), 1)
    lanes_c = slice(c * TQ, (c + 1) * TQ)

    q = qT_ref[0, :, lanes_c]
    for r in range(REP):
        qr = q[r * HEAD_DIM:(r + 1) * HEAD_DIM, :]
        qall[SEL, 0:HEAD_DIM, r * TQ:(r + 1) * TQ] = qr
        qall[WIN, 0:HEAD_DIM, r * TQ:(r + 1) * TQ] = qr

    gall = g_ref[0, 0, :, lanes_c]
    gates = [gall[br * REP:(br + 1) * REP, :] for br in range(3)]

    def stage_scores(br, tiles_masks):
        col_max = None
        for h, (tile, mask) in enumerate(tiles_masks):
            r0 = pl.multiple_of(tile * TK, TK)
            s = jnp.dot(kall[br, pl.ds(r0, TK), :], qall[br], preferred_element_type=F32)
            if mask is not None:
                s = s + bias_tbl[mask]
            sbuf[h * TK:(h + 1) * TK, :] = s
            c = jnp.max(s, axis=0, keepdims=True)
            col_max = c if col_max is None else jnp.maximum(col_max, c)
            yield
        m_old = m_all[br]
        m_new = jnp.maximum(m_old, col_max)
        m_all[br] = m_new
        return jnp.exp2(m_old - m_new)

    def stage_probs(br, ntile):
        for h in range(ntile):
            rows = slice(h * TK, (h + 1) * TK)
            pbuf[rows, :] = jnp.exp2(sbuf[rows, :] - m_all[br]).astype(BF16)
            yield

    def stage_values(br, tiles, al_row):
        acc = acc_all[br] * al_row
        for h, tile in enumerate(tiles):
            acc = acc + jnp.dot(vall[br, tile], pbuf[h * TK:(h + 1) * TK, :], preferred_element_type=F32)
            if h + 1 < len(tiles):
                yield
        acc_all[br] = acc
        yield

    def start_branch(br, tl, mask1):
        al_s[...] = yield from stage_scores(br, [(tl[0], MASK_CAUSAL), (tl[1], mask1)])

    def pair_items(br, n, tl):
        npair = jnp.maximum(n - 2, 0) // 2

        def head(j):
            al_prev = al_s[...]
            _drain(stage_probs(br, 2))
            al_s[...] = _drain(stage_scores(br, [(tl[2 * j], None), (tl[2 * j + 1], None)]))
            return al_prev

        def tail(j, al_prev):
            _drain(stage_values(br, [tl[2 * j - 2], tl[2 * j - 1]], al_prev))

        return npair, head, tail

    def finish_branch(br, n, tl):
        rest = jnp.maximum(n - 2, 0)
        last = 2 * (rest // 2)
        odd = rest % 2 == 1
        t_single = tl[jnp.where(odd, n - 1, 0)]
        al_prev = al_s[...]
        yield from stage_probs(br, 2)
        al_k = yield from stage_scores(br, [(t_single, jnp.where(odd, MASK_NONE, MASK_ALL))])
        yield from stage_values(br, [tl[last], tl[last + 1]], al_prev)
        yield from stage_probs(br, 1)
        yield from stage_values(br, [t_single], al_k)

    m_all[...] = jnp.full(m_all.shape, NEG, F32)
    acc_all[...] = jnp.zeros(acc_all.shape, F32)
    w1 = jnp.maximum(qi - 2, 0)
    w2 = jnp.maximum(qi - 1, 0)
    al_w0 = yield from stage_scores(
        WIN, [(qi, MASK_CAUSAL),
              (w1, jnp.where(qi >= 2, MASK_WINDOW_LOW, jnp.where(qi >= 1, MASK_NONE, MASK_ALL)))])

    c0 = pl.multiple_of(ncmp - cmp_per_tile * qi, cmp_per_tile)
    sc = jnp.dot(kca[...], qall[WIN], preferred_element_type=F32) + cmask[pl.ds(c0, ncmp), :]
    yield
    e = jnp.exp2(sc - jnp.max(sc, axis=0, keepdims=True))
    tq = q0 + lane % TQ
    p = e * ((1.0 / jnp.sum(e, axis=0, keepdims=True)) * (tq >= CMP_LEN - 1).astype(F32))
    ocT = jnp.dot(vcT_ref[0, 0], p.astype(BF16), preferred_element_type=F32)
    for r in range(REP):
        oacc[:, r * TQ:(r + 1) * TQ] = gates[0][r:r + 1, :] * ocT[:, r * TQ:(r + 1) * TQ]
    yield

    yield from stage_probs(WIN, 2)
    al_w1 = yield from stage_scores(WIN, [(w2, jnp.where(qi >= 2, MASK_NONE, MASK_ALL))])
    yield from stage_values(WIN, [qi, w1], al_w0)

    psum = p[:, 0:TQ]
    for r in range(1, REP):
        psum = psum + p[:, r * TQ:(r + 1) * TQ]
    jj = lax.broadcasted_iota(I32, (nblk, ncmp), 0)
    cc = lax.broadcasted_iota(I32, (nblk, ncmp), 1)
    ovT = ((CMP_STRIDE * cc < SLC_BLOCK * jj + SLC_BLOCK)
           & (CMP_STRIDE * cc + CMP_LEN > SLC_BLOCK * jj)).astype(F32)
    impT = jnp.dot(ovT, psum, preferred_element_type=F32, precision=lax.Precision.HIGHEST)

    j_i = lax.broadcasted_iota(I32, (nblk, TQ), 0)
    t1 = q0 + lax.broadcasted_iota(I32, (nblk, TQ), 1)
    cur = t1 // SLC_BLOCK
    forced = (j_i == 0) | (j_i == cur) | (j_i == cur - 1)
    visible = SLC_BLOCK * j_i <= t1
    val_s[...] = jnp.where(forced, FORCED, jnp.where(visible, impT, NEG))
    yield
    ngrp = nblk // 8
    rank_s[...] = jnp.zeros(rank_s.shape, F32)
    j8 = lax.broadcasted_iota(I32, (8, TQ), 0)
    for ib in range(ngrp):
        @pl.when(8 * ib * SLC_BLOCK < q0 + TQ)
        def _(ib=ib):
            vals = [val_s[8 * jb:8 * jb + 8, :] for jb in range(ngrp)]
            ranks = [rank_s[8 * jb:8 * jb + 8, :] for jb in range(ngrp)]
            for i in range(8 * ib, 8 * ib + 8):
                row = jnp.broadcast_to(val_s[i:i + 1, :], (8, TQ))
                for jb in range(ngrp):
                    if jb > ib:
                        hit = row >= vals[jb]
                    elif jb < ib:
                        hit = row > vals[jb]
                    else:
                        hit = jnp.where(j8 > i - 8 * jb, jnp.where(row >= vals[jb], 1.0, 0.0),
                                        jnp.where(row > vals[jb], 1.0, 0.0)) > 0.5
                    ranks[jb] = ranks[jb] + jnp.where(hit, 1.0, 0.0)
            for jb in range(ngrp):
                rank_s[8 * jb:8 * jb + 8, :] = ranks[jb]
    yield
    ranks = [rank_s[8 * jb:8 * jb + 8, :] for jb in range(ngrp)]
    blocks_per_tile = TK // SLC_BLOCK
    st_s[0] = qi
    n_selt = jnp.int32(1)
    for jb in range(ngrp):
        chosen = ranks[jb] < n_sel
        selb = jnp.where(chosen, 0.0, NEG)
        for r in range(REP):
            selbias_s[8 * jb:8 * jb + 8, r * TQ:(r + 1) * TQ] = selb
        any_q = jnp.max(jnp.where(chosen, 1.0, 0.0), axis=1, keepdims=True)
        for hh in range(8 // blocks_per_tile):
            kt = (8 * jb) // blocks_per_tile + hh
            hit = jnp.max(any_q[hh * blocks_per_tile:(hh + 1) * blocks_per_tile, :]) > 0.5
            st_s[n_selt] = kt
            n_selt = n_selt + jnp.where(hit & (kt < qi), 1, 0)
    qall[SEL, HEAD_DIM:HEAD_DIM + nblk, :] = selbias_s[...].astype(BF16)
    yield

    yield from stage_probs(WIN, 1)
    yield from stage_values(WIN, [w2], al_w1)
    yield from start_branch(SEL, st_s, jnp.where(n_selt >= 2, MASK_NONE, MASK_ALL))
    yield pair_items(SEL, n_selt, st_s)
    yield from finish_branch(SEL, n_selt, st_s)
    yield

    def normalised(br):
        acc = acc_all[br]
        return acc[0:HEAD_DIM, :] / acc[HEAD_DIM:HEAD_DIM + 1, :]

    o_sel = normalised(SEL)
    o_win = normalised(WIN)
    for r in range(REP):
        lanes = slice(r * TQ, (r + 1) * TQ)
        oT_s[r * HEAD_DIM:(r + 1) * HEAD_DIM, :] = (oacc[:, lanes] + gates[1][r:r + 1, :] * o_sel[:, lanes]
                                                    + gates[2][r:r + 1, :] * o_win[:, lanes])
    o_ref[0, lanes_c, :] = oT_s[...].T.astype(BF16)


def _attention(qT, ks, vs, kw, vw, kc, vcT, gT):
    B, _, S = qT.shape
    G = N_KV
    assert S % (NCHAIN * TQ) == 0 and (S // SLC_BLOCK) % 8 == 0 and S // SLC_BLOCK <= HEAD_DIM
    nkt = S // TK
    ncmp = kc.shape[2]
    assert ncmp == S // CMP_STRIDE and ncmp // 4 < 256
    nblk = S // SLC_BLOCK
    n_sel = min(SLC_TOPK, nblk)
    W4 = REP * TQ
    res4 = lambda a: pl.BlockSpec((1, 1) + a.shape[2:], lambda b, g, i: (b, g) + (0,) * (a.ndim - 2))
    return pl.pallas_call(
        functools.partial(_attn_kernel, n_sel=n_sel),
        grid=(B, G, S // (NCHAIN * TQ)),
        in_specs=[pl.BlockSpec((1, REP * HEAD_DIM, NCHAIN * TQ), lambda b, g, i: (b, g, i)),
                  res4(ks), res4(vs), res4(kw), res4(vw), res4(kc), res4(vcT),
                  pl.BlockSpec((1, 1, GATE_ROWS, NCHAIN * TQ), lambda b, g, i: (b, g, 0, i))],
        out_specs=pl.BlockSpec((1, NCHAIN * TQ, REP * HEAD_DIM), lambda b, g, i: (b, i, g)),
        out_shape=jax.ShapeDtypeStruct((B, S, D_ATT), BF16),
        scratch_shapes=[pltpu.VMEM((2, S, KAUG), BF16),
                        pltpu.VMEM((2, nkt, VROWS, TK), BF16),
                        pltpu.VMEM((NCHAIN, 2, KAUG, W4), BF16),
                        pltpu.VMEM((ncmp, KAUG), BF16),
                        pltpu.VMEM((NCHAIN, 2, 1, W4), F32),
                        pltpu.VMEM((NCHAIN, 2, VROWS, W4), F32),
                        pltpu.VMEM((NCHAIN, 1, W4), F32),
                        pltpu.VMEM((NCHAIN, 2 * TK, W4), F32),
                        pltpu.VMEM((NCHAIN, 2 * TK, W4), BF16),
                        pltpu.VMEM((NCHAIN, HEAD_DIM, W4), F32),
                        pltpu.VMEM((NCHAIN, nblk, TQ), F32),
                        pltpu.VMEM((NCHAIN, nblk, TQ), F32),
                        pltpu.VMEM((NCHAIN, nblk, W4), F32),
                        pltpu.VMEM((4, TK, W4), F32),
                        pltpu.VMEM((2 * ncmp, W4), F32),
                        pltpu.VMEM((NCHAIN, REP * HEAD_DIM, TQ), F32),
                        pltpu.SMEM((NCHAIN, nkt + 2), I32)],
        compiler_params=pltpu.CompilerParams(dimension_semantics=("arbitrary", "arbitrary", "arbitrary"),
                                             vmem_limit_bytes=VMEM_LIMIT),
        name="attn",
    )(qT, ks, vs, kw, vw, kc, vcT, gT)


def _outmlp_kernel(x_ref, rnn_ref, att_ref, gatt_ref, wo_ref, gpost_ref, g1_ref, gpre_ref, sc2_ref, sh2_ref,
                   w1_ref, w2_ref, gpost2_ref, g2_ref, o_ref):
    tm = x_ref.shape[1]
    sub = tm // OUT_SUBTILES
    fc = 1024

    def head(rows, att_n):
        y = (jnp.dot(rnn_ref[0, rows, :], wo_ref[0:D_RNN, :], preferred_element_type=F32)
             + jnp.dot(att_n, wo_ref[D_RNN:, :], preferred_element_type=F32))
        x1 = x_ref[0, rows, :] + (1.0 + g1_ref[0]) * (_rms(y) * gpost_ref[...])
        return x1, (_rms(x1) * (gpre_ref[...] * (1.0 + sc2_ref[0])) + sh2_ref[0]).astype(BF16)

    def mlp(rows, x1, h2):
        ff = jnp.zeros(x1.shape, F32)
        for c in range(D_FF // fc):
            hid = jnp.maximum(jnp.dot(h2, w1_ref[:, c * fc:(c + 1) * fc], preferred_element_type=F32), 0.0)
            ff = ff + jnp.dot((hid * hid).astype(BF16), w2_ref[c * fc:(c + 1) * fc, :],
                              preferred_element_type=F32)
        o_ref[0, rows, :] = x1 + (1.0 + g2_ref[0]) * (_rms(ff) * gpost2_ref[...])

    tiles = [slice(k * sub, (k + 1) * sub) for k in range(OUT_SUBTILES)]
    att_ns = [(_rms(att_ref[0, rows, :].astype(F32)) * gatt_ref[...]).astype(BF16) for rows in tiles]
    heads = [head(tiles[0], att_ns[0])]
    for k in range(OUT_SUBTILES):
        if k + 1 < OUT_SUBTILES:
            heads.append(head(tiles[k + 1], att_ns[k + 1]))
        mlp(tiles[k], *heads[k])


def _outmlp(x, rnn_n, att, gatt, wo, gpost, g1, gpre, sc2, sh2, w1, w2, gpost2, g2):
    B, S, D = x.shape
    tm = min(TM_OUT, S)
    row = lambda n: pl.BlockSpec((1, n), lambda b, s: (0, 0))
    per_b = lambda n: pl.BlockSpec((1, 1, n), lambda b, s: (b, 0, 0))
    const = lambda a: pl.BlockSpec(a.shape, lambda b, s: (0,) * a.ndim, pipeline_mode=pl.Buffered(1))
    tok = lambda n: pl.BlockSpec((1, tm, n), lambda b, s: (b, s, 0))
    return pl.pallas_call(
        _outmlp_kernel,
        grid=(B, S // tm),
        in_specs=[tok(D), tok(D_RNN), tok(D_ATT), row(D_ATT), const(wo), row(D), per_b(D), row(D),
                  per_b(D), per_b(D), const(w1), const(w2), row(D), per_b(D)],
        out_specs=tok(D),
        out_shape=jax.ShapeDtypeStruct((B, S, D), F32),
        compiler_params=pltpu.CompilerParams(dimension_semantics=("arbitrary", "arbitrary"),
                                             vmem_limit_bytes=VMEM_LIMIT),
        name="outmlp",
    )(x, rnn_n, att, gatt, wo, gpost, g1, gpre, sc2, sh2, w1, w2, gpost2, g2)


def _block_diag(w):
    n, k, _ = w.shape
    return jnp.einsum('nij,nm->nimj', w, jnp.eye(n, dtype=w.dtype)).reshape(n * k, n * k)


def _layer(x, c, ada_w, ada_b, pre_norm_mix, w_in, conv_w, conv_b, lru_wa, lru_ba, lru_wx, lru_bx, lru_lambda,
           cmp_pos_k, cmp_w1_k, cmp_w2_k, cmp_pos_v, cmp_w1_v, cmp_w2_v, norm_rnn_out, norm_att_out, w_out,
           post_norm_mix, pre_norm_mlp, w_ff1, w_ff2, post_norm_mlp):
    B, S, D = x.shape
    G = N_KV
    row = lambda v: v.reshape(1, -1)

    mod = _ada(c, ada_w, ada_b)
    sh1, sc1, g1, sh2, sc2, g2 = [m.reshape(B, 1, D) for m in jnp.split(mod, 6, axis=-1)]

    gate_cols = [OFF_GATE + br * N_HEADS + g * REP + r for g in range(G) for br in range(3) for r in range(REP)]
    w_gate = w_in[:, jnp.asarray(gate_cols)].reshape(D, G, 3 * REP)
    w_gate = jnp.pad(w_gate, ((0, 0), (0, 0), (0, GATE_ROWS - 3 * REP))).reshape(D, G * GATE_ROWS)
    w_in_p = jnp.concatenate([w_in[:, :OFF_GATE], jnp.pad(w_gate, ((0, 0), (0, GATE_PAD - G * GATE_ROWS)))],
                             axis=1).astype(BF16)
    wa = _block_diag(lru_wa).astype(BF16)
    wx = _block_diag(lru_wx).astype(BF16)
    half = CMP_LEN // 2 * HEAD_DIM

    def w1_cat(w1):
        return jnp.concatenate([w1[:half], w1[half:]], axis=1).astype(BF16)

    def pos_rows(pos):
        return jnp.pad(pos.reshape(2, half), ((0, 14), (0, 0))).astype(BF16)

    rnn_n, qT, ks, vsT, kw, vwT, kc_in, vc_in, gT = _inproj(
        x, row(pre_norm_mix), sc1, sh1, w_in_p, conv_w, row(conv_b), wa, row(lru_ba), wx, row(lru_bx),
        row(lru_lambda), row(norm_rnn_out))
    kc, vc = _compress(kc_in, vc_in, w1_cat(cmp_w1_k), cmp_w2_k.astype(BF16), pos_rows(cmp_pos_k),
                       w1_cat(cmp_w1_v), cmp_w2_v.astype(BF16), pos_rows(cmp_pos_v))
    att = _attention(qT, ks, vsT, kw, vwT, kc, vc.transpose(0, 1, 3, 2), gT)

    return _outmlp(x, rnn_n, att, row(norm_att_out), w_out.astype(BF16), row(post_norm_mix), g1,
                   row(pre_norm_mlp), sc2, sh2, w_ff1.astype(BF16), w_ff2.astype(BF16), row(post_norm_mlp), g2)


def kernel(x, c, ada_w, ada_b, pre_norm_mix, w_in, conv_w, conv_b, lru_wa, lru_ba, lru_wx, lru_bx, lru_lambda,
           cmp_pos_k, cmp_w1_k, cmp_w2_k, cmp_pos_v, cmp_w1_v, cmp_w2_v, norm_rnn_out, norm_att_out, w_out,
           post_norm_mix, pre_norm_mlp, w_ff1, w_ff2, post_norm_mlp):
    for l in range(ada_w.shape[0]):
        x = _layer(x, c, ada_w[l], ada_b[l], pre_norm_mix[l], w_in[l], conv_w[l], conv_b[l], lru_wa[l], lru_ba[l],
                   lru_wx[l], lru_bx[l], lru_lambda[l], cmp_pos_k[l], cmp_w1_k[l], cmp_w2_k[l], cmp_pos_v[l],
                   cmp_w1_v[l], cmp_w2_v[l], norm_rnn_out[l], norm_att_out[l], w_out[l], post_norm_mix[l],
                   pre_norm_mlp[l], w_ff1[l], w_ff2[l], post_norm_mlp[l])
    return x
```

```python
import functools

import jax
import jax.numpy as jnp
from jax import lax
from jax.experimental import pallas as pl
from jax.experimental.pallas import tpu as pltpu

F32 = jnp.float32
BF16 = jnp.bfloat16
I32 = jnp.int32

D_MODEL = 1024
D_RNN = 512
CONV_WIDTH = 4
LRU_C = 8.0
N_HEADS = 8
HEAD_DIM = 64
N_KV = 2
REP = N_HEADS // N_KV
D_ATT = N_HEADS * HEAD_DIM
CMP_LEN = 32
CMP_STRIDE = 16
CMP_HIDDEN = 256
SLC_BLOCK = 64
SLC_TOPK = 16
WINDOW = 512
D_FF = 4 * D_MODEL
EPS = 1e-6
NEG = -1e30
FORCED = 1e4
LOG2E = 1.4426950408889634

KV_COLS = 6 * N_KV * HEAD_DIM
N_GATE = 3 * N_HEADS
GATE_PAD = 128
GATE_ROWS = 16
OFF_Q = 2 * D_RNN
OFF_KV = OFF_Q + D_ATT
OFF_GATE = OFF_KV + KV_COLS
D_IN_PAD = OFF_GATE + GATE_PAD

TM_IN = 1024
TM_OUT = 1024
OUT_SUBTILES = 4
TQ = 256
TK = 256
NCHAIN = 4
KAUG = 256
VROWS = 80
VMEM_LIMIT = 56 * 1024 * 1024

assert TQ == TK and WINDOW == 2 * TK
assert TK % SLC_BLOCK == 0 and CMP_LEN == 2 * CMP_STRIDE and CONV_WIDTH == 4
assert N_GATE <= N_KV * GATE_ROWS <= GATE_PAD and D_IN_PAD % 128 == 0
assert 2 * HEAD_DIM + 4 <= KAUG and HEAD_DIM < VROWS

SEL, WIN = 0, 1
MASK_NONE, MASK_CAUSAL, MASK_WINDOW_LOW = 0, 1, 2
POS_COL = 2 * HEAD_DIM
DUMMY_COL = POS_COL + 4


def _gelu_tanh(x):
    return 0.5 * x * (1.0 + jnp.tanh(0.7978845608028654 * (x + 0.044715 * (x * x * x))))


def _sigmoid(x):
    return 0.5 * jnp.tanh(0.5 * x) + 0.5


def _rms(x):
    return x * lax.rsqrt(jnp.mean(x * x, axis=-1, keepdims=True) + EPS)


def _ada_kernel(c_ref, w_ref, b_ref, o_ref):
    c = c_ref[...]
    a = c * jax.nn.sigmoid(c)
    o_ref[...] = jnp.dot(a, w_ref[...], preferred_element_type=F32,
                         precision=lax.Precision.HIGHEST) + b_ref[...]


def _ada(c, w, b):
    B, D = c.shape
    N = w.shape[1]
    tn = 1024
    return pl.pallas_call(
        _ada_kernel,
        grid=(N // tn,),
        in_specs=[pl.BlockSpec((B, D), lambda j: (0, 0)),
                  pl.BlockSpec((D, tn), lambda j: (0, j)),
                  pl.BlockSpec((1, tn), lambda j: (0, j))],
        out_specs=pl.BlockSpec((B, tn), lambda j: (0, j)),
        out_shape=jax.ShapeDtypeStruct((B, N), F32),
        name="ada",
    )(c, w, b.reshape(1, N))


def _inproj_kernel(x_ref, gain_ref, sc_ref, sh_ref, w_ref, cw_ref, cb_ref, wa_ref, ba_ref, wx_ref, bx_ref,
                   lam_ref, grnn_ref,
                   rnn_ref, qT_ref, ks_ref, vs_ref, kw_ref, vw_ref, kc_ref, vc_ref, g_ref,
                   xbuf, hcar, a_s, u_s, h_s):
    tm = x_ref.shape[1]
    hd = HEAD_DIM

    @pl.when(pl.program_id(1) == 0)
    def _():
        xbuf[0:8, :] = jnp.zeros((8, D_RNN), F32)
        hcar[...] = jnp.zeros((1, D_RNN), F32)

    x = x_ref[0]
    h = _rms(x) * (gain_ref[...] * (1.0 + sc_ref[0])) + sh_ref[0]
    hb = h.astype(BF16)

    xr = jnp.dot(hb, w_ref[:, D_RNN:OFF_Q], preferred_element_type=F32)
    xbuf[8:8 + tm, :] = xr
    y = (cw_ref[3:4, :] * xr + cw_ref[2:3, :] * xbuf[7:7 + tm, :]
         + cw_ref[1:2, :] * xbuf[6:6 + tm, :] + cw_ref[0:1, :] * xbuf[5:5 + tm, :]) + cb_ref[...]
    xbuf[0:8, :] = xbuf[tm:tm + 8, :]

    yb = y.astype(BF16)
    r = _sigmoid(jnp.dot(yb, wa_ref[...], preferred_element_type=F32) + ba_ref[...])
    i = _sigmoid(jnp.dot(yb, wx_ref[...], preferred_element_type=F32) + bx_ref[...])
    nl = -lam_ref[...]
    softplus = jnp.maximum(nl, 0.0) + jnp.log(1.0 + jnp.exp(-jnp.abs(nl)))
    a = jnp.exp((-LRU_C) * r * softplus)
    a_s[...] = a
    u_s[...] = jnp.sqrt(1.0 - a * a) * (i * y)

    qT_ref[0] = (jnp.dot(hb, w_ref[:, OFF_Q:OFF_KV], preferred_element_type=F32)
                 * (HEAD_DIM ** -0.5 * LOG2E)).T.astype(BF16)
    kv = jnp.dot(hb, w_ref[:, OFF_KV:OFF_GATE], preferred_element_type=F32)
    width = N_KV * hd
    vsT = kv[:, 3 * width:4 * width].T
    vwT = kv[:, 5 * width:6 * width].T
    for gi in range(N_KV):
        kc_ref[0, gi] = kv[:, gi * hd:(gi + 1) * hd]
        vc_ref[0, gi] = kv[:, width + gi * hd:width + (gi + 1) * hd]
        ks_ref[0, gi] = kv[:, 2 * width + gi * hd:2 * width + (gi + 1) * hd].astype(BF16)
        kw_ref[0, gi] = kv[:, 4 * width + gi * hd:4 * width + (gi + 1) * hd].astype(BF16)
        for j in range(tm // TK):
            vs_ref[0, gi, j] = vsT[gi * hd:(gi + 1) * hd, j * TK:(j + 1) * TK].astype(BF16)
            vw_ref[0, gi, j] = vwT[gi * hd:(gi + 1) * hd, j * TK:(j + 1) * TK].astype(BF16)
    gates = _sigmoid(jnp.dot(hb, w_ref[:, OFF_GATE:D_IN_PAD], preferred_element_type=F32)).T
    for gi in range(N_KV):
        g_ref[0, gi] = gates[gi * GATE_ROWS:(gi + 1) * GATE_ROWS, :]
    g = jnp.dot(hb, w_ref[:, 0:D_RNN], preferred_element_type=F32)


    rows = lax.broadcasted_iota(I32, (8, D_RNN), 0)
    hprev = hcar[...]
    for gi in range(tm // 8):
        ag = a_s[gi * 8:gi * 8 + 8, :]
        ug = u_s[gi * 8:gi * 8 + 8, :]
        for k in (1, 2, 4):
            a_sh = jnp.where(rows >= k, pltpu.roll(ag, k, 0), 1.0)
            u_sh = jnp.where(rows >= k, pltpu.roll(ug, k, 0), 0.0)
            ug = ag * u_sh + ug
            ag = ag * a_sh
        hg = ag * hprev + ug
        h_s[gi * 8:gi * 8 + 8, :] = hg
        hprev = hg[7:8, :]
    hcar[...] = hprev

    rnn = _gelu_tanh(g) * h_s[...]
    rnn_ref[0] = (_rms(rnn) * grnn_ref[...]).astype(BF16)


def _inproj(x, gain, sc, sh, w_in, conv_w, conv_b, wa, ba, wx, bx, lam, grnn):
    B, S, D = x.shape
    tm = min(TM_IN, S)
    row = lambda n: pl.BlockSpec((1, n), lambda b, s: (0, 0))
    per_b = lambda n: pl.BlockSpec((1, 1, n), lambda b, s: (b, 0, 0))
    full = lambda a: pl.BlockSpec(a.shape, lambda b, s: (0,) * a.ndim)
    tok = lambda n: pl.BlockSpec((1, tm, n), lambda b, s: (b, s, 0))
    G = N_KV
    grp = pl.BlockSpec((1, G, tm, HEAD_DIM), lambda b, s: (b, 0, s, 0))
    grpT = pl.BlockSpec((1, G, tm // TK, HEAD_DIM, TK), lambda b, s: (b, 0, s, 0, 0))
    return pl.pallas_call(
        _inproj_kernel,
        grid=(B, S // tm),
        in_specs=[tok(D), row(D), per_b(D), per_b(D), full(w_in), full(conv_w), row(D_RNN),
                  full(wa), row(D_RNN), full(wx), row(D_RNN), row(D_RNN), row(D_RNN)],
        out_specs=[tok(D_RNN),
                   pl.BlockSpec((1, D_ATT, tm), lambda b, s: (b, 0, s)),
                   grp, grpT, grp, grpT, grp, grp,
                   pl.BlockSpec((1, G, GATE_ROWS, tm), lambda b, s: (b, 0, 0, s))],
        out_shape=[jax.ShapeDtypeStruct((B, S, D_RNN), BF16),
                   jax.ShapeDtypeStruct((B, D_ATT, S), BF16),
                   jax.ShapeDtypeStruct((B, G, S, HEAD_DIM), BF16),
                   jax.ShapeDtypeStruct((B, G, S // TK, HEAD_DIM, TK), BF16),
                   jax.ShapeDtypeStruct((B, G, S, HEAD_DIM), BF16),
                   jax.ShapeDtypeStruct((B, G, S // TK, HEAD_DIM, TK), BF16),
                   jax.ShapeDtypeStruct((B, G, S, HEAD_DIM), F32),
                   jax.ShapeDtypeStruct((B, G, S, HEAD_DIM), F32),
                   jax.ShapeDtypeStruct((B, G, GATE_ROWS, S), F32)],
        scratch_shapes=[pltpu.VMEM((tm + 8, D_RNN), F32), pltpu.VMEM((1, D_RNN), F32),
                        pltpu.VMEM((tm, D_RNN), F32), pltpu.VMEM((tm, D_RNN), F32),
                        pltpu.VMEM((tm, D_RNN), F32)],
        compiler_params=pltpu.CompilerParams(dimension_semantics=("arbitrary", "arbitrary"),
                                             vmem_limit_bytes=VMEM_LIMIT),
        name="inproj",
    )(x, gain, sc, sh, w_in, conv_w, conv_b, wa, ba, wx, bx, lam, grnn)


def _compress_kernel(rk_ref, rv_ref, w1k_ref, w2k_ref, pk_ref, w1v_ref, w2v_ref, pv_ref, kc_ref, vc_ref):
    def one(r_ref, w1_ref, w2_ref, pos_ref, o_ref):
        n = r_ref.shape[2] // CMP_STRIDE
        rows = jnp.concatenate([r_ref[0, 0, pl.ds(l, n, stride=CMP_STRIDE), :] for l in range(CMP_STRIDE)], axis=1)
        p = jnp.dot(rows.astype(BF16), w1_ref[...], preferred_element_type=F32)
        posb = jnp.dot(pos_ref[...], w1_ref[...], preferred_element_type=F32)
        bias = posb[0:1, 0:CMP_HIDDEN] + posb[1:2, CMP_HIDDEN:]
        pre = p[:, 0:CMP_HIDDEN] + pltpu.roll(p[:, CMP_HIDDEN:], n - 1, 0) + bias
        hid = _gelu_tanh(pre).astype(BF16)
        o_ref[0, 0] = jnp.dot(hid, w2_ref[...], preferred_element_type=F32).astype(BF16)

    one(rk_ref, w1k_ref, w2k_ref, pk_ref, kc_ref)
    one(rv_ref, w1v_ref, w2v_ref, pv_ref, vc_ref)


def _compress(rk, rv, w1k, w2k, posk, w1v, w2v, posv):
    B, G, S, width = rk.shape
    n = S // CMP_STRIDE
    blk = pl.BlockSpec((1, 1, S, width), lambda b, g: (b, g, 0, 0))
    full = lambda a: pl.BlockSpec(a.shape, lambda b, g: (0,) * a.ndim)
    out = pl.BlockSpec((1, 1, n, HEAD_DIM), lambda b, g: (b, g, 0, 0))
    return pl.pallas_call(
        _compress_kernel,
        grid=(B, G),
        in_specs=[blk, blk, full(w1k), full(w2k), full(posk), full(w1v), full(w2v), full(posv)],
        out_specs=[out, out],
        out_shape=[jax.ShapeDtypeStruct((B, G, n, HEAD_DIM), BF16)] * 2,
        compiler_params=pltpu.CompilerParams(vmem_limit_bytes=VMEM_LIMIT),
        name="compress",
    )(rk, rv, w1k, w2k, posk, w1v, w2v, posv)


def _attn_kernel(qT_ref, ks_ref, vs_ref, kw_ref, vw_ref, kc_ref, vcT_ref, g_ref,
                 o_ref,
                 kall, vall, qall, kca, m_all, acc_all, al_s, sbuf, pbuf, oacc, val_s, rank_s, selbias_s,
                 bias_tbl, cmask,
                 oT_s, st_s, *, n_sel):
    S = ks_ref.shape[2]
    nkt = S // TK
    ncmp = kc_ref.shape[2]
    b = pl.program_id(0)
    g = pl.program_id(1)
    step = pl.program_id(2)
    W4 = REP * TQ

    @pl.when((b == 0) & (g == 0) & (step == 0))
    def _():
        col = lax.broadcasted_iota(I32, (TK, KAUG), 1)
        is_blk = (col == POS_COL) | (col == POS_COL + 2)
        is_off = (col == POS_COL + 1) | (col == POS_COL + 3)
        ones_row = (lax.broadcasted_iota(I32, (VROWS - HEAD_DIM, TK), 0) == 0).astype(BF16)
        for kt in range(nkt):
            key = kt * TK + lax.broadcasted_iota(I32, (TK, KAUG), 0)
            pos = jnp.where(is_blk, (key // SLC_BLOCK).astype(F32),
                            jnp.where(is_off, (key % SLC_BLOCK).astype(F32), 0.0))
            onehot = (col - HEAD_DIM == key // SLC_BLOCK).astype(F32)
            kall[SEL, kt * TK:(kt + 1) * TK, :] = (pos + onehot).astype(BF16)
            kall[WIN, kt * TK:(kt + 1) * TK, :] = pos.astype(BF16)
            vall[SEL, kt, HEAD_DIM:VROWS, :] = ones_row
            vall[WIN, kt, HEAD_DIM:VROWS, :] = ones_row
        dummy = (col == DUMMY_COL).astype(BF16)
        for br in (SEL, WIN):
            kall[br, nkt * TK:(nkt + 1) * TK, :] = dummy
            vall[br, nkt] = jnp.zeros((VROWS, TK), BF16)
        ko = lax.broadcasted_iota(I32, (TK, W4), 0)
        to = lax.broadcasted_iota(I32, (TK, W4), 1) % TQ
        bias_tbl[MASK_NONE] = jnp.zeros((TK, W4), F32)
        bias_tbl[MASK_CAUSAL] = jnp.where(ko <= to, 0.0, NEG)
        bias_tbl[MASK_WINDOW_LOW] = jnp.where(ko > to, 0.0, NEG)
        u = lax.broadcasted_iota(I32, (2 * ncmp, W4), 0)
        tc = lax.broadcasted_iota(I32, (2 * ncmp, W4), 1) % TQ
        cmask[...] = jnp.where(u <= ncmp + ((tc + 1) // CMP_STRIDE) - 2, 0.0, NEG)
        cc = lax.broadcasted_iota(I32, (ncmp, KAUG), 0)
        colc = lax.broadcasted_iota(I32, (ncmp, KAUG), 1)
        kca[...] = jnp.where((colc == POS_COL) | (colc == POS_COL + 2), (cc // 4).astype(F32),
                             jnp.where((colc == POS_COL + 1) | (colc == POS_COL + 3),
                                       (CMP_STRIDE * (cc % 4)).astype(F32), 0.0)).astype(BF16)
        qall[...] = jnp.zeros(qall.shape, BF16)

    lane = lax.broadcasted_iota(I32, (1, W4), 1)
    head = g * REP + lane // TQ
    slope = lax.bitcast_convert_type((126 - head) << 23, F32)

    @pl.when(step == 0)
    def _():
        for kt in range(nkt):
            kall[SEL, kt * TK:(kt + 1) * TK, 0:HEAD_DIM] = ks_ref[0, 0, kt * TK:(kt + 1) * TK, :]
            kall[WIN, kt * TK:(kt + 1) * TK, 0:HEAD_DIM] = kw_ref[0, 0, kt * TK:(kt + 1) * TK, :]
            vall[SEL, kt, 0:HEAD_DIM, :] = vs_ref[0, 0, kt]
            vall[WIN, kt, 0:HEAD_DIM, :] = vw_ref[0, 0, kt]
        kca[:, 0:HEAD_DIM] = kc_ref[0, 0]
        c_hi = (slope * LOG2E).astype(BF16).astype(F32)
        c_lo = slope * LOG2E - c_hi
        r16 = lax.broadcasted_iota(I32, (16, W4), 0)
        alibi = jnp.where(r16 == 0, c_hi * SLC_BLOCK, jnp.where(r16 == 1, c_hi, jnp.where(
            r16 == 2, c_lo * SLC_BLOCK, jnp.where(r16 == 3, c_lo, jnp.where(
                r16 == DUMMY_COL - POS_COL, NEG, 0.0))))).astype(BF16)
        for c in range(NCHAIN):
            qall[c, SEL, POS_COL:POS_COL + 16, :] = alibi
            qall[c, WIN, POS_COL:POS_COL + 16, :] = alibi

    chains = [_attn_chain(c, NCHAIN * step + c, qT_ref, kc_ref, vcT_ref, g_ref, o_ref, kall, vall, qall.at[c], kca,
                          m_all.at[c], acc_all.at[c], al_s.at[c], sbuf.at[c], pbuf.at[c], oacc.at[c], val_s.at[c],
                          rank_s.at[c], selbias_s.at[c], bias_tbl, cmask, oT_s.at[c], st_s.at[c], n_sel)
              for c in range(NCHAIN)]
    live = list(chains)
    while live:
        handed = [next(ch, _DONE) for ch in live]
        live = [ch for ch, r in zip(live, handed) if r is not _DONE]
        loops = [r for r in handed if r is not _DONE and r is not None]
        if loops:
            joint = functools.reduce(jnp.minimum, [npair for npair, _, _ in loops])

            def trip(j, group):
                als = [head(j) for _, head, _ in group]
                for (_, _, tail), al in zip(group, als):
                    tail(j, al)

            lax.fori_loop(1, 1 + joint, lambda j, carry: (trip(j, loops), carry)[1], 0)
            for item in loops:
                lax.fori_loop(1 + joint, 1 + item[0], lambda j, carry, item=item: (trip(j, [item]), carry)[1], 0)


_DONE = object()


def _drain(gen):
    try:
        while True:
            next(gen)
    except StopIteration as stop:
        return stop.value


def _attn_chain(c, qi, qT_ref, kc_ref, vcT_ref, g_ref, o_ref, kall, vall, qall, kca, m_all, acc_all, al_s, sbuf,
                pbuf, oacc, val_s, rank_s, selbias_s, bias_tbl, cmask, oT_s, st_s, n_sel):
    S = kall.shape[1] - TK
    ncmp = kc_ref.shape[2]
    nblk = S // SLC_BLOCK
    cmp_per_tile = TQ // CMP_STRIDE
    q0 = qi * TQ
    W4 = REP * TQ
    lane = lax.broadcasted_iota(I32, (1, W4), 1)
    lanes_c = slice(c * TQ, (c + 1) * TQ)

    q = qT_ref[0, :, lanes_c]
    for r in range(REP):
        qr = q[r * HEAD_DIM:(r + 1) * HEAD_DIM, :]
        qall[SEL, 0:HEAD_DIM, r * TQ:(r + 1) * TQ] = qr
        qall[WIN, 0:HEAD_DIM, r * TQ:(r + 1) * TQ] = qr

    gall = g_ref[0, 0, :, lanes_c]
    gates = [gall[br * REP:(br + 1) * REP, :] for br in range(3)]

    def stage_scores(br, tiles_masks):
        col_max = None
        for h, (tile, mask) in enumerate(tiles_masks):
            r0 = pl.multiple_of(tile * TK, TK)
            s = jnp.dot(kall[br, pl.ds(r0, TK), :], qall[br], preferred_element_type=F32)
            if mask is not None:
                s = s + bias_tbl[mask]
            sbuf[h * TK:(h + 1) * TK, :] = s
            c = jnp.max(s, axis=0, keepdims=True)
            col_max = c if col_max is None else jnp.maximum(col_max, c)
            yield
        m_old = m_all[br]
        m_new = jnp.maximum(m_old, col_max)
        m_all[br] = m_new
        return jnp.exp2(m_old - m_new)

    def stage_probs(br, ntile):
        for h in range(ntile):
            rows = slice(h * TK, (h + 1) * TK)
            pbuf[rows, :] = jnp.exp2(sbuf[rows, :] - m_all[br]).astype(BF16)
            yield

    def stage_values(br, tiles, al_row):
        acc = acc_all[br] * al_row
        for h, tile in enumerate(tiles):
            acc = acc + jnp.dot(vall[br, tile], pbuf[h * TK:(h + 1) * TK, :], preferred_element_type=F32)
            if h + 1 < len(tiles):
                yield
        acc_all[br] = acc
        yield

    def start_branch(br, tl):
        al_s[...] = yield from stage_scores(br, [(tl[0], MASK_CAUSAL), (tl[1], None)])

    def pair_items(br, n, tl):
        npair = jnp.maximum(n - 2, 0) // 2

        def head(j):
            al_prev = al_s[...]
            _drain(stage_probs(br, 2))
            al_s[...] = _drain(stage_scores(br, [(tl[2 * j], None), (tl[2 * j + 1], None)]))
            return al_prev

        def tail(j, al_prev):
            _drain(stage_values(br, [tl[2 * j - 2], tl[2 * j - 1]], al_prev))

        return npair, head, tail

    def finish_branch(br, n, tl):
        rest = jnp.maximum(n - 2, 0)
        last = 2 * (rest // 2)
        t_single = tl[jnp.where(rest % 2 == 1, n - 1, n)]
        al_prev = al_s[...]
        yield from stage_probs(br, 2)
        al_k = yield from stage_scores(br, [(t_single, None)])
        yield from stage_values(br, [tl[last], tl[last + 1]], al_prev)
        yield from stage_probs(br, 1)
        yield from stage_values(br, [t_single], al_k)

    m_all[...] = jnp.full(m_all.shape, NEG, F32)
    acc_all[...] = jnp.zeros(acc_all.shape, F32)
    nkt = S // TK
    w1 = jnp.where(qi >= 2, qi - 2, jnp.where(qi >= 1, 0, nkt))
    w2 = jnp.where(qi >= 2, qi - 1, nkt)
    al_w0 = yield from stage_scores(
        WIN, [(qi, MASK_CAUSAL), (w1, jnp.where(qi >= 2, MASK_WINDOW_LOW, MASK_NONE))])

    c0 = pl.multiple_of(ncmp - cmp_per_tile * qi, cmp_per_tile)
    sc = jnp.dot(kca[...], qall[WIN], preferred_element_type=F32) + cmask[pl.ds(c0, ncmp), :]
    yield
    e = jnp.exp2(sc - jnp.max(sc, axis=0, keepdims=True))
    tq = q0 + lane % TQ
    p = e * ((1.0 / jnp.sum(e, axis=0, keepdims=True)) * (tq >= CMP_LEN - 1).astype(F32))
    ocT = jnp.dot(vcT_ref[0, 0], p.astype(BF16), preferred_element_type=F32)
    for r in range(REP):
        oacc[:, r * TQ:(r + 1) * TQ] = gates[0][r:r + 1, :] * ocT[:, r * TQ:(r + 1) * TQ]
    yield

    yield from stage_probs(WIN, 2)
    al_w1 = yield from stage_scores(WIN, [(w2, None)])
    yield from stage_values(WIN, [qi, w1], al_w0)

    psum = p[:, 0:TQ]
    for r in range(1, REP):
        psum = psum + p[:, r * TQ:(r + 1) * TQ]
    jj = lax.broadcasted_iota(I32, (nblk, ncmp), 0)
    cc = lax.broadcasted_iota(I32, (nblk, ncmp), 1)
    ovT = ((CMP_STRIDE * cc < SLC_BLOCK * jj + SLC_BLOCK)
           & (CMP_STRIDE * cc + CMP_LEN > SLC_BLOCK * jj)).astype(F32)
    impT = jnp.dot(ovT, psum, preferred_element_type=F32, precision=lax.Precision.HIGHEST)

    j_i = lax.broadcasted_iota(I32, (nblk, TQ), 0)
    t1 = q0 + lax.broadcasted_iota(I32, (nblk, TQ), 1)
    cur = t1 // SLC_BLOCK
    forced = (j_i == 0) | (j_i == cur) | (j_i == cur - 1)
    visible = SLC_BLOCK * j_i <= t1
    val_s[...] = jnp.where(forced, FORCED, jnp.where(visible, impT, NEG))
    yield
    ngrp = nblk // 8
    rank_s[...] = jnp.zeros(rank_s.shape, F32)
    j8 = lax.broadcasted_iota(I32, (8, TQ), 0)
    for ib in range(ngrp):
        @pl.when(8 * ib * SLC_BLOCK < q0 + TQ)
        def _(ib=ib):
            vals = [val_s[8 * jb:8 * jb + 8, :] for jb in range(ngrp)]
            ranks = [rank_s[8 * jb:8 * jb + 8, :] for jb in range(ngrp)]
            for i in range(8 * ib, 8 * ib + 8):
                row = jnp.broadcast_to(val_s[i:i + 1, :], (8, TQ))
                for jb in range(ngrp):
                    if jb > ib:
                        hit = row >= vals[jb]
                    elif jb < ib:
                        hit = row > vals[jb]
                    else:
                        hit = jnp.where(j8 > i - 8 * jb, jnp.where(row >= vals[jb], 1.0, 0.0),
                                        jnp.where(row > vals[jb], 1.0, 0.0)) > 0.5
                    ranks[jb] = ranks[jb] + jnp.where(hit, 1.0, 0.0)
            for jb in range(ngrp):
                rank_s[8 * jb:8 * jb + 8, :] = ranks[jb]
    yield
    ranks = [rank_s[8 * jb:8 * jb + 8, :] for jb in range(ngrp)]
    blocks_per_tile = TK // SLC_BLOCK
    st_s[0] = qi
    n_selt = jnp.int32(1)
    for jb in range(ngrp):
        chosen = ranks[jb] < n_sel
        selb = jnp.where(chosen, 0.0, NEG)
        for r in range(REP):
            selbias_s[8 * jb:8 * jb + 8, r * TQ:(r + 1) * TQ] = selb
        any_q = jnp.max(jnp.where(chosen, 1.0, 0.0), axis=1, keepdims=True)
        for hh in range(8 // blocks_per_tile):
            kt = (8 * jb) // blocks_per_tile + hh
            hit = jnp.max(any_q[hh * blocks_per_tile:(hh + 1) * blocks_per_tile, :]) > 0.5
            st_s[n_selt] = kt
            n_selt = n_selt + jnp.where(hit & (kt < qi), 1, 0)
    st_s[n_selt] = nkt
    qall[SEL, HEAD_DIM:HEAD_DIM + nblk, :] = selbias_s[...].astype(BF16)
    yield

    yield from stage_probs(WIN, 1)
    yield from stage_values(WIN, [w2], al_w1)
    yield from start_branch(SEL, st_s)
    yield pair_items(SEL, n_selt, st_s)
    yield from finish_branch(SEL, n_selt, st_s)
    yield

    def normalised(br):
        acc = acc_all[br]
        return acc[0:HEAD_DIM, :] / acc[HEAD_DIM:HEAD_DIM + 1, :]

    o_sel = normalised(SEL)
    o_win = normalised(WIN)
    for r in range(REP):
        lanes = slice(r * TQ, (r + 1) * TQ)
        oT_s[r * HEAD_DIM:(r + 1) * HEAD_DIM, :] = (oacc[:, lanes] + gates[1][r:r + 1, :] * o_sel[:, lanes]
                                                    + gates[2][r:r + 1, :] * o_win[:, lanes])
    o_ref[0, lanes_c, :] = oT_s[...].T.astype(BF16)


def _attention(qT, ks, vs, kw, vw, kc, vcT, gT):
    B, _, S = qT.shape
    G = N_KV
    assert S % (NCHAIN * TQ) == 0 and (S // SLC_BLOCK) % 8 == 0 and S // SLC_BLOCK <= HEAD_DIM
    nkt = S // TK
    ncmp = kc.shape[2]
    assert ncmp == S // CMP_STRIDE and ncmp // 4 < 256
    nblk = S // SLC_BLOCK
    n_sel = min(SLC_TOPK, nblk)
    W4 = REP * TQ
    res4 = lambda a: pl.BlockSpec((1, 1) + a.shape[2:], lambda b, g, i: (b, g) + (0,) * (a.ndim - 2))
    return pl.pallas_call(
        functools.partial(_attn_kernel, n_sel=n_sel),
        grid=(B, G, S // (NCHAIN * TQ)),
        in_specs=[pl.BlockSpec((1, REP * HEAD_DIM, NCHAIN * TQ), lambda b, g, i: (b, g, i)),
                  res4(ks), res4(vs), res4(kw), res4(vw), res4(kc), res4(vcT),
                  pl.BlockSpec((1, 1, GATE_ROWS, NCHAIN * TQ), lambda b, g, i: (b, g, 0, i))],
        out_specs=pl.BlockSpec((1, NCHAIN * TQ, REP * HEAD_DIM), lambda b, g, i: (b, i, g)),
        out_shape=jax.ShapeDtypeStruct((B, S, D_ATT), BF16),
        scratch_shapes=[pltpu.VMEM((2, S + TK, KAUG), BF16),
                        pltpu.VMEM((2, nkt + 1, VROWS, TK), BF16),
                        pltpu.VMEM((NCHAIN, 2, KAUG, W4), BF16),
                        pltpu.VMEM((ncmp, KAUG), BF16),
                        pltpu.VMEM((NCHAIN, 2, 1, W4), F32),
                        pltpu.VMEM((NCHAIN, 2, VROWS, W4), F32),
                        pltpu.VMEM((NCHAIN, 1, W4), F32),
                        pltpu.VMEM((NCHAIN, 2 * TK, W4), F32),
                        pltpu.VMEM((NCHAIN, 2 * TK, W4), BF16),
                        pltpu.VMEM((NCHAIN, HEAD_DIM, W4), F32),
                        pltpu.VMEM((NCHAIN, nblk, TQ), F32),
                        pltpu.VMEM((NCHAIN, nblk, TQ), F32),
                        pltpu.VMEM((NCHAIN, nblk, W4), F32),
                        pltpu.VMEM((3, TK, W4), F32),
                        pltpu.VMEM((2 * ncmp, W4), F32),
                        pltpu.VMEM((NCHAIN, REP * HEAD_DIM, TQ), F32),
                        pltpu.SMEM((NCHAIN, nkt + 2), I32)],
        compiler_params=pltpu.CompilerParams(dimension_semantics=("arbitrary", "arbitrary", "arbitrary"),
                                             vmem_limit_bytes=VMEM_LIMIT),
        name="attn",
    )(qT, ks, vs, kw, vw, kc, vcT, gT)


def _outmlp_kernel(x_ref, rnn_ref, att_ref, gatt_ref, wo_ref, gpost_ref, g1_ref, gpre_ref, sc2_ref, sh2_ref,
                   w1_ref, w2_ref, gpost2_ref, g2_ref, o_ref):
    tm = x_ref.shape[1]
    sub = tm // OUT_SUBTILES
    fc = 1024

    def head(rows, att_n):
        y = (jnp.dot(rnn_ref[0, rows, :], wo_ref[0:D_RNN, :], preferred_element_type=F32)
             + jnp.dot(att_n, wo_ref[D_RNN:, :], preferred_element_type=F32))
        x1 = x_ref[0, rows, :] + (1.0 + g1_ref[0]) * (_rms(y) * gpost_ref[...])
        return x1, (_rms(x1) * (gpre_ref[...] * (1.0 + sc2_ref[0])) + sh2_ref[0]).astype(BF16)

    def mlp(rows, x1, h2):
        ff = jnp.zeros(x1.shape, F32)
        for c in range(D_FF // fc):
            hid = jnp.maximum(jnp.dot(h2, w1_ref[:, c * fc:(c + 1) * fc], preferred_element_type=F32), 0.0)
            ff = ff + jnp.dot((hid * hid).astype(BF16), w2_ref[c * fc:(c + 1) * fc, :],
                              preferred_element_type=F32)
        o_ref[0, rows, :] = x1 + (1.0 + g2_ref[0]) * (_rms(ff) * gpost2_ref[...])

    tiles = [slice(k * sub, (k + 1) * sub) for k in range(OUT_SUBTILES)]
    att_ns = [(_rms(att_ref[0, rows, :].astype(F32)) * gatt_ref[...]).astype(BF16) for rows in tiles]
    heads = [head(tiles[0], att_ns[0])]
    for k in range(OUT_SUBTILES):
        if k + 1 < OUT_SUBTILES:
            heads.append(head(tiles[k + 1], att_ns[k + 1]))
        mlp(tiles[k], *heads[k])


def _outmlp(x, rnn_n, att, gatt, wo, gpost, g1, gpre, sc2, sh2, w1, w2, gpost2, g2):
    B, S, D = x.shape
    tm = min(TM_OUT, S)
    row = lambda n: pl.BlockSpec((1, n), lambda b, s: (0, 0))
    per_b = lambda n: pl.BlockSpec((1, 1, n), lambda b, s: (b, 0, 0))
    const = lambda a: pl.BlockSpec(a.shape, lambda b, s: (0,) * a.ndim, pipeline_mode=pl.Buffered(1))
    tok = lambda n: pl.BlockSpec((1, tm, n), lambda b, s: (b, s, 0))
    return pl.pallas_call(
        _outmlp_kernel,
        grid=(B, S // tm),
        in_specs=[tok(D), tok(D_RNN), tok(D_ATT), row(D_ATT), const(wo), row(D), per_b(D), row(D),
                  per_b(D), per_b(D), const(w1), const(w2), row(D), per_b(D)],
        out_specs=tok(D),
        out_shape=jax.ShapeDtypeStruct((B, S, D), F32),
        compiler_params=pltpu.CompilerParams(dimension_semantics=("arbitrary", "arbitrary"),
                                             vmem_limit_bytes=VMEM_LIMIT),
        name="outmlp",
    )(x, rnn_n, att, gatt, wo, gpost, g1, gpre, sc2, sh2, w1, w2, gpost2, g2)


def _block_diag(w):
    n, k, _ = w.shape
    return jnp.einsum('nij,nm->nimj', w, jnp.eye(n, dtype=w.dtype)).reshape(n * k, n * k)


def _layer(x, c, ada_w, ada_b, pre_norm_mix, w_in, conv_w, conv_b, lru_wa, lru_ba, lru_wx, lru_bx, lru_lambda,
           cmp_pos_k, cmp_w1_k, cmp_w2_k, cmp_pos_v, cmp_w1_v, cmp_w2_v, norm_rnn_out, norm_att_out, w_out,
           post_norm_mix, pre_norm_mlp, w_ff1, w_ff2, post_norm_mlp):
    B, S, D = x.shape
    G = N_KV
    row = lambda v: v.reshape(1, -1)

    mod = _ada(c, ada_w, ada_b)
    sh1, sc1, g1, sh2, sc2, g2 = [m.reshape(B, 1, D) for m in jnp.split(mod, 6, axis=-1)]

    gate_cols = [OFF_GATE + br * N_HEADS + g * REP + r for g in range(G) for br in range(3) for r in range(REP)]
    w_gate = w_in[:, jnp.asarray(gate_cols)].reshape(D, G, 3 * REP)
    w_gate = jnp.pad(w_gate, ((0, 0), (0, 0), (0, GATE_ROWS - 3 * REP))).reshape(D, G * GATE_ROWS)
    w_in_p = jnp.concatenate([w_in[:, :OFF_GATE], jnp.pad(w_gate, ((0, 0), (0, GATE_PAD - G * GATE_ROWS)))],
                             axis=1).astype(BF16)
    wa = _block_diag(lru_wa).astype(BF16)
    wx = _block_diag(lru_wx).astype(BF16)
    half = CMP_LEN // 2 * HEAD_DIM

    def w1_cat(w1):
        return jnp.concatenate([w1[:half], w1[half:]], axis=1).astype(BF16)

    def pos_rows(pos):
        return jnp.pad(pos.reshape(2, half), ((0, 14), (0, 0))).astype(BF16)

    rnn_n, qT, ks, vsT, kw, vwT, kc_in, vc_in, gT = _inproj(
        x, row(pre_norm_mix), sc1, sh1, w_in_p, conv_w, row(conv_b), wa, row(lru_ba), wx, row(lru_bx),
        row(lru_lambda), row(norm_rnn_out))
    kc, vc = _compress(kc_in, vc_in, w1_cat(cmp_w1_k), cmp_w2_k.astype(BF16), pos_rows(cmp_pos_k),
                       w1_cat(cmp_w1_v), cmp_w2_v.astype(BF16), pos_rows(cmp_pos_v))
    att = _attention(qT, ks, vsT, kw, vwT, kc, vc.transpose(0, 1, 3, 2), gT)

    return _outmlp(x, rnn_n, att, row(norm_att_out), w_out.astype(BF16), row(post_norm_mix), g1,
                   row(pre_norm_mlp), sc2, sh2, w_ff1.astype(BF16), w_ff2.astype(BF16), row(post_norm_mlp), g2)


def kernel(x, c, ada_w, ada_b, pre_norm_mix, w_in, conv_w, conv_b, lru_wa, lru_ba, lru_wx, lru_bx, lru_lambda,
           cmp_pos_k, cmp_w1_k, cmp_w2_k, cmp_pos_v, cmp_w1_v, cmp_w2_v, norm_rnn_out, norm_att_out, w_out,
           post_norm_mix, pre_norm_mlp, w_ff1, w_ff2, post_norm_mlp):
    for l in range(ada_w.shape[0]):
        x = _layer(x, c, ada_w[l], ada_b[l], pre_norm_mix[l], w_in[l], conv_w[l], conv_b[l], lru_wa[l], lru_ba[l],
                   lru_wx[l], lru_bx[l], lru_lambda[l], cmp_pos_k[l], cmp_w1_k[l], cmp_w2_k[l], cmp_pos_v[l],
                   cmp_w1_v[l], cmp_w2_v[l], norm_rnn_out[l], norm_att_out[l], w_out[l], post_norm_mix[l],
                   pre_norm_mlp[l], w_ff1[l], w_ff2[l], post_norm_mlp[l])
    return x
```

```python
import functools

import jax
import jax.numpy as jnp
from jax import lax
from jax.experimental import pallas as pl
from jax.experimental.pallas import tpu as pltpu

F32 = jnp.float32
BF16 = jnp.bfloat16
I32 = jnp.int32

D_MODEL = 1024
D_RNN = 512
CONV_WIDTH = 4
LRU_C = 8.0
N_HEADS = 8
HEAD_DIM = 64
N_KV = 2
REP = N_HEADS // N_KV
D_ATT = N_HEADS * HEAD_DIM
CMP_LEN = 32
CMP_STRIDE = 16
CMP_HIDDEN = 256
SLC_BLOCK = 64
SLC_TOPK = 16
WINDOW = 512
D_FF = 4 * D_MODEL
EPS = 1e-6
NEG = -1e30
FORCED = 1e4
LOG2E = 1.4426950408889634

KV_COLS = 6 * N_KV * HEAD_DIM
N_GATE = 3 * N_HEADS
GATE_PAD = 128
GATE_ROWS = 16
OFF_Q = 2 * D_RNN
OFF_KV = OFF_Q + D_ATT
OFF_GATE = OFF_KV + KV_COLS
D_IN_PAD = OFF_GATE + GATE_PAD

TM_IN = 1024
TM_OUT = 1024
OUT_SUBTILES = 4
TQ = 256
TK = 256
NCHAIN = 4
KAUG = 256
VROWS = 80
VMEM_LIMIT = 56 * 1024 * 1024

assert TQ == TK and WINDOW == 2 * TK
assert TK % SLC_BLOCK == 0 and CMP_LEN == 2 * CMP_STRIDE and CONV_WIDTH == 4
assert N_GATE <= N_KV * GATE_ROWS <= GATE_PAD and D_IN_PAD % 128 == 0
assert 2 * HEAD_DIM + 4 <= KAUG and HEAD_DIM < VROWS

SEL, WIN = 0, 1
MASK_NONE, MASK_CAUSAL, MASK_WINDOW_LOW = 0, 1, 2
POS_COL = 2 * HEAD_DIM
DUMMY_COL = POS_COL + 4


def _gelu_tanh(x):
    return 0.5 * x * (1.0 + jnp.tanh(0.7978845608028654 * (x + 0.044715 * (x * x * x))))


def _sigmoid(x):
    return 0.5 * jnp.tanh(0.5 * x) + 0.5


def _rms(x):
    return x * lax.rsqrt(jnp.mean(x * x, axis=-1, keepdims=True) + EPS)


def _ada_kernel(c_ref, w_ref, b_ref, o_ref):
    c = c_ref[...]
    a = c * jax.nn.sigmoid(c)
    o_ref[...] = jnp.dot(a, w_ref[...], preferred_element_type=F32,
                         precision=lax.Precision.HIGHEST) + b_ref[...]


def _ada(c, w, b):
    B, D = c.shape
    N = w.shape[1]
    tn = 1024
    return pl.pallas_call(
        _ada_kernel,
        grid=(N // tn,),
        in_specs=[pl.BlockSpec((B, D), lambda j: (0, 0)),
                  pl.BlockSpec((D, tn), lambda j: (0, j)),
                  pl.BlockSpec((1, tn), lambda j: (0, j))],
        out_specs=pl.BlockSpec((B, tn), lambda j: (0, j)),
        out_shape=jax.ShapeDtypeStruct((B, N), F32),
        name="ada",
    )(c, w, b.reshape(1, N))


def _inproj_kernel(x_ref, gain_ref, sc_ref, sh_ref, w_ref, cw_ref, cb_ref, wa_ref, ba_ref, wx_ref, bx_ref,
                   lam_ref, grnn_ref,
                   rnn_ref, qT_ref, ks_ref, vs_ref, kw_ref, vw_ref, kc_ref, vc_ref, g_ref,
                   xbuf, hcar, a_s, u_s, h_s):
    tm = x_ref.shape[1]
    hd = HEAD_DIM

    @pl.when(pl.program_id(1) == 0)
    def _():
        xbuf[0:8, :] = jnp.zeros((8, D_RNN), F32)
        hcar[...] = jnp.zeros((1, D_RNN), F32)

    x = x_ref[0]
    h = _rms(x) * (gain_ref[...] * (1.0 + sc_ref[0])) + sh_ref[0]
    hb = h.astype(BF16)

    xr = jnp.dot(hb, w_ref[:, D_RNN:OFF_Q], preferred_element_type=F32)
    xbuf[8:8 + tm, :] = xr
    y = (cw_ref[3:4, :] * xr + cw_ref[2:3, :] * xbuf[7:7 + tm, :]
         + cw_ref[1:2, :] * xbuf[6:6 + tm, :] + cw_ref[0:1, :] * xbuf[5:5 + tm, :]) + cb_ref[...]
    xbuf[0:8, :] = xbuf[tm:tm + 8, :]

    yb = y.astype(BF16)
    r = _sigmoid(jnp.dot(yb, wa_ref[...], preferred_element_type=F32) + ba_ref[...])
    i = _sigmoid(jnp.dot(yb, wx_ref[...], preferred_element_type=F32) + bx_ref[...])
    nl = -lam_ref[...]
    softplus = jnp.maximum(nl, 0.0) + jnp.log(1.0 + jnp.exp(-jnp.abs(nl)))
    a = jnp.exp((-LRU_C) * r * softplus)
    a_s[...] = a
    u_s[...] = jnp.sqrt(1.0 - a * a) * (i * y)

    qT_ref[0] = (jnp.dot(hb, w_ref[:, OFF_Q:OFF_KV], preferred_element_type=F32)
                 * (HEAD_DIM ** -0.5 * LOG2E)).T.astype(BF16)
    kv = jnp.dot(hb, w_ref[:, OFF_KV:OFF_GATE], preferred_element_type=F32)
    width = N_KV * hd
    vsT = kv[:, 3 * width:4 * width].T
    vwT = kv[:, 5 * width:6 * width].T
    for gi in range(N_KV):
        kc_ref[0, gi] = kv[:, gi * hd:(gi + 1) * hd]
        vc_ref[0, gi] = kv[:, width + gi * hd:width + (gi + 1) * hd]
        ks_ref[0, gi] = kv[:, 2 * width + gi * hd:2 * width + (gi + 1) * hd].astype(BF16)
        kw_ref[0, gi] = kv[:, 4 * width + gi * hd:4 * width + (gi + 1) * hd].astype(BF16)
        for j in range(tm // TK):
            vs_ref[0, gi, j] = vsT[gi * hd:(gi + 1) * hd, j * TK:(j + 1) * TK].astype(BF16)
            vw_ref[0, gi, j] = vwT[gi * hd:(gi + 1) * hd, j * TK:(j + 1) * TK].astype(BF16)
    gates = _sigmoid(jnp.dot(hb, w_ref[:, OFF_GATE:D_IN_PAD], preferred_element_type=F32)).T
    for gi in range(N_KV):
        g_ref[0, gi] = gates[gi * GATE_ROWS:(gi + 1) * GATE_ROWS, :]
    g = jnp.dot(hb, w_ref[:, 0:D_RNN], preferred_element_type=F32)


    rows = lax.broadcasted_iota(I32, (8, D_RNN), 0)
    hprev = hcar[...]
    for gi in range(tm // 8):
        ag = a_s[gi * 8:gi * 8 + 8, :]
        ug = u_s[gi * 8:gi * 8 + 8, :]
        for k in (1, 2, 4):
            a_sh = jnp.where(rows >= k, pltpu.roll(ag, k, 0), 1.0)
            u_sh = jnp.where(rows >= k, pltpu.roll(ug, k, 0), 0.0)
            ug = ag * u_sh + ug
            ag = ag * a_sh
        hg = ag * hprev + ug
        h_s[gi * 8:gi * 8 + 8, :] = hg
        hprev = hg[7:8, :]
    hcar[...] = hprev

    rnn = _gelu_tanh(g) * h_s[...]
    rnn_ref[0] = (_rms(rnn) * grnn_ref[...]).astype(BF16)


def _inproj(x, gain, sc, sh, w_in, conv_w, conv_b, wa, ba, wx, bx, lam, grnn):
    B, S, D = x.shape
    tm = min(TM_IN, S)
    row = lambda n: pl.BlockSpec((1, n), lambda b, s: (0, 0))
    per_b = lambda n: pl.BlockSpec((1, 1, n), lambda b, s: (b, 0, 0))
    full = lambda a: pl.BlockSpec(a.shape, lambda b, s: (0,) * a.ndim)
    tok = lambda n: pl.BlockSpec((1, tm, n), lambda b, s: (b, s, 0))
    G = N_KV
    grp = pl.BlockSpec((1, G, tm, HEAD_DIM), lambda b, s: (b, 0, s, 0))
    grpT = pl.BlockSpec((1, G, tm // TK, HEAD_DIM, TK), lambda b, s: (b, 0, s, 0, 0))
    return pl.pallas_call(
        _inproj_kernel,
        grid=(B, S // tm),
        in_specs=[tok(D), row(D), per_b(D), per_b(D), full(w_in), full(conv_w), row(D_RNN),
                  full(wa), row(D_RNN), full(wx), row(D_RNN), row(D_RNN), row(D_RNN)],
        out_specs=[tok(D_RNN),
                   pl.BlockSpec((1, D_ATT, tm), lambda b, s: (b, 0, s)),
                   grp, grpT, grp, grpT, grp, grp,
                   pl.BlockSpec((1, G, GATE_ROWS, tm), lambda b, s: (b, 0, 0, s))],
        out_shape=[jax.ShapeDtypeStruct((B, S, D_RNN), BF16),
                   jax.ShapeDtypeStruct((B, D_ATT, S), BF16),
                   jax.ShapeDtypeStruct((B, G, S, HEAD_DIM), BF16),
                   jax.ShapeDtypeStruct((B, G, S // TK, HEAD_DIM, TK), BF16),
                   jax.ShapeDtypeStruct((B, G, S, HEAD_DIM), BF16),
                   jax.ShapeDtypeStruct((B, G, S // TK, HEAD_DIM, TK), BF16),
                   jax.ShapeDtypeStruct((B, G, S, HEAD_DIM), F32),
                   jax.ShapeDtypeStruct((B, G, S, HEAD_DIM), F32),
                   jax.ShapeDtypeStruct((B, G, GATE_ROWS, S), F32)],
        scratch_shapes=[pltpu.VMEM((tm + 8, D_RNN), F32), pltpu.VMEM((1, D_RNN), F32),
                        pltpu.VMEM((tm, D_RNN), F32), pltpu.VMEM((tm, D_RNN), F32),
                        pltpu.VMEM((tm, D_RNN), F32)],
        compiler_params=pltpu.CompilerParams(dimension_semantics=("arbitrary", "arbitrary"),
                                             vmem_limit_bytes=VMEM_LIMIT),
        name="inproj",
    )(x, gain, sc, sh, w_in, conv_w, conv_b, wa, ba, wx, bx, lam, grnn)


def _compress_kernel(rk_ref, rv_ref, w1k_ref, w2k_ref, pk_ref, w1v_ref, w2v_ref, pv_ref, kc_ref, vc_ref):
    def one(r_ref, w1_ref, w2_ref, pos_ref, o_ref):
        n = r_ref.shape[2] // CMP_STRIDE
        rows = jnp.concatenate([r_ref[0, 0, pl.ds(l, n, stride=CMP_STRIDE), :] for l in range(CMP_STRIDE)], axis=1)
        p = jnp.dot(rows.astype(BF16), w1_ref[...], preferred_element_type=F32)
        posb = jnp.dot(pos_ref[...], w1_ref[...], preferred_element_type=F32)
        bias = posb[0:1, 0:CMP_HIDDEN] + posb[1:2, CMP_HIDDEN:]
        pre = p[:, 0:CMP_HIDDEN] + pltpu.roll(p[:, CMP_HIDDEN:], n - 1, 0) + bias
        hid = _gelu_tanh(pre).astype(BF16)
        o_ref[0, 0] = jnp.dot(hid, w2_ref[...], preferred_element_type=F32).astype(BF16)

    one(rk_ref, w1k_ref, w2k_ref, pk_ref, kc_ref)
    one(rv_ref, w1v_ref, w2v_ref, pv_ref, vc_ref)


def _compress(rk, rv, w1k, w2k, posk, w1v, w2v, posv):
    B, G, S, width = rk.shape
    n = S // CMP_STRIDE
    blk = pl.BlockSpec((1, 1, S, width), lambda b, g: (b, g, 0, 0))
    full = lambda a: pl.BlockSpec(a.shape, lambda b, g: (0,) * a.ndim)
    out = pl.BlockSpec((1, 1, n, HEAD_DIM), lambda b, g: (b, g, 0, 0))
    return pl.pallas_call(
        _compress_kernel,
        grid=(B, G),
        in_specs=[blk, blk, full(w1k), full(w2k), full(posk), full(w1v), full(w2v), full(posv)],
        out_specs=[out, out],
        out_shape=[jax.ShapeDtypeStruct((B, G, n, HEAD_DIM), BF16)] * 2,
        compiler_params=pltpu.CompilerParams(vmem_limit_bytes=VMEM_LIMIT),
        name="compress",
    )(rk, rv, w1k, w2k, posk, w1v, w2v, posv)


def _attn_kernel(qT_ref, ks_ref, vs_ref, kw_ref, vw_ref, kc_ref, vcT_ref, g_ref,
                 o_ref,
                 kall, vall, qall, kca, m_all, acc_all, al_s, sbuf, pbuf, oacc, val_s, rank_s, selbias_s,
                 bias_tbl, cmask,
                 oT_s, st_s, *, n_sel):
    S = ks_ref.shape[2]
    nkt = S // TK
    ncmp = kc_ref.shape[2]
    b = pl.program_id(0)
    g = pl.program_id(1)
    step = pl.program_id(2)
    W4 = REP * TQ

    @pl.when((b == 0) & (g == 0) & (step == 0))
    def _():
        col = lax.broadcasted_iota(I32, (TK, KAUG), 1)
        is_blk = (col == POS_COL) | (col == POS_COL + 2)
        is_off = (col == POS_COL + 1) | (col == POS_COL + 3)
        ones_row = (lax.broadcasted_iota(I32, (VROWS - HEAD_DIM, TK), 0) == 0).astype(BF16)
        for kt in range(nkt):
            key = kt * TK + lax.broadcasted_iota(I32, (TK, KAUG), 0)
            pos = jnp.where(is_blk, (key // SLC_BLOCK).astype(F32),
                            jnp.where(is_off, (key % SLC_BLOCK).astype(F32), 0.0))
            onehot = (col - HEAD_DIM == key // SLC_BLOCK).astype(F32)
            kall[SEL, kt * TK:(kt + 1) * TK, :] = (pos + onehot).astype(BF16)
            kall[WIN, kt * TK:(kt + 1) * TK, :] = pos.astype(BF16)
            vall[SEL, kt, HEAD_DIM:VROWS, :] = ones_row
            vall[WIN, kt, HEAD_DIM:VROWS, :] = ones_row
        dummy = (col == DUMMY_COL).astype(BF16)
        for br in (SEL, WIN):
            kall[br, nkt * TK:(nkt + 1) * TK, :] = dummy
            vall[br, nkt] = jnp.zeros((VROWS, TK), BF16)
        ko = lax.broadcasted_iota(I32, (TK, W4), 0)
        to = lax.broadcasted_iota(I32, (TK, W4), 1) % TQ
        bias_tbl[MASK_NONE] = jnp.zeros((TK, W4), F32)
        bias_tbl[MASK_CAUSAL] = jnp.where(ko <= to, 0.0, NEG)
        bias_tbl[MASK_WINDOW_LOW] = jnp.where(ko > to, 0.0, NEG)
        u = lax.broadcasted_iota(I32, (2 * ncmp, W4), 0)
        tc = lax.broadcasted_iota(I32, (2 * ncmp, W4), 1) % TQ
        cmask[...] = jnp.where(u <= ncmp + ((tc + 1) // CMP_STRIDE) - 2, 0.0, NEG)
        cc = lax.broadcasted_iota(I32, (ncmp, KAUG), 0)
        colc = lax.broadcasted_iota(I32, (ncmp, KAUG), 1)
        kca[...] = jnp.where((colc == POS_COL) | (colc == POS_COL + 2), (cc // 4).astype(F32),
                             jnp.where((colc == POS_COL + 1) | (colc == POS_COL + 3),
                                       (CMP_STRIDE * (cc % 4)).astype(F32), 0.0)).astype(BF16)
        qall[...] = jnp.zeros(qall.shape, BF16)

    lane = lax.broadcasted_iota(I32, (1, W4), 1)
    head = g * REP + lane // TQ
    slope = lax.bitcast_convert_type((126 - head) << 23, F32)

    @pl.when(step == 0)
    def _():
        for kt in range(nkt):
            kall[SEL, kt * TK:(kt + 1) * TK, 0:HEAD_DIM] = ks_ref[0, 0, kt * TK:(kt + 1) * TK, :]
            kall[WIN, kt * TK:(kt + 1) * TK, 0:HEAD_DIM] = kw_ref[0, 0, kt * TK:(kt + 1) * TK, :]
            vall[SEL, kt, 0:HEAD_DIM, :] = vs_ref[0, 0, kt]
            vall[WIN, kt, 0:HEAD_DIM, :] = vw_ref[0, 0, kt]
        kca[:, 0:HEAD_DIM] = kc_ref[0, 0]
        c_hi = (slope * LOG2E).astype(BF16).astype(F32)
        c_lo = slope * LOG2E - c_hi
        r16 = lax.broadcasted_iota(I32, (16, W4), 0)
        alibi = jnp.where(r16 == 0, c_hi * SLC_BLOCK, jnp.where(r16 == 1, c_hi, jnp.where(
            r16 == 2, c_lo * SLC_BLOCK, jnp.where(r16 == 3, c_lo, jnp.where(
                r16 == DUMMY_COL - POS_COL, NEG, 0.0))))).astype(BF16)
        for c in range(NCHAIN):
            qall[c, SEL, POS_COL:POS_COL + 16, :] = alibi
            qall[c, WIN, POS_COL:POS_COL + 16, :] = alibi

    chains = [_attn_chain(c, NCHAIN * step + c, qT_ref, kc_ref, vcT_ref, g_ref, o_ref, kall, vall, qall.at[c], kca,
                          m_all.at[c], acc_all.at[c], al_s.at[c], sbuf.at[c], pbuf.at[c], oacc.at[c], val_s.at[c],
                          rank_s.at[c], selbias_s.at[c], bias_tbl, cmask, oT_s.at[c], st_s.at[c], n_sel)
              for c in range(NCHAIN)]
    live = list(chains)
    while live:
        handed = [next(ch, _DONE) for ch in live]
        live = [ch for ch, r in zip(live, handed) if r is not _DONE]
        loops = [r for r in handed if r is not _DONE and r is not None]
        if loops:
            joint = functools.reduce(jnp.minimum, [npair for npair, _, _ in loops])

            def trip(j, group):
                als = [head(j) for _, head, _ in group]
                for (_, _, tail), al in zip(group, als):
                    tail(j, al)

            lax.fori_loop(1, 1 + joint, lambda j, carry: (trip(j, loops), carry)[1], 0)
            for item in loops:
                lax.fori_loop(1 + joint, 1 + item[0], lambda j, carry, item=item: (trip(j, [item]), carry)[1], 0)


_DONE = object()


def _drain(gen):
    try:
        while True:
            next(gen)
    except StopIteration as stop:
        return stop.value


def _attn_chain(c, qi, qT_ref, kc_ref, vcT_ref, g_ref, o_ref, kall, vall, qall, kca, m_all, acc_all, al_s, sbuf,
                pbuf, oacc, val_s, rank_s, selbias_s, bias_tbl, cmask, oT_s, st_s, n_sel):
    S = kall.shape[1] - TK
    ncmp = kc_ref.shape[2]
    nblk = S // SLC_BLOCK
    cmp_per_tile = TQ // CMP_STRIDE
    q0 = qi * TQ
    W4 = REP * TQ
    lane = lax.broadcasted_iota(I32, (1, W4), 1)
    lanes_c = slice(c * TQ, (c + 1) * TQ)

    q = qT_ref[0, :, lanes_c]
    for r in range(REP):
        qr = q[r * HEAD_DIM:(r + 1) * HEAD_DIM, :]
        qall[SEL, 0:HEAD_DIM, r * TQ:(r + 1) * TQ] = qr
        qall[WIN, 0:HEAD_DIM, r * TQ:(r + 1) * TQ] = qr

    gall = g_ref[0, 0, :, lanes_c]
    gates = [gall[br * REP:(br + 1) * REP, :] for br in range(3)]

    def stage_scores(br, tiles_masks):
        col_max = None
        for h, (tile, mask) in enumerate(tiles_masks):
            r0 = pl.multiple_of(tile * TK, TK)
            s = jnp.dot(kall[br, pl.ds(r0, TK), :], qall[br], preferred_element_type=F32)
            if mask is not None:
                s = s + bias_tbl[mask]
            sbuf[h * TK:(h + 1) * TK, :] = s
            c = jnp.max(s, axis=0, keepdims=True)
            col_max = c if col_max is None else jnp.maximum(col_max, c)
            yield
        m_old = m_all[br]
        m_new = jnp.maximum(m_old, col_max)
        m_all[br] = m_new
        return jnp.exp2(m_old - m_new)

    def stage_probs(br, ntile):
        for h in range(ntile):
            rows = slice(h * TK, (h + 1) * TK)
            pbuf[rows, :] = jnp.exp2(sbuf[rows, :] - m_all[br]).astype(BF16)
            yield

    def stage_values(br, tiles, al_row):
        acc = acc_all[br] * al_row
        for h, tile in enumerate(tiles):
            acc = acc + jnp.dot(vall[br, tile], pbuf[h * TK:(h + 1) * TK, :], preferred_element_type=F32)
            if h + 1 < len(tiles):
                yield
        acc_all[br] = acc
        yield

    def start_branch(br, tl):
        al_s[...] = yield from stage_scores(br, [(tl[0], MASK_CAUSAL), (tl[1], None)])

    def pair_items(br, n, tl):
        npair = jnp.maximum(n - 2, 0) // 2

        def head(j):
            al_prev = al_s[...]
            _drain(stage_probs(br, 2))
            al_s[...] = _drain(stage_scores(br, [(tl[2 * j], None), (tl[2 * j + 1], None)]))
            return al_prev

        def tail(j, al_prev):
            _drain(stage_values(br, [tl[2 * j - 2], tl[2 * j - 1]], al_prev))

        return npair, head, tail

    def finish_branch(br, n, tl):
        rest = jnp.maximum(n - 2, 0)
        last = 2 * (rest // 2)
        t_single = tl[jnp.where(rest % 2 == 1, n - 1, n)]
        al_prev = al_s[...]
        yield from stage_probs(br, 2)
        al_k = yield from stage_scores(br, [(t_single, None)])
        yield from stage_values(br, [tl[last], tl[last + 1]], al_prev)
        yield from stage_probs(br, 1)
        yield from stage_values(br, [t_single], al_k)

    m_all[...] = jnp.full(m_all.shape, NEG, F32)
    acc_all[...] = jnp.zeros(acc_all.shape, F32)
    nkt = S // TK
    w1 = jnp.where(qi >= 2, qi - 2, jnp.where(qi >= 1, 0, nkt))
    w2 = jnp.where(qi >= 2, qi - 1, nkt)
    c0 = pl.multiple_of(ncmp - cmp_per_tile * qi, cmp_per_tile)
    sc = jnp.dot(kca[...], qall[WIN], preferred_element_type=F32) + cmask[pl.ds(c0, ncmp), :]
    yield
    al_w0 = yield from stage_scores(
        WIN, [(qi, MASK_CAUSAL), (w1, jnp.where(qi >= 2, MASK_WINDOW_LOW, MASK_NONE))])
    e = jnp.exp2(sc - jnp.max(sc, axis=0, keepdims=True))
    tq = q0 + lane % TQ
    p = e * ((1.0 / jnp.sum(e, axis=0, keepdims=True)) * (tq >= CMP_LEN - 1).astype(F32))
    ocT = jnp.dot(vcT_ref[0, 0], p.astype(BF16), preferred_element_type=F32)
    for r in range(REP):
        oacc[:, r * TQ:(r + 1) * TQ] = gates[0][r:r + 1, :] * ocT[:, r * TQ:(r + 1) * TQ]
    yield

    yield from stage_probs(WIN, 2)
    al_w1 = yield from stage_scores(WIN, [(w2, None)])
    yield from stage_values(WIN, [qi, w1], al_w0)

    psum = p[:, 0:TQ]
    for r in range(1, REP):
        psum = psum + p[:, r * TQ:(r + 1) * TQ]
    jj = lax.broadcasted_iota(I32, (nblk, ncmp), 0)
    cc = lax.broadcasted_iota(I32, (nblk, ncmp), 1)
    ovT = ((CMP_STRIDE * cc < SLC_BLOCK * jj + SLC_BLOCK)
           & (CMP_STRIDE * cc + CMP_LEN > SLC_BLOCK * jj)).astype(F32)
    impT = jnp.dot(ovT, psum, preferred_element_type=F32, precision=lax.Precision.HIGHEST)

    j_i = lax.broadcasted_iota(I32, (nblk, TQ), 0)
    t1 = q0 + lax.broadcasted_iota(I32, (nblk, TQ), 1)
    cur = t1 // SLC_BLOCK
    forced = (j_i == 0) | (j_i == cur) | (j_i == cur - 1)
    visible = SLC_BLOCK * j_i <= t1
    val_s[...] = jnp.where(forced, FORCED, jnp.where(visible, impT, NEG))
    yield
    ngrp = nblk // 8
    rank_s[...] = jnp.zeros(rank_s.shape, F32)
    j8 = lax.broadcasted_iota(I32, (8, TQ), 0)
    for ib in range(ngrp):
        @pl.when(8 * ib * SLC_BLOCK < q0 + TQ)
        def _(ib=ib):
            vals = [val_s[8 * jb:8 * jb + 8, :] for jb in range(ngrp)]
            ranks = [rank_s[8 * jb:8 * jb + 8, :] for jb in range(ngrp)]
            for i in range(8 * ib, 8 * ib + 8):
                row = jnp.broadcast_to(val_s[i:i + 1, :], (8, TQ))
                for jb in range(ngrp):
                    if jb > ib:
                        hit = row >= vals[jb]
                    elif jb < ib:
                        hit = row > vals[jb]
                    else:
                        hit = jnp.where(j8 > i - 8 * jb, jnp.where(row >= vals[jb], 1.0, 0.0),
                                        jnp.where(row > vals[jb], 1.0, 0.0)) > 0.5
                    ranks[jb] = ranks[jb] + jnp.where(hit, 1.0, 0.0)
            for jb in range(ngrp):
                rank_s[8 * jb:8 * jb + 8, :] = ranks[jb]
    yield
    ranks = [rank_s[8 * jb:8 * jb + 8, :] for jb in range(ngrp)]
    blocks_per_tile = TK // SLC_BLOCK
    st_s[0] = qi
    n_selt = jnp.int32(1)
    for jb in range(ngrp):
        chosen = ranks[jb] < n_sel
        selb = jnp.where(chosen, 0.0, NEG)
        for r in range(REP):
            selbias_s[8 * jb:8 * jb + 8, r * TQ:(r + 1) * TQ] = selb
        any_q = jnp.max(jnp.where(chosen, 1.0, 0.0), axis=1, keepdims=True)
        for hh in range(8 // blocks_per_tile):
            kt = (8 * jb) // blocks_per_tile + hh
            hit = jnp.max(any_q[hh * blocks_per_tile:(hh + 1) * blocks_per_tile, :]) > 0.5
            st_s[n_selt] = kt
            n_selt = n_selt + jnp.where(hit & (kt < qi), 1, 0)
    st_s[n_selt] = nkt
    qall[SEL, HEAD_DIM:HEAD_DIM + nblk, :] = selbias_s[...].astype(BF16)
    yield

    yield from stage_probs(WIN, 1)
    yield from stage_values(WIN, [w2], al_w1)
    yield from start_branch(SEL, st_s)
    yield pair_items(SEL, n_selt, st_s)
    yield from finish_branch(SEL, n_selt, st_s)
    yield

    def normalised(br):
        acc = acc_all[br]
        return acc[0:HEAD_DIM, :] / acc[HEAD_DIM:HEAD_DIM + 1, :]

    o_sel = normalised(SEL)
    o_win = normalised(WIN)
    for r in range(REP):
        lanes = slice(r * TQ, (r + 1) * TQ)
        oT_s[r * HEAD_DIM:(r + 1) * HEAD_DIM, :] = (oacc[:, lanes] + gates[1][r:r + 1, :] * o_sel[:, lanes]
                                                    + gates[2][r:r + 1, :] * o_win[:, lanes])
    o_ref[0, lanes_c, :] = oT_s[...].T.astype(BF16)


def _attention(qT, ks, vs, kw, vw, kc, vcT, gT):
    B, _, S = qT.shape
    G = N_KV
    assert S % (NCHAIN * TQ) == 0 and (S // SLC_BLOCK) % 8 == 0 and S // SLC_BLOCK <= HEAD_DIM
    nkt = S // TK
    ncmp = kc.shape[2]
    assert ncmp == S // CMP_STRIDE and ncmp // 4 < 256
    nblk = S // SLC_BLOCK
    n_sel = min(SLC_TOPK, nblk)
    W4 = REP * TQ
    res4 = lambda a: pl.BlockSpec((1, 1) + a.shape[2:], lambda b, g, i: (b, g) + (0,) * (a.ndim - 2))
    return pl.pallas_call(
        functools.partial(_attn_kernel, n_sel=n_sel),
        grid=(B, G, S // (NCHAIN * TQ)),
        in_specs=[pl.BlockSpec((1, REP * HEAD_DIM, NCHAIN * TQ), lambda b, g, i: (b, g, i)),
                  res4(ks), res4(vs), res4(kw), res4(vw), res4(kc), res4(vcT),
                  pl.BlockSpec((1, 1, GATE_ROWS, NCHAIN * TQ), lambda b, g, i: (b, g, 0, i))],
        out_specs=pl.BlockSpec((1, NCHAIN * TQ, REP * HEAD_DIM), lambda b, g, i: (b, i, g)),
        out_shape=jax.ShapeDtypeStruct((B, S, D_ATT), BF16),
        scratch_shapes=[pltpu.VMEM((2, S + TK, KAUG), BF16),
                        pltpu.VMEM((2, nkt + 1, VROWS, TK), BF16),
                        pltpu.VMEM((NCHAIN, 2, KAUG, W4), BF16),
                        pltpu.VMEM((ncmp, KAUG), BF16),
                        pltpu.VMEM((NCHAIN, 2, 1, W4), F32),
                        pltpu.VMEM((NCHAIN, 2, VROWS, W4), F32),
                        pltpu.VMEM((NCHAIN, 1, W4), F32),
                        pltpu.VMEM((NCHAIN, 2 * TK, W4), F32),
                        pltpu.VMEM((NCHAIN, 2 * TK, W4), BF16),
                        pltpu.VMEM((NCHAIN, HEAD_DIM, W4), F32),
                        pltpu.VMEM((NCHAIN, nblk, TQ), F32),
                        pltpu.VMEM((NCHAIN, nblk, TQ), F32),
                        pltpu.VMEM((NCHAIN, nblk, W4), F32),
                        pltpu.VMEM((3, TK, W4), F32),
                        pltpu.VMEM((2 * ncmp, W4), F32),
                        pltpu.VMEM((NCHAIN, REP * HEAD_DIM, TQ), F32),
                        pltpu.SMEM((NCHAIN, nkt + 2), I32)],
        compiler_params=pltpu.CompilerParams(dimension_semantics=("arbitrary", "arbitrary", "arbitrary"),
                                             vmem_limit_bytes=VMEM_LIMIT),
        name="attn",
    )(qT, ks, vs, kw, vw, kc, vcT, gT)


def _outmlp_kernel(x_ref, rnn_ref, att_ref, gatt_ref, wo_ref, gpost_ref, g1_ref, gpre_ref, sc2_ref, sh2_ref,
                   w1_ref, w2_ref, gpost2_ref, g2_ref, o_ref):
    tm = x_ref.shape[1]
    sub = tm // OUT_SUBTILES
    fc = 1024

    def head(rows, att_n):
        y = (jnp.dot(rnn_ref[0, rows, :], wo_ref[0:D_RNN, :], preferred_element_type=F32)
             + jnp.dot(att_n, wo_ref[D_RNN:, :], preferred_element_type=F32))
        x1 = x_ref[0, rows, :] + (1.0 + g1_ref[0]) * (_rms(y) * gpost_ref[...])
        return x1, (_rms(x1) * (gpre_ref[...] * (1.0 + sc2_ref[0])) + sh2_ref[0]).astype(BF16)

    def mlp(rows, x1, h2):
        ff = jnp.zeros(x1.shape, F32)
        for c in range(D_FF // fc):
            hid = jnp.maximum(jnp.dot(h2, w1_ref[:, c * fc:(c + 1) * fc], preferred_element_type=F32), 0.0)
            ff = ff + jnp.dot((hid * hid).astype(BF16), w2_ref[c * fc:(c + 1) * fc, :],
                              preferred_element_type=F32)
        o_ref[0, rows, :] = x1 + (1.0 + g2_ref[0]) * (_rms(ff) * gpost2_ref[...])

    tiles = [slice(k * sub, (k + 1) * sub) for k in range(OUT_SUBTILES)]
    att_ns = [(_rms(att_ref[0, rows, :].astype(F32)) * gatt_ref[...]).astype(BF16) for rows in tiles]
    heads = [head(tiles[0], att_ns[0])]
    for k in range(OUT_SUBTILES):
        if k + 1 < OUT_SUBTILES:
            heads.append(head(tiles[k + 1], att_ns[k + 1]))
        mlp(tiles[k], *heads[k])


def _outmlp(x, rnn_n, att, gatt, wo, gpost, g1, gpre, sc2, sh2, w1, w2, gpost2, g2):
    B, S, D = x.shape
    tm = min(TM_OUT, S)
    row = lambda n: pl.BlockSpec((1, n), lambda b, s: (0, 0))
    per_b = lambda n: pl.BlockSpec((1, 1, n), lambda b, s: (b, 0, 0))
    const = lambda a: pl.BlockSpec(a.shape, lambda b, s: (0,) * a.ndim, pipeline_mode=pl.Buffered(1))
    tok = lambda n: pl.BlockSpec((1, tm, n), lambda b, s: (b, s, 0))
    return pl.pallas_call(
        _outmlp_kernel,
        grid=(B, S // tm),
        in_specs=[tok(D), tok(D_RNN), tok(D_ATT), row(D_ATT), const(wo), row(D), per_b(D), row(D),
                  per_b(D), per_b(D), const(w1), const(w2), row(D), per_b(D)],
        out_specs=tok(D),
        out_shape=jax.ShapeDtypeStruct((B, S, D), F32),
        compiler_params=pltpu.CompilerParams(dimension_semantics=("arbitrary", "arbitrary"),
                                             vmem_limit_bytes=VMEM_LIMIT),
        name="outmlp",
    )(x, rnn_n, att, gatt, wo, gpost, g1, gpre, sc2, sh2, w1, w2, gpost2, g2)


def _block_diag(w):
    n, k, _ = w.shape
    return jnp.einsum('nij,nm->nimj', w, jnp.eye(n, dtype=w.dtype)).reshape(n * k, n * k)


def _layer(x, c, ada_w, ada_b, pre_norm_mix, w_in, conv_w, conv_b, lru_wa, lru_ba, lru_wx, lru_bx, lru_lambda,
           cmp_pos_k, cmp_w1_k, cmp_w2_k, cmp_pos_v, cmp_w1_v, cmp_w2_v, norm_rnn_out, norm_att_out, w_out,
           post_norm_mix, pre_norm_mlp, w_ff1, w_ff2, post_norm_mlp):
    B, S, D = x.shape
    G = N_KV
    row = lambda v: v.reshape(1, -1)

    mod = _ada(c, ada_w, ada_b)
    sh1, sc1, g1, sh2, sc2, g2 = [m.reshape(B, 1, D) for m in jnp.split(mod, 6, axis=-1)]

    gate_cols = [OFF_GATE + br * N_HEADS + g * REP + r for g in range(G) for br in range(3) for r in range(REP)]
    w_gate = w_in[:, jnp.asarray(gate_cols)].reshape(D, G, 3 * REP)
    w_gate = jnp.pad(w_gate, ((0, 0), (0, 0), (0, GATE_ROWS - 3 * REP))).reshape(D, G * GATE_ROWS)
    w_in_p = jnp.concatenate([w_in[:, :OFF_GATE], jnp.pad(w_gate, ((0, 0), (0, GATE_PAD - G * GATE_ROWS)))],
                             axis=1).astype(BF16)
    wa = _block_diag(lru_wa).astype(BF16)
    wx = _block_diag(lru_wx).astype(BF16)
    half = CMP_LEN // 2 * HEAD_DIM

    def w1_cat(w1):
        return jnp.concatenate([w1[:half], w1[half:]], axis=1).astype(BF16)

    def pos_rows(pos):
        return jnp.pad(pos.reshape(2, half), ((0, 14), (0, 0))).astype(BF16)

    rnn_n, qT, ks, vsT, kw, vwT, kc_in, vc_in, gT = _inproj(
        x, row(pre_norm_mix), sc1, sh1, w_in_p, conv_w, row(conv_b), wa, row(lru_ba), wx, row(lru_bx),
        row(lru_lambda), row(norm_rnn_out))
    kc, vc = _compress(kc_in, vc_in, w1_cat(cmp_w1_k), cmp_w2_k.astype(BF16), pos_rows(cmp_pos_k),
                       w1_cat(cmp_w1_v), cmp_w2_v.astype(BF16), pos_rows(cmp_pos_v))
    att = _attention(qT, ks, vsT, kw, vwT, kc, vc.transpose(0, 1, 3, 2), gT)

    return _outmlp(x, rnn_n, att, row(norm_att_out), w_out.astype(BF16), row(post_norm_mix), g1,
                   row(pre_norm_mlp), sc2, sh2, w_ff1.astype(BF16), w_ff2.astype(BF16), row(post_norm_mlp), g2)


def kernel(x, c, ada_w, ada_b, pre_norm_mix, w_in, conv_w, conv_b, lru_wa, lru_ba, lru_wx, lru_bx, lru_lambda,
           cmp_pos_k, cmp_w1_k, cmp_w2_k, cmp_pos_v, cmp_w1_v, cmp_w2_v, norm_rnn_out, norm_att_out, w_out,
           post_norm_mix, pre_norm_mlp, w_ff1, w_ff2, post_norm_mlp):
    for l in range(ada_w.shape[0]):
        x = _layer(x, c, ada_w[l], ada_b[l], pre_norm_mix[l], w_in[l], conv_w[l], conv_b[l], lru_wa[l], lru_ba[l],
                   lru_wx[l], lru_bx[l], lru_lambda[l], cmp_pos_k[l], cmp_w1_k[l], cmp_w2_k[l], cmp_pos_v[l],
                   cmp_w1_v[l], cmp_w2_v[l], norm_rnn_out[l], norm_att_out[l], w_out[l], post_norm_mix[l],
                   pre_norm_mlp[l], w_ff1[l], w_ff2[l], post_norm_mlp[l])
    return x
```

```python
import functools

import jax
import jax.numpy as jnp
from jax import lax
from jax.experimental import pallas as pl
from jax.experimental.pallas import tpu as pltpu

F32 = jnp.float32
BF16 = jnp.bfloat16
I32 = jnp.int32

D_MODEL = 1024
D_RNN = 512
CONV_WIDTH = 4
LRU_C = 8.0
N_HEADS = 8
HEAD_DIM = 64
N_KV = 2
REP = N_HEADS // N_KV
D_ATT = N_HEADS * HEAD_DIM
CMP_LEN = 32
CMP_STRIDE = 16
CMP_HIDDEN = 256
SLC_BLOCK = 64
SLC_TOPK = 16
WINDOW = 512
D_FF = 4 * D_MODEL
EPS = 1e-6
NEG = -1e30
FORCED = 1e4
LOG2E = 1.4426950408889634

KV_COLS = 6 * N_KV * HEAD_DIM
N_GATE = 3 * N_HEADS
GATE_PAD = 128
GATE_ROWS = 16
OFF_Q = 2 * D_RNN
OFF_KV = OFF_Q + D_ATT
OFF_GATE = OFF_KV + KV_COLS
D_IN_PAD = OFF_GATE + GATE_PAD

TM_IN = 1024
TM_OUT = 1024
OUT_SUBTILES = 4
TQ = 256
TK = 256
NCHAIN = 4
KAUG = 256
VROWS = 80
VMEM_LIMIT = 56 * 1024 * 1024

assert TQ == TK and WINDOW == 2 * TK
assert TK % SLC_BLOCK == 0 and CMP_LEN == 2 * CMP_STRIDE and CONV_WIDTH == 4
assert N_GATE <= N_KV * GATE_ROWS <= GATE_PAD and D_IN_PAD % 128 == 0
assert 2 * HEAD_DIM + 4 <= KAUG and HEAD_DIM < VROWS

SEL, WIN = 0, 1
MASK_NONE, MASK_CAUSAL, MASK_WINDOW_LOW = 0, 1, 2
POS_COL = 2 * HEAD_DIM
DUMMY_COL = POS_COL + 4


def _gelu_tanh(x):
    return 0.5 * x * (1.0 + jnp.tanh(0.7978845608028654 * (x + 0.044715 * (x * x * x))))


def _sigmoid(x):
    return 0.5 * jnp.tanh(0.5 * x) + 0.5


def _rms(x):
    return x * lax.rsqrt(jnp.mean(x * x, axis=-1, keepdims=True) + EPS)


def _ada_kernel(c_ref, w_ref, b_ref, o_ref):
    c = c_ref[...]
    a = c * jax.nn.sigmoid(c)
    o_ref[...] = jnp.dot(a, w_ref[...], preferred_element_type=F32,
                         precision=lax.Precision.HIGHEST) + b_ref[...]


def _ada(c, w, b):
    B, D = c.shape
    N = w.shape[1]
    tn = 1024
    return pl.pallas_call(
        _ada_kernel,
        grid=(N // tn,),
        in_specs=[pl.BlockSpec((B, D), lambda j: (0, 0)),
                  pl.BlockSpec((D, tn), lambda j: (0, j)),
                  pl.BlockSpec((1, tn), lambda j: (0, j))],
        out_specs=pl.BlockSpec((B, tn), lambda j: (0, j)),
        out_shape=jax.ShapeDtypeStruct((B, N), F32),
        name="ada",
    )(c, w, b.reshape(1, N))


def _inproj_kernel(x_ref, gain_ref, sc_ref, sh_ref, w_ref, cw_ref, cb_ref, wa_ref, ba_ref, wx_ref, bx_ref,
                   lam_ref, grnn_ref,
                   rnn_ref, qT_ref, ks_ref, vs_ref, kw_ref, vw_ref, kc_ref, vc_ref, g_ref,
                   xbuf, hcar, a_s, u_s, h_s):
    tm = x_ref.shape[1]
    hd = HEAD_DIM

    @pl.when(pl.program_id(1) == 0)
    def _():
        xbuf[0:8, :] = jnp.zeros((8, D_RNN), F32)
        hcar[...] = jnp.zeros((1, D_RNN), F32)

    x = x_ref[0]
    h = _rms(x) * (gain_ref[...] * (1.0 + sc_ref[0])) + sh_ref[0]
    hb = h.astype(BF16)

    xr = jnp.dot(hb, w_ref[:, D_RNN:OFF_Q], preferred_element_type=F32)
    xbuf[8:8 + tm, :] = xr
    y = (cw_ref[3:4, :] * xr + cw_ref[2:3, :] * xbuf[7:7 + tm, :]
         + cw_ref[1:2, :] * xbuf[6:6 + tm, :] + cw_ref[0:1, :] * xbuf[5:5 + tm, :]) + cb_ref[...]
    xbuf[0:8, :] = xbuf[tm:tm + 8, :]

    yb = y.astype(BF16)
    r = _sigmoid(jnp.dot(yb, wa_ref[...], preferred_element_type=F32) + ba_ref[...])
    i = _sigmoid(jnp.dot(yb, wx_ref[...], preferred_element_type=F32) + bx_ref[...])
    nl = -lam_ref[...]
    softplus = jnp.maximum(nl, 0.0) + jnp.log(1.0 + jnp.exp(-jnp.abs(nl)))
    a = jnp.exp((-LRU_C) * r * softplus)
    a_s[...] = a
    u_s[...] = jnp.sqrt(1.0 - a * a) * (i * y)

    qT_ref[0] = (jnp.dot(hb, w_ref[:, OFF_Q:OFF_KV], preferred_element_type=F32)
                 * (HEAD_DIM ** -0.5 * LOG2E)).T.astype(BF16)
    kv = jnp.dot(hb, w_ref[:, OFF_KV:OFF_GATE], preferred_element_type=F32)
    width = N_KV * hd
    vsT = kv[:, 3 * width:4 * width].T
    vwT = kv[:, 5 * width:6 * width].T
    for gi in range(N_KV):
        kc_ref[0, gi] = kv[:, gi * hd:(gi + 1) * hd]
        vc_ref[0, gi] = kv[:, width + gi * hd:width + (gi + 1) * hd]
        ks_ref[0, gi] = kv[:, 2 * width + gi * hd:2 * width + (gi + 1) * hd].astype(BF16)
        kw_ref[0, gi] = kv[:, 4 * width + gi * hd:4 * width + (gi + 1) * hd].astype(BF16)
        for j in range(tm // TK):
            vs_ref[0, gi, j] = vsT[gi * hd:(gi + 1) * hd, j * TK:(j + 1) * TK].astype(BF16)
            vw_ref[0, gi, j] = vwT[gi * hd:(gi + 1) * hd, j * TK:(j + 1) * TK].astype(BF16)
    gates = _sigmoid(jnp.dot(hb, w_ref[:, OFF_GATE:D_IN_PAD], preferred_element_type=F32)).T
    for gi in range(N_KV):
        g_ref[0, gi] = gates[gi * GATE_ROWS:(gi + 1) * GATE_ROWS, :]
    g = jnp.dot(hb, w_ref[:, 0:D_RNN], preferred_element_type=F32)


    rows = lax.broadcasted_iota(I32, (8, D_RNN), 0)
    hprev = hcar[...]
    for gi in range(tm // 8):
        ag = a_s[gi * 8:gi * 8 + 8, :]
        ug = u_s[gi * 8:gi * 8 + 8, :]
        for k in (1, 2, 4):
            a_sh = jnp.where(rows >= k, pltpu.roll(ag, k, 0), 1.0)
            u_sh = jnp.where(rows >= k, pltpu.roll(ug, k, 0), 0.0)
            ug = ag * u_sh + ug
            ag = ag * a_sh
        hg = ag * hprev + ug
        h_s[gi * 8:gi * 8 + 8, :] = hg
        hprev = hg[7:8, :]
    hcar[...] = hprev

    rnn = _gelu_tanh(g) * h_s[...]
    rnn_ref[0] = (_rms(rnn) * grnn_ref[...]).astype(BF16)


def _inproj(x, gain, sc, sh, w_in, conv_w, conv_b, wa, ba, wx, bx, lam, grnn):
    B, S, D = x.shape
    tm = min(TM_IN, S)
    row = lambda n: pl.BlockSpec((1, n), lambda b, s: (0, 0))
    per_b = lambda n: pl.BlockSpec((1, 1, n), lambda b, s: (b, 0, 0))
    full = lambda a: pl.BlockSpec(a.shape, lambda b, s: (0,) * a.ndim)
    tok = lambda n: pl.BlockSpec((1, tm, n), lambda b, s: (b, s, 0))
    G = N_KV
    grp = pl.BlockSpec((1, G, tm, HEAD_DIM), lambda b, s: (b, 0, s, 0))
    grpT = pl.BlockSpec((1, G, tm // TK, HEAD_DIM, TK), lambda b, s: (b, 0, s, 0, 0))
    return pl.pallas_call(
        _inproj_kernel,
        grid=(B, S // tm),
        in_specs=[tok(D), row(D), per_b(D), per_b(D), full(w_in), full(conv_w), row(D_RNN),
                  full(wa), row(D_RNN), full(wx), row(D_RNN), row(D_RNN), row(D_RNN)],
        out_specs=[tok(D_RNN),
                   pl.BlockSpec((1, D_ATT, tm), lambda b, s: (b, 0, s)),
                   grp, grpT, grp, grpT, grp, grp,
                   pl.BlockSpec((1, G, GATE_ROWS, tm), lambda b, s: (b, 0, 0, s))],
        out_shape=[jax.ShapeDtypeStruct((B, S, D_RNN), BF16),
                   jax.ShapeDtypeStruct((B, D_ATT, S), BF16),
                   jax.ShapeDtypeStruct((B, G, S, HEAD_DIM), BF16),
                   jax.ShapeDtypeStruct((B, G, S // TK, HEAD_DIM, TK), BF16),
                   jax.ShapeDtypeStruct((B, G, S, HEAD_DIM), BF16),
                   jax.ShapeDtypeStruct((B, G, S // TK, HEAD_DIM, TK), BF16),
                   jax.ShapeDtypeStruct((B, G, S, HEAD_DIM), F32),
                   jax.ShapeDtypeStruct((B, G, S, HEAD_DIM), F32),
                   jax.ShapeDtypeStruct((B, G, GATE_ROWS, S), F32)],
        scratch_shapes=[pltpu.VMEM((tm + 8, D_RNN), F32), pltpu.VMEM((1, D_RNN), F32),
                        pltpu.VMEM((tm, D_RNN), F32), pltpu.VMEM((tm, D_RNN), F32),
                        pltpu.VMEM((tm, D_RNN), F32)],
        compiler_params=pltpu.CompilerParams(dimension_semantics=("arbitrary", "arbitrary"),
                                             vmem_limit_bytes=VMEM_LIMIT),
        name="inproj",
    )(x, gain, sc, sh, w_in, conv_w, conv_b, wa, ba, wx, bx, lam, grnn)


def _compress_kernel(rk_ref, rv_ref, w1k_ref, w2k_ref, pk_ref, w1v_ref, w2v_ref, pv_ref, kc_ref, vc_ref):
    def one(r_ref, w1_ref, w2_ref, pos_ref, o_ref):
        n = r_ref.shape[2] // CMP_STRIDE
        rows = jnp.concatenate([r_ref[0, 0, pl.ds(l, n, stride=CMP_STRIDE), :] for l in range(CMP_STRIDE)], axis=1)
        p = jnp.dot(rows.astype(BF16), w1_ref[...], preferred_element_type=F32)
        posb = jnp.dot(pos_ref[...], w1_ref[...], preferred_element_type=F32)
        bias = posb[0:1, 0:CMP_HIDDEN] + posb[1:2, CMP_HIDDEN:]
        pre = p[:, 0:CMP_HIDDEN] + pltpu.roll(p[:, CMP_HIDDEN:], n - 1, 0) + bias
        hid = _gelu_tanh(pre).astype(BF16)
        o_ref[0, 0] = jnp.dot(hid, w2_ref[...], preferred_element_type=F32).astype(BF16)

    one(rk_ref, w1k_ref, w2k_ref, pk_ref, kc_ref)
    one(rv_ref, w1v_ref, w2v_ref, pv_ref, vc_ref)


def _compress(rk, rv, w1k, w2k, posk, w1v, w2v, posv):
    B, G, S, width = rk.shape
    n = S // CMP_STRIDE
    blk = pl.BlockSpec((1, 1, S, width), lambda b, g: (b, g, 0, 0))
    full = lambda a: pl.BlockSpec(a.shape, lambda b, g: (0,) * a.ndim)
    out = pl.BlockSpec((1, 1, n, HEAD_DIM), lambda b, g: (b, g, 0, 0))
    return pl.pallas_call(
        _compress_kernel,
        grid=(B, G),
        in_specs=[blk, blk, full(w1k), full(w2k), full(posk), full(w1v), full(w2v), full(posv)],
        out_specs=[out, out],
        out_shape=[jax.ShapeDtypeStruct((B, G, n, HEAD_DIM), BF16)] * 2,
        compiler_params=pltpu.CompilerParams(vmem_limit_bytes=VMEM_LIMIT),
        name="compress",
    )(rk, rv, w1k, w2k, posk, w1v, w2v, posv)


def _attn_kernel(qT_ref, ks_ref, vs_ref, kw_ref, vw_ref, kc_ref, vcT_ref, g_ref,
                 o_ref,
                 kall, vall, qall, kca, m_all, acc_all, al_s, sbuf, pbuf, oacc, val_s, rank_s, selbias_s,
                 bias_tbl, cmask,
                 oT_s, st_s, *, n_sel):
    S = ks_ref.shape[2]
    nkt = S // TK
    ncmp = kc_ref.shape[2]
    b = pl.program_id(0)
    g = pl.program_id(1)
    step = pl.program_id(2)
    W4 = REP * TQ

    @pl.when((b == 0) & (g == 0) & (step == 0))
    def _():
        col = lax.broadcasted_iota(I32, (TK, KAUG), 1)
        is_blk = (col == POS_COL) | (col == POS_COL + 2)
        is_off = (col == POS_COL + 1) | (col == POS_COL + 3)
        ones_row = (lax.broadcasted_iota(I32, (VROWS - HEAD_DIM, TK), 0) == 0).astype(BF16)
        for kt in range(nkt):
            key = kt * TK + lax.broadcasted_iota(I32, (TK, KAUG), 0)
            pos = jnp.where(is_blk, (key // SLC_BLOCK).astype(F32),
                            jnp.where(is_off, (key % SLC_BLOCK).astype(F32), 0.0))
            onehot = (col - HEAD_DIM == key // SLC_BLOCK).astype(F32)
            kall[SEL, kt * TK:(kt + 1) * TK, :] = (pos + onehot).astype(BF16)
            kall[WIN, kt * TK:(kt + 1) * TK, :] = pos.astype(BF16)
            vall[SEL, kt, HEAD_DIM:VROWS, :] = ones_row
            vall[WIN, kt, HEAD_DIM:VROWS, :] = ones_row
        dummy = (col == DUMMY_COL).astype(BF16)
        for br in (SEL, WIN):
            kall[br, nkt * TK:(nkt + 1) * TK, :] = dummy
            vall[br, nkt] = jnp.zeros((VROWS, TK), BF16)
        ko = lax.broadcasted_iota(I32, (TK, W4), 0)
        to = lax.broadcasted_iota(I32, (TK, W4), 1) % TQ
        bias_tbl[MASK_NONE] = jnp.zeros((TK, W4), F32)
        bias_tbl[MASK_CAUSAL] = jnp.where(ko <= to, 0.0, NEG)
        bias_tbl[MASK_WINDOW_LOW] = jnp.where(ko > to, 0.0, NEG)
        u = lax.broadcasted_iota(I32, (2 * ncmp, W4), 0)
        tc = lax.broadcasted_iota(I32, (2 * ncmp, W4), 1) % TQ
        cmask[...] = jnp.where(u <= ncmp + ((tc + 1) // CMP_STRIDE) - 2, 0.0, NEG)
        cc = lax.broadcasted_iota(I32, (ncmp, KAUG), 0)
        colc = lax.broadcasted_iota(I32, (ncmp, KAUG), 1)
        kca[...] = jnp.where((colc == POS_COL) | (colc == POS_COL + 2), (cc // 4).astype(F32),
                             jnp.where((colc == POS_COL + 1) | (colc == POS_COL + 3),
                                       (CMP_STRIDE * (cc % 4)).astype(F32), 0.0)).astype(BF16)
        qall[...] = jnp.zeros(qall.shape, BF16)

    lane = lax.broadcasted_iota(I32, (1, W4), 1)
    head = g * REP + lane // TQ
    slope = lax.bitcast_convert_type((126 - head) << 23, F32)

    @pl.when(step == 0)
    def _():
        for kt in range(nkt):
            kall[SEL, kt * TK:(kt + 1) * TK, 0:HEAD_DIM] = ks_ref[0, 0, kt * TK:(kt + 1) * TK, :]
            kall[WIN, kt * TK:(kt + 1) * TK, 0:HEAD_DIM] = kw_ref[0, 0, kt * TK:(kt + 1) * TK, :]
            vall[SEL, kt, 0:HEAD_DIM, :] = vs_ref[0, 0, kt]
            vall[WIN, kt, 0:HEAD_DIM, :] = vw_ref[0, 0, kt]
        kca[:, 0:HEAD_DIM] = kc_ref[0, 0]
        c_hi = (slope * LOG2E).astype(BF16).astype(F32)
        c_lo = slope * LOG2E - c_hi
        r16 = lax.broadcasted_iota(I32, (16, W4), 0)
        alibi = jnp.where(r16 == 0, c_hi * SLC_BLOCK, jnp.where(r16 == 1, c_hi, jnp.where(
            r16 == 2, c_lo * SLC_BLOCK, jnp.where(r16 == 3, c_lo, jnp.where(
                r16 == DUMMY_COL - POS_COL, NEG, 0.0))))).astype(BF16)
        for c in range(NCHAIN):
            qall[c, SEL, POS_COL:POS_COL + 16, :] = alibi
            qall[c, WIN, POS_COL:POS_COL + 16, :] = alibi

    chains = [_attn_chain(c, NCHAIN * step + c, qT_ref, kc_ref, vcT_ref, g_ref, o_ref, kall, vall, qall.at[c], kca,
                          m_all.at[c], acc_all.at[c], al_s.at[c], sbuf.at[c], pbuf.at[c], oacc.at[c], val_s.at[c],
                          rank_s.at[c], selbias_s.at[c], bias_tbl, cmask, oT_s.at[c], st_s.at[c], n_sel)
              for c in range(NCHAIN)]
    live = list(chains)
    while live:
        handed = [next(ch, _DONE) for ch in live]
        live = [ch for ch, r in zip(live, handed) if r is not _DONE]
        loops = [r for r in handed if r is not _DONE and r is not None]
        if loops:
            joint = functools.reduce(jnp.minimum, [npair for npair, _, _ in loops])

            def trip(j, group):
                als = [head(j) for _, head, _ in group]
                for (_, _, tail), al in zip(group, als):
                    tail(j, al)

            lax.fori_loop(1, 1 + joint, lambda j, carry: (trip(j, loops), carry)[1], 0)
            for item in loops:
                lax.fori_loop(1 + joint, 1 + item[0], lambda j, carry, item=item: (trip(j, [item]), carry)[1], 0)


_DONE = object()


def _drain(gen):
    try:
        while True:
            next(gen)
    except StopIteration as stop:
        return stop.value


def _attn_chain(c, qi, qT_ref, kc_ref, vcT_ref, g_ref, o_ref, kall, vall, qall, kca, m_all, acc_all, al_s, sbuf,
                pbuf, oacc, val_s, rank_s, selbias_s, bias_tbl, cmask, oT_s, st_s, n_sel):
    S = kall.shape[1] - TK
    ncmp = kc_ref.shape[2]
    nblk = S // SLC_BLOCK
    cmp_per_tile = TQ // CMP_STRIDE
    q0 = qi * TQ
    W4 = REP * TQ
    lane = lax.broadcasted_iota(I32, (1, W4), 1)
    lanes_c = slice(c * TQ, (c + 1) * TQ)

    q = qT_ref[0, :, lanes_c]
    for r in range(REP):
        qr = q[r * HEAD_DIM:(r + 1) * HEAD_DIM, :]
        qall[SEL, 0:HEAD_DIM, r * TQ:(r + 1) * TQ] = qr
        qall[WIN, 0:HEAD_DIM, r * TQ:(r + 1) * TQ] = qr

    gall = g_ref[0, 0, :, lanes_c]
    gates = [gall[br * REP:(br + 1) * REP, :] for br in range(3)]

    def stage_scores(br, tiles_masks):
        col_max = None
        for h, (tile, mask) in enumerate(tiles_masks):
            r0 = pl.multiple_of(tile * TK, TK)
            s = jnp.dot(kall[br, pl.ds(r0, TK), :], qall[br], preferred_element_type=F32)
            if mask is not None:
                s = s + bias_tbl[mask]
            sbuf[h * TK:(h + 1) * TK, :] = s
            c = jnp.max(s, axis=0, keepdims=True)
            col_max = c if col_max is None else jnp.maximum(col_max, c)
            yield
        m_old = m_all[br]
        m_new = jnp.maximum(m_old, col_max)
        m_all[br] = m_new
        return jnp.exp2(m_old - m_new)

    def stage_probs(br, ntile):
        for h in range(ntile):
            rows = slice(h * TK, (h + 1) * TK)
            pbuf[rows, :] = jnp.exp2(sbuf[rows, :] - m_all[br]).astype(BF16)
            yield

    def stage_values(br, tiles, al_row):
        acc = acc_all[br] * al_row
        for h, tile in enumerate(tiles):
            acc = acc + jnp.dot(vall[br, tile], pbuf[h * TK:(h + 1) * TK, :], preferred_element_type=F32)
            if h + 1 < len(tiles):
                yield
        acc_all[br] = acc
        yield

    def start_branch(br, tl):
        al_s[...] = yield from stage_scores(br, [(tl[0], MASK_CAUSAL), (tl[1], None)])

    def pair_items(br, n, tl):
        npair = jnp.maximum(n - 2, 0) // 2

        def head(j):
            al_prev = al_s[...]
            _drain(stage_probs(br, 2))
            al_s[...] = _drain(stage_scores(br, [(tl[2 * j], None), (tl[2 * j + 1], None)]))
            return al_prev

        def tail(j, al_prev):
            _drain(stage_values(br, [tl[2 * j - 2], tl[2 * j - 1]], al_prev))

        return npair, head, tail

    def finish_branch(br, n, tl):
        rest = jnp.maximum(n - 2, 0)
        last = 2 * (rest // 2)
        t_single = tl[jnp.where(rest % 2 == 1, n - 1, n)]
        al_prev = al_s[...]
        yield from stage_probs(br, 2)
        al_k = yield from stage_scores(br, [(t_single, None)])
        yield from stage_values(br, [tl[last], tl[last + 1]], al_prev)
        yield from stage_probs(br, 1)
        yield from stage_values(br, [t_single], al_k)

    m_all[...] = jnp.full(m_all.shape, NEG, F32)
    acc_all[...] = jnp.zeros(acc_all.shape, F32)
    nkt = S // TK
    w1 = jnp.where(qi >= 2, qi - 2, jnp.where(qi >= 1, 0, nkt))
    w2 = jnp.where(qi >= 2, qi - 1, nkt)
    al_w0 = yield from stage_scores(
        WIN, [(qi, MASK_CAUSAL), (w1, jnp.where(qi >= 2, MASK_WINDOW_LOW, MASK_NONE))])

    c0 = pl.multiple_of(ncmp - cmp_per_tile * qi, cmp_per_tile)
    sc = jnp.dot(kca[...], qall[WIN], preferred_element_type=F32) + cmask[pl.ds(c0, ncmp), :]
    yield
    e = jnp.exp2(sc - jnp.max(sc, axis=0, keepdims=True))
    tq = q0 + lane % TQ
    p = e * ((1.0 / jnp.sum(e, axis=0, keepdims=True)) * (tq >= CMP_LEN - 1).astype(F32))
    ocT = jnp.dot(vcT_ref[0, 0], p.astype(BF16), preferred_element_type=F32)
    for r in range(REP):
        oacc[:, r * TQ:(r + 1) * TQ] = gates[0][r:r + 1, :] * ocT[:, r * TQ:(r + 1) * TQ]
    yield

    yield from stage_probs(WIN, 2)
    al_w1 = yield from stage_scores(WIN, [(w2, None)])
    yield from stage_values(WIN, [qi, w1], al_w0)

    psum = p[:, 0:TQ]
    for r in range(1, REP):
        psum = psum + p[:, r * TQ:(r + 1) * TQ]
    jj = lax.broadcasted_iota(I32, (nblk, ncmp), 0)
    cc = lax.broadcasted_iota(I32, (nblk, ncmp), 1)
    ovT = ((CMP_STRIDE * cc < SLC_BLOCK * jj + SLC_BLOCK)
           & (CMP_STRIDE * cc + CMP_LEN > SLC_BLOCK * jj)).astype(BF16)
    p_hi = psum.astype(BF16)
    rest = psum - p_hi.astype(F32)
    p_mid = rest.astype(BF16)
    p_lo = (rest - p_mid.astype(F32)).astype(BF16)
    impT = (jnp.dot(ovT, p_hi, preferred_element_type=F32) + jnp.dot(ovT, p_mid, preferred_element_type=F32)
            + jnp.dot(ovT, p_lo, preferred_element_type=F32))

    j_i = lax.broadcasted_iota(I32, (nblk, TQ), 0)
    t1 = q0 + lax.broadcasted_iota(I32, (nblk, TQ), 1)
    cur = t1 // SLC_BLOCK
    forced = (j_i == 0) | (j_i == cur) | (j_i == cur - 1)
    visible = SLC_BLOCK * j_i <= t1
    val_s[...] = jnp.where(forced, FORCED, jnp.where(visible, impT, NEG))
    yield
    ngrp = nblk // 8
    rank_s[...] = jnp.zeros(rank_s.shape, F32)
    j8 = lax.broadcasted_iota(I32, (8, TQ), 0)
    for ib in range(ngrp):
        @pl.when(8 * ib * SLC_BLOCK < q0 + TQ)
        def _(ib=ib):
            vals = [val_s[8 * jb:8 * jb + 8, :] for jb in range(ngrp)]
            ranks = [rank_s[8 * jb:8 * jb + 8, :] for jb in range(ngrp)]
            for i in range(8 * ib, 8 * ib + 8):
                row = jnp.broadcast_to(val_s[i:i + 1, :], (8, TQ))
                for jb in range(ngrp):
                    if jb > ib:
                        hit = row >= vals[jb]
                    elif jb < ib:
                        hit = row > vals[jb]
                    else:
                        hit = jnp.where(j8 > i - 8 * jb, jnp.where(row >= vals[jb], 1.0, 0.0),
                                        jnp.where(row > vals[jb], 1.0, 0.0)) > 0.5
                    ranks[jb] = ranks[jb] + jnp.where(hit, 1.0, 0.0)
            for jb in range(ngrp):
                rank_s[8 * jb:8 * jb + 8, :] = ranks[jb]
    yield
    ranks = [rank_s[8 * jb:8 * jb + 8, :] for jb in range(ngrp)]
    blocks_per_tile = TK // SLC_BLOCK
    st_s[0] = qi
    n_selt = jnp.int32(1)
    for jb in range(ngrp):
        chosen = ranks[jb] < n_sel
        selb = jnp.where(chosen, 0.0, NEG)
        for r in range(REP):
            selbias_s[8 * jb:8 * jb + 8, r * TQ:(r + 1) * TQ] = selb
        any_q = jnp.max(jnp.where(chosen, 1.0, 0.0), axis=1, keepdims=True)
        for hh in range(8 // blocks_per_tile):
            kt = (8 * jb) // blocks_per_tile + hh
            hit = jnp.max(any_q[hh * blocks_per_tile:(hh + 1) * blocks_per_tile, :]) > 0.5
            st_s[n_selt] = kt
            n_selt = n_selt + jnp.where(hit & (kt < qi), 1, 0)
    st_s[n_selt] = nkt
    qall[SEL, HEAD_DIM:HEAD_DIM + nblk, :] = selbias_s[...].astype(BF16)
    yield

    yield from stage_probs(WIN, 1)
    yield from stage_values(WIN, [w2], al_w1)
    yield from start_branch(SEL, st_s)
    yield pair_items(SEL, n_selt, st_s)
    yield from finish_branch(SEL, n_selt, st_s)
    yield

    def normalised(br):
        acc = acc_all[br]
        return acc[0:HEAD_DIM, :] / acc[HEAD_DIM:HEAD_DIM + 1, :]

    o_sel = normalised(SEL)
    o_win = normalised(WIN)
    for r in range(REP):
        lanes = slice(r * TQ, (r + 1) * TQ)
        oT_s[r * HEAD_DIM:(r + 1) * HEAD_DIM, :] = (oacc[:, lanes] + gates[1][r:r + 1, :] * o_sel[:, lanes]
                                                    + gates[2][r:r + 1, :] * o_win[:, lanes])
    o_ref[0, lanes_c, :] = oT_s[...].T.astype(BF16)


def _attention(qT, ks, vs, kw, vw, kc, vcT, gT):
    B, _, S = qT.shape
    G = N_KV
    assert S % (NCHAIN * TQ) == 0 and (S // SLC_BLOCK) % 8 == 0 and S // SLC_BLOCK <= HEAD_DIM
    nkt = S // TK
    ncmp = kc.shape[2]
    assert ncmp == S // CMP_STRIDE and ncmp // 4 < 256
    nblk = S // SLC_BLOCK
    n_sel = min(SLC_TOPK, nblk)
    W4 = REP * TQ
    res4 = lambda a: pl.BlockSpec((1, 1) + a.shape[2:], lambda b, g, i: (b, g) + (0,) * (a.ndim - 2))
    return pl.pallas_call(
        functools.partial(_attn_kernel, n_sel=n_sel),
        grid=(B, G, S // (NCHAIN * TQ)),
        in_specs=[pl.BlockSpec((1, REP * HEAD_DIM, NCHAIN * TQ), lambda b, g, i: (b, g, i)),
                  res4(ks), res4(vs), res4(kw), res4(vw), res4(kc), res4(vcT),
                  pl.BlockSpec((1, 1, GATE_ROWS, NCHAIN * TQ), lambda b, g, i: (b, g, 0, i))],
        out_specs=pl.BlockSpec((1, NCHAIN * TQ, REP * HEAD_DIM), lambda b, g, i: (b, i, g)),
        out_shape=jax.ShapeDtypeStruct((B, S, D_ATT), BF16),
        scratch_shapes=[pltpu.VMEM((2, S + TK, KAUG), BF16),
                        pltpu.VMEM((2, nkt + 1, VROWS, TK), BF16),
                        pltpu.VMEM((NCHAIN, 2, KAUG, W4), BF16),
                        pltpu.VMEM((ncmp, KAUG), BF16),
                        pltpu.VMEM((NCHAIN, 2, 1, W4), F32),
                        pltpu.VMEM((NCHAIN, 2, VROWS, W4), F32),
                        pltpu.VMEM((NCHAIN, 1, W4), F32),
                        pltpu.VMEM((NCHAIN, 2 * TK, W4), F32),
                        pltpu.VMEM((NCHAIN, 2 * TK, W4), BF16),
                        pltpu.VMEM((NCHAIN, HEAD_DIM, W4), F32),
                        pltpu.VMEM((NCHAIN, nblk, TQ), F32),
                        pltpu.VMEM((NCHAIN, nblk, TQ), F32),
                        pltpu.VMEM((NCHAIN, nblk, W4), F32),
                        pltpu.VMEM((3, TK, W4), F32),
                        pltpu.VMEM((2 * ncmp, W4), F32),
                        pltpu.VMEM((NCHAIN, REP * HEAD_DIM, TQ), F32),
                        pltpu.SMEM((NCHAIN, nkt + 2), I32)],
        compiler_params=pltpu.CompilerParams(dimension_semantics=("arbitrary", "arbitrary", "arbitrary"),
                                             vmem_limit_bytes=VMEM_LIMIT),
        name="attn",
    )(qT, ks, vs, kw, vw, kc, vcT, gT)


def _outmlp_kernel(x_ref, rnn_ref, att_ref, gatt_ref, wo_ref, gpost_ref, g1_ref, gpre_ref, sc2_ref, sh2_ref,
                   w1_ref, w2_ref, gpost2_ref, g2_ref, o_ref):
    tm = x_ref.shape[1]
    sub = tm // OUT_SUBTILES
    fc = 1024

    def head(rows, att_n):
        y = (jnp.dot(rnn_ref[0, rows, :], wo_ref[0:D_RNN, :], preferred_element_type=F32)
             + jnp.dot(att_n, wo_ref[D_RNN:, :], preferred_element_type=F32))
        x1 = x_ref[0, rows, :] + (1.0 + g1_ref[0]) * (_rms(y) * gpost_ref[...])
        return x1, (_rms(x1) * (gpre_ref[...] * (1.0 + sc2_ref[0])) + sh2_ref[0]).astype(BF16)

    def mlp(rows, x1, h2):
        ff = jnp.zeros(x1.shape, F32)
        for c in range(D_FF // fc):
            hid = jnp.maximum(jnp.dot(h2, w1_ref[:, c * fc:(c + 1) * fc], preferred_element_type=F32), 0.0)
            ff = ff + jnp.dot((hid * hid).astype(BF16), w2_ref[c * fc:(c + 1) * fc, :],
                              preferred_element_type=F32)
        o_ref[0, rows, :] = x1 + (1.0 + g2_ref[0]) * (_rms(ff) * gpost2_ref[...])

    tiles = [slice(k * sub, (k + 1) * sub) for k in range(OUT_SUBTILES)]
    att_ns = [(_rms(att_ref[0, rows, :].astype(F32)) * gatt_ref[...]).astype(BF16) for rows in tiles]
    heads = [head(tiles[0], att_ns[0])]
    for k in range(OUT_SUBTILES):
        if k + 1 < OUT_SUBTILES:
            heads.append(head(tiles[k + 1], att_ns[k + 1]))
        mlp(tiles[k], *heads[k])


def _outmlp(x, rnn_n, att, gatt, wo, gpost, g1, gpre, sc2, sh2, w1, w2, gpost2, g2):
    B, S, D = x.shape
    tm = min(TM_OUT, S)
    row = lambda n: pl.BlockSpec((1, n), lambda b, s: (0, 0))
    per_b = lambda n: pl.BlockSpec((1, 1, n), lambda b, s: (b, 0, 0))
    const = lambda a: pl.BlockSpec(a.shape, lambda b, s: (0,) * a.ndim, pipeline_mode=pl.Buffered(1))
    tok = lambda n: pl.BlockSpec((1, tm, n), lambda b, s: (b, s, 0))
    return pl.pallas_call(
        _outmlp_kernel,
        grid=(B, S // tm),
        in_specs=[tok(D), tok(D_RNN), tok(D_ATT), row(D_ATT), const(wo), row(D), per_b(D), row(D),
                  per_b(D), per_b(D), const(w1), const(w2), row(D), per_b(D)],
        out_specs=tok(D),
        out_shape=jax.ShapeDtypeStruct((B, S, D), F32),
        compiler_params=pltpu.CompilerParams(dimension_semantics=("arbitrary", "arbitrary"),
                                             vmem_limit_bytes=VMEM_LIMIT),
        name="outmlp",
    )(x, rnn_n, att, gatt, wo, gpost, g1, gpre, sc2, sh2, w1, w2, gpost2, g2)


def _block_diag(w):
    n, k, _ = w.shape
    return jnp.einsum('nij,nm->nimj', w, jnp.eye(n, dtype=w.dtype)).reshape(n * k, n * k)


def _layer(x, c, ada_w, ada_b, pre_norm_mix, w_in, conv_w, conv_b, lru_wa, lru_ba, lru_wx, lru_bx, lru_lambda,
           cmp_pos_k, cmp_w1_k, cmp_w2_k, cmp_pos_v, cmp_w1_v, cmp_w2_v, norm_rnn_out, norm_att_out, w_out,
           post_norm_mix, pre_norm_mlp, w_ff1, w_ff2, post_norm_mlp):
    B, S, D = x.shape
    G = N_KV
    row = lambda v: v.reshape(1, -1)

    mod = _ada(c, ada_w, ada_b)
    sh1, sc1, g1, sh2, sc2, g2 = [m.reshape(B, 1, D) for m in jnp.split(mod, 6, axis=-1)]

    gate_cols = [OFF_GATE + br * N_HEADS + g * REP + r for g in range(G) for br in range(3) for r in range(REP)]
    w_gate = w_in[:, jnp.asarray(gate_cols)].reshape(D, G, 3 * REP)
    w_gate = jnp.pad(w_gate, ((0, 0), (0, 0), (0, GATE_ROWS - 3 * REP))).reshape(D, G * GATE_ROWS)
    w_in_p = jnp.concatenate([w_in[:, :OFF_GATE], jnp.pad(w_gate, ((0, 0), (0, GATE_PAD - G * GATE_ROWS)))],
                             axis=1).astype(BF16)
    wa = _block_diag(lru_wa).astype(BF16)
    wx = _block_diag(lru_wx).astype(BF16)
    half = CMP_LEN // 2 * HEAD_DIM

    def w1_cat(w1):
        return jnp.concatenate([w1[:half], w1[half:]], axis=1).astype(BF16)

    def pos_rows(pos):
        return jnp.pad(pos.reshape(2, half), ((0, 14), (0, 0))).astype(BF16)

    rnn_n, qT, ks, vsT, kw, vwT, kc_in, vc_in, gT = _inproj(
        x, row(pre_norm_mix), sc1, sh1, w_in_p, conv_w, row(conv_b), wa, row(lru_ba), wx, row(lru_bx),
        row(lru_lambda), row(norm_rnn_out))
    kc, vc = _compress(kc_in, vc_in, w1_cat(cmp_w1_k), cmp_w2_k.astype(BF16), pos_rows(cmp_pos_k),
                       w1_cat(cmp_w1_v), cmp_w2_v.astype(BF16), pos_rows(cmp_pos_v))
    att = _attention(qT, ks, vsT, kw, vwT, kc, vc.transpose(0, 1, 3, 2), gT)

    return _outmlp(x, rnn_n, att, row(norm_att_out), w_out.astype(BF16), row(post_norm_mix), g1,
                   row(pre_norm_mlp), sc2, sh2, w_ff1.astype(BF16), w_ff2.astype(BF16), row(post_norm_mlp), g2)


def kernel(x, c, ada_w, ada_b, pre_norm_mix, w_in, conv_w, conv_b, lru_wa, lru_ba, lru_wx, lru_bx, lru_lambda,
           cmp_pos_k, cmp_w1_k, cmp_w2_k, cmp_pos_v, cmp_w1_v, cmp_w2_v, norm_rnn_out, norm_att_out, w_out,
           post_norm_mix, pre_norm_mlp, w_ff1, w_ff2, post_norm_mlp):
    for l in range(ada_w.shape[0]):
        x = _layer(x, c, ada_w[l], ada_b[l], pre_norm_mix[l], w_in[l], conv_w[l], conv_b[l], lru_wa[l], lru_ba[l],
                   lru_wx[l], lru_bx[l], lru_lambda[l], cmp_pos_k[l], cmp_w1_k[l], cmp_w2_k[l], cmp_pos_v[l],
                   cmp_w1_v[l], cmp_w2_v[l], norm_rnn_out[l], norm_att_out[l], w_out[l], post_norm_mix[l],
                   pre_norm_mlp[l], w_ff1[l], w_ff2[l], post_norm_mlp[l])
    return x
```

```python
import functools

import jax
import jax.numpy as jnp
from jax import lax
from jax.experimental import pallas as pl
from jax.experimental.pallas import tpu as pltpu

F32 = jnp.float32
BF16 = jnp.bfloat16
I32 = jnp.int32

D_MODEL = 1024
D_RNN = 512
CONV_WIDTH = 4
LRU_C = 8.0
N_HEADS = 8
HEAD_DIM = 64
N_KV = 2
REP = N_HEADS // N_KV
D_ATT = N_HEADS * HEAD_DIM
CMP_LEN = 32
CMP_STRIDE = 16
CMP_HIDDEN = 256
SLC_BLOCK = 64
SLC_TOPK = 16
WINDOW = 512
D_FF = 4 * D_MODEL
EPS = 1e-6
NEG = -1e30
FORCED = 1e4
LOG2E = 1.4426950408889634

KV_COLS = 6 * N_KV * HEAD_DIM
N_GATE = 3 * N_HEADS
GATE_PAD = 128
GATE_ROWS = 16
OFF_Q = 2 * D_RNN
OFF_KV = OFF_Q + D_ATT
OFF_GATE = OFF_KV + KV_COLS
D_IN_PAD = OFF_GATE + GATE_PAD

TM_IN = 1024
TM_OUT = 1024
OUT_SUBTILES = 4
TQ = 256
TK = 256
NCHAIN = 4
KAUG = 256
VROWS = 80
VMEM_LIMIT = 56 * 1024 * 1024

assert TQ == TK and WINDOW == 2 * TK
assert TK % SLC_BLOCK == 0 and CMP_LEN == 2 * CMP_STRIDE and CONV_WIDTH == 4
assert N_GATE <= N_KV * GATE_ROWS <= GATE_PAD and D_IN_PAD % 128 == 0
assert 2 * HEAD_DIM + 4 <= KAUG and HEAD_DIM < VROWS

SEL, WIN = 0, 1
MASK_CAUSAL, MASK_WINDOW_LOW = 0, 1
POS_COL = 2 * HEAD_DIM
DUMMY_COL = POS_COL + 4


def _gelu_tanh(x):
    return 0.5 * x * (1.0 + jnp.tanh(0.7978845608028654 * (x + 0.044715 * (x * x * x))))


def _sigmoid(x):
    return 0.5 * jnp.tanh(0.5 * x) + 0.5


def _rms(x):
    return x * lax.rsqrt(jnp.mean(x * x, axis=-1, keepdims=True) + EPS)


def _ada_kernel(c_ref, w_ref, b_ref, o_ref):
    c = c_ref[...]
    a = c * jax.nn.sigmoid(c)
    o_ref[...] = jnp.dot(a, w_ref[...], preferred_element_type=F32,
                         precision=lax.Precision.HIGHEST) + b_ref[...]


def _ada(c, w, b):
    B, D = c.shape
    N = w.shape[1]
    tn = 1024
    return pl.pallas_call(
        _ada_kernel,
        grid=(N // tn,),
        in_specs=[pl.BlockSpec((B, D), lambda j: (0, 0)),
                  pl.BlockSpec((D, tn), lambda j: (0, j)),
                  pl.BlockSpec((1, tn), lambda j: (0, j))],
        out_specs=pl.BlockSpec((B, tn), lambda j: (0, j)),
        out_shape=jax.ShapeDtypeStruct((B, N), F32),
        name="ada",
    )(c, w, b.reshape(1, N))


def _inproj_kernel(x_ref, gain_ref, sc_ref, sh_ref, w_ref, cw_ref, cb_ref, wa_ref, ba_ref, wx_ref, bx_ref,
                   lam_ref, grnn_ref,
                   rnn_ref, qT_ref, ks_ref, vs_ref, kw_ref, vw_ref, kc_ref, vc_ref, g_ref,
                   xbuf, hcar, a_s, u_s, h_s):
    tm = x_ref.shape[1]
    hd = HEAD_DIM

    @pl.when(pl.program_id(1) == 0)
    def _():
        xbuf[0:8, :] = jnp.zeros((8, D_RNN), F32)
        hcar[...] = jnp.zeros((1, D_RNN), F32)

    x = x_ref[0]
    h = _rms(x) * (gain_ref[...] * (1.0 + sc_ref[0])) + sh_ref[0]
    hb = h.astype(BF16)

    xr = jnp.dot(hb, w_ref[:, D_RNN:OFF_Q], preferred_element_type=F32)
    xbuf[8:8 + tm, :] = xr
    y = (cw_ref[3:4, :] * xr + cw_ref[2:3, :] * xbuf[7:7 + tm, :]
         + cw_ref[1:2, :] * xbuf[6:6 + tm, :] + cw_ref[0:1, :] * xbuf[5:5 + tm, :]) + cb_ref[...]
    xbuf[0:8, :] = xbuf[tm:tm + 8, :]

    yb = y.astype(BF16)
    r = _sigmoid(jnp.dot(yb, wa_ref[...], preferred_element_type=F32) + ba_ref[...])
    i = _sigmoid(jnp.dot(yb, wx_ref[...], preferred_element_type=F32) + bx_ref[...])
    nl = -lam_ref[...]
    softplus = jnp.maximum(nl, 0.0) + jnp.log(1.0 + jnp.exp(-jnp.abs(nl)))
    a = jnp.exp((-LRU_C) * r * softplus)
    a_s[...] = a
    u_s[...] = jnp.sqrt(1.0 - a * a) * (i * y)

    qT_ref[0] = (jnp.dot(hb, w_ref[:, OFF_Q:OFF_KV], preferred_element_type=F32)
                 * (HEAD_DIM ** -0.5 * LOG2E)).T.astype(BF16)
    kv = jnp.dot(hb, w_ref[:, OFF_KV:OFF_GATE], preferred_element_type=F32)
    width = N_KV * hd
    vsT = kv[:, 3 * width:4 * width].T
    vwT = kv[:, 5 * width:6 * width].T
    for gi in range(N_KV):
        kc_ref[0, gi] = kv[:, gi * hd:(gi + 1) * hd]
        vc_ref[0, gi] = kv[:, width + gi * hd:width + (gi + 1) * hd]
        ks_ref[0, gi] = kv[:, 2 * width + gi * hd:2 * width + (gi + 1) * hd].astype(BF16)
        kw_ref[0, gi] = kv[:, 4 * width + gi * hd:4 * width + (gi + 1) * hd].astype(BF16)
        for j in range(tm // TK):
            vs_ref[0, gi, j] = vsT[gi * hd:(gi + 1) * hd, j * TK:(j + 1) * TK].astype(BF16)
            vw_ref[0, gi, j] = vwT[gi * hd:(gi + 1) * hd, j * TK:(j + 1) * TK].astype(BF16)
    gates = _sigmoid(jnp.dot(hb, w_ref[:, OFF_GATE:D_IN_PAD], preferred_element_type=F32)).T
    for gi in range(N_KV):
        g_ref[0, gi] = gates[gi * GATE_ROWS:(gi + 1) * GATE_ROWS, :]
    g = jnp.dot(hb, w_ref[:, 0:D_RNN], preferred_element_type=F32)


    rows = lax.broadcasted_iota(I32, (8, D_RNN), 0)
    hprev = hcar[...]
    for gi in range(tm // 8):
        ag = a_s[gi * 8:gi * 8 + 8, :]
        ug = u_s[gi * 8:gi * 8 + 8, :]
        for k in (1, 2, 4):
            a_sh = jnp.where(rows >= k, pltpu.roll(ag, k, 0), 1.0)
            u_sh = jnp.where(rows >= k, pltpu.roll(ug, k, 0), 0.0)
            ug = ag * u_sh + ug
            ag = ag * a_sh
        hg = ag * hprev + ug
        h_s[gi * 8:gi * 8 + 8, :] = hg
        hprev = hg[7:8, :]
    hcar[...] = hprev

    rnn = _gelu_tanh(g) * h_s[...]
    rnn_ref[0] = (_rms(rnn) * grnn_ref[...]).astype(BF16)


def _inproj(x, gain, sc, sh, w_in, conv_w, conv_b, wa, ba, wx, bx, lam, grnn):
    B, S, D = x.shape
    tm = min(TM_IN, S)
    row = lambda n: pl.BlockSpec((1, n), lambda b, s: (0, 0))
    per_b = lambda n: pl.BlockSpec((1, 1, n), lambda b, s: (b, 0, 0))
    full = lambda a: pl.BlockSpec(a.shape, lambda b, s: (0,) * a.ndim)
    tok = lambda n: pl.BlockSpec((1, tm, n), lambda b, s: (b, s, 0))
    G = N_KV
    grp = pl.BlockSpec((1, G, tm, HEAD_DIM), lambda b, s: (b, 0, s, 0))
    grpT = pl.BlockSpec((1, G, tm // TK, HEAD_DIM, TK), lambda b, s: (b, 0, s, 0, 0))
    return pl.pallas_call(
        _inproj_kernel,
        grid=(B, S // tm),
        in_specs=[tok(D), row(D), per_b(D), per_b(D), full(w_in), full(conv_w), row(D_RNN),
                  full(wa), row(D_RNN), full(wx), row(D_RNN), row(D_RNN), row(D_RNN)],
        out_specs=[tok(D_RNN),
                   pl.BlockSpec((1, D_ATT, tm), lambda b, s: (b, 0, s)),
                   grp, grpT, grp, grpT, grp, grp,
                   pl.BlockSpec((1, G, GATE_ROWS, tm), lambda b, s: (b, 0, 0, s))],
        out_shape=[jax.ShapeDtypeStruct((B, S, D_RNN), BF16),
                   jax.ShapeDtypeStruct((B, D_ATT, S), BF16),
                   jax.ShapeDtypeStruct((B, G, S, HEAD_DIM), BF16),
                   jax.ShapeDtypeStruct((B, G, S // TK, HEAD_DIM, TK), BF16),
                   jax.ShapeDtypeStruct((B, G, S, HEAD_DIM), BF16),
                   jax.ShapeDtypeStruct((B, G, S // TK, HEAD_DIM, TK), BF16),
                   jax.ShapeDtypeStruct((B, G, S, HEAD_DIM), F32),
                   jax.ShapeDtypeStruct((B, G, S, HEAD_DIM), F32),
                   jax.ShapeDtypeStruct((B, G, GATE_ROWS, S), F32)],
        scratch_shapes=[pltpu.VMEM((tm + 8, D_RNN), F32), pltpu.VMEM((1, D_RNN), F32),
                        pltpu.VMEM((tm, D_RNN), F32), pltpu.VMEM((tm, D_RNN), F32),
                        pltpu.VMEM((tm, D_RNN), F32)],
        compiler_params=pltpu.CompilerParams(dimension_semantics=("arbitrary", "arbitrary"),
                                             vmem_limit_bytes=VMEM_LIMIT),
        name="inproj",
    )(x, gain, sc, sh, w_in, conv_w, conv_b, wa, ba, wx, bx, lam, grnn)


def _compress_kernel(rk_ref, rv_ref, w1k_ref, w2k_ref, pk_ref, w1v_ref, w2v_ref, pv_ref, kc_ref, vc_ref):
    def one(r_ref, w1_ref, w2_ref, pos_ref, o_ref):
        n = r_ref.shape[2] // CMP_STRIDE
        rows = jnp.concatenate([r_ref[0, 0, pl.ds(l, n, stride=CMP_STRIDE), :] for l in range(CMP_STRIDE)], axis=1)
        p = jnp.dot(rows.astype(BF16), w1_ref[...], preferred_element_type=F32)
        posb = jnp.dot(pos_ref[...], w1_ref[...], preferred_element_type=F32)
        bias = posb[0:1, 0:CMP_HIDDEN] + posb[1:2, CMP_HIDDEN:]
        pre = p[:, 0:CMP_HIDDEN] + pltpu.roll(p[:, CMP_HIDDEN:], n - 1, 0) + bias
        hid = _gelu_tanh(pre).astype(BF16)
        o_ref[0, 0] = jnp.dot(hid, w2_ref[...], preferred_element_type=F32).astype(BF16)

    one(rk_ref, w1k_ref, w2k_ref, pk_ref, kc_ref)
    one(rv_ref, w1v_ref, w2v_ref, pv_ref, vc_ref)


def _compress(rk, rv, w1k, w2k, posk, w1v, w2v, posv):
    B, G, S, width = rk.shape
    n = S // CMP_STRIDE
    blk = pl.BlockSpec((1, 1, S, width), lambda b, g: (b, g, 0, 0))
    full = lambda a: pl.BlockSpec(a.shape, lambda b, g: (0,) * a.ndim)
    out = pl.BlockSpec((1, 1, n, HEAD_DIM), lambda b, g: (b, g, 0, 0))
    return pl.pallas_call(
        _compress_kernel,
        grid=(B, G),
        in_specs=[blk, blk, full(w1k), full(w2k), full(posk), full(w1v), full(w2v), full(posv)],
        out_specs=[out, out],
        out_shape=[jax.ShapeDtypeStruct((B, G, n, HEAD_DIM), BF16)] * 2,
        compiler_params=pltpu.CompilerParams(vmem_limit_bytes=VMEM_LIMIT),
        name="compress",
    )(rk, rv, w1k, w2k, posk, w1v, w2v, posv)


def _attn_kernel(qT_ref, ks_ref, vs_ref, kw_ref, vw_ref, kc_ref, vcT_ref, g_ref,
                 o_ref,
                 kall, vall, qall, kca, m_all, acc_all, al_s, sbuf, pbuf, oacc, val_s, rank_s, selbias_s,
                 bias_tbl, cmask,
                 oT_s, st_s, *, n_sel):
    S = ks_ref.shape[2]
    nkt = S // TK
    ncmp = kc_ref.shape[2]
    b = pl.program_id(0)
    g = pl.program_id(1)
    step = pl.program_id(2)
    W4 = REP * TQ

    @pl.when((b == 0) & (g == 0) & (step == 0))
    def _():
        col = lax.broadcasted_iota(I32, (TK, KAUG), 1)
        is_blk = (col == POS_COL) | (col == POS_COL + 2)
        is_off = (col == POS_COL + 1) | (col == POS_COL + 3)
        ones_row = (lax.broadcasted_iota(I32, (VROWS - HEAD_DIM, TK), 0) == 0).astype(BF16)
        for kt in range(nkt):
            key = kt * TK + lax.broadcasted_iota(I32, (TK, KAUG), 0)
            pos = jnp.where(is_blk, (key // SLC_BLOCK).astype(F32),
                            jnp.where(is_off, (key % SLC_BLOCK).astype(F32), 0.0))
            onehot = (col - HEAD_DIM == key // SLC_BLOCK).astype(F32)
            kall[SEL, kt * TK:(kt + 1) * TK, :] = (pos + onehot).astype(BF16)
            kall[WIN, kt * TK:(kt + 1) * TK, :] = pos.astype(BF16)
            vall[SEL, kt, HEAD_DIM:VROWS, :] = ones_row
            vall[WIN, kt, HEAD_DIM:VROWS, :] = ones_row
        dummy = (col == DUMMY_COL).astype(BF16)
        for br in (SEL, WIN):
            kall[br, nkt * TK:(nkt + 1) * TK, :] = dummy
            vall[br, nkt] = jnp.zeros((VROWS, TK), BF16)
        ko = lax.broadcasted_iota(I32, (TK, W4), 0)
        to = lax.broadcasted_iota(I32, (TK, W4), 1) % TQ
        bias_tbl[MASK_CAUSAL] = jnp.where(ko <= to, 0.0, NEG)
        bias_tbl[MASK_WINDOW_LOW] = jnp.where(ko > to, 0.0, NEG)
        u = lax.broadcasted_iota(I32, (2 * ncmp, W4), 0)
        tc = lax.broadcasted_iota(I32, (2 * ncmp, W4), 1) % TQ
        cmask[...] = jnp.where(u <= ncmp + ((tc + 1) // CMP_STRIDE) - 2, 0.0, NEG)
        cc = lax.broadcasted_iota(I32, (ncmp, KAUG), 0)
        colc = lax.broadcasted_iota(I32, (ncmp, KAUG), 1)
        kca[...] = jnp.where((colc == POS_COL) | (colc == POS_COL + 2), (cc // 4).astype(F32),
                             jnp.where((colc == POS_COL + 1) | (colc == POS_COL + 3),
                                       (CMP_STRIDE * (cc % 4)).astype(F32), 0.0)).astype(BF16)
        qall[...] = jnp.zeros(qall.shape, BF16)

    lane = lax.broadcasted_iota(I32, (1, W4), 1)
    head = g * REP + lane // TQ
    slope = lax.bitcast_convert_type((126 - head) << 23, F32)

    @pl.when(step == 0)
    def _():
        for kt in range(nkt):
            kall[SEL, kt * TK:(kt + 1) * TK, 0:HEAD_DIM] = ks_ref[0, 0, kt * TK:(kt + 1) * TK, :]
            kall[WIN, kt * TK:(kt + 1) * TK, 0:HEAD_DIM] = kw_ref[0, 0, kt * TK:(kt + 1) * TK, :]
            vall[SEL, kt, 0:HEAD_DIM, :] = vs_ref[0, 0, kt]
            vall[WIN, kt, 0:HEAD_DIM, :] = vw_ref[0, 0, kt]
        kca[:, 0:HEAD_DIM] = kc_ref[0, 0]
        c_hi = (slope * LOG2E).astype(BF16).astype(F32)
        c_lo = slope * LOG2E - c_hi
        r16 = lax.broadcasted_iota(I32, (16, W4), 0)
        alibi = jnp.where(r16 == 0, c_hi * SLC_BLOCK, jnp.where(r16 == 1, c_hi, jnp.where(
            r16 == 2, c_lo * SLC_BLOCK, jnp.where(r16 == 3, c_lo, jnp.where(
                r16 == DUMMY_COL - POS_COL, NEG, 0.0))))).astype(BF16)
        for c in range(NCHAIN):
            qall[c, SEL, POS_COL:POS_COL + 16, :] = alibi
            qall[c, WIN, POS_COL:POS_COL + 16, :] = alibi

    chains = [_attn_chain(c, NCHAIN * step + c, qT_ref, kc_ref, vcT_ref, g_ref, o_ref, kall, vall, qall.at[c], kca,
                          m_all.at[c], acc_all.at[c], al_s.at[c], sbuf.at[c], pbuf.at[c], oacc.at[c], val_s.at[c],
                          rank_s.at[c], selbias_s.at[c], bias_tbl, cmask, oT_s.at[c], st_s.at[c], n_sel)
              for c in range(NCHAIN)]
    live = list(chains)
    while live:
        handed = [next(ch, _DONE) for ch in live]
        live = [ch for ch, r in zip(live, handed) if r is not _DONE]
        loops = [r for r in handed if r is not _DONE and r is not None]
        if loops:
            joint = functools.reduce(jnp.minimum, [npair for npair, _, _ in loops])

            def trip(j, group):
                als = [head(j) for _, head, _ in group]
                for (_, _, tail), al in zip(group, als):
                    tail(j, al)

            lax.fori_loop(1, 1 + joint, lambda j, carry: (trip(j, loops), carry)[1], 0)
            for item in loops:
                lax.fori_loop(1 + joint, 1 + item[0], lambda j, carry, item=item: (trip(j, [item]), carry)[1], 0)


_DONE = object()


def _drain(gen):
    try:
        while True:
            next(gen)
    except StopIteration as stop:
        return stop.value


def _attn_chain(c, qi, qT_ref, kc_ref, vcT_ref, g_ref, o_ref, kall, vall, qall, kca, m_all, acc_all, al_s, sbuf,
                pbuf, oacc, val_s, rank_s, selbias_s, bias_tbl, cmask, oT_s, st_s, n_sel):
    S = kall.shape[1] - TK
    ncmp = kc_ref.shape[2]
    nblk = S // SLC_BLOCK
    cmp_per_tile = TQ // CMP_STRIDE
    q0 = qi * TQ
    W4 = REP * TQ
    lane = lax.broadcasted_iota(I32, (1, W4), 1)
    lanes_c = slice(c * TQ, (c + 1) * TQ)

    q = qT_ref[0, :, lanes_c]
    for r in range(REP):
        qr = q[r * HEAD_DIM:(r + 1) * HEAD_DIM, :]
        qall[SEL, 0:HEAD_DIM, r * TQ:(r + 1) * TQ] = qr
        qall[WIN, 0:HEAD_DIM, r * TQ:(r + 1) * TQ] = qr

    gall = g_ref[0, 0, :, lanes_c]
    gates = [gall[br * REP:(br + 1) * REP, :] for br in range(3)]

    def stage_scores(br, tiles_masks):
        col_max = None
        for h, (tile, mask) in enumerate(tiles_masks):
            r0 = pl.multiple_of(tile * TK, TK)
            s = jnp.dot(kall[br, pl.ds(r0, TK), :], qall[br], preferred_element_type=F32)
            if mask is not None:
                s = s + bias_tbl[mask]
            sbuf[h * TK:(h + 1) * TK, :] = s
            c = jnp.max(s, axis=0, keepdims=True)
            col_max = c if col_max is None else jnp.maximum(col_max, c)
            yield
        m_old = m_all[br]
        m_new = jnp.maximum(m_old, col_max)
        m_all[br] = m_new
        return jnp.exp2(m_old - m_new)

    def stage_probs(br, ntile):
        for h in range(ntile):
            rows = slice(h * TK, (h + 1) * TK)
            pbuf[rows, :] = jnp.exp2(sbuf[rows, :] - m_all[br]).astype(BF16)
            yield

    def stage_values(br, tiles, al_row):
        acc = acc_all[br] * al_row
        for h, tile in enumerate(tiles):
            acc = acc + jnp.dot(vall[br, tile], pbuf[h * TK:(h + 1) * TK, :], preferred_element_type=F32)
            if h + 1 < len(tiles):
                yield
        acc_all[br] = acc
        yield

    def start_branch(br, tl):
        al_s[...] = yield from stage_scores(br, [(tl[0], MASK_CAUSAL), (tl[1], None)])

    def pair_items(br, n, tl):
        npair = jnp.maximum(n - 2, 0) // 2

        def head(j):
            al_prev = al_s[...]
            _drain(stage_probs(br, 2))
            al_s[...] = _drain(stage_scores(br, [(tl[2 * j], None), (tl[2 * j + 1], None)]))
            return al_prev

        def tail(j, al_prev):
            _drain(stage_values(br, [tl[2 * j - 2], tl[2 * j - 1]], al_prev))

        return npair, head, tail

    def finish_branch(br, n, tl):
        rest = jnp.maximum(n - 2, 0)
        last = 2 * (rest // 2)
        t_single = tl[jnp.where(rest % 2 == 1, n - 1, n)]
        al_prev = al_s[...]
        yield from stage_probs(br, 2)
        al_k = yield from stage_scores(br, [(t_single, None)])
        yield from stage_values(br, [tl[last], tl[last + 1]], al_prev)
        yield from stage_probs(br, 1)
        yield from stage_values(br, [t_single], al_k)

    m_all[...] = jnp.full(m_all.shape, NEG, F32)
    acc_all[...] = jnp.zeros(acc_all.shape, F32)
    nkt = S // TK
    w1 = jnp.where(qi >= 2, qi - 2, nkt)
    w2 = jnp.where(qi >= 1, qi - 1, nkt)
    al_w0 = yield from stage_scores(WIN, [(qi, MASK_CAUSAL), (w1, MASK_WINDOW_LOW)])

    c0 = pl.multiple_of(ncmp - cmp_per_tile * qi, cmp_per_tile)
    sc = jnp.dot(kca[...], qall[WIN], preferred_element_type=F32) + cmask[pl.ds(c0, ncmp), :]
    yield
    e = jnp.exp2(sc - jnp.max(sc, axis=0, keepdims=True))
    tq = q0 + lane % TQ
    p = e * ((1.0 / jnp.sum(e, axis=0, keepdims=True)) * (tq >= CMP_LEN - 1).astype(F32))
    ocT = jnp.dot(vcT_ref[0, 0], p.astype(BF16), preferred_element_type=F32)
    for r in range(REP):
        oacc[:, r * TQ:(r + 1) * TQ] = gates[0][r:r + 1, :] * ocT[:, r * TQ:(r + 1) * TQ]
    yield

    yield from stage_probs(WIN, 2)
    al_w1 = yield from stage_scores(WIN, [(w2, None)])
    yield from stage_values(WIN, [qi, w1], al_w0)

    psum = p[:, 0:TQ]
    for r in range(1, REP):
        psum = psum + p[:, r * TQ:(r + 1) * TQ]
    jj = lax.broadcasted_iota(I32, (nblk, ncmp), 0)
    cc = lax.broadcasted_iota(I32, (nblk, ncmp), 1)
    ovT = ((CMP_STRIDE * cc < SLC_BLOCK * jj + SLC_BLOCK)
           & (CMP_STRIDE * cc + CMP_LEN > SLC_BLOCK * jj)).astype(BF16)
    p_hi = psum.astype(BF16)
    rest = psum - p_hi.astype(F32)
    p_mid = rest.astype(BF16)
    p_lo = (rest - p_mid.astype(F32)).astype(BF16)
    impT = (jnp.dot(ovT, p_hi, preferred_element_type=F32) + jnp.dot(ovT, p_mid, preferred_element_type=F32)
            + jnp.dot(ovT, p_lo, preferred_element_type=F32))

    j_i = lax.broadcasted_iota(I32, (nblk, TQ), 0)
    t1 = q0 + lax.broadcasted_iota(I32, (nblk, TQ), 1)
    cur = t1 // SLC_BLOCK
    forced = (j_i == 0) | (j_i == cur) | (j_i == cur - 1)
    visible = SLC_BLOCK * j_i <= t1
    val_s[...] = jnp.where(forced, FORCED, jnp.where(visible, impT, NEG))
    yield
    ngrp = nblk // 8
    rank_s[...] = jnp.zeros(rank_s.shape, F32)
    j8 = lax.broadcasted_iota(I32, (8, TQ), 0)
    for ib in range(ngrp):
        @pl.when(8 * ib * SLC_BLOCK < q0 + TQ)
        def _(ib=ib):
            vals = [val_s[8 * jb:8 * jb + 8, :] for jb in range(ngrp)]
            ranks = [rank_s[8 * jb:8 * jb + 8, :] for jb in range(ngrp)]
            for i in range(8 * ib, 8 * ib + 8):
                row = jnp.broadcast_to(val_s[i:i + 1, :], (8, TQ))
                for jb in range(ngrp):
                    if jb > ib:
                        hit = row >= vals[jb]
                    elif jb < ib:
                        hit = row > vals[jb]
                    else:
                        hit = jnp.where(j8 > i - 8 * jb, jnp.where(row >= vals[jb], 1.0, 0.0),
                                        jnp.where(row > vals[jb], 1.0, 0.0)) > 0.5
                    ranks[jb] = ranks[jb] + jnp.where(hit, 1.0, 0.0)
            for jb in range(ngrp):
                rank_s[8 * jb:8 * jb + 8, :] = ranks[jb]
    yield
    ranks = [rank_s[8 * jb:8 * jb + 8, :] for jb in range(ngrp)]
    blocks_per_tile = TK // SLC_BLOCK
    st_s[0] = qi
    n_selt = jnp.int32(1)
    for jb in range(ngrp):
        chosen = ranks[jb] < n_sel
        selb = jnp.where(chosen, 0.0, NEG)
        for r in range(REP):
            selbias_s[8 * jb:8 * jb + 8, r * TQ:(r + 1) * TQ] = selb
        any_q = jnp.max(jnp.where(chosen, 1.0, 0.0), axis=1, keepdims=True)
        for hh in range(8 // blocks_per_tile):
            kt = (8 * jb) // blocks_per_tile + hh
            hit = jnp.max(any_q[hh * blocks_per_tile:(hh + 1) * blocks_per_tile, :]) > 0.5
            st_s[n_selt] = kt
            n_selt = n_selt + jnp.where(hit & (kt < qi), 1, 0)
    st_s[n_selt] = nkt
    qall[SEL, HEAD_DIM:HEAD_DIM + nblk, :] = selbias_s[...].astype(BF16)
    yield

    yield from stage_probs(WIN, 1)
    yield from stage_values(WIN, [w2], al_w1)
    yield from start_branch(SEL, st_s)
    yield pair_items(SEL, n_selt, st_s)
    yield from finish_branch(SEL, n_selt, st_s)
    yield

    def normalised(br):
        acc = acc_all[br]
        return acc[0:HEAD_DIM, :] / acc[HEAD_DIM:HEAD_DIM + 1, :]

    o_sel = normalised(SEL)
    o_win = normalised(WIN)
    for r in range(REP):
        lanes = slice(r * TQ, (r + 1) * TQ)
        oT_s[r * HEAD_DIM:(r + 1) * HEAD_DIM, :] = (oacc[:, lanes] + gates[1][r:r + 1, :] * o_sel[:, lanes]
                                                    + gates[2][r:r + 1, :] * o_win[:, lanes])
    o_ref[0, lanes_c, :] = oT_s[...].T.astype(BF16)


def _attention(qT, ks, vs, kw, vw, kc, vcT, gT):
    B, _, S = qT.shape
    G = N_KV
    assert S % (NCHAIN * TQ) == 0 and (S // SLC_BLOCK) % 8 == 0 and S // SLC_BLOCK <= HEAD_DIM
    nkt = S // TK
    ncmp = kc.shape[2]
    assert ncmp == S // CMP_STRIDE and ncmp // 4 < 256
    nblk = S // SLC_BLOCK
    n_sel = min(SLC_TOPK, nblk)
    W4 = REP * TQ
    res4 = lambda a: pl.BlockSpec((1, 1) + a.shape[2:], lambda b, g, i: (b, g) + (0,) * (a.ndim - 2))
    return pl.pallas_call(
        functools.partial(_attn_kernel, n_sel=n_sel),
        grid=(B, G, S // (NCHAIN * TQ)),
        in_specs=[pl.BlockSpec((1, REP * HEAD_DIM, NCHAIN * TQ), lambda b, g, i: (b, g, i)),
                  res4(ks), res4(vs), res4(kw), res4(vw), res4(kc), res4(vcT),
                  pl.BlockSpec((1, 1, GATE_ROWS, NCHAIN * TQ), lambda b, g, i: (b, g, 0, i))],
        out_specs=pl.BlockSpec((1, NCHAIN * TQ, REP * HEAD_DIM), lambda b, g, i: (b, i, g)),
        out_shape=jax.ShapeDtypeStruct((B, S, D_ATT), BF16),
        scratch_shapes=[pltpu.VMEM((2, S + TK, KAUG), BF16),
                        pltpu.VMEM((2, nkt + 1, VROWS, TK), BF16),
                        pltpu.VMEM((NCHAIN, 2, KAUG, W4), BF16),
                        pltpu.VMEM((ncmp, KAUG), BF16),
                        pltpu.VMEM((NCHAIN, 2, 1, W4), F32),
                        pltpu.VMEM((NCHAIN, 2, VROWS, W4), F32),
                        pltpu.VMEM((NCHAIN, 1, W4), F32),
                        pltpu.VMEM((NCHAIN, 2 * TK, W4), F32),
                        pltpu.VMEM((NCHAIN, 2 * TK, W4), BF16),
                        pltpu.VMEM((NCHAIN, HEAD_DIM, W4), F32),
                        pltpu.VMEM((NCHAIN, nblk, TQ), F32),
                        pltpu.VMEM((NCHAIN, nblk, TQ), F32),
                        pltpu.VMEM((NCHAIN, nblk, W4), F32),
                        pltpu.VMEM((2, TK, W4), F32),
                        pltpu.VMEM((2 * ncmp, W4), F32),
                        pltpu.VMEM((NCHAIN, REP * HEAD_DIM, TQ), F32),
                        pltpu.SMEM((NCHAIN, nkt + 2), I32)],
        compiler_params=pltpu.CompilerParams(dimension_semantics=("arbitrary", "arbitrary", "arbitrary"),
                                             vmem_limit_bytes=VMEM_LIMIT),
        name="attn",
    )(qT, ks, vs, kw, vw, kc, vcT, gT)


def _outmlp_kernel(x_ref, rnn_ref, att_ref, gatt_ref, wo_ref, gpost_ref, g1_ref, gpre_ref, sc2_ref, sh2_ref,
                   w1_ref, w2_ref, gpost2_ref, g2_ref, o_ref):
    tm = x_ref.shape[1]
    sub = tm // OUT_SUBTILES
    fc = 1024

    def head(rows, att_n):
        y = (jnp.dot(rnn_ref[0, rows, :], wo_ref[0:D_RNN, :], preferred_element_type=F32)
             + jnp.dot(att_n, wo_ref[D_RNN:, :], preferred_element_type=F32))
        x1 = x_ref[0, rows, :] + (1.0 + g1_ref[0]) * (_rms(y) * gpost_ref[...])
        return x1, (_rms(x1) * (gpre_ref[...] * (1.0 + sc2_ref[0])) + sh2_ref[0]).astype(BF16)

    def mlp(rows, x1, h2):
        ff = jnp.zeros(x1.shape, F32)
        for c in range(D_FF // fc):
            hid = jnp.maximum(jnp.dot(h2, w1_ref[:, c * fc:(c + 1) * fc], preferred_element_type=F32), 0.0)
            ff = ff + jnp.dot((hid * hid).astype(BF16), w2_ref[c * fc:(c + 1) * fc, :],
                              preferred_element_type=F32)
        o_ref[0, rows, :] = x1 + (1.0 + g2_ref[0]) * (_rms(ff) * gpost2_ref[...])

    tiles = [slice(k * sub, (k + 1) * sub) for k in range(OUT_SUBTILES)]
    att_ns = [(_rms(att_ref[0, rows, :].astype(F32)) * gatt_ref[...]).astype(BF16) for rows in tiles]
    heads = [head(tiles[0], att_ns[0])]
    for k in range(OUT_SUBTILES):
        if k + 1 < OUT_SUBTILES:
            heads.append(head(tiles[k + 1], att_ns[k + 1]))
        mlp(tiles[k], *heads[k])


def _outmlp(x, rnn_n, att, gatt, wo, gpost, g1, gpre, sc2, sh2, w1, w2, gpost2, g2):
    B, S, D = x.shape
    tm = min(TM_OUT, S)
    row = lambda n: pl.BlockSpec((1, n), lambda b, s: (0, 0))
    per_b = lambda n: pl.BlockSpec((1, 1, n), lambda b, s: (b, 0, 0))
    const = lambda a: pl.BlockSpec(a.shape, lambda b, s: (0,) * a.ndim, pipeline_mode=pl.Buffered(1))
    tok = lambda n: pl.BlockSpec((1, tm, n), lambda b, s: (b, s, 0))
    return pl.pallas_call(
        _outmlp_kernel,
        grid=(B, S // tm),
        in_specs=[tok(D), tok(D_RNN), tok(D_ATT), row(D_ATT), const(wo), row(D), per_b(D), row(D),
                  per_b(D), per_b(D), const(w1), const(w2), row(D), per_b(D)],
        out_specs=tok(D),
        out_shape=jax.ShapeDtypeStruct((B, S, D), F32),
        compiler_params=pltpu.CompilerParams(dimension_semantics=("arbitrary", "arbitrary"),
                                             vmem_limit_bytes=VMEM_LIMIT),
        name="outmlp",
    )(x, rnn_n, att, gatt, wo, gpost, g1, gpre, sc2, sh2, w1, w2, gpost2, g2)


def _block_diag(w):
    n, k, _ = w.shape
    return jnp.einsum('nij,nm->nimj', w, jnp.eye(n, dtype=w.dtype)).reshape(n * k, n * k)


def _layer(x, c, ada_w, ada_b, pre_norm_mix, w_in, conv_w, conv_b, lru_wa, lru_ba, lru_wx, lru_bx, lru_lambda,
           cmp_pos_k, cmp_w1_k, cmp_w2_k, cmp_pos_v, cmp_w1_v, cmp_w2_v, norm_rnn_out, norm_att_out, w_out,
           post_norm_mix, pre_norm_mlp, w_ff1, w_ff2, post_norm_mlp):
    B, S, D = x.shape
    G = N_KV
    row = lambda v: v.reshape(1, -1)

    mod = _ada(c, ada_w, ada_b)
    sh1, sc1, g1, sh2, sc2, g2 = [m.reshape(B, 1, D) for m in jnp.split(mod, 6, axis=-1)]

    gate_cols = [OFF_GATE + br * N_HEADS + g * REP + r for g in range(G) for br in range(3) for r in range(REP)]
    w_gate = w_in[:, jnp.asarray(gate_cols)].reshape(D, G, 3 * REP)
    w_gate = jnp.pad(w_gate, ((0, 0), (0, 0), (0, GATE_ROWS - 3 * REP))).reshape(D, G * GATE_ROWS)
    w_in_p = jnp.concatenate([w_in[:, :OFF_GATE], jnp.pad(w_gate, ((0, 0), (0, GATE_PAD - G * GATE_ROWS)))],
                             axis=1).astype(BF16)
    wa = _block_diag(lru_wa).astype(BF16)
    wx = _block_diag(lru_wx).astype(BF16)
    half = CMP_LEN // 2 * HEAD_DIM

    def w1_cat(w1):
        return jnp.concatenate([w1[:half], w1[half:]], axis=1).astype(BF16)

    def pos_rows(pos):
        return jnp.pad(pos.reshape(2, half), ((0, 14), (0, 0))).astype(BF16)

    rnn_n, qT, ks, vsT, kw, vwT, kc_in, vc_in, gT = _inproj(
        x, row(pre_norm_mix), sc1, sh1, w_in_p, conv_w, row(conv_b), wa, row(lru_ba), wx, row(lru_bx),
        row(lru_lambda), row(norm_rnn_out))
    kc, vc = _compress(kc_in, vc_in, w1_cat(cmp_w1_k), cmp_w2_k.astype(BF16), pos_rows(cmp_pos_k),
                       w1_cat(cmp_w1_v), cmp_w2_v.astype(BF16), pos_rows(cmp_pos_v))
    att = _attention(qT, ks, vsT, kw, vwT, kc, vc.transpose(0, 1, 3, 2), gT)

    return _outmlp(x, rnn_n, att, row(norm_att_out), w_out.astype(BF16), row(post_norm_mix), g1,
                   row(pre_norm_mlp), sc2, sh2, w_ff1.astype(BF16), w_ff2.astype(BF16), row(post_norm_mlp), g2)


def kernel(x, c, ada_w, ada_b, pre_norm_mix, w_in, conv_w, conv_b, lru_wa, lru_ba, lru_wx, lru_bx, lru_lambda,
           cmp_pos_k, cmp_w1_k, cmp_w2_k, cmp_pos_v, cmp_w1_v, cmp_w2_v, norm_rnn_out, norm_att_out, w_out,
           post_norm_mix, pre_norm_mlp, w_ff1, w_ff2, post_norm_mlp):
    for l in range(ada_w.shape[0]):
        x = _layer(x, c, ada_w[l], ada_b[l], pre_norm_mix[l], w_in[l], conv_w[l], conv_b[l], lru_wa[l], lru_ba[l],
                   lru_wx[l], lru_bx[l], lru_lambda[l], cmp_pos_k[l], cmp_w1_k[l], cmp_w2_k[l], cmp_pos_v[l],
                   cmp_w1_v[l], cmp_w2_v[l], norm_rnn_out[l], norm_att_out[l], w_out[l], post_norm_mix[l],
                   pre_norm_mlp[l], w_ff1[l], w_ff2[l], post_norm_mlp[l])
    return x
```
